```python
import math
import jax, jax.numpy as jnp
from jax import lax
import numpy as np

D_MODEL = 1024
BATCH = 1
SEQ = 16384
DEPTH = 2
DEC_BATCH = 2
DEC_SEQ = 8192
PAST_LEN = 128

N_MIXERS = 2
EPS = 1e-6
RET_HEADS = 4
RET_DK = D_MODEL // RET_HEADS
RET_DV = 2 * D_MODEL // RET_HEADS
RET_CHUNK = 128
ROPE_BASE = 10000.0
RET_IN_DIM = 2 * D_MODEL + 2 * 2 * D_MODEL
SSM_DINNER = 2 * D_MODEL
SSM_HEADDIM = 64
SSM_HEADS = SSM_DINNER // SSM_HEADDIM
SSM_GROUPS = 4
SSM_DSTATE = 128
SSM_CONV = 5
SSM_CHUNK = 128
SSM_CONV_DIM = SSM_DINNER + 2 * SSM_GROUPS * SSM_DSTATE
SSM_IN_DIM = SSM_DINNER + SSM_CONV_DIM + 2 * SSM_HEADS
FFN_DENSE = 2816
N_EXPERTS = 8
TOP_K = 2
FFN_EXPERT = 3584

kernel_name = "hybrid_retnet_mamba2_bidir_encoder"


def rms_norm(x, g):
    xf = x.astype(jnp.float32)
    y = xf * lax.rsqrt(jnp.mean(xf * xf, axis=-1, keepdims=True) + EPS)
    return (y * g.astype(jnp.float32)).astype(x.dtype)


def rms_normalize_f32(x):
    return x * lax.rsqrt(jnp.mean(x * x, axis=-1, keepdims=True) + EPS)


def rotary(x, pos):
    half = x.shape[-1] // 2
    inv = ROPE_BASE ** (-jnp.arange(half, dtype=jnp.float32) / half)
    ang = pos.astype(jnp.float32)[:, None] * inv[None, :]
    cos = jnp.cos(ang)[None, :, None, :]
    sin = jnp.sin(ang)[None, :, None, :]
    x1, x2 = x[..., :half], x[..., half:]
    return jnp.concatenate([x1 * cos - x2 * sin, x1 * sin + x2 * cos], axis=-1)


def retention(h, w_in, w_out):
    Bsz, S, _ = h.shape
    C = RET_CHUNK
    NC = S // C
    H = RET_HEADS
    proj = h @ w_in
    q, k, v, g = jnp.split(proj, [D_MODEL, 2 * D_MODEL, 4 * D_MODEL], axis=-1)
    pos = jnp.arange(S)
    q = rotary(q.astype(jnp.float32).reshape(Bsz, S, H, RET_DK), pos)
    k = rotary(k.astype(jnp.float32).reshape(Bsz, S, H, RET_DK), pos) * (RET_DK ** -0.5)
    v = v.astype(jnp.float32).reshape(Bsz, S, H, RET_DV)
    log_gamma = jnp.log1p(-jnp.power(2.0, -5.0 - jnp.arange(H, dtype=jnp.float32)))

    qc = q.reshape(Bsz, NC, C, H, RET_DK)
    kc = k.reshape(Bsz, NC, C, H, RET_DK)
    vc = v.reshape(Bsz, NC, C, H, RET_DV)

    idx = jnp.arange(C, dtype=jnp.float32)
    dist = jnp.abs(idx[:, None] - idx[None, :])
    decay_in = jnp.exp(log_gamma[:, None, None] * dist)
    s = jnp.einsum('bnqhd,bnkhd->bnhqk', qc, kc) * decay_in
    o_in = jnp.einsum('bnhqk,bnkhe->bnqhe', s, vc)

    lg = log_gamma[None, :]
    a = idx[:, None]
    wq_f = jnp.exp(lg * (a + 1.0))[..., None]
    wk_f = jnp.exp(lg * (C - 1.0 - a))[..., None]
    wq_b = jnp.exp(lg * (C - a))[..., None]
    wk_b = jnp.exp(lg * a)[..., None]
    chunk_decay = jnp.exp(log_gamma * C)[None, :, None, None]

    xs = (jnp.moveaxis(qc, 1, 0), jnp.moveaxis(kc, 1, 0), jnp.moveaxis(vc, 1, 0))
    state0 = jnp.zeros((Bsz, H, RET_DK, RET_DV), jnp.float32)

    def make_step(wq, wk):
        def step(state, inp):
            qn, kn, vn = inp
            o = jnp.einsum('bqhd,bhde->bqhe', qn * wq, state)
            state = state * chunk_decay + jnp.einsum('bkhd,bkhe->bhde', kn * wk, vn)
            return state, o
        return step

    _, o_f = lax.scan(make_step(wq_f, wk_f), state0, xs)
    _, o_b = lax.scan(make_step(wq_b, wk_b), state0, xs, reverse=True)
    o = o_in + jnp.moveaxis(o_f + o_b, 0, 1)
    o = rms_normalize_f32(o.reshape(Bsz, S, H, RET_DV)).reshape(Bsz, S, 2 * D_MODEL)
    y = (jax.nn.silu(g.astype(jnp.float32)) * o).astype(h.dtype)
    return y @ w_out


def ssd_chunked(x, dt, A, Bm, Cm, strict):
    Bsz, S, H, P = x.shape
    G, N = Bm.shape[2], Bm.shape[3]
    Hg = H // G
    L = SSM_CHUNK
    NC = S // L
    x = x.reshape(Bsz, NC, L, G, Hg, P)
    dt = dt.reshape(Bsz, NC, L, G, Hg)
    Bm = Bm.reshape(Bsz, NC, L, G, N)
    Cm = Cm.reshape(Bsz, NC, L, G, N)
    acs = jnp.cumsum(dt * A.reshape(G, Hg), axis=2)
    xdt = x * dt[..., None]

    acs_t = jnp.moveaxis(acs, 2, -1)
    seg = acs_t[..., :, None] - acs_t[..., None, :]
    li = jnp.arange(L)
    mask = (li[:, None] > li[None, :]) if strict else (li[:, None] >= li[None, :])
    decay = jnp.exp(jnp.where(mask, seg, -jnp.inf))
    cb = jnp.einsum('bclgn,bcsgn->bcgls', Cm, Bm)
    scores = cb[:, :, :, None] * decay
    y_diag = jnp.einsum('bcghls,bcsghp->bclghp', scores, xdt)

    w = xdt * jnp.exp(acs[:, :, -1:] - acs)[..., None]
    states = jnp.einsum('bclgn,bclghp->bcghpn', Bm, w)
    chunk_decay = jnp.exp(acs[:, :, -1])

    def step(hst, inp):
        st, cd = inp
        return hst * cd[..., None, None] + st, hst

    h0 = jnp.zeros((Bsz, G, Hg, P, N), jnp.float32)
    _, h_in = lax.scan(step, h0, (jnp.moveaxis(states, 1, 0), jnp.moveaxis(chunk_decay, 1, 0)))
    h_in = jnp.moveaxis(h_in, 0, 1)
    y_off = jnp.einsum('bclgn,bcghpn->bclghp', Cm, h_in) * jnp.exp(acs)[..., None]
    return (y_diag + y_off).reshape(Bsz, S, H, P)


def mamba2_bidir(h, w_in, conv_w, conv_b, dt_bias, A_log, D_skip, norm_w, w_out):
    Bsz, S, _ = h.shape
    zxbcdt = h @ w_in
    z, xBC, dt = jnp.split(zxbcdt, [SSM_DINNER, SSM_DINNER + SSM_CONV_DIM], axis=-1)
    pad = SSM_CONV // 2
    xBC = lax.conv_general_dilated(xBC, conv_w, window_strides=(1,), padding=[(pad, pad)],
                                   dimension_numbers=('NWC', 'WIO', 'NWC'),
                                   feature_group_count=SSM_CONV_DIM) + conv_b
    xBC = jax.nn.silu(xBC.astype(jnp.float32))
    xs, Bm, Cm = jnp.split(xBC, [SSM_DINNER, SSM_DINNER + SSM_GROUPS * SSM_DSTATE], axis=-1)
    xs = xs.reshape(Bsz, S, SSM_HEADS, SSM_HEADDIM)
    Bm = Bm.reshape(Bsz, S, SSM_GROUPS, SSM_DSTATE)
    Cm = Cm.reshape(Bsz, S, SSM_GROUPS, SSM_DSTATE)
    dt = jax.nn.softplus(dt.astype(jnp.float32).reshape(Bsz, S, 2, SSM_HEADS)
                         + dt_bias.astype(jnp.float32))
    A = -jnp.exp(A_log.astype(jnp.float32))
    y_f = ssd_chunked(xs, dt[:, :, 0], A[0], Bm, Cm, strict=False)
    fl = lambda t: jnp.flip(t, axis=1)
    y_b = fl(ssd_chunked(fl(xs), fl(dt[:, :, 1]), A[1], fl(Bm), fl(Cm), strict=True))
    y = y_f + y_b + xs * D_skip.astype(jnp.float32)[:, None]
    y = y.reshape(Bsz, S, SSM_DINNER) * jax.nn.silu(z.astype(jnp.float32))
    y = rms_normalize_f32(y.reshape(Bsz, S, SSM_GROUPS, SSM_DINNER // SSM_GROUPS))
    y = y.reshape(Bsz, S, SSM_DINNER) * norm_w.astype(jnp.float32)
    return y.astype(h.dtype) @ w_out


def swiglu(h, w_gu, w_down):
    g, u = jnp.split(h @ w_gu, 2, axis=-1)
    return (jax.nn.silu(g) * u) @ w_down


def moe_swiglu(h, router_w, router_b, w_gu, w_down):
    Bsz, S, D = h.shape
    t = h.reshape(Bsz * S, D)
    logits = (t @ router_w).astype(jnp.float32) + router_b.astype(jnp.float32)
    top_v, top_i = lax.top_k(logits, TOP_K)
    gates = jax.nn.softmax(top_v, axis=-1)
    combine = jnp.sum(jax.nn.one_hot(top_i, N_EXPERTS, dtype=jnp.float32) * gates[..., None], axis=1)
    out = jnp.zeros_like(t)
    for e in range(N_EXPERTS):
        out = out + swiglu(t, w_gu[e], w_down[e]) * combine[:, e:e + 1].astype(t.dtype)
    return out.reshape(Bsz, S, D)


def setup_inputs(seed: int = 0) -> dict:
    key = jax.random.key(seed)
    ks = jax.random.split(key, 24)
    f32 = jnp.float32
    nrm = lambda k, shape, fan: jax.random.normal(k, shape, f32) * (fan ** -0.5)
    gain = lambda k, n: 1.0 + 0.02 * jax.random.normal(k, (n,), f32)
    u = jax.random.uniform(ks[10], (2, SSM_HEADS), f32)
    dt = jnp.exp(u * (math.log(0.1) - math.log(0.001)) + math.log(0.001))
    dt = jnp.maximum(dt, 1e-4)
    return {
        "x_prompt": jax.random.normal(ks[0], (BATCH, SEQ, D_MODEL), f32),
        "x_sample": jax.random.normal(ks[1], (DEC_BATCH, DEC_SEQ, D_MODEL), f32),
        "ln_mix0": gain(ks[2], D_MODEL),
        "ret_w_in": nrm(ks[3], (D_MODEL, RET_IN_DIM), D_MODEL),
        "ret_w_out": nrm(ks[4], (2 * D_MODEL, D_MODEL), 2 * D_MODEL),
        "ln_ffn0": gain(ks[5], D_MODEL),
        "ffn_w_gu": nrm(ks[6], (D_MODEL, 2 * FFN_DENSE), D_MODEL),
        "ffn_w_down": nrm(ks[7], (FFN_DENSE, D_MODEL), FFN_DENSE),
        "ln_mix1": gain(ks[8], D_MODEL),
        "ssm_w_in": nrm(ks[9], (D_MODEL, SSM_IN_DIM), D_MODEL),
        "ssm_conv_w": nrm(ks[11], (SSM_CONV, 1, SSM_CONV_DIM), SSM_CONV),
        "ssm_conv_b": 0.02 * jax.random.normal(ks[12], (SSM_CONV_DIM,), f32),
        "ssm_dt_bias": dt + jnp.log(-jnp.expm1(-dt)),
        "ssm_A_log": jnp.log(jax.random.uniform(ks[13], (2, SSM_HEADS), f32, 1.0, 16.0)),
        "ssm_D": gain(ks[14], SSM_HEADS),
        "ssm_norm_w": gain(ks[15], SSM_DINNER),
        "ssm_w_out": nrm(ks[16], (SSM_DINNER, D_MODEL), SSM_DINNER),
        "ln_ffn1": gain(ks[17], D_MODEL),
        "moe_router_w": nrm(ks[18], (D_MODEL, N_EXPERTS), D_MODEL),
        "moe_router_b": 0.01 * jax.random.normal(ks[19], (N_EXPERTS,), f32),
        "moe_w_gu": nrm(ks[20], (N_EXPERTS, D_MODEL, 2 * FFN_EXPERT), D_MODEL),
        "moe_w_down": nrm(ks[21], (N_EXPERTS, FFN_EXPERT, D_MODEL), FFN_EXPERT),
        "final_norm": gain(ks[22], D_MODEL),
    }


def reference(x_prompt, x_sample, ln_mix0, ret_w_in, ret_w_out, ln_ffn0, ffn_w_gu, ffn_w_down,
              ln_mix1, ssm_w_in, ssm_conv_w, ssm_conv_b, ssm_dt_bias, ssm_A_log, ssm_D,
              ssm_norm_w, ssm_w_out, ln_ffn1, moe_router_w, moe_router_b, moe_w_gu, moe_w_down,
              final_norm):
    mixers = [
        lambda t: retention(rms_norm(t, ln_mix0), ret_w_in, ret_w_out),
        lambda t: mamba2_bidir(rms_norm(t, ln_mix1), ssm_w_in, ssm_conv_w, ssm_conv_b, ssm_dt_bias,
                               ssm_A_log, ssm_D, ssm_norm_w, ssm_w_out),
    ]
    channel = [
        lambda t: swiglu(rms_norm(t, ln_ffn0), ffn_w_gu, ffn_w_down),
        lambda t: moe_swiglu(rms_norm(t, ln_ffn1), moe_router_w, moe_router_b, moe_w_gu, moe_w_down),
    ]

    def trunk(x):
        h = x
        for i in range(DEPTH):
            h = h + mixers[i % N_MIXERS](h)
            h = h + channel[i % 2](h)
        return rms_norm(h, final_norm)

    y_prompt = trunk(x_prompt)
    y_sample = trunk(x_sample)
    return (y_prompt, y_sample)
```

```python
import functools

import numpy as np
import jax
import jax.numpy as jnp
from jax import lax
from jax.experimental import pallas as pl
from jax.experimental.pallas import tpu as pltpu

F32 = jnp.float32
BF16 = jnp.bfloat16

D_MODEL = 1024
EPS = 1e-6
RET_HEADS = 4
RET_DK = 256
RET_DV = 512
ROPE_BASE = 10000.0
SSM_DINNER = 2048
SSM_HEADDIM = 64
SSM_HEADS = 32
SSM_GROUPS = 4
SSM_DSTATE = 128
SSM_CONV = 5
SSM_CONV_DIM = 3072
SSM_GROUP_W = SSM_DINNER // SSM_GROUPS
FFN_DENSE = 2816
FFN_DENSE_CHUNK = 1408
N_EXPERTS = 8
FFN_EXPERT = 3584

CHUNK = 128
LANES = 128
TOK_TILE = 512
SCAN_BLOCK = 512
MOE_TOK = 512
MOE_SLOT = 1024
MOE_FC = 512
HALO = 16


def _cparams(n_axes, vmem_mb):
    return pltpu.CompilerParams(dimension_semantics=("arbitrary",) * n_axes,
                                vmem_limit_bytes=vmem_mb << 20)


def _resident(shape):
    nd = len(shape)
    return pl.BlockSpec(shape, lambda *_: (0,) * nd, pipeline_mode=pl.Buffered(1))


def _rms(x):
    return x * lax.rsqrt(jnp.mean(x * x, axis=-1, keepdims=True) + EPS)


def _silu(x):
    return x * jax.nn.sigmoid(x)


def _dot(a, b):
    return jnp.dot(a, b, preferred_element_type=F32)


def _dot_nt(a, b):
    return lax.dot_general(a, b, (((1,), (1,)), ((), ())), preferred_element_type=F32)


def _dot_tn(a, b):
    return lax.dot_general(a, b, (((0,), (0,)), ((), ())), preferred_element_type=F32)


def _split2(x):
    hi = x.astype(BF16)
    lo = (x - hi.astype(F32)).astype(BF16)
    return hi, lo


def _split3(x):
    hi = x.astype(BF16)
    r = x - hi.astype(F32)
    mid = r.astype(BF16)
    lo = (r - mid.astype(F32)).astype(BF16)
    return hi, mid, lo


def _ret_in_kernel(x_ref, ln_ref, w_ref, cos_ref, sin_ref, q_ref, k_ref, v_ref, g_ref):
    xn = (_rms(x_ref[...]) * ln_ref[...]).astype(BF16)
    cos = cos_ref[...]
    sin = sin_ref[...]
    half = RET_DK // 2

    def rotary(dst, col0, scale):
        for h in range(RET_HEADS):
            c = col0 + h * RET_DK
            p = _dot(xn, w_ref[:, c:c + RET_DK])
            p1, p2 = p[:, :half], p[:, half:]
            dst[:, h * RET_DK:h * RET_DK + half] = ((p1 * cos - p2 * sin) * scale).astype(BF16)
            dst[:, h * RET_DK + half:(h + 1) * RET_DK] = ((p1 * sin + p2 * cos) * scale).astype(BF16)

    rotary(q_ref, 0, 1.0)
    rotary(k_ref, D_MODEL, RET_DK ** -0.5)
    for j in range(2 * D_MODEL // 512):
        v_ref[:, j * 512:(j + 1) * 512] = _dot(
            xn, w_ref[:, 2 * D_MODEL + j * 512:2 * D_MODEL + (j + 1) * 512]).astype(BF16)
        g_ref[:, j * 512:(j + 1) * 512] = _dot(
            xn, w_ref[:, 4 * D_MODEL + j * 512:4 * D_MODEL + (j + 1) * 512]).astype(BF16)


def _ret_in(x, ln, w_in, cos, sin):
    T = x.shape[0]
    tm = TOK_TILE
    tok = lambda n: pl.BlockSpec((tm, n), lambda i: (i, 0))
    return pl.pallas_call(
        _ret_in_kernel,
        grid=(T // tm,),
        in_specs=[tok(D_MODEL), _resident((1, D_MODEL)), _resident(w_in.shape), tok(LANES), tok(LANES)],
        out_specs=[tok(D_MODEL), tok(D_MODEL), tok(2 * D_MODEL), tok(2 * D_MODEL)],
        out_shape=[jax.ShapeDtypeStruct((T, D_MODEL), BF16), jax.ShapeDtypeStruct((T, D_MODEL), BF16),
                   jax.ShapeDtypeStruct((T, 2 * D_MODEL), BF16), jax.ShapeDtypeStruct((T, 2 * D_MODEL), BF16)],
        compiler_params=_cparams(1, 48),
        name="ret_in",
    )(x, ln, w_in, cos, sin)


def _ret_scan_kernel(reset_ref, *refs, reverse, chunk_decay):
    if reverse:
        q_ref, k_ref, v_ref, wq_ref, wk_ref, out_ref, state_ref = refs
    else:
        q_ref, k_ref, v_ref, ob_ref, g_ref, wq_ref, wk_ref, din_ref, out_ref, state_ref = refs
    nch = SCAN_BLOCK // CHUNK

    @pl.when(reset_ref[pl.program_id(0)] == 1)
    def _():
        state_ref[...] = jnp.zeros_like(state_ref)

    def chunk_body(ci, carry):
        cj = nch - 1 - ci if reverse else ci
        rows = pl.ds(pl.multiple_of(cj * CHUNK, CHUNK), CHUNK)
        for h in range(RET_HEADS):
            kcols = slice(h * RET_DK, (h + 1) * RET_DK)
            vcols = slice(h * RET_DV, (h + 1) * RET_DV)
            qh = q_ref[rows, kcols]
            kh = k_ref[rows, kcols]
            vh = v_ref[rows, vcols]
            state = state_ref[h]
            o = _dot(qh, state.astype(BF16)) * wq_ref[h]
            if reverse:
                out_ref[rows, vcols] = o.astype(BF16)
            else:
                s = _dot_nt(qh, kh) * din_ref[h]
                o = o + _dot(s.astype(BF16), vh) + ob_ref[rows, vcols].astype(F32)
                gate = g_ref[rows, vcols].astype(F32)
                out_ref[rows, vcols] = (_silu(gate) * _rms(o)).astype(BF16)
            ks = (kh.astype(F32) * wk_ref[h]).astype(BF16)
            state_ref[h] = state * chunk_decay[h] + _dot_tn(ks, vh)
        return carry

    lax.fori_loop(0, nch, chunk_body, 0)


def _ret_tables():
    lg = np.log1p(-np.power(2.0, -5.0 - np.arange(RET_HEADS, dtype=np.float32))).astype(np.float32)
    a = np.arange(CHUNK, dtype=np.float32)
    col = lambda e, w: np.broadcast_to(np.exp(lg[:, None, None] * e[None, :, None]),
                                       (RET_HEADS, CHUNK, w)).astype(np.float32)
    dist = np.abs(a[:, None] - a[None, :])
    return dict(
        din=np.exp(lg[:, None, None] * dist[None]).astype(np.float32),
        wq_f=col(a + 1.0, RET_DV), wk_f=col(CHUNK - 1.0 - a, RET_DK),
        wq_b=col(CHUNK - a, RET_DV), wk_b=col(a, RET_DK),
        chunk_decay=tuple(float(v) for v in np.exp(lg * CHUNK)),
    )


def _ret_scan(q, k, v, g, reset_f, reset_b):
    T = q.shape[0]
    nb = T // SCAN_BLOCK
    tb = _ret_tables()
    state = pltpu.VMEM((RET_HEADS, RET_DK, RET_DV), F32)
    out_shape = jax.ShapeDtypeStruct((T, 2 * D_MODEL), BF16)

    def specs(imap):
        blk = lambda n: pl.BlockSpec((SCAN_BLOCK, n), imap)
        return blk(D_MODEL), blk(D_MODEL), blk(2 * D_MODEL)

    rev = lambda i, r: (nb - 1 - i, 0)
    bq, bk, bv = specs(rev)
    o_b = pl.pallas_call(
        functools.partial(_ret_scan_kernel, reverse=True, chunk_decay=tb["chunk_decay"]),
        grid_spec=pltpu.PrefetchScalarGridSpec(
            num_scalar_prefetch=1, grid=(nb,),
            in_specs=[bq, bk, bv, _resident(tb["wq_b"].shape), _resident(tb["wk_b"].shape)],
            out_specs=bv, scratch_shapes=[state]),
        out_shape=out_shape, compiler_params=_cparams(1, 40), name="ret_scan_bwd",
    )(reset_b, q, k, v, tb["wq_b"], tb["wk_b"])

    fwd = lambda i, r: (i, 0)
    bq, bk, bv = specs(fwd)
    return pl.pallas_call(
        functools.partial(_ret_scan_kernel, reverse=False, chunk_decay=tb["chunk_decay"]),
        grid_spec=pltpu.PrefetchScalarGridSpec(
            num_scalar_prefetch=1, grid=(nb,),
            in_specs=[bq, bk, bv, bv, bv, _resident(tb["wq_f"].shape), _resident(tb["wk_f"].shape),
                      _resident(tb["din"].shape)],
            out_specs=bv, scratch_shapes=[state]),
        out_shape=out_shape, compiler_params=_cparams(1, 40), name="ret_scan_fwd",
    )(reset_f, q, k, v, o_b, g, tb["wq_f"], tb["wk_f"], tb["din"])


def _post0_kernel(y_ref, h_ref, wo_ref, ln_ref, wgu_ref, wd_ref, out_ref):
    h1 = h_ref[...] + _dot(y_ref[...], wo_ref[...])
    xn = (_rms(h1) * ln_ref[...]).astype(BF16)
    acc = h1
    fc = FFN_DENSE_CHUNK
    for c in range(FFN_DENSE // fc):
        gate = _dot(xn, wgu_ref[:, c * fc:(c + 1) * fc])
        up = _dot(xn, wgu_ref[:, FFN_DENSE + c * fc:FFN_DENSE + (c + 1) * fc])
        act = (_silu(gate) * up).astype(BF16)
        acc = acc + _dot(act, wd_ref[c * fc:(c + 1) * fc, :])
    out_ref[...] = acc


def _post0(y, h, w_out, ln, w_gu, w_down):
    T = h.shape[0]
    tm = TOK_TILE
    tok = lambda n: pl.BlockSpec((tm, n), lambda i: (i, 0))
    return pl.pallas_call(
        _post0_kernel,
        grid=(T // tm,),
        in_specs=[tok(2 * D_MODEL), tok(D_MODEL), _resident(w_out.shape), _resident((1, D_MODEL)),
                  _resident(w_gu.shape), _resident(w_down.shape)],
        out_specs=tok(D_MODEL),
        out_shape=jax.ShapeDtypeStruct((T, D_MODEL), F32),
        compiler_params=_cparams(1, 52),
        name="ret_out_ffn",
    )(y, h, w_out, ln, w_gu, w_down)


def _ssm_in_kernel(x_ref, ln_ref, wz_ref, wx_ref, wdh_ref, wdl_ref, z_ref, xbc_ref, dt_ref):
    xh, xl = _split2(_rms(x_ref[...]) * ln_ref[...])
    for j in range(SSM_DINNER // 512):
        z_ref[:, j * 512:(j + 1) * 512] = _dot(xh, wz_ref[:, j * 512:(j + 1) * 512]).astype(BF16)
    for j in range(SSM_CONV_DIM // 512):
        xbc_ref[:, j * 512:(j + 1) * 512] = _dot(xh, wx_ref[:, j * 512:(j + 1) * 512]).astype(BF16)
    dt_ref[...] = _dot(xh, wdh_ref[...]) + _dot(xl, wdh_ref[...]) + _dot(xh, wdl_ref[...])


def _ssm_in(x, ln, wz, wx, wdh, wdl):
    T = x.shape[0]
    tm = TOK_TILE
    tok = lambda n: pl.BlockSpec((tm, n), lambda i: (i, 0))
    return pl.pallas_call(
        _ssm_in_kernel,
        grid=(T // tm,),
        in_specs=[tok(D_MODEL), _resident((1, D_MODEL)), _resident(wz.shape), _resident(wx.shape),
                  _resident(wdh.shape), _resident(wdl.shape)],
        out_specs=[tok(SSM_DINNER), tok(SSM_CONV_DIM), tok(LANES)],
        out_shape=[jax.ShapeDtypeStruct((T, SSM_DINNER), BF16), jax.ShapeDtypeStruct((T, SSM_CONV_DIM), BF16),
                   jax.ShapeDtypeStruct((T, LANES), F32)],
        compiler_params=_cparams(1, 40),
        name="ssm_in",
    )(x, ln, wz, wx, wdh, wdl)


def _conv_kernel(first_ref, last_ref, prev_ref, main_ref, next_ref, w_ref, b_ref, out_ref, ext_ref):
    i = pl.program_id(0)
    keep_prev = jnp.where(first_ref[i] == 1, 0.0, 1.0).astype(F32)
    keep_next = jnp.where(last_ref[i] == 1, 0.0, 1.0).astype(F32)
    rb = SCAN_BLOCK
    pad = SSM_CONV // 2
    for s in range(SSM_CONV_DIM // 512):
        cs = slice(s * 512, (s + 1) * 512)
        ext_ref[0:8, :] = prev_ref[:, cs].astype(F32)[HALO - 8:HALO] * keep_prev
        ext_ref[8:8 + rb, :] = main_ref[:, cs].astype(F32)
        ext_ref[8 + rb:16 + rb, :] = next_ref[:, cs].astype(F32)[0:8] * keep_next
        acc = jnp.broadcast_to(b_ref[:, cs], (rb, 512))
        for j in range(SSM_CONV):
            acc = acc + ext_ref[8 - pad + j:8 - pad + j + rb, :] * w_ref[j:j + 1, cs]
        out_ref[:, cs] = _silu(acc).astype(BF16)


def _conv(xbc, conv_w, conv_b, first, last):
    T = xbc.shape[0]
    rb = SCAN_BLOCK
    nb = T // rb
    per = rb // HALO
    nh = T // HALO
    return pl.pallas_call(
        _conv_kernel,
        grid_spec=pltpu.PrefetchScalarGridSpec(
            num_scalar_prefetch=2, grid=(nb,),
            in_specs=[
                pl.BlockSpec((HALO, SSM_CONV_DIM), lambda i, f, l: (jnp.maximum(i * per - 1, 0), 0)),
                pl.BlockSpec((rb, SSM_CONV_DIM), lambda i, f, l: (i, 0)),
                pl.BlockSpec((HALO, SSM_CONV_DIM), lambda i, f, l: (jnp.minimum((i + 1) * per, nh - 1), 0)),
                _resident(conv_w.shape), _resident(conv_b.shape)],
            out_specs=pl.BlockSpec((rb, SSM_CONV_DIM), lambda i, f, l: (i, 0)),
            scratch_shapes=[pltpu.VMEM((rb + 16, 512), F32)]),
        out_shape=jax.ShapeDtypeStruct((T, SSM_CONV_DIM), BF16),
        compiler_params=_cparams(1, 40),
        name="ssm_conv",
    )(first, last, xbc, xbc, xbc, conv_w, conv_b)


def _ssd_tables():
    r = np.arange(CHUNK)
    tri_l = (r[None, :] <= r[:, None]).astype(np.float32)
    tri_u = (r[None, :] >= r[:, None]).astype(np.float32)
    c = np.arange(SSM_DINNER) // SSM_HEADDIM
    j = np.arange(LANES)
    e_f = (j[:, None] == c[None, :]).astype(np.float32)
    e_b = (j[:, None] == (c[None, :] + SSM_HEADS)).astype(np.float32)
    as_bf16 = lambda x: jnp.asarray(x, dtype=BF16)
    return dict(tri_l3=as_bf16(np.concatenate([tri_l] * 3, axis=1)),
                tri_u3=as_bf16(np.concatenate([tri_u] * 3, axis=1)),
                e2_f=as_bf16(np.concatenate([e_f, e_f], axis=0)),
                e2_b=as_bf16(np.concatenate([e_b, e_b], axis=0)))


def _ssd_chunk_scalars(dt_raw, bias_ref, a_ref, tri_l3_ref, tri_u3_ref):
    x = dt_raw + bias_ref[...]
    dt = jnp.maximum(x, 0.0) + jnp.log1p(jnp.exp(-jnp.abs(x)))
    a = dt * a_ref[...]
    a3 = jnp.concatenate(_split3(a), axis=0)
    cum_f = _dot(tri_l3_ref[...], a3)
    cum_b = _dot(tri_u3_ref[...], a3)
    lane = lax.broadcasted_iota(jnp.int32, (CHUNK, LANES), 1)
    cum = jnp.where(lane < SSM_HEADS, cum_f, cum_b)
    tot = cum_f[CHUNK - 1:CHUNK, :]
    return dt, cum, tot


def _expand(e2_ref, *rows):
    x = jnp.concatenate(rows, axis=0)
    hi, lo = _split2(x)
    return _dot(jnp.concatenate([hi, lo], axis=1), e2_ref[...])


def _ssd_bwd_kernel(reset_ref, xa_ref, dt_ref, bias_ref, a_ref, tri_l3_ref, tri_u3_ref, e2_ref,
                    yb_ref, state_ref):
    nch = SCAN_BLOCK // CHUNK

    @pl.when(reset_ref[pl.program_id(0)] == 1)
    def _():
        state_ref[...] = jnp.zeros_like(state_ref)

    def chunk_body(ci, carry):
        rows = pl.ds(pl.multiple_of((nch - 1 - ci) * CHUNK, CHUNK), CHUNK)
        dt, cum, tot = _ssd_chunk_scalars(dt_ref[rows, :], bias_ref, a_ref, tri_l3_ref, tri_u3_ref)
        ex = _expand(e2_ref, dt * jnp.exp(tot - cum), jnp.exp(cum),
                     jnp.broadcast_to(jnp.exp(tot), (16, LANES)))
        for g in range(SSM_GROUPS):
            gc = slice(g * SSM_GROUP_W, (g + 1) * SSM_GROUP_W)
            bm = xa_ref[rows, SSM_DINNER + g * SSM_DSTATE:SSM_DINNER + (g + 1) * SSM_DSTATE]
            cm = xa_ref[rows, SSM_DINNER + (SSM_GROUPS + g) * SSM_DSTATE:
                        SSM_DINNER + (SSM_GROUPS + g + 1) * SSM_DSTATE]
            state = state_ref[g]
            yb_ref[rows, gc] = (_dot(cm, state.astype(BF16)) * ex[CHUNK:2 * CHUNK, gc]).astype(BF16)
            xw = (xa_ref[rows, gc].astype(F32) * ex[0:CHUNK, gc]).astype(BF16)
            state_ref[g] = state * ex[2 * CHUNK:2 * CHUNK + 1, gc] + _dot_tn(bm, xw)
        return carry

    lax.fori_loop(0, nch, chunk_body, 0)


def _ssd_fwd_kernel(reset_ref, xa_ref, dt_ref, z_ref, yb_ref, bias_ref, a_ref, tri_l3_ref, tri_u3_ref,
                    e2_ref, dskip_ref, nw_ref, out_ref, state_ref, y_ref):
    nch = SCAN_BLOCK // CHUNK

    @pl.when(reset_ref[pl.program_id(0)] == 1)
    def _():
        state_ref[...] = jnp.zeros_like(state_ref)

    def chunk_body(ci, carry):
        rows = pl.ds(pl.multiple_of(ci * CHUNK, CHUNK), CHUNK)
        dt, cum, tot = _ssd_chunk_scalars(dt_ref[rows, :], bias_ref, a_ref, tri_l3_ref, tri_u3_ref)
        ex = _expand(e2_ref, dt * jnp.exp(tot - cum), jnp.exp(cum),
                     jnp.broadcast_to(jnp.exp(tot), (16, LANES)))
        cum_t = cum.T
        dt_t = dt.T
        li = lax.broadcasted_iota(jnp.int32, (CHUNK, CHUNK), 0)
        si = lax.broadcasted_iota(jnp.int32, (CHUNK, CHUNK), 1)
        lower = li >= si
        upper = li < si
        first_half = si < SSM_HEADDIM
        for g in range(SSM_GROUPS):
            gc = slice(g * SSM_GROUP_W, (g + 1) * SSM_GROUP_W)
            bm = xa_ref[rows, SSM_DINNER + g * SSM_DSTATE:SSM_DINNER + (g + 1) * SSM_DSTATE]
            cm = xa_ref[rows, SSM_DINNER + (SSM_GROUPS + g) * SSM_DSTATE:
                        SSM_DINNER + (SSM_GROUPS + g + 1) * SSM_DSTATE]
            cb = _dot_nt(cm, bm)
            heads_per_group = SSM_HEADS // SSM_GROUPS
            for pair in range(heads_per_group // 2):
                mats = []
                for hh in range(2):
                    h = g * heads_per_group + 2 * pair + hh
                    hb = SSM_HEADS + h
                    seg_f = cum[:, h:h + 1] - cum_t[h:h + 1, :]
                    seg_b = cum[:, hb:hb + 1] - cum_t[hb:hb + 1, :]
                    dec = (jnp.exp(jnp.where(lower, seg_f, -jnp.inf)) * dt_t[h:h + 1, :]
                           + jnp.exp(jnp.where(upper, seg_b, -jnp.inf)) * dt_t[hb:hb + 1, :])
                    mats.append((cb * dec).astype(BF16))
                pc = slice((g * heads_per_group + 2 * pair) * SSM_HEADDIM,
                           (g * heads_per_group + 2 * pair + 2) * SSM_HEADDIM)
                yy = _dot(jnp.concatenate(mats, axis=0), xa_ref[rows, pc])
                y_ref[:, pc] = jnp.where(first_half, yy[0:CHUNK], yy[CHUNK:2 * CHUNK])
            state = state_ref[g]
            xs = xa_ref[rows, gc].astype(F32)
            y = (y_ref[:, gc] + _dot(cm, state.astype(BF16)) * ex[CHUNK:2 * CHUNK, gc]
                 + yb_ref[rows, gc].astype(F32) + xs * dskip_ref[:, gc])
            y = y * _silu(z_ref[rows, gc].astype(F32))
            out_ref[rows, gc] = (_rms(y) * nw_ref[:, gc]).astype(BF16)
            xw = (xs * ex[0:CHUNK, gc]).astype(BF16)
            state_ref[g] = state * ex[2 * CHUNK:2 * CHUNK + 1, gc] + _dot_tn(bm, xw)
        return carry

    lax.fori_loop(0, nch, chunk_body, 0)


def _ssd(xa, dt_raw, z, bias_row, a_row, dskip, norm_w, reset_f, reset_b):
    T = xa.shape[0]
    nb = T // SCAN_BLOCK
    tb = _ssd_tables()
    state = pltpu.VMEM((SSM_GROUPS, SSM_DSTATE, SSM_GROUP_W), F32)
    consts = [bias_row, a_row, tb["tri_l3"], tb["tri_u3"]]
    const_specs = [_resident(c.shape) for c in consts]
    out_shape = jax.ShapeDtypeStruct((T, SSM_DINNER), BF16)

    rev = lambda i, r: (nb - 1 - i, 0)
    blk = lambda n, imap: pl.BlockSpec((SCAN_BLOCK, n), imap)
    y_b = pl.pallas_call(
        _ssd_bwd_kernel,
        grid_spec=pltpu.PrefetchScalarGridSpec(
            num_scalar_prefetch=1, grid=(nb,),
            in_specs=[blk(SSM_CONV_DIM, rev), blk(LANES, rev)] + const_specs + [_resident(tb["e2_b"].shape)],
            out_specs=blk(SSM_DINNER, rev), scratch_shapes=[state]),
        out_shape=out_shape, compiler_params=_cparams(1, 40), name="ssd_bwd",
    )(reset_b, xa, dt_raw, *consts, tb["e2_b"])

    fwd = lambda i, r: (i, 0)
    return pl.pallas_call(
        _ssd_fwd_kernel,
        grid_spec=pltpu.PrefetchScalarGridSpec(
            num_scalar_prefetch=1, grid=(nb,),
            in_specs=[blk(SSM_CONV_DIM, fwd), blk(LANES, fwd), blk(SSM_DINNER, fwd), blk(SSM_DINNER, fwd)]
            + const_specs + [_resident(tb["e2_f"].shape), _resident(dskip.shape), _resident(norm_w.shape)],
            out_specs=blk(SSM_DINNER, fwd),
            scratch_shapes=[state, pltpu.VMEM((CHUNK, SSM_DINNER), F32)]),
        out_shape=out_shape, compiler_params=_cparams(1, 40), name="ssd_fwd",
    )(reset_f, xa, dt_raw, z, y_b, *consts, tb["e2_f"], dskip, norm_w)


def _router_kernel(y_ref, h_ref, wo_ref, ln_ref, wh_ref, wl_ref, b_ref, tri_ref,
                   h3_ref, xn_ref, info_ref, cstart_ref, cend_ref, base_ref):
    @pl.when(pl.program_id(0) == 0)
    def _():
        base_ref[...] = jnp.zeros_like(base_ref)

    h3 = h_ref[...] + _dot(y_ref[...], wo_ref[...])
    h3_ref[...] = h3
    xh, xl = _split2(_rms(h3) * ln_ref[...])
    xn_ref[...] = xh
    logits = _dot(xh, wh_ref[...]) + _dot(xl, wh_ref[...]) + _dot(xh, wl_ref[...]) + b_ref[...]
    col = lax.broadcasted_iota(jnp.int32, logits.shape, 1)
    m1 = jnp.max(logits, axis=-1, keepdims=True)
    i1 = jnp.min(jnp.where(logits == m1, col, LANES), axis=-1, keepdims=True)
    rest = jnp.where(col == i1, -jnp.inf, logits)
    m2 = jnp.max(rest, axis=-1, keepdims=True)
    i2 = jnp.min(jnp.where(rest == m2, col, LANES), axis=-1, keepdims=True)
    e = jnp.exp(m2 - m1)
    g1 = 1.0 / (1.0 + e)
    g2 = e / (1.0 + e)
    oh1 = col == i1
    oh2 = col == i2
    oh = jnp.where(oh1 | oh2, 1.0, 0.0)
    base = base_ref[...]
    prefix = _dot(tri_ref[...], oh.astype(BF16)) + base
    r1 = jnp.sum(jnp.where(oh1, prefix, 0.0), axis=-1, keepdims=True)
    r2 = jnp.sum(jnp.where(oh2, prefix, 0.0), axis=-1, keepdims=True)
    cstart_ref[0] = base
    base = base + jnp.sum(oh, axis=0, keepdims=True)
    cend_ref[0] = base
    base_ref[...] = base
    fields = [i1.astype(F32), i2.astype(F32), g1, g2, r1, r2]
    info = jnp.zeros(logits.shape, F32)
    for c, f in enumerate(fields):
        info = jnp.where(col == c, f, info)
    info_ref[...] = info


def _router(y, h, w_out, ln, wh, wl, b):
    T = h.shape[0]
    tb = MOE_TOK
    nb = T // tb
    r = np.arange(tb)
    tri = jnp.asarray((r[None, :] < r[:, None]).astype(np.float32), dtype=BF16)
    tok = lambda n: pl.BlockSpec((tb, n), lambda i: (i, 0))
    cnt = pl.BlockSpec((1, 1, LANES), lambda i: (i, 0, 0))
    return pl.pallas_call(
        _router_kernel,
        grid=(nb,),
        in_specs=[tok(SSM_DINNER), tok(D_MODEL), _resident(w_out.shape), _resident((1, D_MODEL)),
                  _resident(wh.shape), _resident(wl.shape), _resident(b.shape), _resident(tri.shape)],
        out_specs=[tok(D_MODEL), tok(D_MODEL), tok(LANES), cnt, cnt],
        out_shape=[jax.ShapeDtypeStruct((T, D_MODEL), F32), jax.ShapeDtypeStruct((T, D_MODEL), BF16),
                   jax.ShapeDtypeStruct((T, LANES), F32), jax.ShapeDtypeStruct((nb, 1, LANES), F32),
                   jax.ShapeDtypeStruct((nb, 1, LANES), F32)],
        scratch_shapes=[pltpu.VMEM((1, LANES), F32)],
        compiler_params=_cparams(1, 32),
        name="ssm_out_router",
    )(y, h, w_out, ln, wh, wl, b, tri)


def _dispatch_kernel(ptile_ref, ptb_ref, pfirst_ref, plast_ref, pvalid_ref,
                     xn_ref, pos0_ref, pos1_ref, out_ref, acc_ref):
    p = pl.program_id(0)

    @pl.when(pvalid_ref[p] == 1)
    def _():
        @pl.when(pfirst_ref[p] == 1)
        def _():
            acc_ref[...] = jnp.zeros_like(acc_ref)

        slot = ptile_ref[p] * MOE_SLOT + lax.broadcasted_iota(jnp.int32, (MOE_SLOT, MOE_TOK), 0)
        sel = (pos0_ref[0] == slot) | (pos1_ref[0] == slot)
        acc_ref[...] += _dot(jnp.where(sel, 1.0, 0.0).astype(BF16), xn_ref[...])

        @pl.when(plast_ref[p] == 1)
        def _():
            out_ref[...] = acc_ref[...].astype(BF16)


def _dispatch(xn, pos0_rows, pos1_rows, pairs, n_tiles):
    maxp = pairs[0].shape[0]
    row = pl.BlockSpec((1, 1, MOE_TOK), lambda p, pt, pb, *_: (pb[p], 0, 0))
    return pl.pallas_call(
        _dispatch_kernel,
        grid_spec=pltpu.PrefetchScalarGridSpec(
            num_scalar_prefetch=5, grid=(maxp,),
            in_specs=[pl.BlockSpec((MOE_TOK, D_MODEL), lambda p, pt, pb, *_: (pb[p], 0)), row, row],
            out_specs=pl.BlockSpec((MOE_SLOT, D_MODEL), lambda p, pt, pb, *_: (pt[p], 0)),
            scratch_shapes=[pltpu.VMEM((MOE_SLOT, D_MODEL), F32)]),
        out_shape=jax.ShapeDtypeStruct((n_tiles * MOE_SLOT, D_MODEL), BF16),
        compiler_params=_cparams(1, 40),
        name="moe_dispatch",
    )(*pairs, xn, pos0_rows, pos1_rows)


def _expert_kernel(texp_ref, tvalid_ref, x_ref, wg_ref, wu_ref, wd_ref, out_ref, acc_ref):
    i = pl.program_id(0)
    f = pl.program_id(1)
    nf = pl.num_programs(1)

    @pl.when(tvalid_ref[i] == 1)
    def _():
        @pl.when(f == 0)
        def _():
            acc_ref[...] = jnp.zeros_like(acc_ref)

        x = x_ref[...]
        act = (_silu(_dot(x, wg_ref[0])) * _dot(x, wu_ref[0])).astype(BF16)
        acc_ref[...] += _dot(act, wd_ref[0])

        @pl.when(f == nf - 1)
        def _():
            out_ref[...] = acc_ref[...].astype(BF16)

    @pl.when((tvalid_ref[i] == 0) & (f == nf - 1))
    def _():
        out_ref[...] = jnp.zeros_like(out_ref)


def _experts(xs, w_gu, w_down, tile_expert, tile_valid):
    n_tiles = xs.shape[0] // MOE_SLOT
    nf = FFN_EXPERT // MOE_FC
    return pl.pallas_call(
        _expert_kernel,
        grid_spec=pltpu.PrefetchScalarGridSpec(
            num_scalar_prefetch=2, grid=(n_tiles, nf),
            in_specs=[pl.BlockSpec((MOE_SLOT, D_MODEL), lambda i, f, te, tv: (i, 0)),
                      pl.BlockSpec((1, D_MODEL, MOE_FC), lambda i, f, te, tv: (te[i], 0, f)),
                      pl.BlockSpec((1, D_MODEL, MOE_FC), lambda i, f, te, tv: (te[i], 0, nf + f)),
                      pl.BlockSpec((1, MOE_FC, D_MODEL), lambda i, f, te, tv: (te[i], f, 0))],
            out_specs=pl.BlockSpec((MOE_SLOT, D_MODEL), lambda i, f, te, tv: (i, 0)),
            scratch_shapes=[pltpu.VMEM((MOE_SLOT, D_MODEL), F32)]),
        out_shape=jax.ShapeDtypeStruct(xs.shape, BF16),
        compiler_params=_cparams(2, 40),
        name="moe_experts",
    )(tile_expert, tile_valid, xs, w_gu, w_gu, w_down)


def _combine_kernel(ptile_ref, ptb_ref, pfirst_ref, plast_ref, pvalid_ref,
                    ys_ref, tok_ref, h_ref, fn_ref, out_ref, acc_ref):
    p = pl.program_id(0)

    @pl.when(pvalid_ref[p] == 1)
    def _():
        @pl.when(pfirst_ref[p] == 1)
        def _():
            acc_ref[...] = jnp.zeros_like(acc_ref)

        slot = (ptile_ref[p] * MOE_SLOT
                + lax.broadcasted_iota(jnp.int32, (MOE_TOK, MOE_SLOT), 1)).astype(F32)
        tok = tok_ref[...]
        w = (jnp.where(tok[:, 0:1] == slot, tok[:, 2:3], 0.0)
             + jnp.where(tok[:, 1:2] == slot, tok[:, 3:4], 0.0))
        acc_ref[...] += _dot(w.astype(BF16), ys_ref[...])

        @pl.when(plast_ref[p] == 1)
        def _():
            out_ref[...] = _rms(h_ref[...] + acc_ref[...]) * fn_ref[...]


def _combine(ys, tok_info, h3, final_norm, pairs):
    T = h3.shape[0]
    maxp = pairs[0].shape[0]
    tokblk = lambda n: pl.BlockSpec((MOE_TOK, n), lambda p, pt, pb, *_: (pb[p], 0))
    return pl.pallas_call(
        _combine_kernel,
        grid_spec=pltpu.PrefetchScalarGridSpec(
            num_scalar_prefetch=5, grid=(maxp,),
            in_specs=[pl.BlockSpec((MOE_SLOT, D_MODEL), lambda p, pt, pb, *_: (pt[p], 0)),
                      tokblk(tok_info.shape[1]), tokblk(D_MODEL), _resident((1, D_MODEL))],
            out_specs=tokblk(D_MODEL),
            scratch_shapes=[pltpu.VMEM((MOE_TOK, D_MODEL), F32)]),
        out_shape=jax.ShapeDtypeStruct((T, D_MODEL), F32),
        compiler_params=_cparams(1, 40),
        name="moe_combine",
    )(*pairs, ys, tok_info, h3, final_norm)


def _pair_lists(overlap, major_tile):
    n_tiles, nb = overlap.shape
    maxp = n_tiles + N_EXPERTS * nb
    flat = overlap.reshape(-1) if major_tile else overlap.T.reshape(-1)
    n = jnp.sum(flat.astype(jnp.int32))
    idx = jnp.nonzero(flat, size=maxp, fill_value=0)[0].astype(jnp.int32)
    pos = jnp.arange(maxp, dtype=jnp.int32)
    valid = pos < n
    idx = jnp.where(valid, idx, idx[jnp.maximum(n - 1, 0)])
    if major_tile:
        tile, blk = idx // nb, idx % nb
        key = tile
    else:
        blk, tile = idx // n_tiles, idx % n_tiles
        key = blk
    prev = jnp.concatenate([jnp.full((1,), -1, jnp.int32), key[:-1]])
    nxt = jnp.concatenate([key[1:], jnp.full((1,), -1, jnp.int32)])
    first = (key != prev) & valid
    last = ((key != nxt) | (pos == n - 1)) & valid
    i32 = lambda x: x.astype(jnp.int32)
    return tile, blk, i32(first), i32(last), i32(valid)


def _moe(y_ssm, h2, w_out, ln, wr_hi, wr_lo, rb, w_gu, w_down, final_norm):
    T = h2.shape[0]
    nb = T // MOE_TOK
    n_tiles = 2 * T // MOE_SLOT + N_EXPERTS
    h3, xn, info, cstart, cend = _router(y_ssm, h2, w_out, ln, wr_hi, wr_lo, rb)

    cstart = cstart[:, 0, :N_EXPERTS].astype(jnp.int32)
    cend = cend[:, 0, :N_EXPERTS].astype(jnp.int32)
    counts = cend[-1]
    tiles_e = (counts + MOE_SLOT - 1) // MOE_SLOT
    tiles_cum = jnp.cumsum(tiles_e)
    gstart = (tiles_cum - tiles_e) * MOE_SLOT
    i1 = info[:, 0].astype(jnp.int32)
    i2 = info[:, 1].astype(jnp.int32)
    pos0 = gstart[i1] + info[:, 4].astype(jnp.int32)
    pos1 = gstart[i2] + info[:, 5].astype(jnp.int32)
    tile_ids = jnp.arange(n_tiles, dtype=jnp.int32)
    tile_valid = tile_ids < tiles_cum[-1]
    tile_expert = jnp.minimum(jnp.searchsorted(tiles_cum, tile_ids, side="right"), N_EXPERTS - 1).astype(jnp.int32)
    rank0 = tile_ids * MOE_SLOT - gstart[tile_expert]
    overlap = (tile_valid[:, None]
               & (cstart.T[tile_expert] < rank0[:, None] + MOE_SLOT)
               & (cend.T[tile_expert] > rank0[:, None]))
    empty = ~jnp.any(overlap, axis=1)
    overlap = overlap.at[:, 0].set(overlap[:, 0] | empty)
    pos0_rows = pos0.reshape(nb, 1, MOE_TOK)
    pos1_rows = pos1.reshape(nb, 1, MOE_TOK)
    tok_info = jnp.stack([pos0.astype(F32), pos1.astype(F32), info[:, 2], info[:, 3]], axis=-1)
    tok_info = jnp.pad(tok_info, ((0, 0), (0, 4)))

    xs = _dispatch(xn, pos0_rows, pos1_rows, _pair_lists(overlap, True), n_tiles)
    ys = _experts(xs, w_gu, w_down, tile_expert, tile_valid.astype(jnp.int32))
    return _combine(ys, tok_info, h3, final_norm, _pair_lists(overlap, False))


def _seq_flags(seq_lens, block):
    first, last = [], []
    for n in seq_lens:
        nb = n // block
        first += [1] + [0] * (nb - 1)
        last += [0] * (nb - 1) + [1]
    return np.asarray(first, np.int32), np.asarray(last, np.int32)


def _trunk(x, seq_lens, ln_mix0, ret_w_in, ret_w_out, ln_ffn0, ffn_w_gu, ffn_w_down, ln_mix1, ssm_w_in,
           ssm_conv_w, ssm_conv_b, ssm_dt_bias, ssm_A_log, ssm_D, ssm_norm_w, ssm_w_out, ln_ffn1,
           moe_router_w, moe_router_b, moe_w_gu, moe_w_down, final_norm):
    assert all(n % SCAN_BLOCK == 0 and n % MOE_TOK == 0 and n % TOK_TILE == 0 for n in seq_lens)
    row = lambda v: v.astype(F32).reshape(1, -1)
    first, last = _seq_flags(seq_lens, SCAN_BLOCK)
    reset_f = jnp.asarray(first)
    reset_b = jnp.asarray(last[::-1].copy())

    half = RET_DK // 2
    inv = ROPE_BASE ** (-jnp.arange(half, dtype=F32) / half)
    pos = jnp.concatenate([jnp.arange(n) for n in seq_lens])
    ang = pos.astype(F32)[:, None] * inv[None, :]
    q, k, v, g = _ret_in(x, row(ln_mix0), ret_w_in.astype(BF16), jnp.cos(ang), jnp.sin(ang))
    y = _ret_scan(q, k, v, g, reset_f, reset_b)
    h2 = _post0(y, x, ret_w_out.astype(BF16), row(ln_ffn0), ffn_w_gu.astype(BF16), ffn_w_down.astype(BF16))

    wz = ssm_w_in[:, :SSM_DINNER].astype(BF16)
    wx = ssm_w_in[:, SSM_DINNER:SSM_DINNER + SSM_CONV_DIM].astype(BF16)
    wdt = jnp.pad(ssm_w_in[:, SSM_DINNER + SSM_CONV_DIM:], ((0, 0), (0, LANES - 2 * SSM_HEADS)))
    wdh, wdl = _split2(wdt)
    z, xbc, dt_raw = _ssm_in(h2, row(ln_mix1), wz, wx, wdh, wdl)
    xa = _conv(xbc, ssm_conv_w.reshape(SSM_CONV, SSM_CONV_DIM).astype(F32), row(ssm_conv_b),
               jnp.asarray(first), jnp.asarray(last))
    pad_row = lambda v: jnp.pad(v.astype(F32).reshape(1, -1), ((0, 0), (0, LANES - 2 * SSM_HEADS)))
    bias_row = pad_row(ssm_dt_bias)
    a_row = pad_row(-jnp.exp(ssm_A_log.astype(F32)))
    dskip = jnp.repeat(ssm_D.astype(F32), SSM_HEADDIM).reshape(1, SSM_DINNER)
    y = _ssd(xa, dt_raw, z, bias_row, a_row, dskip, row(ssm_norm_w), reset_f, reset_b)

    wr = jnp.pad(moe_router_w.astype(F32), ((0, 0), (0, LANES - N_EXPERTS)))
    wr_hi, wr_lo = _split2(wr)
    rb = jnp.pad(moe_router_b.astype(F32).reshape(1, -1), ((0, 0), (0, LANES - N_EXPERTS)),
                 constant_values=-1e30)
    return _moe(y, h2, ssm_w_out.astype(BF16), row(ln_ffn1), wr_hi, wr_lo, rb,
                moe_w_gu.astype(BF16), moe_w_down.astype(BF16), row(final_norm))


def kernel(x_prompt, x_sample, ln_mix0, ret_w_in, ret_w_out, ln_ffn0, ffn_w_gu, ffn_w_down, ln_mix1, ssm_w_in,
           ssm_conv_w, ssm_conv_b, ssm_dt_bias, ssm_A_log, ssm_D, ssm_norm_w, ssm_w_out, ln_ffn1, moe_router_w,
           moe_router_b, moe_w_gu, moe_w_down, final_norm):
    seq_lens = [x_prompt.shape[1]] * x_prompt.shape[0] + [x_sample.shape[1]] * x_sample.shape[0]
    x = jnp.concatenate([x_prompt.reshape(-1, D_MODEL), x_sample.reshape(-1, D_MODEL)], axis=0)
    out = _trunk(x, seq_lens, ln_mix0, ret_w_in, ret_w_out, ln_ffn0, ffn_w_gu, ffn_w_down, ln_mix1, ssm_w_in,
                 ssm_conv_w, ssm_conv_b, ssm_dt_bias, ssm_A_log, ssm_D, ssm_norm_w, ssm_w_out, ln_ffn1,
                 moe_router_w, moe_router_b, moe_w_gu, moe_w_down, final_norm)
    n_prompt = x_prompt.shape[0] * x_prompt.shape[1]
    return (out[:n_prompt].reshape(x_prompt.shape), out[n_prompt:].reshape(x_sample.shape))
```

```python
import functools

import numpy as np
import jax
import jax.numpy as jnp
from jax import lax
from jax.experimental import pallas as pl
from jax.experimental.pallas import tpu as pltpu

F32 = jnp.float32
BF16 = jnp.bfloat16

D_MODEL = 1024
EPS = 1e-6
RET_HEADS = 4
RET_DK = 256
RET_DV = 512
ROPE_BASE = 10000.0
SSM_DINNER = 2048
SSM_HEADDIM = 64
SSM_HEADS = 32
SSM_GROUPS = 4
SSM_DSTATE = 128
SSM_CONV = 5
SSM_CONV_DIM = 3072
SSM_GROUP_W = SSM_DINNER // SSM_GROUPS
FFN_DENSE = 2816
FFN_DENSE_CHUNK = 1408
N_EXPERTS = 8
FFN_EXPERT = 3584

CHUNK = 128
LANES = 128
TOK_TILE = 512
SCAN_BLOCK = 512
MOE_TOK = 512
MOE_SLOT = 1024
MOE_DSLOT = 256
MOE_FC = 512
HALO = 16


def _cparams(n_axes, vmem_mb):
    return pltpu.CompilerParams(dimension_semantics=("arbitrary",) * n_axes,
                                vmem_limit_bytes=vmem_mb << 20)


def _resident(shape):
    nd = len(shape)
    return pl.BlockSpec(shape, lambda *_: (0,) * nd, pipeline_mode=pl.Buffered(1))


def _rms(x):
    return x * lax.rsqrt(jnp.mean(x * x, axis=-1, keepdims=True) + EPS)


def _silu(x):
    return x * jax.nn.sigmoid(x)


def _dot(a, b):
    return jnp.dot(a, b, preferred_element_type=F32)


def _dot_nt(a, b):
    return lax.dot_general(a, b, (((1,), (1,)), ((), ())), preferred_element_type=F32)


def _dot_tn(a, b):
    return lax.dot_general(a, b, (((0,), (0,)), ((), ())), preferred_element_type=F32)


def _split2(x):
    hi = x.astype(BF16)
    lo = (x - hi.astype(F32)).astype(BF16)
    return hi, lo


def _split3(x):
    hi = x.astype(BF16)
    r = x - hi.astype(F32)
    mid = r.astype(BF16)
    lo = (r - mid.astype(F32)).astype(BF16)
    return hi, mid, lo


def _two_stream_specs(tm, n_a):
    return (pl.BlockSpec((tm, D_MODEL), lambda i: (jnp.minimum(i, n_a - 1), 0)),
            pl.BlockSpec((tm, D_MODEL), lambda i: (jnp.maximum(i - n_a, 0), 0)))


def _ret_in_kernel(xa_ref, xb_ref, ln_ref, w_ref, cos_ref, sin_ref, q_ref, k_ref, v_ref, g_ref, *, n_a):
    x = jnp.where(pl.program_id(0) < n_a, xa_ref[...], xb_ref[...])
    xn = (_rms(x) * ln_ref[...]).astype(BF16)
    cos = cos_ref[...]
    sin = sin_ref[...]
    half = RET_DK // 2

    def rotary(dst, col0, scale):
        for h in range(RET_HEADS):
            c = col0 + h * RET_DK
            p = _dot(xn, w_ref[:, c:c + RET_DK])
            p1, p2 = p[:, :half], p[:, half:]
            dst[:, h * RET_DK:h * RET_DK + half] = ((p1 * cos - p2 * sin) * scale).astype(BF16)
            dst[:, h * RET_DK + half:(h + 1) * RET_DK] = ((p1 * sin + p2 * cos) * scale).astype(BF16)

    rotary(q_ref, 0, 1.0)
    rotary(k_ref, D_MODEL, RET_DK ** -0.5)
    for j in range(2 * D_MODEL // 512):
        v_ref[:, j * 512:(j + 1) * 512] = _dot(
            xn, w_ref[:, 2 * D_MODEL + j * 512:2 * D_MODEL + (j + 1) * 512]).astype(BF16)
        g_ref[:, j * 512:(j + 1) * 512] = _dot(
            xn, w_ref[:, 4 * D_MODEL + j * 512:4 * D_MODEL + (j + 1) * 512]).astype(BF16)


def _ret_in(xa, xb, ln, w_in, cos, sin):
    T = xa.shape[0] + xb.shape[0]
    tm = TOK_TILE
    n_a = xa.shape[0] // tm
    tok = lambda n: pl.BlockSpec((tm, n), lambda i: (i, 0))
    return pl.pallas_call(
        functools.partial(_ret_in_kernel, n_a=n_a),
        grid=(T // tm,),
        in_specs=[*_two_stream_specs(tm, n_a), _resident((1, D_MODEL)), _resident(w_in.shape),
                  tok(LANES), tok(LANES)],
        out_specs=[tok(D_MODEL), tok(D_MODEL), tok(2 * D_MODEL), tok(2 * D_MODEL)],
        out_shape=[jax.ShapeDtypeStruct((T, D_MODEL), BF16), jax.ShapeDtypeStruct((T, D_MODEL), BF16),
                   jax.ShapeDtypeStruct((T, 2 * D_MODEL), BF16), jax.ShapeDtypeStruct((T, 2 * D_MODEL), BF16)],
        compiler_params=_cparams(1, 48),
        name="ret_in",
    )(xa, xb, ln, w_in, cos, sin)


def _ret_scan_kernel(reset_ref, *refs, reverse, chunk_decay):
    if reverse:
        q_ref, k_ref, v_ref, wq_ref, wk_ref, out_ref, state_ref = refs
    else:
        q_ref, k_ref, v_ref, ob_ref, g_ref, wq_ref, wk_ref, din_ref, out_ref, state_ref = refs
    nch = SCAN_BLOCK // CHUNK

    @pl.when(reset_ref[pl.program_id(0)] == 1)
    def _():
        state_ref[...] = jnp.zeros_like(state_ref)

    def chunk_body(ci, carry):
        cj = nch - 1 - ci if reverse else ci
        rows = pl.ds(pl.multiple_of(cj * CHUNK, CHUNK), CHUNK)
        for h in range(RET_HEADS):
            kcols = slice(h * RET_DK, (h + 1) * RET_DK)
            vcols = slice(h * RET_DV, (h + 1) * RET_DV)
            qh = q_ref[rows, kcols]
            kh = k_ref[rows, kcols]
            vh = v_ref[rows, vcols]
            state = state_ref[h]
            o = _dot(qh, state.astype(BF16)) * wq_ref[h]
            if reverse:
                out_ref[rows, vcols] = o.astype(BF16)
            else:
                s = _dot_nt(qh, kh) * din_ref[h]
                o = o + _dot(s.astype(BF16), vh) + ob_ref[rows, vcols].astype(F32)
                gate = g_ref[rows, vcols].astype(F32)
                out_ref[rows, vcols] = (_silu(gate) * _rms(o)).astype(BF16)
            ks = (kh.astype(F32) * wk_ref[h]).astype(BF16)
            state_ref[h] = state * chunk_decay[h] + _dot_tn(ks, vh)
        return carry

    lax.fori_loop(0, nch, chunk_body, 0)


def _ret_tables():
    lg = np.log1p(-np.power(2.0, -5.0 - np.arange(RET_HEADS, dtype=np.float32))).astype(np.float32)
    a = np.arange(CHUNK, dtype=np.float32)
    col = lambda e, w: np.broadcast_to(np.exp(lg[:, None, None] * e[None, :, None]),
                                       (RET_HEADS, CHUNK, w)).astype(np.float32)
    dist = np.abs(a[:, None] - a[None, :])
    return dict(
        din=np.exp(lg[:, None, None] * dist[None]).astype(np.float32),
        wq_f=col(a + 1.0, RET_DV), wk_f=col(CHUNK - 1.0 - a, RET_DK),
        wq_b=col(CHUNK - a, RET_DV), wk_b=col(a, RET_DK),
        chunk_decay=tuple(float(v) for v in np.exp(lg * CHUNK)),
    )


def _ret_scan(q, k, v, g, reset_f, reset_b):
    T = q.shape[0]
    nb = T // SCAN_BLOCK
    tb = _ret_tables()
    state = pltpu.VMEM((RET_HEADS, RET_DK, RET_DV), F32)
    out_shape = jax.ShapeDtypeStruct((T, 2 * D_MODEL), BF16)

    def specs(imap):
        blk = lambda n: pl.BlockSpec((SCAN_BLOCK, n), imap)
        return blk(D_MODEL), blk(D_MODEL), blk(2 * D_MODEL)

    rev = lambda i, r: (nb - 1 - i, 0)
    bq, bk, bv = specs(rev)
    o_b = pl.pallas_call(
        functools.partial(_ret_scan_kernel, reverse=True, chunk_decay=tb["chunk_decay"]),
        grid_spec=pltpu.PrefetchScalarGridSpec(
            num_scalar_prefetch=1, grid=(nb,),
            in_specs=[bq, bk, bv, _resident(tb["wq_b"].shape), _resident(tb["wk_b"].shape)],
            out_specs=bv, scratch_shapes=[state]),
        out_shape=out_shape, compiler_params=_cparams(1, 40), name="ret_scan_bwd",
    )(reset_b, q, k, v, tb["wq_b"], tb["wk_b"])

    fwd = lambda i, r: (i, 0)
    bq, bk, bv = specs(fwd)
    return pl.pallas_call(
        functools.partial(_ret_scan_kernel, reverse=False, chunk_decay=tb["chunk_decay"]),
        grid_spec=pltpu.PrefetchScalarGridSpec(
            num_scalar_prefetch=1, grid=(nb,),
            in_specs=[bq, bk, bv, bv, bv, _resident(tb["wq_f"].shape), _resident(tb["wk_f"].shape),
                      _resident(tb["din"].shape)],
            out_specs=bv, scratch_shapes=[state]),
        out_shape=out_shape, compiler_params=_cparams(1, 40), name="ret_scan_fwd",
    )(reset_f, q, k, v, o_b, g, tb["wq_f"], tb["wk_f"], tb["din"])


def _post0_kernel(y_ref, xa_ref, xb_ref, wo_ref, ln_ref, wgu_ref, wd_ref, out_ref, *, n_a):
    x = jnp.where(pl.program_id(0) < n_a, xa_ref[...], xb_ref[...])
    h1 = x + _dot(y_ref[...], wo_ref[...])
    xn = (_rms(h1) * ln_ref[...]).astype(BF16)
    acc = h1
    fc = FFN_DENSE_CHUNK
    for c in range(FFN_DENSE // fc):
        gate = _dot(xn, wgu_ref[:, c * fc:(c + 1) * fc])
        up = _dot(xn, wgu_ref[:, FFN_DENSE + c * fc:FFN_DENSE + (c + 1) * fc])
        act = (_silu(gate) * up).astype(BF16)
        acc = acc + _dot(act, wd_ref[c * fc:(c + 1) * fc, :])
    out_ref[...] = acc


def _post0(y, xa, xb, w_out, ln, w_gu, w_down):
    T = y.shape[0]
    tm = TOK_TILE
    n_a = xa.shape[0] // tm
    tok = lambda n: pl.BlockSpec((tm, n), lambda i: (i, 0))
    return pl.pallas_call(
        functools.partial(_post0_kernel, n_a=n_a),
        grid=(T // tm,),
        in_specs=[tok(2 * D_MODEL), *_two_stream_specs(tm, n_a), _resident(w_out.shape),
                  _resident((1, D_MODEL)), _resident(w_gu.shape), _resident(w_down.shape)],
        out_specs=tok(D_MODEL),
        out_shape=jax.ShapeDtypeStruct((T, D_MODEL), F32),
        compiler_params=_cparams(1, 52),
        name="ret_out_ffn",
    )(y, xa, xb, w_out, ln, w_gu, w_down)


def _ssm_in_kernel(x_ref, ln_ref, wz_ref, wx_ref, wdh_ref, wdl_ref, bias_ref, gate_ref, xbc_ref, dt_ref):
    xh, xl = _split2(_rms(x_ref[...]) * ln_ref[...])
    for j in range(SSM_DINNER // 512):
        gate_ref[:, j * 512:(j + 1) * 512] = _silu(_dot(xh, wz_ref[:, j * 512:(j + 1) * 512])).astype(BF16)
    for j in range(SSM_CONV_DIM // 512):
        xbc_ref[:, j * 512:(j + 1) * 512] = _dot(xh, wx_ref[:, j * 512:(j + 1) * 512]).astype(BF16)
    x = _dot(xh, wdh_ref[...]) + _dot(xl, wdh_ref[...]) + _dot(xh, wdl_ref[...]) + bias_ref[...]
    dt_ref[...] = jnp.maximum(x, 0.0) + jnp.log1p(jnp.exp(-jnp.abs(x)))


def _ssm_in(x, ln, wz, wx, wdh, wdl, bias_row):
    T = x.shape[0]
    tm = TOK_TILE
    tok = lambda n: pl.BlockSpec((tm, n), lambda i: (i, 0))
    return pl.pallas_call(
        _ssm_in_kernel,
        grid=(T // tm,),
        in_specs=[tok(D_MODEL), _resident((1, D_MODEL)), _resident(wz.shape), _resident(wx.shape),
                  _resident(wdh.shape), _resident(wdl.shape), _resident(bias_row.shape)],
        out_specs=[tok(SSM_DINNER), tok(SSM_CONV_DIM), tok(LANES)],
        out_shape=[jax.ShapeDtypeStruct((T, SSM_DINNER), BF16), jax.ShapeDtypeStruct((T, SSM_CONV_DIM), BF16),
                   jax.ShapeDtypeStruct((T, LANES), F32)],
        compiler_params=_cparams(1, 40),
        name="ssm_in",
    )(x, ln, wz, wx, wdh, wdl, bias_row)


def _conv_kernel(first_ref, last_ref, prev_ref, main_ref, next_ref, w_ref, b_ref, out_ref, ext_ref):
    i = pl.program_id(0)
    keep_prev = jnp.where(first_ref[i] == 1, 0.0, 1.0).astype(F32)
    keep_next = jnp.where(last_ref[i] == 1, 0.0, 1.0).astype(F32)
    rb = SCAN_BLOCK
    pad = SSM_CONV // 2
    for s in range(SSM_CONV_DIM // 512):
        cs = slice(s * 512, (s + 1) * 512)
        ext_ref[0:8, :] = prev_ref[:, cs].astype(F32)[HALO - 8:HALO] * keep_prev
        ext_ref[8:8 + rb, :] = main_ref[:, cs].astype(F32)
        ext_ref[8 + rb:16 + rb, :] = next_ref[:, cs].astype(F32)[0:8] * keep_next
        acc = jnp.broadcast_to(b_ref[:, cs], (rb, 512))
        for j in range(SSM_CONV):
            acc = acc + ext_ref[8 - pad + j:8 - pad + j + rb, :] * w_ref[j:j + 1, cs]
        out_ref[:, cs] = _silu(acc).astype(BF16)


def _conv(xbc, conv_w, conv_b, first, last):
    T = xbc.shape[0]
    rb = SCAN_BLOCK
    nb = T // rb
    per = rb // HALO
    nh = T // HALO
    return pl.pallas_call(
        _conv_kernel,
        grid_spec=pltpu.PrefetchScalarGridSpec(
            num_scalar_prefetch=2, grid=(nb,),
            in_specs=[
                pl.BlockSpec((HALO, SSM_CONV_DIM), lambda i, f, l: (jnp.maximum(i * per - 1, 0), 0)),
                pl.BlockSpec((rb, SSM_CONV_DIM), lambda i, f, l: (i, 0)),
                pl.BlockSpec((HALO, SSM_CONV_DIM), lambda i, f, l: (jnp.minimum((i + 1) * per, nh - 1), 0)),
                _resident(conv_w.shape), _resident(conv_b.shape)],
            out_specs=pl.BlockSpec((rb, SSM_CONV_DIM), lambda i, f, l: (i, 0)),
            scratch_shapes=[pltpu.VMEM((rb + 16, 512), F32)]),
        out_shape=jax.ShapeDtypeStruct((T, SSM_CONV_DIM), BF16),
        compiler_params=_cparams(1, 40),
        name="ssm_conv",
    )(first, last, xbc, xbc, xbc, conv_w, conv_b)


def _ssd_tables():
    r = np.arange(CHUNK)
    tri_l = (r[None, :] <= r[:, None]).astype(np.float32)
    tri_u = (r[None, :] >= r[:, None]).astype(np.float32)
    c = np.arange(SSM_DINNER) // SSM_HEADDIM
    j = np.arange(LANES)
    e_f = (j[:, None] == c[None, :]).astype(np.float32)
    e_b = (j[:, None] == (c[None, :] + SSM_HEADS)).astype(np.float32)
    as_bf16 = lambda x: jnp.asarray(x, dtype=BF16)
    return dict(tri_l3=as_bf16(np.concatenate([tri_l] * 3, axis=1)),
                tri_u3=as_bf16(np.concatenate([tri_u] * 3, axis=1)),
                e2_f=as_bf16(np.concatenate([e_f, e_f], axis=0)),
                e2_b=as_bf16(np.concatenate([e_b, e_b], axis=0)))


def _ssd_chunk_scalars(dt, a_ref, tri_l3_ref, tri_u3_ref):
    a = dt * a_ref[...]
    a3 = jnp.concatenate(_split3(a), axis=0)
    cum_f = _dot(tri_l3_ref[...], a3)
    cum_b = _dot(tri_u3_ref[...], a3)
    lane = lax.broadcasted_iota(jnp.int32, (CHUNK, LANES), 1)
    cum = jnp.where(lane < SSM_HEADS, cum_f, cum_b)
    tot = cum_f[CHUNK - 1:CHUNK, :]
    return cum, tot


def _expand(e2_ref, *rows):
    x = jnp.concatenate(rows, axis=0)
    hi, lo = _split2(x)
    return _dot(jnp.concatenate([hi, lo], axis=1), e2_ref[...])


def _ssd_bwd_kernel(reset_ref, xa_ref, dt_ref, a_ref, tri_l3_ref, tri_u3_ref, e2_ref,
                    yb_ref, state_ref):
    nch = SCAN_BLOCK // CHUNK

    @pl.when(reset_ref[pl.program_id(0)] == 1)
    def _():
        state_ref[...] = jnp.zeros_like(state_ref)

    def chunk_body(ci, carry):
        rows = pl.ds(pl.multiple_of((nch - 1 - ci) * CHUNK, CHUNK), CHUNK)
        dt = dt_ref[rows, :]
        cum, tot = _ssd_chunk_scalars(dt, a_ref, tri_l3_ref, tri_u3_ref)
        ex = _expand(e2_ref, dt * jnp.exp(tot - cum), jnp.exp(cum),
                     jnp.broadcast_to(jnp.exp(tot), (16, LANES)))
        for g in range(SSM_GROUPS):
            gc = slice(g * SSM_GROUP_W, (g + 1) * SSM_GROUP_W)
            bm = xa_ref[rows, SSM_DINNER + g * SSM_DSTATE:SSM_DINNER + (g + 1) * SSM_DSTATE]
            cm = xa_ref[rows, SSM_DINNER + (SSM_GROUPS + g) * SSM_DSTATE:
                        SSM_DINNER + (SSM_GROUPS + g + 1) * SSM_DSTATE]
            state = state_ref[g]
            yb_ref[rows, gc] = (_dot(cm, state.astype(BF16)) * ex[CHUNK:2 * CHUNK, gc]).astype(BF16)
            xw = (xa_ref[rows, gc].astype(F32) * ex[0:CHUNK, gc]).astype(BF16)
            state_ref[g] = state * ex[2 * CHUNK:2 * CHUNK + 1, gc] + _dot_tn(bm, xw)
        return carry

    lax.fori_loop(0, nch, chunk_body, 0)


def _ssd_fwd_kernel(reset_ref, xa_ref, dt_ref, gate_ref, yb_ref, a_ref, tri_l3_ref, tri_u3_ref,
                    e2_ref, dskip_ref, nw_ref, out_ref, state_ref, y_ref):
    nch = SCAN_BLOCK // CHUNK

    @pl.when(reset_ref[pl.program_id(0)] == 1)
    def _():
        state_ref[...] = jnp.zeros_like(state_ref)

    def chunk_body(ci, carry):
        rows = pl.ds(pl.multiple_of(ci * CHUNK, CHUNK), CHUNK)
        dt = dt_ref[rows, :]
        cum, tot = _ssd_chunk_scalars(dt, a_ref, tri_l3_ref, tri_u3_ref)
        ex = _expand(e2_ref, dt * jnp.exp(tot - cum), jnp.exp(cum),
                     jnp.broadcast_to(jnp.exp(tot), (16, LANES)))
        cum_t = cum.T
        dt_t = dt.T
        li = lax.broadcasted_iota(jnp.int32, (CHUNK, CHUNK), 0)
        si = lax.broadcasted_iota(jnp.int32, (CHUNK, CHUNK), 1)
        lower = li >= si
        upper = li < si
        first_half = si < SSM_HEADDIM
        for g in range(SSM_GROUPS):
            gc = slice(g * SSM_GROUP_W, (g + 1) * SSM_GROUP_W)
            bm = xa_ref[rows, SSM_DINNER + g * SSM_DSTATE:SSM_DINNER + (g + 1) * SSM_DSTATE]
            cm = xa_ref[rows, SSM_DINNER + (SSM_GROUPS + g) * SSM_DSTATE:
                        SSM_DINNER + (SSM_GROUPS + g + 1) * SSM_DSTATE]
            cb = _dot_nt(cm, bm)
            heads_per_group = SSM_HEADS // SSM_GROUPS
            for pair in range(heads_per_group // 2):
                mats = []
                for hh in range(2):
                    h = g * heads_per_group + 2 * pair + hh
                    hb = SSM_HEADS + h
                    seg_f = cum[:, h:h + 1] - cum_t[h:h + 1, :]
                    seg_b = cum[:, hb:hb + 1] - cum_t[hb:hb + 1, :]
                    dec = (jnp.exp(jnp.where(lower, seg_f, -jnp.inf)) * dt_t[h:h + 1, :]
                           + jnp.exp(jnp.where(upper, seg_b, -jnp.inf)) * dt_t[hb:hb + 1, :])
                    mats.append((cb * dec).astype(BF16))
                pc = slice((g * heads_per_group + 2 * pair) * SSM_HEADDIM,
                           (g * heads_per_group + 2 * pair + 2) * SSM_HEADDIM)
                yy = _dot(jnp.concatenate(mats, axis=0), xa_ref[rows, pc])
                y_ref[:, pc] = jnp.where(first_half, yy[0:CHUNK], yy[CHUNK:2 * CHUNK])
            state = state_ref[g]
            xs = xa_ref[rows, gc].astype(F32)
            y = (y_ref[:, gc] + _dot(cm, state.astype(BF16)) * ex[CHUNK:2 * CHUNK, gc]
                 + yb_ref[rows, gc].astype(F32) + xs * dskip_ref[:, gc])
            y = y * gate_ref[rows, gc].astype(F32)
            out_ref[rows, gc] = (_rms(y) * nw_ref[:, gc]).astype(BF16)
            xw = (xs * ex[0:CHUNK, gc]).astype(BF16)
            state_ref[g] = state * ex[2 * CHUNK:2 * CHUNK + 1, gc] + _dot_tn(bm, xw)
        return carry

    lax.fori_loop(0, nch, chunk_body, 0)


def _ssd(xa, dt, gate, a_row, dskip, norm_w, reset_f, reset_b):
    T = xa.shape[0]
    nb = T // SCAN_BLOCK
    tb = _ssd_tables()
    state = pltpu.VMEM((SSM_GROUPS, SSM_DSTATE, SSM_GROUP_W), F32)
    consts = [a_row, tb["tri_l3"], tb["tri_u3"]]
    const_specs = [_resident(c.shape) for c in consts]
    out_shape = jax.ShapeDtypeStruct((T, SSM_DINNER), BF16)

    rev = lambda i, r: (nb - 1 - i, 0)
    blk = lambda n, imap: pl.BlockSpec((SCAN_BLOCK, n), imap)
    y_b = pl.pallas_call(
        _ssd_bwd_kernel,
        grid_spec=pltpu.PrefetchScalarGridSpec(
            num_scalar_prefetch=1, grid=(nb,),
            in_specs=[blk(SSM_CONV_DIM, rev), blk(LANES, rev)] + const_specs + [_resident(tb["e2_b"].shape)],
            out_specs=blk(SSM_DINNER, rev), scratch_shapes=[state]),
        out_shape=out_shape, compiler_params=_cparams(1, 40), name="ssd_bwd",
    )(reset_b, xa, dt, *consts, tb["e2_b"])

    fwd = lambda i, r: (i, 0)
    return pl.pallas_call(
        _ssd_fwd_kernel,
        grid_spec=pltpu.PrefetchScalarGridSpec(
            num_scalar_prefetch=1, grid=(nb,),
            in_specs=[blk(SSM_CONV_DIM, fwd), blk(LANES, fwd), blk(SSM_DINNER, fwd), blk(SSM_DINNER, fwd)]
            + const_specs + [_resident(tb["e2_f"].shape), _resident(dskip.shape), _resident(norm_w.shape)],
            out_specs=blk(SSM_DINNER, fwd),
            scratch_shapes=[state, pltpu.VMEM((CHUNK, SSM_DINNER), F32)]),
        out_shape=out_shape, compiler_params=_cparams(1, 40), name="ssd_fwd",
    )(reset_f, xa, dt, gate, y_b, *consts, tb["e2_f"], dskip, norm_w)


def _router_kernel(y_ref, h_ref, wo_ref, ln_ref, wh_ref, wl_ref, b_ref, tri_ref,
                   h3_ref, xn_ref, info_ref, cstart_ref, cend_ref, base_ref):
    @pl.when(pl.program_id(0) == 0)
    def _():
        base_ref[...] = jnp.zeros_like(base_ref)

    h3 = h_ref[...] + _dot(y_ref[...], wo_ref[...])
    h3_ref[...] = h3
    xh, xl = _split2(_rms(h3) * ln_ref[...])
    xn_ref[...] = xh
    logits = _dot(xh, wh_ref[...]) + _dot(xl, wh_ref[...]) + _dot(xh, wl_ref[...]) + b_ref[...]
    col = lax.broadcasted_iota(jnp.int32, logits.shape, 1)
    m1 = jnp.max(logits, axis=-1, keepdims=True)
    i1 = jnp.min(jnp.where(logits == m1, col, LANES), axis=-1, keepdims=True)
    rest = jnp.where(col == i1, -jnp.inf, logits)
    m2 = jnp.max(rest, axis=-1, keepdims=True)
    i2 = jnp.min(jnp.where(rest == m2, col, LANES), axis=-1, keepdims=True)
    e = jnp.exp(m2 - m1)
    g1 = 1.0 / (1.0 + e)
    g2 = e / (1.0 + e)
    oh1 = col == i1
    oh2 = col == i2
    oh = jnp.where(oh1 | oh2, 1.0, 0.0)
    base = base_ref[...]
    prefix = _dot(tri_ref[...], oh.astype(BF16)) + base
    r1 = jnp.sum(jnp.where(oh1, prefix, 0.0), axis=-1, keepdims=True)
    r2 = jnp.sum(jnp.where(oh2, prefix, 0.0), axis=-1, keepdims=True)
    cstart_ref[0] = base
    base = base + jnp.sum(oh, axis=0, keepdims=True)
    cend_ref[0] = base
    base_ref[...] = base
    fields = [i1.astype(F32), i2.astype(F32), g1, g2, r1, r2]
    info = jnp.zeros(logits.shape, F32)
    for c, f in enumerate(fields):
        info = jnp.where(col == c, f, info)
    info_ref[...] = info


def _router(y, h, w_out, ln, wh, wl, b):
    T = h.shape[0]
    tb = MOE_TOK
    nb = T // tb
    r = np.arange(tb)
    tri = jnp.asarray((r[None, :] < r[:, None]).astype(np.float32), dtype=BF16)
    tok = lambda n: pl.BlockSpec((tb, n), lambda i: (i, 0))
    cnt = pl.BlockSpec((1, 1, LANES), lambda i: (i, 0, 0))
    return pl.pallas_call(
        _router_kernel,
        grid=(nb,),
        in_specs=[tok(SSM_DINNER), tok(D_MODEL), _resident(w_out.shape), _resident((1, D_MODEL)),
                  _resident(wh.shape), _resident(wl.shape), _resident(b.shape), _resident(tri.shape)],
        out_specs=[tok(D_MODEL), tok(D_MODEL), tok(LANES), cnt, cnt],
        out_shape=[jax.ShapeDtypeStruct((T, D_MODEL), F32), jax.ShapeDtypeStruct((T, D_MODEL), BF16),
                   jax.ShapeDtypeStruct((T, LANES), F32), jax.ShapeDtypeStruct((nb, 1, LANES), F32),
                   jax.ShapeDtypeStruct((nb, 1, LANES), F32)],
        scratch_shapes=[pltpu.VMEM((1, LANES), F32)],
        compiler_params=_cparams(1, 32),
        name="ssm_out_router",
    )(y, h, w_out, ln, wh, wl, b, tri)


def _dispatch_kernel(ptile_ref, ptb_ref, pfirst_ref, plast_ref, pvalid_ref,
                     xn_ref, pos0_ref, pos1_ref, out_ref, acc_ref):
    p = pl.program_id(0)

    @pl.when(pvalid_ref[p] == 1)
    def _():
        @pl.when(pfirst_ref[p] == 1)
        def _():
            acc_ref[...] = jnp.zeros_like(acc_ref)

        slot = ptile_ref[p] * MOE_DSLOT + lax.broadcasted_iota(jnp.int32, (MOE_DSLOT, MOE_TOK), 0)
        sel = (pos0_ref[0] == slot) | (pos1_ref[0] == slot)
        acc_ref[...] += _dot(jnp.where(sel, 1.0, 0.0).astype(BF16), xn_ref[...])

        @pl.when(plast_ref[p] == 1)
        def _():
            out_ref[...] = acc_ref[...].astype(BF16)


def _dispatch(xn, pos0_rows, pos1_rows, pairs, n_tiles):
    maxp = pairs[0].shape[0]
    row = pl.BlockSpec((1, 1, MOE_TOK), lambda p, pt, pb, *_: (pb[p], 0, 0))
    return pl.pallas_call(
        _dispatch_kernel,
        grid_spec=pltpu.PrefetchScalarGridSpec(
            num_scalar_prefetch=5, grid=(maxp,),
            in_specs=[pl.BlockSpec((MOE_TOK, D_MODEL), lambda p, pt, pb, *_: (pb[p], 0)), row, row],
            out_specs=pl.BlockSpec((MOE_DSLOT, D_MODEL), lambda p, pt, pb, *_: (pt[p], 0)),
            scratch_shapes=[pltpu.VMEM((MOE_DSLOT, D_MODEL), F32)]),
        out_shape=jax.ShapeDtypeStruct((n_tiles * MOE_DSLOT, D_MODEL), BF16),
        compiler_params=_cparams(1, 40),
        name="moe_dispatch",
    )(*pairs, xn, pos0_rows, pos1_rows)


def _expert_kernel(texp_ref, tvalid_ref, x_ref, wg_ref, wu_ref, wd_ref, out_ref, acc_ref):
    i = pl.program_id(0)
    f = pl.program_id(1)
    nf = pl.num_programs(1)

    @pl.when(tvalid_ref[i] == 1)
    def _():
        @pl.when(f == 0)
        def _():
            acc_ref[...] = jnp.zeros_like(acc_ref)

        x = x_ref[...]
        act = (_silu(_dot(x, wg_ref[0])) * _dot(x, wu_ref[0])).astype(BF16)
        acc_ref[...] += _dot(act, wd_ref[0])

        @pl.when(f == nf - 1)
        def _():
            out_ref[...] = acc_ref[...].astype(BF16)

    @pl.when((tvalid_ref[i] == 0) & (f == nf - 1))
    def _():
        out_ref[...] = jnp.zeros_like(out_ref)


def _experts(xs, w_gu, w_down, tile_expert, tile_valid):
    n_tiles = xs.shape[0] // MOE_SLOT
    nf = FFN_EXPERT // MOE_FC
    return pl.pallas_call(
        _expert_kernel,
        grid_spec=pltpu.PrefetchScalarGridSpec(
            num_scalar_prefetch=2, grid=(n_tiles, nf),
            in_specs=[pl.BlockSpec((MOE_SLOT, D_MODEL), lambda i, f, te, tv: (i, 0)),
                      pl.BlockSpec((1, D_MODEL, MOE_FC), lambda i, f, te, tv: (te[i], 0, f)),
                      pl.BlockSpec((1, D_MODEL, MOE_FC), lambda i, f, te, tv: (te[i], 0, nf + f)),
                      pl.BlockSpec((1, MOE_FC, D_MODEL), lambda i, f, te, tv: (te[i], f, 0))],
            out_specs=pl.BlockSpec((MOE_SLOT, D_MODEL), lambda i, f, te, tv: (i, 0)),
            scratch_shapes=[pltpu.VMEM((MOE_SLOT, D_MODEL), F32)]),
        out_shape=jax.ShapeDtypeStruct(xs.shape, BF16),
        compiler_params=_cparams(2, 40),
        name="moe_experts",
    )(tile_expert, tile_valid, xs, w_gu, w_gu, w_down)


def _combine_kernel(ptile_ref, ptb_ref, pfirst_ref, plast_ref, pvalid_ref,
                    ys_ref, tok_ref, h_ref, fn_ref, outa_ref, outb_ref, acc_ref, *, nb_a):
    p = pl.program_id(0)

    @pl.when(pvalid_ref[p] == 1)
    def _():
        @pl.when(pfirst_ref[p] == 1)
        def _():
            acc_ref[...] = jnp.zeros_like(acc_ref)

        slot = (ptile_ref[p] * MOE_DSLOT
                + lax.broadcasted_iota(jnp.int32, (MOE_TOK, MOE_DSLOT), 1)).astype(F32)
        tok = tok_ref[...]
        w = (jnp.where(tok[:, 0:1] == slot, tok[:, 2:3], 0.0)
             + jnp.where(tok[:, 1:2] == slot, tok[:, 3:4], 0.0))
        acc_ref[...] += _dot(w.astype(BF16), ys_ref[...])

        @pl.when((plast_ref[p] == 1) & (ptb_ref[p] < nb_a))
        def _():
            outa_ref[...] = _rms(h_ref[...] + acc_ref[...]) * fn_ref[...]

        @pl.when((plast_ref[p] == 1) & (ptb_ref[p] >= nb_a))
        def _():
            outb_ref[...] = _rms(h_ref[...] + acc_ref[...]) * fn_ref[...]


def _combine(ys, tok_info, h3, final_norm, pairs, n_a):
    T = h3.shape[0]
    maxp = pairs[0].shape[0]
    nb_a = n_a // MOE_TOK
    tokblk = lambda n: pl.BlockSpec((MOE_TOK, n), lambda p, pt, pb, *_: (pb[p], 0))
    return pl.pallas_call(
        functools.partial(_combine_kernel, nb_a=nb_a),
        grid_spec=pltpu.PrefetchScalarGridSpec(
            num_scalar_prefetch=5, grid=(maxp,),
            in_specs=[pl.BlockSpec((MOE_DSLOT, D_MODEL), lambda p, pt, pb, *_: (pt[p], 0)),
                      tokblk(tok_info.shape[1]), tokblk(D_MODEL), _resident((1, D_MODEL))],
            out_specs=[
                pl.BlockSpec((MOE_TOK, D_MODEL), lambda p, pt, pb, *_: (jnp.minimum(pb[p], nb_a - 1), 0)),
                pl.BlockSpec((MOE_TOK, D_MODEL), lambda p, pt, pb, *_: (jnp.maximum(pb[p] - nb_a, 0), 0))],
            scratch_shapes=[pltpu.VMEM((MOE_TOK, D_MODEL), F32)]),
        out_shape=[jax.ShapeDtypeStruct((n_a, D_MODEL), F32), jax.ShapeDtypeStruct((T - n_a, D_MODEL), F32)],
        compiler_params=_cparams(1, 40),
        name="moe_combine",
    )(*pairs, ys, tok_info, h3, final_norm)


def _pair_lists(overlap, major_tile):
    n_tiles, nb = overlap.shape
    maxp = n_tiles + N_EXPERTS * nb
    flat = overlap.reshape(-1) if major_tile else overlap.T.reshape(-1)
    n = jnp.sum(flat.astype(jnp.int32))
    idx = jnp.nonzero(flat, size=maxp, fill_value=0)[0].astype(jnp.int32)
    pos = jnp.arange(maxp, dtype=jnp.int32)
    valid = pos < n
    idx = jnp.where(valid, idx, idx[jnp.maximum(n - 1, 0)])
    if major_tile:
        tile, blk = idx // nb, idx % nb
        key = tile
    else:
        blk, tile = idx // n_tiles, idx % n_tiles
        key = blk
    prev = jnp.concatenate([jnp.full((1,), -1, jnp.int32), key[:-1]])
    nxt = jnp.concatenate([key[1:], jnp.full((1,), -1, jnp.int32)])
    first = (key != prev) & valid
    last = ((key != nxt) | (pos == n - 1)) & valid
    i32 = lambda x: x.astype(jnp.int32)
    return tile, blk, i32(first), i32(last), i32(valid)


def _moe(y_ssm, h2, w_out, ln, wr_hi, wr_lo, rb, w_gu, w_down, final_norm, n_a):
    T = h2.shape[0]
    nb = T // MOE_TOK
    n_tiles = 2 * T // MOE_SLOT + N_EXPERTS
    sub = MOE_SLOT // MOE_DSLOT
    h3, xn, info, cstart, cend = _router(y_ssm, h2, w_out, ln, wr_hi, wr_lo, rb)

    cstart = cstart[:, 0, :N_EXPERTS].astype(jnp.int32)
    cend = cend[:, 0, :N_EXPERTS].astype(jnp.int32)
    counts = cend[-1]
    tiles_e = (counts + MOE_SLOT - 1) // MOE_SLOT
    tiles_cum = jnp.cumsum(tiles_e)
    gstart = (tiles_cum - tiles_e) * MOE_SLOT
    i1 = info[:, 0].astype(jnp.int32)
    i2 = info[:, 1].astype(jnp.int32)
    pos0 = gstart[i1] + info[:, 4].astype(jnp.int32)
    pos1 = gstart[i2] + info[:, 5].astype(jnp.int32)
    tile_ids = jnp.arange(n_tiles, dtype=jnp.int32)
    tile_valid = tile_ids < tiles_cum[-1]
    tile_expert = jnp.minimum(jnp.searchsorted(tiles_cum, tile_ids, side="right"), N_EXPERTS - 1).astype(jnp.int32)
    dtile_ids = jnp.arange(n_tiles * sub, dtype=jnp.int32)
    dtile_expert = tile_expert[dtile_ids // sub]
    rank0 = dtile_ids * MOE_DSLOT - gstart[dtile_expert]
    overlap = (tile_valid[dtile_ids // sub][:, None]
               & (cstart.T[dtile_expert] < rank0[:, None] + MOE_DSLOT)
               & (cend.T[dtile_expert] > rank0[:, None]))
    empty = ~jnp.any(overlap, axis=1)
    overlap = overlap.at[:, 0].set(overlap[:, 0] | empty)
    pos0_rows = pos0.reshape(nb, 1, MOE_TOK)
    pos1_rows = pos1.reshape(nb, 1, MOE_TOK)
    tok_info = jnp.stack([pos0.astype(F32), pos1.astype(F32), info[:, 2], info[:, 3]], axis=-1)
    tok_info = jnp.pad(tok_info, ((0, 0), (0, 4)))

    xs = _dispatch(xn, pos0_rows, pos1_rows, _pair_lists(overlap, True), n_tiles * sub)
    ys = _experts(xs, w_gu, w_down, tile_expert, tile_valid.astype(jnp.int32))
    return _combine(ys, tok_info, h3, final_norm, _pair_lists(overlap, False), n_a)


def _seq_flags(seq_lens, block):
    first, last = [], []
    for n in seq_lens:
        nb = n // block
        first += [1] + [0] * (nb - 1)
        last += [0] * (nb - 1) + [1]
    return np.asarray(first, np.int32), np.asarray(last, np.int32)


def _trunk(xa, xb, seq_lens, ln_mix0, ret_w_in, ret_w_out, ln_ffn0, ffn_w_gu, ffn_w_down, ln_mix1, ssm_w_in,
           ssm_conv_w, ssm_conv_b, ssm_dt_bias, ssm_A_log, ssm_D, ssm_norm_w, ssm_w_out, ln_ffn1,
           moe_router_w, moe_router_b, moe_w_gu, moe_w_down, final_norm):
    assert all(n % SCAN_BLOCK == 0 and n % MOE_TOK == 0 and n % TOK_TILE == 0 for n in seq_lens)
    row = lambda v: v.astype(F32).reshape(1, -1)
    first, last = _seq_flags(seq_lens, SCAN_BLOCK)
    reset_f = jnp.asarray(first)
    reset_b = jnp.asarray(last[::-1].copy())

    half = RET_DK // 2
    inv = ROPE_BASE ** (-jnp.arange(half, dtype=F32) / half)
    pos = jnp.concatenate([jnp.arange(n) for n in seq_lens])
    ang = pos.astype(F32)[:, None] * inv[None, :]
    q, k, v, g = _ret_in(xa, xb, row(ln_mix0), ret_w_in.astype(BF16), jnp.cos(ang), jnp.sin(ang))
    y = _ret_scan(q, k, v, g, reset_f, reset_b)
    h2 = _post0(y, xa, xb, ret_w_out.astype(BF16), row(ln_ffn0), ffn_w_gu.astype(BF16),
                ffn_w_down.astype(BF16))

    wz = ssm_w_in[:, :SSM_DINNER].astype(BF16)
    wx = ssm_w_in[:, SSM_DINNER:SSM_DINNER + SSM_CONV_DIM].astype(BF16)
    wdt = jnp.pad(ssm_w_in[:, SSM_DINNER + SSM_CONV_DIM:], ((0, 0), (0, LANES - 2 * SSM_HEADS)))
    wdh, wdl = _split2(wdt)
    pad_row = lambda v: jnp.pad(v.astype(F32).reshape(1, -1), ((0, 0), (0, LANES - 2 * SSM_HEADS)))
    gate, xbc, dt = _ssm_in(h2, row(ln_mix1), wz, wx, wdh, wdl, pad_row(ssm_dt_bias))
    xact = _conv(xbc, ssm_conv_w.reshape(SSM_CONV, SSM_CONV_DIM).astype(F32), row(ssm_conv_b),
                 jnp.asarray(first), jnp.asarray(last))
    a_row = pad_row(-jnp.exp(ssm_A_log.astype(F32)))
    dskip = jnp.repeat(ssm_D.astype(F32), SSM_HEADDIM).reshape(1, SSM_DINNER)
    y = _ssd(xact, dt, gate, a_row, dskip, row(ssm_norm_w), reset_f, reset_b)

    wr = jnp.pad(moe_router_w.astype(F32), ((0, 0), (0, LANES - N_EXPERTS)))
    wr_hi, wr_lo = _split2(wr)
    rb = jnp.pad(moe_router_b.astype(F32).reshape(1, -1), ((0, 0), (0, LANES - N_EXPERTS)),
                 constant_values=-1e30)
    return _moe(y, h2, ssm_w_out.astype(BF16), row(ln_ffn1), wr_hi, wr_lo, rb,
                moe_w_gu.astype(BF16), moe_w_down.astype(BF16), row(final_norm), xa.shape[0])


def kernel(x_prompt, x_sample, ln_mix0, ret_w_in, ret_w_out, ln_ffn0, ffn_w_gu, ffn_w_down, ln_mix1, ssm_w_in,
           ssm_conv_w, ssm_conv_b, ssm_dt_bias, ssm_A_log, ssm_D, ssm_norm_w, ssm_w_out, ln_ffn1, moe_router_w,
           moe_router_b, moe_w_gu, moe_w_down, final_norm):
    seq_lens = [x_prompt.shape[1]] * x_prompt.shape[0] + [x_sample.shape[1]] * x_sample.shape[0]
    out_a, out_b = _trunk(x_prompt.reshape(-1, D_MODEL), x_sample.reshape(-1, D_MODEL), seq_lens, ln_mix0, ret_w_in, ret_w_out, ln_ffn0, ffn_w_gu, ffn_w_down, ln_mix1, ssm_w_in,
                 ssm_conv_w, ssm_conv_b, ssm_dt_bias, ssm_A_log, ssm_D, ssm_norm_w, ssm_w_out, ln_ffn1,
                 moe_router_w, moe_router_b, moe_w_gu, moe_w_down, final_norm)
    return (out_a.reshape(x_prompt.shape), out_b.reshape(x_sample.shape))
```

```python
import functools

import numpy as np
import jax
import jax.numpy as jnp
from jax import lax
from jax.experimental import pallas as pl
from jax.experimental.pallas import tpu as pltpu
from jax.experimental.pallas import tpu_sc as plsc

F32 = jnp.float32
BF16 = jnp.bfloat16

D_MODEL = 1024
EPS = 1e-6
RET_HEADS = 4
RET_DK = 256
RET_DV = 512
ROPE_BASE = 10000.0
SSM_DINNER = 2048
SSM_HEADDIM = 64
SSM_HEADS = 32
SSM_GROUPS = 4
SSM_DSTATE = 128
SSM_CONV = 5
SSM_CONV_DIM = 3072
SSM_GROUP_W = SSM_DINNER // SSM_GROUPS
FFN_DENSE = 2816
FFN_DENSE_CHUNK = 1408
N_EXPERTS = 8
FFN_EXPERT = 3584

CHUNK = 128
RET_CHUNK = 256
LANES = 128
TOK_TILE = 512
SCAN_BLOCK = 512
MOE_TOK = 512
MOE_SLOT = 1024
MOE_FC = 512
SC_CORES = 2
SC_SUBCORES = 16
SC_CHUNK = 64
HALO = 16


def _cparams(n_axes, vmem_mb):
    return pltpu.CompilerParams(dimension_semantics=("arbitrary",) * n_axes,
                                vmem_limit_bytes=vmem_mb << 20)


def _resident(shape):
    nd = len(shape)
    return pl.BlockSpec(shape, lambda *_: (0,) * nd, pipeline_mode=pl.Buffered(1))


def _rms(x):
    return x * lax.rsqrt(jnp.mean(x * x, axis=-1, keepdims=True) + EPS)


def _silu(x):
    return x * jax.nn.sigmoid(x)


def _dot(a, b):
    return jnp.dot(a, b, preferred_element_type=F32)


def _dot_nt(a, b):
    return lax.dot_general(a, b, (((1,), (1,)), ((), ())), preferred_element_type=F32)


def _dot_tn(a, b):
    return lax.dot_general(a, b, (((0,), (0,)), ((), ())), preferred_element_type=F32)


def _split2(x):
    hi = x.astype(BF16)
    lo = (x - hi.astype(F32)).astype(BF16)
    return hi, lo


def _split3(x):
    hi = x.astype(BF16)
    r = x - hi.astype(F32)
    mid = r.astype(BF16)
    lo = (r - mid.astype(F32)).astype(BF16)
    return hi, mid, lo


def _two_stream_specs(tm, n_a):
    return (pl.BlockSpec((tm, D_MODEL), lambda i: (jnp.minimum(i, n_a - 1), 0)),
            pl.BlockSpec((tm, D_MODEL), lambda i: (jnp.maximum(i - n_a, 0), 0)))


def _ret_in_kernel(xa_ref, xb_ref, ln_ref, w_ref, cos_ref, sin_ref, q_ref, k_ref, v_ref, g_ref, *, n_a):
    x = jnp.where(pl.program_id(0) < n_a, xa_ref[...], xb_ref[...])
    xn = (_rms(x) * ln_ref[...]).astype(BF16)
    cos = cos_ref[...]
    sin = sin_ref[...]
    half = RET_DK // 2

    def rotary(dst, col0, scale):
        for h in range(RET_HEADS):
            c = col0 + h * RET_DK
            p = _dot(xn, w_ref[:, c:c + RET_DK])
            p1, p2 = p[:, :half], p[:, half:]
            dst[:, h * RET_DK:h * RET_DK + half] = ((p1 * cos - p2 * sin) * scale).astype(BF16)
            dst[:, h * RET_DK + half:(h + 1) * RET_DK] = ((p1 * sin + p2 * cos) * scale).astype(BF16)

    rotary(q_ref, 0, 1.0)
    rotary(k_ref, D_MODEL, RET_DK ** -0.5)
    for j in range(2 * D_MODEL // 512):
        v_ref[:, j * 512:(j + 1) * 512] = _dot(
            xn, w_ref[:, 2 * D_MODEL + j * 512:2 * D_MODEL + (j + 1) * 512]).astype(BF16)
        g_ref[:, j * 512:(j + 1) * 512] = _dot(
            xn, w_ref[:, 4 * D_MODEL + j * 512:4 * D_MODEL + (j + 1) * 512]).astype(BF16)


def _ret_in(xa, xb, ln, w_in, cos, sin):
    T = xa.shape[0] + xb.shape[0]
    tm = TOK_TILE
    n_a = xa.shape[0] // tm
    tok = lambda n: pl.BlockSpec((tm, n), lambda i: (i, 0))
    return pl.pallas_call(
        functools.partial(_ret_in_kernel, n_a=n_a),
        grid=(T // tm,),
        in_specs=[*_two_stream_specs(tm, n_a), _resident((1, D_MODEL)), _resident(w_in.shape),
                  tok(LANES), tok(LANES)],
        out_specs=[tok(D_MODEL), tok(D_MODEL), tok(2 * D_MODEL), tok(2 * D_MODEL)],
        out_shape=[jax.ShapeDtypeStruct((T, D_MODEL), BF16), jax.ShapeDtypeStruct((T, D_MODEL), BF16),
                   jax.ShapeDtypeStruct((T, 2 * D_MODEL), BF16), jax.ShapeDtypeStruct((T, 2 * D_MODEL), BF16)],
        compiler_params=_cparams(1, 48),
        name="ret_in",
    )(xa, xb, ln, w_in, cos, sin)


def _ret_scan_kernel(reset_ref, *refs, reverse, chunk_decay):
    if reverse:
        q_ref, k_ref, v_ref, wq_ref, wk_ref, out_ref, state_ref = refs
    else:
        q_ref, k_ref, v_ref, ob_ref, g_ref, wq_ref, wk_ref, din_ref, out_ref, state_ref = refs
    nch = SCAN_BLOCK // RET_CHUNK

    @pl.when(reset_ref[pl.program_id(0)] == 1)
    def _():
        state_ref[...] = jnp.zeros_like(state_ref)

    def chunk_body(ci, carry):
        cj = nch - 1 - ci if reverse else ci
        rows = pl.ds(pl.multiple_of(cj * RET_CHUNK, RET_CHUNK), RET_CHUNK)
        for h in range(RET_HEADS):
            kcols = slice(h * RET_DK, (h + 1) * RET_DK)
            vcols = slice(h * RET_DV, (h + 1) * RET_DV)
            qh = q_ref[rows, kcols]
            kh = k_ref[rows, kcols]
            vh = v_ref[rows, vcols]
            state = state_ref[h]
            o = _dot(qh, state.astype(BF16)) * wq_ref[h]
            if reverse:
                out_ref[rows, vcols] = o.astype(BF16)
            else:
                s = _dot_nt(qh, kh) * din_ref[h]
                o = o + _dot(s.astype(BF16), vh) + ob_ref[rows, vcols].astype(F32)
                gate = g_ref[rows, vcols].astype(F32)
                out_ref[rows, vcols] = (_silu(gate) * _rms(o)).astype(BF16)
            ks = (kh.astype(F32) * wk_ref[h]).astype(BF16)
            state_ref[h] = state * chunk_decay[h] + _dot_tn(ks, vh)
        return carry

    lax.fori_loop(0, nch, chunk_body, 0)


def _ret_tables():
    lg = np.log1p(-np.power(2.0, -5.0 - np.arange(RET_HEADS, dtype=np.float32))).astype(np.float32)
    c = RET_CHUNK
    a = np.arange(c, dtype=np.float32)
    col = lambda e, w: np.broadcast_to(np.exp(lg[:, None, None] * e[None, :, None]),
                                       (RET_HEADS, c, w)).astype(np.float32)
    dist = np.abs(a[:, None] - a[None, :])
    return dict(
        din=np.exp(lg[:, None, None] * dist[None]).astype(np.float32),
        wq_f=col(a + 1.0, RET_DV), wk_f=col(c - 1.0 - a, RET_DK),
        wq_b=col(c - a, RET_DV), wk_b=col(a, RET_DK),
        chunk_decay=tuple(float(v) for v in np.exp(lg * c)),
    )


def _ret_scan(q, k, v, g, reset_f, reset_b):
    T = q.shape[0]
    nb = T // SCAN_BLOCK
    tb = _ret_tables()
    state = pltpu.VMEM((RET_HEADS, RET_DK, RET_DV), F32)
    out_shape = jax.ShapeDtypeStruct((T, 2 * D_MODEL), BF16)

    def specs(imap):
        blk = lambda n: pl.BlockSpec((SCAN_BLOCK, n), imap)
        return blk(D_MODEL), blk(D_MODEL), blk(2 * D_MODEL)

    rev = lambda i, r: (nb - 1 - i, 0)
    bq, bk, bv = specs(rev)
    o_b = pl.pallas_call(
        functools.partial(_ret_scan_kernel, reverse=True, chunk_decay=tb["chunk_decay"]),
        grid_spec=pltpu.PrefetchScalarGridSpec(
            num_scalar_prefetch=1, grid=(nb,),
            in_specs=[bq, bk, bv, _resident(tb["wq_b"].shape), _resident(tb["wk_b"].shape)],
            out_specs=bv, scratch_shapes=[state]),
        out_shape=out_shape, compiler_params=_cparams(1, 40), name="ret_scan_bwd",
    )(reset_b, q, k, v, tb["wq_b"], tb["wk_b"])

    fwd = lambda i, r: (i, 0)
    bq, bk, bv = specs(fwd)
    return pl.pallas_call(
        functools.partial(_ret_scan_kernel, reverse=False, chunk_decay=tb["chunk_decay"]),
        grid_spec=pltpu.PrefetchScalarGridSpec(
            num_scalar_prefetch=1, grid=(nb,),
            in_specs=[bq, bk, bv, bv, bv, _resident(tb["wq_f"].shape), _resident(tb["wk_f"].shape),
                      _resident(tb["din"].shape)],
            out_specs=bv, scratch_shapes=[state]),
        out_shape=out_shape, compiler_params=_cparams(1, 40), name="ret_scan_fwd",
    )(reset_f, q, k, v, o_b, g, tb["wq_f"], tb["wk_f"], tb["din"])


def _post0_kernel(y_ref, xa_ref, xb_ref, wo_ref, ln_ref, wgu_ref, wd_ref, out_ref, *, n_a):
    x = jnp.where(pl.program_id(0) < n_a, xa_ref[...], xb_ref[...])
    h1 = x + _dot(y_ref[...], wo_ref[...])
    xn = (_rms(h1) * ln_ref[...]).astype(BF16)
    acc = h1
    fc = FFN_DENSE_CHUNK
    for c in range(FFN_DENSE // fc):
        gate = _dot(xn, wgu_ref[:, c * fc:(c + 1) * fc])
        up = _dot(xn, wgu_ref[:, FFN_DENSE + c * fc:FFN_DENSE + (c + 1) * fc])
        act = (_silu(gate) * up).astype(BF16)
        acc = acc + _dot(act, wd_ref[c * fc:(c + 1) * fc, :])
    out_ref[...] = acc


def _post0(y, xa, xb, w_out, ln, w_gu, w_down):
    T = y.shape[0]
    tm = TOK_TILE
    n_a = xa.shape[0] // tm
    tok = lambda n: pl.BlockSpec((tm, n), lambda i: (i, 0))
    return pl.pallas_call(
        functools.partial(_post0_kernel, n_a=n_a),
        grid=(T // tm,),
        in_specs=[tok(2 * D_MODEL), *_two_stream_specs(tm, n_a), _resident(w_out.shape),
                  _resident((1, D_MODEL)), _resident(w_gu.shape), _resident(w_down.shape)],
        out_specs=tok(D_MODEL),
        out_shape=jax.ShapeDtypeStruct((T, D_MODEL), F32),
        compiler_params=_cparams(1, 52),
        name="ret_out_ffn",
    )(y, xa, xb, w_out, ln, w_gu, w_down)


def _ssm_in_kernel(x_ref, ln_ref, wz_ref, wx_ref, wdh_ref, wdl_ref, bias_ref, gate_ref, xbc_ref, dt_ref):
    xh, xl = _split2(_rms(x_ref[...]) * ln_ref[...])
    for j in range(SSM_DINNER // 512):
        gate_ref[:, j * 512:(j + 1) * 512] = _silu(_dot(xh, wz_ref[:, j * 512:(j + 1) * 512])).astype(BF16)
    for j in range(SSM_CONV_DIM // 512):
        xbc_ref[:, j * 512:(j + 1) * 512] = _dot(xh, wx_ref[:, j * 512:(j + 1) * 512]).astype(BF16)
    x = _dot(xh, wdh_ref[...]) + _dot(xl, wdh_ref[...]) + _dot(xh, wdl_ref[...]) + bias_ref[...]
    dt_ref[...] = jnp.maximum(x, 0.0) + jnp.log1p(jnp.exp(-jnp.abs(x)))


def _ssm_in(x, ln, wz, wx, wdh, wdl, bias_row):
    T = x.shape[0]
    tm = TOK_TILE
    tok = lambda n: pl.BlockSpec((tm, n), lambda i: (i, 0))
    return pl.pallas_call(
        _ssm_in_kernel,
        grid=(T // tm,),
        in_specs=[tok(D_MODEL), _resident((1, D_MODEL)), _resident(wz.shape), _resident(wx.shape),
                  _resident(wdh.shape), _resident(wdl.shape), _resident(bias_row.shape)],
        out_specs=[tok(SSM_DINNER), tok(SSM_CONV_DIM), tok(LANES)],
        out_shape=[jax.ShapeDtypeStruct((T, SSM_DINNER), BF16), jax.ShapeDtypeStruct((T, SSM_CONV_DIM), BF16),
                   jax.ShapeDtypeStruct((T, LANES), F32)],
        compiler_params=_cparams(1, 40),
        name="ssm_in",
    )(x, ln, wz, wx, wdh, wdl, bias_row)


def _conv_kernel(first_ref, last_ref, prev_ref, main_ref, next_ref, w_ref, b_ref, out_ref, ext_ref):
    i = pl.program_id(0)
    keep_prev = jnp.where(first_ref[i] == 1, 0.0, 1.0).astype(F32)
    keep_next = jnp.where(last_ref[i] == 1, 0.0, 1.0).astype(F32)
    rb = SCAN_BLOCK
    pad = SSM_CONV // 2
    for s in range(SSM_CONV_DIM // 512):
        cs = slice(s * 512, (s + 1) * 512)
        ext_ref[0:8, :] = prev_ref[:, cs].astype(F32)[HALO - 8:HALO] * keep_prev
        ext_ref[8:8 + rb, :] = main_ref[:, cs].astype(F32)
        ext_ref[8 + rb:16 + rb, :] = next_ref[:, cs].astype(F32)[0:8] * keep_next
        acc = jnp.broadcast_to(b_ref[:, cs], (rb, 512))
        for j in range(SSM_CONV):
            acc = acc + ext_ref[8 - pad + j:8 - pad + j + rb, :] * w_ref[j:j + 1, cs]
        out_ref[:, cs] = _silu(acc).astype(BF16)


def _conv(xbc, conv_w, conv_b, first, last):
    T = xbc.shape[0]
    rb = SCAN_BLOCK
    nb = T // rb
    per = rb // HALO
    nh = T // HALO
    return pl.pallas_call(
        _conv_kernel,
        grid_spec=pltpu.PrefetchScalarGridSpec(
            num_scalar_prefetch=2, grid=(nb,),
            in_specs=[
                pl.BlockSpec((HALO, SSM_CONV_DIM), lambda i, f, l: (jnp.maximum(i * per - 1, 0), 0)),
                pl.BlockSpec((rb, SSM_CONV_DIM), lambda i, f, l: (i, 0)),
                pl.BlockSpec((HALO, SSM_CONV_DIM), lambda i, f, l: (jnp.minimum((i + 1) * per, nh - 1), 0)),
                _resident(conv_w.shape), _resident(conv_b.shape)],
            out_specs=pl.BlockSpec((rb, SSM_CONV_DIM), lambda i, f, l: (i, 0)),
            scratch_shapes=[pltpu.VMEM((rb + 16, 512), F32)]),
        out_shape=jax.ShapeDtypeStruct((T, SSM_CONV_DIM), BF16),
        compiler_params=_cparams(1, 40),
        name="ssm_conv",
    )(first, last, xbc, xbc, xbc, conv_w, conv_b)


def _ssd_tables():
    r = np.arange(CHUNK)
    tri_l = (r[None, :] <= r[:, None]).astype(np.float32)
    tri_u = (r[None, :] >= r[:, None]).astype(np.float32)
    c = np.arange(SSM_DINNER) // SSM_HEADDIM
    j = np.arange(LANES)
    e_f = (j[:, None] == c[None, :]).astype(np.float32)
    e_b = (j[:, None] == (c[None, :] + SSM_HEADS)).astype(np.float32)
    as_bf16 = lambda x: jnp.asarray(x, dtype=BF16)
    return dict(tri_l3=as_bf16(np.concatenate([tri_l] * 3, axis=1)),
                tri_u3=as_bf16(np.concatenate([tri_u] * 3, axis=1)),
                e2_f=as_bf16(np.concatenate([e_f, e_f], axis=0)),
                e2_b=as_bf16(np.concatenate([e_b, e_b], axis=0)))


def _ssd_chunk_scalars(dt, a_ref, tri_l3_ref, tri_u3_ref):
    a = dt * a_ref[...]
    a3 = jnp.concatenate(_split3(a), axis=0)
    cum_f = _dot(tri_l3_ref[...], a3)
    cum_b = _dot(tri_u3_ref[...], a3)
    lane = lax.broadcasted_iota(jnp.int32, (CHUNK, LANES), 1)
    cum = jnp.where(lane < SSM_HEADS, cum_f, cum_b)
    tot = cum_f[CHUNK - 1:CHUNK, :]
    return cum, tot


def _expand(e2_ref, *rows):
    x = jnp.concatenate(rows, axis=0)
    hi, lo = _split2(x)
    return _dot(jnp.concatenate([hi, lo], axis=1), e2_ref[...])


def _ssd_bwd_kernel(reset_ref, xa_ref, dt_ref, a_ref, tri_l3_ref, tri_u3_ref, e2_ref,
                    yb_ref, state_ref):
    nch = SCAN_BLOCK // CHUNK

    @pl.when(reset_ref[pl.program_id(0)] == 1)
    def _():
        state_ref[...] = jnp.zeros_like(state_ref)

    def chunk_body(ci, carry):
        rows = pl.ds(pl.multiple_of((nch - 1 - ci) * CHUNK, CHUNK), CHUNK)
        dt = dt_ref[rows, :]
        cum, tot = _ssd_chunk_scalars(dt, a_ref, tri_l3_ref, tri_u3_ref)
        ex = _expand(e2_ref, dt * jnp.exp(tot - cum), jnp.exp(cum),
                     jnp.broadcast_to(jnp.exp(tot), (16, LANES)))
        for g in range(SSM_GROUPS):
            gc = slice(g * SSM_GROUP_W, (g + 1) * SSM_GROUP_W)
            bm = xa_ref[rows, SSM_DINNER + g * SSM_DSTATE:SSM_DINNER + (g + 1) * SSM_DSTATE]
            cm = xa_ref[rows, SSM_DINNER + (SSM_GROUPS + g) * SSM_DSTATE:
                        SSM_DINNER + (SSM_GROUPS + g + 1) * SSM_DSTATE]
            state = state_ref[g]
            yb_ref[rows, gc] = (_dot(cm, state.astype(BF16)) * ex[CHUNK:2 * CHUNK, gc]).astype(BF16)
            xw = (xa_ref[rows, gc].astype(F32) * ex[0:CHUNK, gc]).astype(BF16)
            state_ref[g] = state * ex[2 * CHUNK:2 * CHUNK + 1, gc] + _dot_tn(bm, xw)
        return carry

    lax.fori_loop(0, nch, chunk_body, 0)


def _ssd_fwd_kernel(reset_ref, xa_ref, dt_ref, gate_ref, yb_ref, a_ref, tri_l3_ref, tri_u3_ref,
                    e2_ref, dskip_ref, nw_ref, out_ref, state_ref, y_ref):
    nch = SCAN_BLOCK // CHUNK

    @pl.when(reset_ref[pl.program_id(0)] == 1)
    def _():
        state_ref[...] = jnp.zeros_like(state_ref)

    def chunk_body(ci, carry):
        rows = pl.ds(pl.multiple_of(ci * CHUNK, CHUNK), CHUNK)
        dt = dt_ref[rows, :]
        cum, tot = _ssd_chunk_scalars(dt, a_ref, tri_l3_ref, tri_u3_ref)
        ex = _expand(e2_ref, dt * jnp.exp(tot - cum), jnp.exp(cum),
                     jnp.broadcast_to(jnp.exp(tot), (16, LANES)))
        row_t = (cum - jnp.log(dt)).T
        li = lax.broadcasted_iota(jnp.int32, (CHUNK, CHUNK), 0)
        si = lax.broadcasted_iota(jnp.int32, (CHUNK, CHUNK), 1)
        lower = li >= si
        first_half = si < SSM_HEADDIM
        for g in range(SSM_GROUPS):
            gc = slice(g * SSM_GROUP_W, (g + 1) * SSM_GROUP_W)
            bm = xa_ref[rows, SSM_DINNER + g * SSM_DSTATE:SSM_DINNER + (g + 1) * SSM_DSTATE]
            cm = xa_ref[rows, SSM_DINNER + (SSM_GROUPS + g) * SSM_DSTATE:
                        SSM_DINNER + (SSM_GROUPS + g + 1) * SSM_DSTATE]
            cb = _dot_nt(cm, bm)
            heads_per_group = SSM_HEADS // SSM_GROUPS
            for pair in range(heads_per_group // 2):
                mats = []
                for hh in range(2):
                    h = g * heads_per_group + 2 * pair + hh
                    hb = SSM_HEADS + h
                    seg = jnp.where(lower, cum[:, h:h + 1] - row_t[h:h + 1, :],
                                    cum[:, hb:hb + 1] - row_t[hb:hb + 1, :])
                    mats.append((cb * jnp.exp(seg)).astype(BF16))
                pc = slice((g * heads_per_group + 2 * pair) * SSM_HEADDIM,
                           (g * heads_per_group + 2 * pair + 2) * SSM_HEADDIM)
                yy = _dot(jnp.concatenate(mats, axis=0), xa_ref[rows, pc])
                y_ref[:, pc] = jnp.where(first_half, yy[0:CHUNK], yy[CHUNK:2 * CHUNK])
            state = state_ref[g]
            xs = xa_ref[rows, gc].astype(F32)
            y = (y_ref[:, gc] + _dot(cm, state.astype(BF16)) * ex[CHUNK:2 * CHUNK, gc]
                 + yb_ref[rows, gc].astype(F32) + xs * dskip_ref[:, gc])
            y = y * gate_ref[rows, gc].astype(F32)
            out_ref[rows, gc] = (_rms(y) * nw_ref[:, gc]).astype(BF16)
            xw = (xs * ex[0:CHUNK, gc]).astype(BF16)
            state_ref[g] = state * ex[2 * CHUNK:2 * CHUNK + 1, gc] + _dot_tn(bm, xw)
        return carry

    lax.fori_loop(0, nch, chunk_body, 0)


def _ssd(xa, dt, gate, a_row, dskip, norm_w, reset_f, reset_b):
    T = xa.shape[0]
    nb = T // SCAN_BLOCK
    tb = _ssd_tables()
    state = pltpu.VMEM((SSM_GROUPS, SSM_DSTATE, SSM_GROUP_W), F32)
    consts = [a_row, tb["tri_l3"], tb["tri_u3"]]
    const_specs = [_resident(c.shape) for c in consts]
    out_shape = jax.ShapeDtypeStruct((T, SSM_DINNER), BF16)

    rev = lambda i, r: (nb - 1 - i, 0)
    blk = lambda n, imap: pl.BlockSpec((SCAN_BLOCK, n), imap)
    y_b = pl.pallas_call(
        _ssd_bwd_kernel,
        grid_spec=pltpu.PrefetchScalarGridSpec(
            num_scalar_prefetch=1, grid=(nb,),
            in_specs=[blk(SSM_CONV_DIM, rev), blk(LANES, rev)] + const_specs + [_resident(tb["e2_b"].shape)],
            out_specs=blk(SSM_DINNER, rev), scratch_shapes=[state]),
        out_shape=out_shape, compiler_params=_cparams(1, 40), name="ssd_bwd",
    )(reset_b, xa, dt, *consts, tb["e2_b"])

    fwd = lambda i, r: (i, 0)
    return pl.pallas_call(
        _ssd_fwd_kernel,
        grid_spec=pltpu.PrefetchScalarGridSpec(
            num_scalar_prefetch=1, grid=(nb,),
            in_specs=[blk(SSM_CONV_DIM, fwd), blk(LANES, fwd), blk(SSM_DINNER, fwd), blk(SSM_DINNER, fwd)]
            + const_specs + [_resident(tb["e2_f"].shape), _resident(dskip.shape), _resident(norm_w.shape)],
            out_specs=blk(SSM_DINNER, fwd),
            scratch_shapes=[state, pltpu.VMEM((CHUNK, SSM_DINNER), F32)]),
        out_shape=out_shape, compiler_params=_cparams(1, 40), name="ssd_fwd",
    )(reset_f, xa, dt, gate, y_b, *consts, tb["e2_f"], dskip, norm_w)


def _pack_halves(x):
    n = x.shape[1] // 2
    hi = lax.bitcast_convert_type(x[:, :n].astype(F32), jnp.uint32)
    lo = lax.bitcast_convert_type(x[:, n:].astype(F32), jnp.uint32)
    return hi | (lo >> 16)


def _unpack_halves(w):
    hi = lax.bitcast_convert_type(w & jnp.uint32(0xFFFF0000), F32)
    lo = lax.bitcast_convert_type(w << 16, F32)
    return hi, lo


def _router_kernel(y_ref, h_ref, wo_ref, ln_ref, wh_ref, wl_ref, b_ref, tri_ref,
                   h3_ref, xn_ref, info_ref, count_ref, base_ref):
    @pl.when(pl.program_id(0) == 0)
    def _():
        base_ref[...] = jnp.zeros_like(base_ref)

    h3 = h_ref[...] + _dot(y_ref[...], wo_ref[...])
    h3_ref[...] = h3
    xh, xl = _split2(_rms(h3) * ln_ref[...])
    xn_ref[...] = _pack_halves(xh)
    logits = _dot(xh, wh_ref[...]) + _dot(xl, wh_ref[...]) + _dot(xh, wl_ref[...]) + b_ref[...]
    col = lax.broadcasted_iota(jnp.int32, logits.shape, 1)
    m1 = jnp.max(logits, axis=-1, keepdims=True)
    i1 = jnp.min(jnp.where(logits == m1, col, LANES), axis=-1, keepdims=True)
    rest = jnp.where(col == i1, -jnp.inf, logits)
    m2 = jnp.max(rest, axis=-1, keepdims=True)
    i2 = jnp.min(jnp.where(rest == m2, col, LANES), axis=-1, keepdims=True)
    e = jnp.exp(m2 - m1)
    g1 = 1.0 / (1.0 + e)
    g2 = e / (1.0 + e)
    oh1 = col == i1
    oh2 = col == i2
    oh = jnp.where(oh1 | oh2, 1.0, 0.0)
    base = base_ref[...]
    prefix = _dot(tri_ref[...], oh.astype(BF16)) + base
    r1 = jnp.sum(jnp.where(oh1, prefix, 0.0), axis=-1, keepdims=True)
    r2 = jnp.sum(jnp.where(oh2, prefix, 0.0), axis=-1, keepdims=True)
    base = base + jnp.sum(oh, axis=0, keepdims=True)
    count_ref[...] = base
    base_ref[...] = base
    fields = [i1.astype(F32), i2.astype(F32), g1, g2, r1, r2]
    info = jnp.zeros(logits.shape, F32)
    for c, f in enumerate(fields):
        info = jnp.where(col == c, f, info)
    info_ref[...] = info


def _router(y, h, w_out, ln, wh, wl, b):
    T = h.shape[0]
    tb = MOE_TOK
    nb = T // tb
    r = np.arange(tb)
    tri = jnp.asarray((r[None, :] < r[:, None]).astype(np.float32), dtype=BF16)
    tok = lambda n: pl.BlockSpec((tb, n), lambda i: (i, 0))
    return pl.pallas_call(
        _router_kernel,
        grid=(nb,),
        in_specs=[tok(SSM_DINNER), tok(D_MODEL), _resident(w_out.shape), _resident((1, D_MODEL)),
                  _resident(wh.shape), _resident(wl.shape), _resident(b.shape), _resident(tri.shape)],
        out_specs=[tok(D_MODEL), tok(D_MODEL // 2), tok(LANES), pl.BlockSpec((1, LANES), lambda i: (0, 0))],
        out_shape=[jax.ShapeDtypeStruct((T, D_MODEL), F32), jax.ShapeDtypeStruct((T, D_MODEL // 2), jnp.uint32),
                   jax.ShapeDtypeStruct((T, LANES), F32), jax.ShapeDtypeStruct((1, LANES), F32)],
        scratch_shapes=[pltpu.VMEM((1, LANES), F32)],
        compiler_params=_cparams(1, 32),
        name="ssm_out_router",
    )(y, h, w_out, ln, wh, wl, b, tri)


def _sc_gather(table, idx):
    n_rows, width = idx.shape[0], table.shape[1]
    workers = SC_CORES * SC_SUBCORES
    nbuf = 2
    assert n_rows % (workers * SC_CHUNK * nbuf) == 0
    per_worker = n_rows // workers
    mesh = plsc.VectorSubcoreMesh(core_axis_name="c", subcore_axis_name="s")

    @functools.partial(
        pl.kernel, mesh=mesh,
        out_type=jax.ShapeDtypeStruct((n_rows, width), table.dtype),
        scratch_types=[pltpu.VMEM((nbuf, SC_CHUNK), jnp.int32),
                       pltpu.VMEM((nbuf, SC_CHUNK, width), table.dtype),
                       pltpu.SemaphoreType.DMA((nbuf,)),
                       pltpu.SemaphoreType.DMA((nbuf,))],
    )
    def gather_rows(table_hbm, idx_hbm, out_hbm, idx_v, rows_v, gsem, wsem):
        base = (lax.axis_index("s") * SC_CORES + lax.axis_index("c")) * per_worker

        def gather(c, b):
            off = pl.multiple_of(base + c * SC_CHUNK, 8)
            pltpu.sync_copy(idx_hbm.at[pl.ds(off, SC_CHUNK)], idx_v.at[b])
            return pltpu.make_async_copy(table_hbm.at[idx_v.at[b]], rows_v.at[b], gsem.at[b])

        def write(c, b):
            off = pl.multiple_of(base + c * SC_CHUNK, 8)
            return pltpu.make_async_copy(rows_v.at[b], out_hbm.at[pl.ds(off, SC_CHUNK)], wsem.at[b])

        @pl.loop(0, per_worker // SC_CHUNK, step=nbuf)
        def _(c0):
            copies = [gather(c0 + b, b) for b in range(nbuf)]
            for cp in copies:
                cp.start()
            writes = []
            for b, cp in enumerate(copies):
                cp.wait()
                writes.append(write(c0 + b, b))
                writes[-1].start()
            for wr in writes:
                wr.wait()

    return gather_rows(table, idx)


def _expert_kernel(texp_ref, tvalid_ref, x_ref, wg_ref, wu_ref, wd_ref, out_ref, acc_ref, xb_ref):
    i = pl.program_id(0)
    f = pl.program_id(1)
    nf = pl.num_programs(1)
    n = D_MODEL // 2

    @pl.when(tvalid_ref[i] == 1)
    def _():
        @pl.when(f == 0)
        def _():
            acc_ref[...] = jnp.zeros_like(acc_ref)
            hi, lo = _unpack_halves(x_ref[...])
            xb_ref[:, :n] = hi.astype(BF16)
            xb_ref[:, n:] = lo.astype(BF16)

        x = xb_ref[...]
        act = (_silu(_dot(x, wg_ref[0])) * _dot(x, wu_ref[0])).astype(BF16)
        acc_ref[...] += _dot(act, wd_ref[0])

        @pl.when(f == nf - 1)
        def _():
            out_ref[...] = _pack_halves(acc_ref[...].astype(BF16))

    @pl.when((tvalid_ref[i] == 0) & (f == nf - 1))
    def _():
        out_ref[...] = jnp.zeros_like(out_ref)


def _experts(xs, w_gu, w_down, tile_expert, tile_valid):
    n_tiles = xs.shape[0] // MOE_SLOT
    nf = FFN_EXPERT // MOE_FC
    return pl.pallas_call(
        _expert_kernel,
        grid_spec=pltpu.PrefetchScalarGridSpec(
            num_scalar_prefetch=2, grid=(n_tiles, nf),
            in_specs=[pl.BlockSpec((MOE_SLOT, D_MODEL // 2), lambda i, f, te, tv: (i, 0)),
                      pl.BlockSpec((1, D_MODEL, MOE_FC), lambda i, f, te, tv: (te[i], 0, f)),
                      pl.BlockSpec((1, D_MODEL, MOE_FC), lambda i, f, te, tv: (te[i], 0, nf + f)),
                      pl.BlockSpec((1, MOE_FC, D_MODEL), lambda i, f, te, tv: (te[i], f, 0))],
            out_specs=pl.BlockSpec((MOE_SLOT, D_MODEL // 2), lambda i, f, te, tv: (i, 0)),
            scratch_shapes=[pltpu.VMEM((MOE_SLOT, D_MODEL), F32), pltpu.VMEM((MOE_SLOT, D_MODEL), BF16)]),
        out_shape=jax.ShapeDtypeStruct(xs.shape, jnp.uint32),
        compiler_params=_cparams(2, 40),
        name="moe_experts",
    )(tile_expert, tile_valid, xs, w_gu, w_gu, w_down)


def _finalize_kernel(y0_ref, y1_ref, info_ref, h_ref, fn_ref, outa_ref, outb_ref, *, nb_a):
    i = pl.program_id(0)
    n = D_MODEL // 2
    info = info_ref[...]
    g0 = info[:, 2:3]
    g1 = info[:, 3:4]
    a_hi, a_lo = _unpack_halves(y0_ref[...])
    b_hi, b_lo = _unpack_halves(y1_ref[...])
    left = h_ref[:, :n] + g0 * a_hi + g1 * b_hi
    right = h_ref[:, n:] + g0 * a_lo + g1 * b_lo
    ms = (jnp.sum(left * left, axis=-1, keepdims=True)
          + jnp.sum(right * right, axis=-1, keepdims=True)) * (1.0 / D_MODEL)
    scale = lax.rsqrt(ms + EPS)
    left = left * scale * fn_ref[:, :n]
    right = right * scale * fn_ref[:, n:]

    @pl.when(i < nb_a)
    def _():
        outa_ref[:, :n] = left
        outa_ref[:, n:] = right

    @pl.when(i >= nb_a)
    def _():
        outb_ref[:, :n] = left
        outb_ref[:, n:] = right


def _finalize(yg, info, h3, final_norm, n_a):
    T = h3.shape[0]
    tb = MOE_TOK
    nb = T // tb
    nb_a = n_a // tb
    tok = lambda w: pl.BlockSpec((tb, w), lambda i: (i, 0))
    return pl.pallas_call(
        functools.partial(_finalize_kernel, nb_a=nb_a),
        grid=(nb,),
        in_specs=[tok(D_MODEL // 2), pl.BlockSpec((tb, D_MODEL // 2), lambda i: (i + nb, 0)),
                  tok(LANES), tok(D_MODEL), _resident((1, D_MODEL))],
        out_specs=[pl.BlockSpec((tb, D_MODEL), lambda i: (jnp.minimum(i, nb_a - 1), 0)),
                   pl.BlockSpec((tb, D_MODEL), lambda i: (jnp.maximum(i - nb_a, 0), 0))],
        out_shape=[jax.ShapeDtypeStruct((n_a, D_MODEL), F32), jax.ShapeDtypeStruct((T - n_a, D_MODEL), F32)],
        compiler_params=_cparams(1, 32),
        name="moe_finalize",
    )(yg, yg, info, h3, final_norm)


def _moe(y_ssm, h2, w_out, ln, wr_hi, wr_lo, rb, w_gu, w_down, final_norm, n_a):
    T = h2.shape[0]
    n_tiles = 2 * T // MOE_SLOT + N_EXPERTS
    h3, xn, info, counts = _router(y_ssm, h2, w_out, ln, wr_hi, wr_lo, rb)

    counts = counts[0, :N_EXPERTS].astype(jnp.int32)
    tiles_e = (counts + MOE_SLOT - 1) // MOE_SLOT
    tiles_cum = jnp.cumsum(tiles_e)
    gstart = (tiles_cum - tiles_e) * MOE_SLOT
    i1 = info[:, 0].astype(jnp.int32)
    i2 = info[:, 1].astype(jnp.int32)
    pos0 = gstart[i1] + info[:, 4].astype(jnp.int32)
    pos1 = gstart[i2] + info[:, 5].astype(jnp.int32)
    tile_ids = jnp.arange(n_tiles, dtype=jnp.int32)
    tile_valid = tile_ids < tiles_cum[-1]
    tile_expert = jnp.minimum(jnp.searchsorted(tiles_cum, tile_ids, side="right"), N_EXPERTS - 1).astype(jnp.int32)
    tok_ids = jnp.arange(T, dtype=jnp.int32)
    slot_token = jnp.zeros((n_tiles * MOE_SLOT,), jnp.int32).at[pos0].set(tok_ids).at[pos1].set(tok_ids)

    xs = _sc_gather(xn, slot_token)
    ys = _experts(xs, w_gu, w_down, tile_expert, tile_valid.astype(jnp.int32))
    yg = _sc_gather(ys, jnp.concatenate([pos0, pos1]))
    return _finalize(yg, info, h3, final_norm, n_a)


def _seq_flags(seq_lens, block):
    first, last = [], []
    for n in seq_lens:
        nb = n // block
        first += [1] + [0] * (nb - 1)
        last += [0] * (nb - 1) + [1]
    return np.asarray(first, np.int32), np.asarray(last, np.int32)


def _trunk(xa, xb, seq_lens, ln_mix0, ret_w_in, ret_w_out, ln_ffn0, ffn_w_gu, ffn_w_down, ln_mix1, ssm_w_in,
           ssm_conv_w, ssm_conv_b, ssm_dt_bias, ssm_A_log, ssm_D, ssm_norm_w, ssm_w_out, ln_ffn1,
           moe_router_w, moe_router_b, moe_w_gu, moe_w_down, final_norm):
    assert all(n % SCAN_BLOCK == 0 and n % MOE_TOK == 0 and n % TOK_TILE == 0 for n in seq_lens)
    row = lambda v: v.astype(F32).reshape(1, -1)
    first, last = _seq_flags(seq_lens, SCAN_BLOCK)
    reset_f = jnp.asarray(first)
    reset_b = jnp.asarray(last[::-1].copy())

    half = RET_DK // 2
    inv = ROPE_BASE ** (-jnp.arange(half, dtype=F32) / half)
    pos = jnp.concatenate([jnp.arange(n) for n in seq_lens])
    ang = pos.astype(F32)[:, None] * inv[None, :]
    q, k, v, g = _ret_in(xa, xb, row(ln_mix0), ret_w_in.astype(BF16), jnp.cos(ang), jnp.sin(ang))
    y = _ret_scan(q, k, v, g, reset_f, reset_b)
    h2 = _post0(y, xa, xb, ret_w_out.astype(BF16), row(ln_ffn0), ffn_w_gu.astype(BF16),
                ffn_w_down.astype(BF16))

    wz = ssm_w_in[:, :SSM_DINNER].astype(BF16)
    wx = ssm_w_in[:, SSM_DINNER:SSM_DINNER + SSM_CONV_DIM].astype(BF16)
    wdt = jnp.pad(ssm_w_in[:, SSM_DINNER + SSM_CONV_DIM:], ((0, 0), (0, LANES - 2 * SSM_HEADS)))
    wdh, wdl = _split2(wdt)
    pad_row = lambda v: jnp.pad(v.astype(F32).reshape(1, -1), ((0, 0), (0, LANES - 2 * SSM_HEADS)))
    gate, xbc, dt = _ssm_in(h2, row(ln_mix1), wz, wx, wdh, wdl, pad_row(ssm_dt_bias))
    xact = _conv(xbc, ssm_conv_w.reshape(SSM_CONV, SSM_CONV_DIM).astype(F32), row(ssm_conv_b),
                 jnp.asarray(first), jnp.asarray(last))
    a_row = pad_row(-jnp.exp(ssm_A_log.astype(F32)))
    dskip = jnp.repeat(ssm_D.astype(F32), SSM_HEADDIM).reshape(1, SSM_DINNER)
    y = _ssd(xact, dt, gate, a_row, dskip, row(ssm_norm_w), reset_f, reset_b)

    wr = jnp.pad(moe_router_w.astype(F32), ((0, 0), (0, LANES - N_EXPERTS)))
    wr_hi, wr_lo = _split2(wr)
    rb = jnp.pad(moe_router_b.astype(F32).reshape(1, -1), ((0, 0), (0, LANES - N_EXPERTS)),
                 constant_values=-1e30)
    return _moe(y, h2, ssm_w_out.astype(BF16), row(ln_ffn1), wr_hi, wr_lo, rb,
                moe_w_gu.astype(BF16), moe_w_down.astype(BF16), row(final_norm), xa.shape[0])


def kernel(x_prompt, x_sample, ln_mix0, ret_w_in, ret_w_out, ln_ffn0, ffn_w_gu, ffn_w_down, ln_mix1, ssm_w_in,
           ssm_conv_w, ssm_conv_b, ssm_dt_bias, ssm_A_log, ssm_D, ssm_norm_w, ssm_w_out, ln_ffn1, moe_router_w,
           moe_router_b, moe_w_gu, moe_w_down, final_norm):
    seq_lens = [x_prompt.shape[1]] * x_prompt.shape[0] + [x_sample.shape[1]] * x_sample.shape[0]
    out_a, out_b = _trunk(x_prompt.reshape(-1, D_MODEL), x_sample.reshape(-1, D_MODEL), seq_lens, ln_mix0, ret_w_in, ret_w_out, ln_ffn0, ffn_w_gu, ffn_w_down, ln_mix1, ssm_w_in,
                 ssm_conv_w, ssm_conv_b, ssm_dt_bias, ssm_A_log, ssm_D, ssm_norm_w, ssm_w_out, ln_ffn1,
                 moe_router_w, moe_router_b, moe_w_gu, moe_w_down, final_norm)
    return (out_a.reshape(x_prompt.shape), out_b.reshape(x_sample.shape))
```

```python
import functools

import numpy as np
import jax
import jax.numpy as jnp
from jax import lax
from jax.experimental import pallas as pl
from jax.experimental.pallas import tpu as pltpu
from jax.experimental.pallas import tpu_sc as plsc

F32 = jnp.float32
BF16 = jnp.bfloat16

D_MODEL = 1024
EPS = 1e-6
RET_HEADS = 4
RET_DK = 256
RET_DV = 512
ROPE_BASE = 10000.0
SSM_DINNER = 2048
SSM_HEADDIM = 64
SSM_HEADS = 32
SSM_GROUPS = 4
SSM_DSTATE = 128
SSM_CONV = 5
SSM_CONV_DIM = 3072
SSM_GROUP_W = SSM_DINNER // SSM_GROUPS
SLAB = SSM_GROUP_W
CONV_ROWS = 64
FFN_DENSE = 2816
FFN_DENSE_CHUNK = 1408
N_EXPERTS = 8
FFN_EXPERT = 3584

CHUNK = 128
RET_CHUNK = 256
LANES = 128
TOK_TILE = 512
SCAN_BLOCK = 512
MOE_TOK = 512
MOE_SLOT = 1024
MOE_FC = 512
SC_CORES = 2
SC_SUBCORES = 16
SC_CHUNK = 64


def _cparams(n_axes, vmem_mb):
    return pltpu.CompilerParams(dimension_semantics=("arbitrary",) * n_axes,
                                vmem_limit_bytes=vmem_mb << 20)


def _resident(shape):
    nd = len(shape)
    return pl.BlockSpec(shape, lambda *_: (0,) * nd, pipeline_mode=pl.Buffered(1))


def _rms(x):
    return x * lax.rsqrt(jnp.mean(x * x, axis=-1, keepdims=True) + EPS)


def _silu(x):
    return x * jax.nn.sigmoid(x)


def _dot(a, b):
    return jnp.dot(a, b, preferred_element_type=F32)


def _dot_nt(a, b):
    return lax.dot_general(a, b, (((1,), (1,)), ((), ())), preferred_element_type=F32)


def _dot_tn(a, b):
    return lax.dot_general(a, b, (((0,), (0,)), ((), ())), preferred_element_type=F32)


def _split2(x):
    hi = x.astype(BF16)
    lo = (x - hi.astype(F32)).astype(BF16)
    return hi, lo


def _split3(x):
    hi = x.astype(BF16)
    r = x - hi.astype(F32)
    mid = r.astype(BF16)
    lo = (r - mid.astype(F32)).astype(BF16)
    return hi, mid, lo


def _two_stream_specs(tm, n_a):
    return (pl.BlockSpec((tm, D_MODEL), lambda i, *_: (jnp.minimum(i, n_a - 1), 0)),
            pl.BlockSpec((tm, D_MODEL), lambda i, *_: (jnp.maximum(i - n_a, 0), 0)))


def _ret_in_kernel(pos_ref, xa_ref, xb_ref, ln_ref, w_ref, cos_ref, sin_ref, q_ref, k_ref, v_ref, g_ref, *, n_a):
    x = jnp.where(pl.program_id(0) < n_a, xa_ref[...], xb_ref[...])
    xn = (_rms(x) * ln_ref[...]).astype(BF16)
    cos = cos_ref[...]
    sin = sin_ref[...]
    half = RET_DK // 2

    def rotary(dst, col0, scale):
        for h in range(RET_HEADS):
            c = col0 + h * RET_DK
            p = _dot(xn, w_ref[:, c:c + RET_DK])
            p1, p2 = p[:, :half], p[:, half:]
            dst[:, h * RET_DK:h * RET_DK + half] = ((p1 * cos - p2 * sin) * scale).astype(BF16)
            dst[:, h * RET_DK + half:(h + 1) * RET_DK] = ((p1 * sin + p2 * cos) * scale).astype(BF16)

    rotary(q_ref, 0, 1.0)
    rotary(k_ref, D_MODEL, RET_DK ** -0.5)
    for j in range(2 * D_MODEL // 512):
        v_ref[:, j * 512:(j + 1) * 512] = _dot(
            xn, w_ref[:, 2 * D_MODEL + j * 512:2 * D_MODEL + (j + 1) * 512]).astype(BF16)
        g_ref[:, j * 512:(j + 1) * 512] = _dot(
            xn, w_ref[:, 4 * D_MODEL + j * 512:4 * D_MODEL + (j + 1) * 512]).astype(BF16)


def _ret_in(xa, xb, ln, w_in, cos, sin, pos_block):
    T = xa.shape[0] + xb.shape[0]
    tm = TOK_TILE
    n_a = xa.shape[0] // tm
    tok = lambda n: pl.BlockSpec((tm, n), lambda i, pb: (i, 0))
    rot = pl.BlockSpec((tm, LANES), lambda i, pb: (pb[i], 0))
    return pl.pallas_call(
        functools.partial(_ret_in_kernel, n_a=n_a),
        grid_spec=pltpu.PrefetchScalarGridSpec(
            num_scalar_prefetch=1, grid=(T // tm,),
            in_specs=[*_two_stream_specs(tm, n_a), _resident((1, D_MODEL)), _resident(w_in.shape), rot, rot],
            out_specs=[tok(D_MODEL), tok(D_MODEL), tok(2 * D_MODEL), tok(2 * D_MODEL)]),
        out_shape=[jax.ShapeDtypeStruct((T, D_MODEL), BF16), jax.ShapeDtypeStruct((T, D_MODEL), BF16),
                   jax.ShapeDtypeStruct((T, 2 * D_MODEL), BF16), jax.ShapeDtypeStruct((T, 2 * D_MODEL), BF16)],
        compiler_params=_cparams(1, 48),
        name="ret_in",
    )(pos_block, xa, xb, ln, w_in, cos, sin)


def _ret_scan_kernel(reset_ref, *refs, reverse, chunk_decay):
    if reverse:
        q_ref, k_ref, v_ref, wq_ref, wk_ref, out_ref, state_ref = refs
    else:
        q_ref, k_ref, v_ref, ob_ref, g_ref, wq_ref, wk_ref, din_ref, out_ref, state_ref = refs
    nch = SCAN_BLOCK // RET_CHUNK

    @pl.when(reset_ref[pl.program_id(0)] == 1)
    def _():
        state_ref[...] = jnp.zeros_like(state_ref)

    def chunk_body(ci, carry):
        cj = nch - 1 - ci if reverse else ci
        rows = pl.ds(pl.multiple_of(cj * RET_CHUNK, RET_CHUNK), RET_CHUNK)
        for h in range(RET_HEADS):
            kcols = slice(h * RET_DK, (h + 1) * RET_DK)
            vcols = slice(h * RET_DV, (h + 1) * RET_DV)
            qh = q_ref[rows, kcols]
            kh = k_ref[rows, kcols]
            vh = v_ref[rows, vcols]
            state = state_ref[h]
            o = _dot(qh, state.astype(BF16)) * wq_ref[h]
            if reverse:
                out_ref[rows, vcols] = o.astype(BF16)
            else:
                s = _dot_nt(qh, kh) * din_ref[h]
                o = o + _dot(s.astype(BF16), vh) + ob_ref[rows, vcols].astype(F32)
                gate = g_ref[rows, vcols].astype(F32)
                out_ref[rows, vcols] = (_silu(gate) * _rms(o)).astype(BF16)
            ks = (kh.astype(F32) * wk_ref[h]).astype(BF16)
            state_ref[h] = state * chunk_decay[h] + _dot_tn(ks, vh)
        return carry

    lax.fori_loop(0, nch, chunk_body, 0)


def _ret_tables():
    lg = np.log1p(-np.power(2.0, -5.0 - np.arange(RET_HEADS, dtype=np.float32))).astype(np.float32)
    c = RET_CHUNK
    a = np.arange(c, dtype=np.float32)
    col = lambda e, w: np.broadcast_to(np.exp(lg[:, None, None] * e[None, :, None]),
                                       (RET_HEADS, c, w)).astype(np.float32)
    dist = np.abs(a[:, None] - a[None, :])
    return dict(
        din=np.exp(lg[:, None, None] * dist[None]).astype(np.float32),
        wq_f=col(a + 1.0, RET_DV), wk_f=col(c - 1.0 - a, RET_DK),
        wq_b=col(c - a, RET_DV), wk_b=col(a, RET_DK),
        chunk_decay=tuple(float(v) for v in np.exp(lg * c)),
    )


def _ret_scan(q, k, v, g, reset_f, reset_b):
    T = q.shape[0]
    nb = T // SCAN_BLOCK
    tb = _ret_tables()
    state = pltpu.VMEM((RET_HEADS, RET_DK, RET_DV), F32)
    out_shape = jax.ShapeDtypeStruct((T, 2 * D_MODEL), BF16)

    def specs(imap):
        blk = lambda n: pl.BlockSpec((SCAN_BLOCK, n), imap)
        return blk(D_MODEL), blk(D_MODEL), blk(2 * D_MODEL)

    rev = lambda i, r: (nb - 1 - i, 0)
    bq, bk, bv = specs(rev)
    o_b = pl.pallas_call(
        functools.partial(_ret_scan_kernel, reverse=True, chunk_decay=tb["chunk_decay"]),
        grid_spec=pltpu.PrefetchScalarGridSpec(
            num_scalar_prefetch=1, grid=(nb,),
            in_specs=[bq, bk, bv, _resident(tb["wq_b"].shape), _resident(tb["wk_b"].shape)],
            out_specs=bv, scratch_shapes=[state]),
        out_shape=out_shape, compiler_params=_cparams(1, 40), name="ret_scan_bwd",
    )(reset_b, q, k, v, tb["wq_b"], tb["wk_b"])

    fwd = lambda i, r: (i, 0)
    bq, bk, bv = specs(fwd)
    return pl.pallas_call(
        functools.partial(_ret_scan_kernel, reverse=False, chunk_decay=tb["chunk_decay"]),
        grid_spec=pltpu.PrefetchScalarGridSpec(
            num_scalar_prefetch=1, grid=(nb,),
            in_specs=[bq, bk, bv, bv, bv, _resident(tb["wq_f"].shape), _resident(tb["wk_f"].shape),
                      _resident(tb["din"].shape)],
            out_specs=bv, scratch_shapes=[state]),
        out_shape=out_shape, compiler_params=_cparams(1, 40), name="ret_scan_fwd",
    )(reset_f, q, k, v, o_b, g, tb["wq_f"], tb["wk_f"], tb["din"])


def _post0_kernel(y_ref, xa_ref, xb_ref, wo_ref, ln_ref, wgu_ref, wd_ref, out_ref, *, n_a):
    x = jnp.where(pl.program_id(0) < n_a, xa_ref[...], xb_ref[...])
    h1 = x + _dot(y_ref[...], wo_ref[...])
    xn = (_rms(h1) * ln_ref[...]).astype(BF16)
    acc = h1
    fc = FFN_DENSE_CHUNK
    for c in range(FFN_DENSE // fc):
        gate = _dot(xn, wgu_ref[:, c * fc:(c + 1) * fc])
        up = _dot(xn, wgu_ref[:, FFN_DENSE + c * fc:FFN_DENSE + (c + 1) * fc])
        act = (_silu(gate) * up).astype(BF16)
        acc = acc + _dot(act, wd_ref[c * fc:(c + 1) * fc, :])
    out_ref[...] = acc


def _post0(y, xa, xb, w_out, ln, w_gu, w_down):
    T = y.shape[0]
    tm = TOK_TILE
    n_a = xa.shape[0] // tm
    tok = lambda n: pl.BlockSpec((tm, n), lambda i: (i, 0))
    return pl.pallas_call(
        functools.partial(_post0_kernel, n_a=n_a),
        grid=(T // tm,),
        in_specs=[tok(2 * D_MODEL), *_two_stream_specs(tm, n_a), _resident(w_out.shape),
                  _resident((1, D_MODEL)), _resident(w_gu.shape), _resident(w_down.shape)],
        out_specs=tok(D_MODEL),
        out_shape=jax.ShapeDtypeStruct((T, D_MODEL), F32),
        compiler_params=_cparams(1, 52),
        name="ret_out_ffn",
    )(y, xa, xb, w_out, ln, w_gu, w_down)


def _ssm_in_kernel(first_ref, last_ref, x_ref, ln_ref, wz_ref, wx_ref, wdh_ref, wdl_ref, bias_ref,
                   cw_ref, cb_ref, gate_ref, xact_ref, dt_ref, xh_ref, ext0_ref, ext1_ref, *, n_tiles):
    i = pl.program_id(0)
    tm = TOK_TILE
    pad = SSM_CONV // 2
    n_gate = SSM_DINNER // SLAB
    n_conv = SSM_CONV_DIM // SLAB

    @pl.when(i == 0)
    def _():
        ext0_ref[...] = jnp.zeros_like(ext0_ref)
        ext1_ref[...] = jnp.zeros_like(ext1_ref)

    cur_tile = jnp.minimum(i, n_tiles - 1)
    conv_tile = jnp.maximum(i - 1, 0)
    keep_prev = jnp.where(first_ref[cur_tile] == 1, 0.0, 1.0).astype(F32)
    keep_next = jnp.where(last_ref[conv_tile] == 1, 0.0, 1.0).astype(F32)

    xh, xl = _split2(_rms(x_ref[...]) * ln_ref[...])
    xh_ref[...] = xh
    x = _dot(xh, wdh_ref[...]) + _dot(xl, wdh_ref[...]) + _dot(xh, wdl_ref[...]) + bias_ref[...]
    dt_ref[...] = jnp.maximum(x, 0.0) + jnp.log1p(jnp.exp(-jnp.abs(x)))

    def run(cur_ref, old_ref):
        def slab(s, with_gate):
            xh = xh_ref[...]
            cur_ref[s, 8:8 + tm, :] = _dot(xh, wx_ref[s])
            old_ref[s, 8 + tm:16 + tm, :] = cur_ref[s, 8:16, :] * keep_next
            cur_ref[s, 0:8, :] = old_ref[s, tm:tm + 8, :] * keep_prev
            for r0 in range(0, tm, CONV_ROWS):
                acc = jnp.broadcast_to(cb_ref[s], (CONV_ROWS, SLAB))
                for j in range(SSM_CONV):
                    lo = 8 - pad + j + r0
                    acc = acc + old_ref[s, lo:lo + CONV_ROWS, :] * cw_ref[s, j:j + 1, :]
                xact_ref[s, r0:r0 + CONV_ROWS, :] = _silu(acc).astype(BF16)
            if with_gate:
                gate_ref[s] = _silu(_dot(xh, wz_ref[s])).astype(BF16)

        def gated(s, carry):
            slab(s, True)
            return carry

        def plain(s, carry):
            slab(s, False)
            return carry

        lax.fori_loop(0, n_gate, gated, 0)
        lax.fori_loop(n_gate, n_conv, plain, 0)

    @pl.when(i % 2 == 0)
    def _():
        run(ext0_ref, ext1_ref)

    @pl.when(i % 2 == 1)
    def _():
        run(ext1_ref, ext0_ref)


def _slabs(w):
    k, n = w.shape
    return w.reshape(k, n // SLAB, SLAB).transpose(1, 0, 2)


def _ssm_in(x, ln, wz, wx, wdh, wdl, bias_row, conv_w, conv_b, first, last):
    T = x.shape[0]
    tm = TOK_TILE
    nt = T // tm
    n_gate = SSM_DINNER // SLAB
    n_conv = SSM_CONV_DIM // SLAB
    wz, wx, conv_w, conv_b = _slabs(wz), _slabs(wx), _slabs(conv_w), _slabs(conv_b)
    cur = lambda n: pl.BlockSpec((tm, n), lambda i, f, l: (jnp.minimum(i, nt - 1), 0))
    return pl.pallas_call(
        functools.partial(_ssm_in_kernel, n_tiles=nt),
        grid_spec=pltpu.PrefetchScalarGridSpec(
            num_scalar_prefetch=2, grid=(nt + 1,),
            in_specs=[cur(D_MODEL), _resident((1, D_MODEL)), _resident(wz.shape), _resident(wx.shape),
                      _resident(wdh.shape), _resident(wdl.shape), _resident(bias_row.shape),
                      _resident(conv_w.shape), _resident(conv_b.shape)],
            out_specs=[pl.BlockSpec((n_gate, tm, SLAB), lambda i, f, l: (0, jnp.minimum(i, nt - 1), 0)),
                       pl.BlockSpec((n_conv, tm, SLAB), lambda i, f, l: (0, jnp.maximum(i - 1, 0), 0)),
                       cur(LANES)],
            scratch_shapes=[pltpu.VMEM((tm, D_MODEL), BF16)]
            + [pltpu.VMEM((n_conv, tm + 16, SLAB), F32)] * 2),
        out_shape=[jax.ShapeDtypeStruct((n_gate, T, SLAB), BF16), jax.ShapeDtypeStruct((n_conv, T, SLAB), BF16),
                   jax.ShapeDtypeStruct((T, LANES), F32)],
        compiler_params=_cparams(1, 52),
        name="ssm_in_conv",
    )(first, last, x, ln, wz, wx, wdh, wdl, bias_row, conv_w, conv_b)


def _ssd_tables():
    r = np.arange(CHUNK)
    tri_l = (r[None, :] <= r[:, None]).astype(np.float32)
    tri_u = (r[None, :] >= r[:, None]).astype(np.float32)
    c = np.arange(SSM_DINNER) // SSM_HEADDIM
    j = np.arange(LANES)
    e_f = (j[:, None] == c[None, :]).astype(np.float32)
    e_b = (j[:, None] == (c[None, :] + SSM_HEADS)).astype(np.float32)
    as_bf16 = lambda x: jnp.asarray(x, dtype=BF16)
    return dict(tri_l3=as_bf16(np.concatenate([tri_l] * 3, axis=1)),
                tri_u3=as_bf16(np.concatenate([tri_u] * 3, axis=1)),
                e2_f=as_bf16(np.concatenate([e_f, e_f], axis=0)),
                e2_b=as_bf16(np.concatenate([e_b, e_b], axis=0)))


def _ssd_chunk_scalars(dt, a_ref, tri_l3_ref, tri_u3_ref):
    a = dt * a_ref[...]
    a3 = jnp.concatenate(_split3(a), axis=0)
    cum_f = _dot(tri_l3_ref[...], a3)
    cum_b = _dot(tri_u3_ref[...], a3)
    lane = lax.broadcasted_iota(jnp.int32, (CHUNK, LANES), 1)
    cum = jnp.where(lane < SSM_HEADS, cum_f, cum_b)
    tot = cum_f[CHUNK - 1:CHUNK, :]
    return cum, tot


def _expand(e2_ref, *rows):
    x = jnp.concatenate(rows, axis=0)
    hi, lo = _split2(x)
    return _dot(jnp.concatenate([hi, lo], axis=1), e2_ref[...])


def _ssd_bwd_kernel(reset_ref, xa_ref, dt_ref, a_ref, tri_l3_ref, tri_u3_ref, e2_ref,
                    yb_ref, state_ref):
    nch = SCAN_BLOCK // CHUNK

    @pl.when(reset_ref[pl.program_id(0)] == 1)
    def _():
        state_ref[...] = jnp.zeros_like(state_ref)

    def chunk_body(ci, carry):
        rows = pl.ds(pl.multiple_of((nch - 1 - ci) * CHUNK, CHUNK), CHUNK)
        dt = dt_ref[rows, :]
        cum, tot = _ssd_chunk_scalars(dt, a_ref, tri_l3_ref, tri_u3_ref)
        ex = _expand(e2_ref, dt * jnp.exp(tot - cum), jnp.exp(cum),
                     jnp.broadcast_to(jnp.exp(tot), (16, LANES)))
        for g in range(SSM_GROUPS):
            gc = slice(g * SSM_GROUP_W, (g + 1) * SSM_GROUP_W)
            bm = xa_ref[SSM_GROUPS, rows, g * SSM_DSTATE:(g + 1) * SSM_DSTATE]
            cm = xa_ref[SSM_GROUPS + 1, rows, g * SSM_DSTATE:(g + 1) * SSM_DSTATE]
            state = state_ref[g]
            yb_ref[rows, gc] = (_dot(cm, state.astype(BF16)) * ex[CHUNK:2 * CHUNK, gc]).astype(BF16)
            xw = (xa_ref[g, rows, :].astype(F32) * ex[0:CHUNK, gc]).astype(BF16)
            state_ref[g] = state * ex[2 * CHUNK:2 * CHUNK + 1, gc] + _dot_tn(bm, xw)
        return carry

    lax.fori_loop(0, nch, chunk_body, 0)


def _ssd_fwd_kernel(reset_ref, xa_ref, dt_ref, gate_ref, yb_ref, a_ref, tri_l3_ref, tri_u3_ref,
                    e2_ref, dskip_ref, nw_ref, out_ref, state_ref, y_ref):
    nch = SCAN_BLOCK // CHUNK

    @pl.when(reset_ref[pl.program_id(0)] == 1)
    def _():
        state_ref[...] = jnp.zeros_like(state_ref)

    def chunk_body(ci, carry):
        rows = pl.ds(pl.multiple_of(ci * CHUNK, CHUNK), CHUNK)
        dt = dt_ref[rows, :]
        cum, tot = _ssd_chunk_scalars(dt, a_ref, tri_l3_ref, tri_u3_ref)
        ex = _expand(e2_ref, dt * jnp.exp(tot - cum), jnp.exp(cum),
                     jnp.broadcast_to(jnp.exp(tot), (16, LANES)))
        row_t = (cum - jnp.log(dt)).T
        li = lax.broadcasted_iota(jnp.int32, (CHUNK, CHUNK), 0)
        si = lax.broadcasted_iota(jnp.int32, (CHUNK, CHUNK), 1)
        lower = li >= si
        first_half = si < SSM_HEADDIM
        for g in range(SSM_GROUPS):
            gc = slice(g * SSM_GROUP_W, (g + 1) * SSM_GROUP_W)
            bm = xa_ref[SSM_GROUPS, rows, g * SSM_DSTATE:(g + 1) * SSM_DSTATE]
            cm = xa_ref[SSM_GROUPS + 1, rows, g * SSM_DSTATE:(g + 1) * SSM_DSTATE]
            cb = _dot_nt(cm, bm)
            heads_per_group = SSM_HEADS // SSM_GROUPS
            for pair in range(heads_per_group // 2):
                mats = []
                for hh in range(2):
                    h = g * heads_per_group + 2 * pair + hh
                    hb = SSM_HEADS + h
                    seg = jnp.where(lower, cum[:, h:h + 1] - row_t[h:h + 1, :],
                                    cum[:, hb:hb + 1] - row_t[hb:hb + 1, :])
                    mats.append((cb * jnp.exp(seg)).astype(BF16))
                pc = slice(2 * pair * SSM_HEADDIM, (2 * pair + 2) * SSM_HEADDIM)
                yy = _dot(jnp.concatenate(mats, axis=0), xa_ref[g, rows, pc])
                y_ref[:, pc] = jnp.where(first_half, yy[0:CHUNK], yy[CHUNK:2 * CHUNK])
            state = state_ref[g]
            xs = xa_ref[g, rows, :].astype(F32)
            y = (y_ref[...] + _dot(cm, state.astype(BF16)) * ex[CHUNK:2 * CHUNK, gc]
                 + yb_ref[rows, gc].astype(F32) + xs * dskip_ref[:, gc])
            y = y * gate_ref[g, rows, :].astype(F32)
            out_ref[rows, gc] = (_rms(y) * nw_ref[:, gc]).astype(BF16)
            xw = (xs * ex[0:CHUNK, gc]).astype(BF16)
            state_ref[g] = state * ex[2 * CHUNK:2 * CHUNK + 1, gc] + _dot_tn(bm, xw)
        return carry

    lax.fori_loop(0, nch, chunk_body, 0)


def _ssd(xa, dt, gate, a_row, dskip, norm_w, reset_f, reset_b):
    T = dt.shape[0]
    nb = T // SCAN_BLOCK
    tb = _ssd_tables()
    state = pltpu.VMEM((SSM_GROUPS, SSM_DSTATE, SSM_GROUP_W), F32)
    consts = [a_row, tb["tri_l3"], tb["tri_u3"]]
    const_specs = [_resident(c.shape) for c in consts]
    out_shape = jax.ShapeDtypeStruct((T, SSM_DINNER), BF16)

    rev = lambda i, r: (nb - 1 - i, 0)
    blk = lambda n, imap: pl.BlockSpec((SCAN_BLOCK, n), imap)
    slabs = lambda a, imap: pl.BlockSpec((a.shape[0], SCAN_BLOCK, SLAB), lambda i, r: (0, imap(i, r)[0], 0))
    y_b = pl.pallas_call(
        _ssd_bwd_kernel,
        grid_spec=pltpu.PrefetchScalarGridSpec(
            num_scalar_prefetch=1, grid=(nb,),
            in_specs=[slabs(xa, rev), blk(LANES, rev)] + const_specs + [_resident(tb["e2_b"].shape)],
            out_specs=blk(SSM_DINNER, rev), scratch_shapes=[state]),
        out_shape=out_shape, compiler_params=_cparams(1, 40), name="ssd_bwd",
    )(reset_b, xa, dt, *consts, tb["e2_b"])

    fwd = lambda i, r: (i, 0)
    return pl.pallas_call(
        _ssd_fwd_kernel,
        grid_spec=pltpu.PrefetchScalarGridSpec(
            num_scalar_prefetch=1, grid=(nb,),
            in_specs=[slabs(xa, fwd), blk(LANES, fwd), slabs(gate, fwd), blk(SSM_DINNER, fwd)]
            + const_specs + [_resident(tb["e2_f"].shape), _resident(dskip.shape), _resident(norm_w.shape)],
            out_specs=blk(SSM_DINNER, fwd),
            scratch_shapes=[state, pltpu.VMEM((CHUNK, SSM_GROUP_W), F32)]),
        out_shape=out_shape, compiler_params=_cparams(1, 40), name="ssd_fwd",
    )(reset_f, xa, dt, gate, y_b, *consts, tb["e2_f"], dskip, norm_w)


def _pack_halves(x):
    n = x.shape[1] // 2
    hi = lax.bitcast_convert_type(x[:, :n].astype(F32), jnp.uint32)
    lo = lax.bitcast_convert_type(x[:, n:].astype(F32), jnp.uint32)
    return hi | (lo >> 16)


def _unpack_halves(w):
    hi = lax.bitcast_convert_type(w & jnp.uint32(0xFFFF0000), F32)
    lo = lax.bitcast_convert_type(w << 16, F32)
    return hi, lo


def _router_kernel(y_ref, h_ref, wo_ref, ln_ref, wh_ref, wl_ref, b_ref, tri_ref,
                   h3_ref, xn_ref, info_ref, count_ref, base_ref):
    @pl.when(pl.program_id(0) == 0)
    def _():
        base_ref[...] = jnp.zeros_like(base_ref)

    h3 = h_ref[...] + _dot(y_ref[...], wo_ref[...])
    h3_ref[...] = h3
    xh, xl = _split2(_rms(h3) * ln_ref[...])
    xn_ref[...] = _pack_halves(xh)
    logits = _dot(xh, wh_ref[...]) + _dot(xl, wh_ref[...]) + _dot(xh, wl_ref[...]) + b_ref[...]
    col = lax.broadcasted_iota(jnp.int32, logits.shape, 1)
    m1 = jnp.max(logits, axis=-1, keepdims=True)
    i1 = jnp.min(jnp.where(logits == m1, col, LANES), axis=-1, keepdims=True)
    rest = jnp.where(col == i1, -jnp.inf, logits)
    m2 = jnp.max(rest, axis=-1, keepdims=True)
    i2 = jnp.min(jnp.where(rest == m2, col, LANES), axis=-1, keepdims=True)
    e = jnp.exp(m2 - m1)
    g1 = 1.0 / (1.0 + e)
    g2 = e / (1.0 + e)
    oh1 = col == i1
    oh2 = col == i2
    oh = jnp.where(oh1 | oh2, 1.0, 0.0)
    base = base_ref[...]
    prefix = _dot(tri_ref[...], oh.astype(BF16)) + base
    r1 = jnp.sum(jnp.where(oh1, prefix, 0.0), axis=-1, keepdims=True)
    r2 = jnp.sum(jnp.where(oh2, prefix, 0.0), axis=-1, keepdims=True)
    base = base + jnp.sum(oh, axis=0, keepdims=True)
    count_ref[...] = base
    base_ref[...] = base
    fields = [i1.astype(F32), i2.astype(F32), g1, g2, r1, r2]
    info = jnp.zeros(logits.shape, F32)
    for c, f in enumerate(fields):
        info = jnp.where(col == c, f, info)
    info_ref[...] = info


def _router(y, h, w_out, ln, wh, wl, b):
    T = h.shape[0]
    tb = MOE_TOK
    nb = T // tb
    r = np.arange(tb)
    tri = jnp.asarray((r[None, :] < r[:, None]).astype(np.float32), dtype=BF16)
    tok = lambda n: pl.BlockSpec((tb, n), lambda i: (i, 0))
    return pl.pallas_call(
        _router_kernel,
        grid=(nb,),
        in_specs=[tok(SSM_DINNER), tok(D_MODEL), _resident(w_out.shape), _resident((1, D_MODEL)),
                  _resident(wh.shape), _resident(wl.shape), _resident(b.shape), _resident(tri.shape)],
        out_specs=[tok(D_MODEL), tok(D_MODEL // 2), tok(LANES), pl.BlockSpec((1, LANES), lambda i: (0, 0))],
        out_shape=[jax.ShapeDtypeStruct((T, D_MODEL), F32), jax.ShapeDtypeStruct((T, D_MODEL // 2), jnp.uint32),
                   jax.ShapeDtypeStruct((T, LANES), F32), jax.ShapeDtypeStruct((1, LANES), F32)],
        scratch_shapes=[pltpu.VMEM((1, LANES), F32)],
        compiler_params=_cparams(1, 32),
        name="ssm_out_router",
    )(y, h, w_out, ln, wh, wl, b, tri)


def _sc_gather(table, idx):
    n_rows, width = idx.shape[0], table.shape[1]
    workers = SC_CORES * SC_SUBCORES
    nbuf = 2
    assert n_rows % (workers * SC_CHUNK * nbuf) == 0
    per_worker = n_rows // workers
    mesh = plsc.VectorSubcoreMesh(core_axis_name="c", subcore_axis_name="s")

    @functools.partial(
        pl.kernel, mesh=mesh,
        out_type=jax.ShapeDtypeStruct((n_rows, width), table.dtype),
        scratch_types=[pltpu.VMEM((nbuf, SC_CHUNK), jnp.int32),
                       pltpu.VMEM((nbuf, SC_CHUNK, width), table.dtype),
                       pltpu.SemaphoreType.DMA((nbuf,)),
                       pltpu.SemaphoreType.DMA((nbuf,))],
    )
    def gather_rows(table_hbm, idx_hbm, out_hbm, idx_v, rows_v, gsem, wsem):
        base = (lax.axis_index("s") * SC_CORES + lax.axis_index("c")) * per_worker

        def gather(c, b):
            off = pl.multiple_of(base + c * SC_CHUNK, 8)
            pltpu.sync_copy(idx_hbm.at[pl.ds(off, SC_CHUNK)], idx_v.at[b])
            return pltpu.make_async_copy(table_hbm.at[idx_v.at[b]], rows_v.at[b], gsem.at[b])

        def write(c, b):
            off = pl.multiple_of(base + c * SC_CHUNK, 8)
            return pltpu.make_async_copy(rows_v.at[b], out_hbm.at[pl.ds(off, SC_CHUNK)], wsem.at[b])

        @pl.loop(0, per_worker // SC_CHUNK, step=nbuf)
        def _(c0):
            copies = [gather(c0 + b, b) for b in range(nbuf)]
            for cp in copies:
                cp.start()
            writes = []
            for b, cp in enumerate(copies):
                cp.wait()
                writes.append(write(c0 + b, b))
                writes[-1].start()
            for wr in writes:
                wr.wait()

    return gather_rows(table, idx)


def _sc_scatter_pairs(table, pos, n_out):
    n_rows, width = table.shape
    workers = SC_CORES * SC_SUBCORES
    nbuf = 2
    assert n_rows % (workers * SC_CHUNK * nbuf) == 0
    per_worker = n_rows // workers
    mesh = plsc.VectorSubcoreMesh(core_axis_name="c", subcore_axis_name="s")

    @functools.partial(
        pl.kernel, mesh=mesh,
        out_type=jax.ShapeDtypeStruct((n_out, width), table.dtype),
        scratch_types=[pltpu.VMEM((nbuf, 2, SC_CHUNK), jnp.int32),
                       pltpu.VMEM((nbuf, SC_CHUNK, width), table.dtype),
                       pltpu.SemaphoreType.DMA((nbuf,)),
                       pltpu.SemaphoreType.DMA((nbuf,))],
    )
    def scatter_rows(table_hbm, pos_hbm, out_hbm, idx_v, rows_v, rsem, wsem):
        base = (lax.axis_index("s") * SC_CORES + lax.axis_index("c")) * per_worker

        def read(c, b):
            off = pl.multiple_of(base + c * SC_CHUNK, 8)
            return pltpu.make_async_copy(table_hbm.at[pl.ds(off, SC_CHUNK)], rows_v.at[b], rsem.at[b])

        def write(b, j):
            return pltpu.make_async_copy(rows_v.at[b], out_hbm.at[idx_v.at[b, j]], wsem.at[b])

        @pl.loop(0, per_worker // SC_CHUNK, step=nbuf)
        def _(c0):
            reads = [read(c0 + b, b) for b in range(nbuf)]
            for rd in reads:
                rd.start()
            for b in range(nbuf):
                pltpu.sync_copy(pos_hbm.at[base // SC_CHUNK + c0 + b], idx_v.at[b])
            writes = []
            for b, rd in enumerate(reads):
                rd.wait()
                for j in range(2):
                    writes.append(write(b, j))
                    writes[-1].start()
            for wr in writes:
                wr.wait()

    return scatter_rows(table, pos)


def _expert_kernel(texp_ref, trows_ref, x_ref, wg_ref, wu_ref, wd_ref, out_ref, acc_ref, xb_ref):
    i = pl.program_id(0)
    f = pl.program_id(1)
    nf = pl.num_programs(1)
    n = D_MODEL // 2

    @pl.when(trows_ref[i] > 0)
    def _():
        @pl.when(f == 0)
        def _():
            acc_ref[...] = jnp.zeros_like(acc_ref)
            row = lax.broadcasted_iota(jnp.int32, x_ref.shape, 0)
            hi, lo = _unpack_halves(jnp.where(row < trows_ref[i], x_ref[...], jnp.uint32(0)))
            xb_ref[:, :n] = hi.astype(BF16)
            xb_ref[:, n:] = lo.astype(BF16)

        x = xb_ref[...]
        act = (_silu(_dot(x, wg_ref[0].astype(BF16))) * _dot(x, wu_ref[0].astype(BF16))).astype(BF16)
        acc_ref[...] += _dot(act, wd_ref[0].astype(BF16))

        @pl.when(f == nf - 1)
        def _():
            out_ref[...] = _pack_halves(acc_ref[...].astype(BF16))

    @pl.when((trows_ref[i] == 0) & (f == nf - 1))
    def _():
        out_ref[...] = jnp.zeros_like(out_ref)


def _experts(xs, w_gu, w_down, tile_expert, tile_rows):
    n_tiles = xs.shape[0] // MOE_SLOT
    nf = FFN_EXPERT // MOE_FC
    return pl.pallas_call(
        _expert_kernel,
        grid_spec=pltpu.PrefetchScalarGridSpec(
            num_scalar_prefetch=2, grid=(n_tiles, nf),
            in_specs=[pl.BlockSpec((MOE_SLOT, D_MODEL // 2), lambda i, f, te, tv: (i, 0)),
                      pl.BlockSpec((1, D_MODEL, MOE_FC), lambda i, f, te, tv: (te[i], 0, f)),
                      pl.BlockSpec((1, D_MODEL, MOE_FC), lambda i, f, te, tv: (te[i], 0, nf + f)),
                      pl.BlockSpec((1, MOE_FC, D_MODEL), lambda i, f, te, tv: (te[i], f, 0))],
            out_specs=pl.BlockSpec((MOE_SLOT, D_MODEL // 2), lambda i, f, te, tv: (i, 0)),
            scratch_shapes=[pltpu.VMEM((MOE_SLOT, D_MODEL), F32), pltpu.VMEM((MOE_SLOT, D_MODEL), BF16)]),
        out_shape=jax.ShapeDtypeStruct(xs.shape, jnp.uint32),
        compiler_params=_cparams(2, 48),
        name="moe_experts",
    )(tile_expert, tile_rows, xs, w_gu, w_gu, w_down)


def _finalize_kernel(y0_ref, y1_ref, info_ref, h_ref, fn_ref, outa_ref, outb_ref, *, nb_a):
    i = pl.program_id(0)
    n = D_MODEL // 2
    info = info_ref[...]
    g0 = info[:, 2:3]
    g1 = info[:, 3:4]
    a_hi, a_lo = _unpack_halves(y0_ref[...])
    b_hi, b_lo = _unpack_halves(y1_ref[...])
    left = h_ref[:, :n] + g0 * a_hi + g1 * b_hi
    right = h_ref[:, n:] + g0 * a_lo + g1 * b_lo
    ms = (jnp.sum(left * left, axis=-1, keepdims=True)
          + jnp.sum(right * right, axis=-1, keepdims=True)) * (1.0 / D_MODEL)
    scale = lax.rsqrt(ms + EPS)
    left = left * scale * fn_ref[:, :n]
    right = right * scale * fn_ref[:, n:]

    @pl.when(i < nb_a)
    def _():
        outa_ref[:, :n] = left
        outa_ref[:, n:] = right

    @pl.when(i >= nb_a)
    def _():
        outb_ref[:, :n] = left
        outb_ref[:, n:] = right


def _finalize(yg, info, h3, final_norm, n_a):
    T = h3.shape[0]
    tb = MOE_TOK
    nb = T // tb
    nb_a = n_a // tb
    tok = lambda w: pl.BlockSpec((tb, w), lambda i: (i, 0))
    return pl.pallas_call(
        functools.partial(_finalize_kernel, nb_a=nb_a),
        grid=(nb,),
        in_specs=[tok(D_MODEL // 2), pl.BlockSpec((tb, D_MODEL // 2), lambda i: (i + nb, 0)),
                  tok(LANES), tok(D_MODEL), _resident((1, D_MODEL))],
        out_specs=[pl.BlockSpec((tb, D_MODEL), lambda i: (jnp.minimum(i, nb_a - 1), 0)),
                   pl.BlockSpec((tb, D_MODEL), lambda i: (jnp.maximum(i - nb_a, 0), 0))],
        out_shape=[jax.ShapeDtypeStruct((n_a, D_MODEL), F32), jax.ShapeDtypeStruct((T - n_a, D_MODEL), F32)],
        compiler_params=_cparams(1, 32),
        name="moe_finalize",
    )(yg, yg, info, h3, final_norm)


def _moe(y_ssm, h2, w_out, ln, wr_hi, wr_lo, rb, w_gu, w_down, final_norm, n_a):
    T = h2.shape[0]
    n_tiles = 2 * T // MOE_SLOT + N_EXPERTS
    h3, xn, info, counts = _router(y_ssm, h2, w_out, ln, wr_hi, wr_lo, rb)

    counts = counts[0, :N_EXPERTS].astype(jnp.int32)
    tiles_e = (counts + MOE_SLOT - 1) // MOE_SLOT
    tiles_cum = jnp.cumsum(tiles_e)
    gstart = (tiles_cum - tiles_e) * MOE_SLOT
    routed = info[:, :8].astype(jnp.int32)
    start_of = lambda e: jnp.sum(jnp.where(e[:, None] == jnp.arange(N_EXPERTS), gstart[None, :], 0), axis=1)
    pos0 = start_of(routed[:, 0]) + routed[:, 4]
    pos1 = start_of(routed[:, 1]) + routed[:, 5]
    tile_ids = jnp.arange(n_tiles, dtype=jnp.int32)
    tile_expert = jnp.minimum(jnp.searchsorted(tiles_cum, tile_ids, side="right"), N_EXPERTS - 1).astype(jnp.int32)
    tile_rows = jnp.clip(counts[tile_expert] - (tile_ids * MOE_SLOT - gstart[tile_expert]), 0, MOE_SLOT)
    tile_rows = jnp.where(tile_ids < tiles_cum[-1], tile_rows, 0).astype(jnp.int32)
    pos = jnp.stack([pos0.reshape(-1, SC_CHUNK), pos1.reshape(-1, SC_CHUNK)], axis=1)

    xs = _sc_scatter_pairs(xn, pos, n_tiles * MOE_SLOT)
    ys = _experts(xs, w_gu, w_down, tile_expert, tile_rows)
    yg = _sc_gather(ys, jnp.concatenate([pos0, pos1]))
    return _finalize(yg, info, h3, final_norm, n_a)


def _seq_flags(seq_lens, block):
    first, last = [], []
    for n in seq_lens:
        nb = n // block
        first += [1] + [0] * (nb - 1)
        last += [0] * (nb - 1) + [1]
    return np.asarray(first, np.int32), np.asarray(last, np.int32)


def _trunk(xa, xb, seq_lens, ln_mix0, ret_w_in, ret_w_out, ln_ffn0, ffn_w_gu, ffn_w_down, ln_mix1, ssm_w_in,
           ssm_conv_w, ssm_conv_b, ssm_dt_bias, ssm_A_log, ssm_D, ssm_norm_w, ssm_w_out, ln_ffn1,
           moe_router_w, moe_router_b, moe_w_gu, moe_w_down, final_norm):
    assert all(n % SCAN_BLOCK == 0 and n % MOE_TOK == 0 and n % TOK_TILE == 0 for n in seq_lens)
    assert TOK_TILE == SCAN_BLOCK
    row = lambda v: v.astype(F32).reshape(1, -1)
    first, last = _seq_flags(seq_lens, SCAN_BLOCK)
    reset_f = jnp.asarray(first)
    reset_b = jnp.asarray(last[::-1].copy())

    half = RET_DK // 2
    inv = ROPE_BASE ** (-jnp.arange(half, dtype=F32) / half)
    ang = jnp.arange(max(seq_lens)).astype(F32)[:, None] * inv[None, :]
    pos_block = jnp.asarray(np.concatenate([np.arange(n // TOK_TILE) for n in seq_lens]).astype(np.int32))
    q, k, v, g = _ret_in(xa, xb, row(ln_mix0), ret_w_in.astype(BF16), jnp.cos(ang), jnp.sin(ang), pos_block)
    y = _ret_scan(q, k, v, g, reset_f, reset_b)
    h2 = _post0(y, xa, xb, ret_w_out.astype(BF16), row(ln_ffn0), ffn_w_gu.astype(BF16),
                ffn_w_down.astype(BF16))

    wz = ssm_w_in[:, :SSM_DINNER].astype(BF16)
    wx = ssm_w_in[:, SSM_DINNER:SSM_DINNER + SSM_CONV_DIM].astype(BF16)
    wdt = jnp.pad(ssm_w_in[:, SSM_DINNER + SSM_CONV_DIM:], ((0, 0), (0, LANES - 2 * SSM_HEADS)))
    wdh, wdl = _split2(wdt)
    pad_row = lambda v: jnp.pad(v.astype(F32).reshape(1, -1), ((0, 0), (0, LANES - 2 * SSM_HEADS)))
    gate, xact, dt = _ssm_in(h2, row(ln_mix1), wz, wx, wdh, wdl, pad_row(ssm_dt_bias),
                             ssm_conv_w.reshape(SSM_CONV, SSM_CONV_DIM).astype(F32), row(ssm_conv_b),
                             jnp.asarray(first), jnp.asarray(last))
    a_row = pad_row(-jnp.exp(ssm_A_log.astype(F32)))
    dskip = jnp.repeat(ssm_D.astype(F32), SSM_HEADDIM).reshape(1, SSM_DINNER)
    y = _ssd(xact, dt, gate, a_row, dskip, row(ssm_norm_w), reset_f, reset_b)

    wr = jnp.pad(moe_router_w.astype(F32), ((0, 0), (0, LANES - N_EXPERTS)))
    wr_hi, wr_lo = _split2(wr)
    rb = jnp.pad(moe_router_b.astype(F32).reshape(1, -1), ((0, 0), (0, LANES - N_EXPERTS)),
                 constant_values=-1e30)
    return _moe(y, h2, ssm_w_out.astype(BF16), row(ln_ffn1), wr_hi, wr_lo, rb,
                moe_w_gu, moe_w_down, row(final_norm), xa.shape[0])


def kernel(x_prompt, x_sample, ln_mix0, ret_w_in, ret_w_out, ln_ffn0, ffn_w_gu, ffn_w_down, ln_mix1, ssm_w_in,
           ssm_conv_w, ssm_conv_b, ssm_dt_bias, ssm_A_log, ssm_D, ssm_norm_w, ssm_w_out, ln_ffn1, moe_router_w,
           moe_router_b, moe_w_gu, moe_w_down, final_norm):
    seq_lens = [x_prompt.shape[1]] * x_prompt.shape[0] + [x_sample.shape[1]] * x_sample.shape[0]
    out_a, out_b = _trunk(x_prompt.reshape(-1, D_MODEL), x_sample.reshape(-1, D_MODEL), seq_lens, ln_mix0, ret_w_in, ret_w_out, ln_ffn0, ffn_w_gu, ffn_w_down, ln_mix1, ssm_w_in,
                 ssm_conv_w, ssm_conv_b, ssm_dt_bias, ssm_A_log, ssm_D, ssm_norm_w, ssm_w_out, ln_ffn1,
                 moe_router_w, moe_router_b, moe_w_gu, moe_w_down, final_norm)
    return (out_a.reshape(x_prompt.shape), out_b.reshape(x_sample.shape))
```

```python
import functools

import numpy as np
import jax
import jax.numpy as jnp
from jax import lax
from jax.experimental import pallas as pl
from jax.experimental.pallas import tpu as pltpu
from jax.experimental.pallas import tpu_sc as plsc

F32 = jnp.float32
BF16 = jnp.bfloat16

D_MODEL = 1024
EPS = 1e-6
RET_HEADS = 4
RET_DK = 256
RET_DV = 512
ROPE_BASE = 10000.0
SSM_DINNER = 2048
SSM_HEADDIM = 64
SSM_HEADS = 32
SSM_GROUPS = 4
SSM_DSTATE = 128
SSM_CONV = 5
SSM_CONV_DIM = 3072
SSM_GROUP_W = SSM_DINNER // SSM_GROUPS
SLAB = SSM_GROUP_W
CONV_ROWS = 128
CONV_WINDOW = 256
HALO = 16
FFN_DENSE = 2816
FFN_DENSE_CHUNK = 1408
N_EXPERTS = 8
FFN_EXPERT = 3584

CHUNK = 128
RET_CHUNK = 256
LANES = 128
TOK_TILE = 512
SCAN_BLOCK = 512
MOE_TOK = 512
MOE_SLOT = 1024
MOE_FC = 896
ROUTE_COLS = 8
SC_CORES = 2
SC_SUBCORES = 16
SC_CHUNK = 64


def _cparams(n_axes, vmem_mb):
    return pltpu.CompilerParams(dimension_semantics=("arbitrary",) * n_axes,
                                vmem_limit_bytes=vmem_mb << 20)


def _resident(shape):
    nd = len(shape)
    return pl.BlockSpec(shape, lambda *_: (0,) * nd, pipeline_mode=pl.Buffered(1))


def _rms(x):
    return x * lax.rsqrt(jnp.mean(x * x, axis=-1, keepdims=True) + EPS)


def _silu(x):
    return x * jax.nn.sigmoid(x)


def _dot(a, b):
    return jnp.dot(a, b, preferred_element_type=F32)


def _dot_nt(a, b):
    return lax.dot_general(a, b, (((1,), (1,)), ((), ())), preferred_element_type=F32)


def _dot_tn(a, b):
    return lax.dot_general(a, b, (((0,), (0,)), ((), ())), preferred_element_type=F32)


def _split2(x):
    hi = x.astype(BF16)
    lo = (x - hi.astype(F32)).astype(BF16)
    return hi, lo


def _split3(x):
    hi = x.astype(BF16)
    r = x - hi.astype(F32)
    mid = r.astype(BF16)
    lo = (r - mid.astype(F32)).astype(BF16)
    return hi, mid, lo


def _two_stream_specs(tm, n_a):
    return (pl.BlockSpec((tm, D_MODEL), lambda i, *_: (jnp.minimum(i, n_a - 1), 0)),
            pl.BlockSpec((tm, D_MODEL), lambda i, *_: (jnp.maximum(i - n_a, 0), 0)))


def _ret_in_kernel(pos_ref, xa_ref, xb_ref, ln_ref, w_ref, cos_ref, sin_ref, q_ref, k_ref, v_ref, g_ref, *, n_a):
    x = jnp.where(pl.program_id(0) < n_a, xa_ref[...], xb_ref[...])
    xn = (_rms(x) * ln_ref[...]).astype(BF16)
    cos = cos_ref[...]
    sin = sin_ref[...]
    half = RET_DK // 2

    def rotary(dst, col0, scale):
        for h in range(RET_HEADS):
            c = col0 + h * RET_DK
            p = _dot(xn, w_ref[:, c:c + RET_DK])
            p1, p2 = p[:, :half], p[:, half:]
            dst[:, h * RET_DK:h * RET_DK + half] = ((p1 * cos - p2 * sin) * scale).astype(BF16)
            dst[:, h * RET_DK + half:(h + 1) * RET_DK] = ((p1 * sin + p2 * cos) * scale).astype(BF16)

    rotary(q_ref, 0, 1.0)
    rotary(k_ref, D_MODEL, RET_DK ** -0.5)
    for j in range(2 * D_MODEL // 512):
        v_ref[:, j * 512:(j + 1) * 512] = _dot(
            xn, w_ref[:, 2 * D_MODEL + j * 512:2 * D_MODEL + (j + 1) * 512]).astype(BF16)
        g_ref[:, j * 512:(j + 1) * 512] = _dot(
            xn, w_ref[:, 4 * D_MODEL + j * 512:4 * D_MODEL + (j + 1) * 512]).astype(BF16)


def _ret_in(xa, xb, ln, w_in, cos, sin, pos_block):
    T = xa.shape[0] + xb.shape[0]
    tm = TOK_TILE
    n_a = xa.shape[0] // tm
    tok = lambda n: pl.BlockSpec((tm, n), lambda i, pb: (i, 0))
    rot = pl.BlockSpec((tm, LANES), lambda i, pb: (pb[i], 0))
    return pl.pallas_call(
        functools.partial(_ret_in_kernel, n_a=n_a),
        grid_spec=pltpu.PrefetchScalarGridSpec(
            num_scalar_prefetch=1, grid=(T // tm,),
            in_specs=[*_two_stream_specs(tm, n_a), _resident((1, D_MODEL)), _resident(w_in.shape), rot, rot],
            out_specs=[tok(D_MODEL), tok(D_MODEL), tok(2 * D_MODEL), tok(2 * D_MODEL)]),
        out_shape=[jax.ShapeDtypeStruct((T, D_MODEL), BF16), jax.ShapeDtypeStruct((T, D_MODEL), BF16),
                   jax.ShapeDtypeStruct((T, 2 * D_MODEL), BF16), jax.ShapeDtypeStruct((T, 2 * D_MODEL), BF16)],
        compiler_params=_cparams(1, 48),
        name="ret_in",
    )(pos_block, xa, xb, ln, w_in, cos, sin)


def _ret_scan_kernel(reset_ref, *refs, reverse, chunk_decay):
    if reverse:
        q_ref, k_ref, v_ref, wq_ref, wk_ref, out_ref, state_ref = refs
    else:
        q_ref, k_ref, v_ref, ob_ref, g_ref, wq_ref, wk_ref, din_ref, out_ref, state_ref = refs
    nch = SCAN_BLOCK // RET_CHUNK

    @pl.when(reset_ref[pl.program_id(0)] == 1)
    def _():
        state_ref[...] = jnp.zeros_like(state_ref)

    def chunk_body(ci, carry):
        cj = nch - 1 - ci if reverse else ci
        rows = pl.ds(pl.multiple_of(cj * RET_CHUNK, RET_CHUNK), RET_CHUNK)
        for h in range(RET_HEADS):
            kcols = slice(h * RET_DK, (h + 1) * RET_DK)
            vcols = slice(h * RET_DV, (h + 1) * RET_DV)
            qh = q_ref[rows, kcols]
            kh = k_ref[rows, kcols]
            vh = v_ref[rows, vcols]
            state = state_ref[h]
            o = _dot(qh, state.astype(BF16)) * wq_ref[h]
            if reverse:
                out_ref[rows, vcols] = o.astype(BF16)
            else:
                s = _dot_nt(qh, kh) * din_ref[h]
                o = o + _dot(s.astype(BF16), vh) + ob_ref[rows, vcols].astype(F32)
                gate = g_ref[rows, vcols].astype(F32)
                out_ref[rows, vcols] = (_silu(gate) * _rms(o)).astype(BF16)
            ks = (kh.astype(F32) * wk_ref[h]).astype(BF16)
            state_ref[h] = state * chunk_decay[h] + _dot_tn(ks, vh)
        return carry

    lax.fori_loop(0, nch, chunk_body, 0)


def _ret_tables():
    lg = np.log1p(-np.power(2.0, -5.0 - np.arange(RET_HEADS, dtype=np.float32))).astype(np.float32)
    c = RET_CHUNK
    a = np.arange(c, dtype=np.float32)
    col = lambda e, w: np.broadcast_to(np.exp(lg[:, None, None] * e[None, :, None]),
                                       (RET_HEADS, c, w)).astype(np.float32)
    dist = np.abs(a[:, None] - a[None, :])
    return dict(
        din=np.exp(lg[:, None, None] * dist[None]).astype(np.float32),
        wq_f=col(a + 1.0, RET_DV), wk_f=col(c - 1.0 - a, RET_DK),
        wq_b=col(c - a, RET_DV), wk_b=col(a, RET_DK),
        chunk_decay=tuple(float(v) for v in np.exp(lg * c)),
    )


def _ret_scan(q, k, v, g, reset_f, reset_b):
    T = q.shape[0]
    nb = T // SCAN_BLOCK
    tb = _ret_tables()
    state = pltpu.VMEM((RET_HEADS, RET_DK, RET_DV), F32)
    out_shape = jax.ShapeDtypeStruct((T, 2 * D_MODEL), BF16)

    def specs(imap):
        blk = lambda n: pl.BlockSpec((SCAN_BLOCK, n), imap)
        return blk(D_MODEL), blk(D_MODEL), blk(2 * D_MODEL)

    rev = lambda i, r: (nb - 1 - i, 0)
    bq, bk, bv = specs(rev)
    o_b = pl.pallas_call(
        functools.partial(_ret_scan_kernel, reverse=True, chunk_decay=tb["chunk_decay"]),
        grid_spec=pltpu.PrefetchScalarGridSpec(
            num_scalar_prefetch=1, grid=(nb,),
            in_specs=[bq, bk, bv, _resident(tb["wq_b"].shape), _resident(tb["wk_b"].shape)],
            out_specs=bv, scratch_shapes=[state]),
        out_shape=out_shape, compiler_params=_cparams(1, 40), name="ret_scan_bwd",
    )(reset_b, q, k, v, tb["wq_b"], tb["wk_b"])

    fwd = lambda i, r: (i, 0)
    bq, bk, bv = specs(fwd)
    return pl.pallas_call(
        functools.partial(_ret_scan_kernel, reverse=False, chunk_decay=tb["chunk_decay"]),
        grid_spec=pltpu.PrefetchScalarGridSpec(
            num_scalar_prefetch=1, grid=(nb,),
            in_specs=[bq, bk, bv, bv, bv, _resident(tb["wq_f"].shape), _resident(tb["wk_f"].shape),
                      _resident(tb["din"].shape)],
            out_specs=bv, scratch_shapes=[state]),
        out_shape=out_shape, compiler_params=_cparams(1, 40), name="ret_scan_fwd",
    )(reset_f, q, k, v, o_b, g, tb["wq_f"], tb["wk_f"], tb["din"])


def _post0_kernel(y_ref, xa_ref, xb_ref, wo_ref, ln_ref, wgu_ref, wd_ref, out_ref, *, n_a):
    x = jnp.where(pl.program_id(0) < n_a, xa_ref[...], xb_ref[...])
    h1 = x + _dot(y_ref[...], wo_ref[...])
    xn = (_rms(h1) * ln_ref[...]).astype(BF16)
    acc = h1
    fc = FFN_DENSE_CHUNK
    for c in range(FFN_DENSE // fc):
        gate = _dot(xn, wgu_ref[:, c * fc:(c + 1) * fc])
        up = _dot(xn, wgu_ref[:, FFN_DENSE + c * fc:FFN_DENSE + (c + 1) * fc])
        act = (_silu(gate) * up).astype(BF16)
        acc = acc + _dot(act, wd_ref[c * fc:(c + 1) * fc, :])
    out_ref[...] = acc


def _post0(y, xa, xb, w_out, ln, w_gu, w_down):
    T = y.shape[0]
    tm = TOK_TILE
    n_a = xa.shape[0] // tm
    tok = lambda n: pl.BlockSpec((tm, n), lambda i: (i, 0))
    return pl.pallas_call(
        functools.partial(_post0_kernel, n_a=n_a),
        grid=(T // tm,),
        in_specs=[tok(2 * D_MODEL), *_two_stream_specs(tm, n_a), _resident(w_out.shape),
                  _resident((1, D_MODEL)), _resident(w_gu.shape), _resident(w_down.shape)],
        out_specs=tok(D_MODEL),
        out_shape=jax.ShapeDtypeStruct((T, D_MODEL), F32),
        compiler_params=_cparams(1, 52),
        name="ret_out_ffn",
    )(y, xa, xb, w_out, ln, w_gu, w_down)


def _ssm_in_kernel(x_ref, ln_ref, wz_ref, wx_ref, wd_ref, wdh_ref, bias_ref, gate_ref, xbc_ref, dt_ref):
    xh, xl = _split2(_rms(x_ref[...]) * ln_ref[...])
    for j in range(SSM_DINNER // SLAB):
        gate_ref[j] = _silu(_dot(xh, wz_ref[:, j * SLAB:(j + 1) * SLAB])).astype(BF16)
    for j in range(SSM_CONV_DIM // SLAB):
        xbc_ref[:, j * SLAB:(j + 1) * SLAB] = _dot(xh, wx_ref[:, j * SLAB:(j + 1) * SLAB]).astype(BF16)
    both = _dot(xh, wd_ref[...])
    x = both[:, :LANES] + both[:, LANES:] + _dot(xl, wdh_ref[...]) + bias_ref[...]
    dt_ref[...] = jnp.maximum(x, 0.0) + jnp.log1p(jnp.exp(-jnp.abs(x)))


def _ssm_in(x, ln, wz, wx, wdh, wdl, bias_row):
    T = x.shape[0]
    tm = TOK_TILE
    n_gate = SSM_DINNER // SLAB
    wd = jnp.concatenate([wdh, wdl], axis=1)
    tok = lambda n: pl.BlockSpec((tm, n), lambda i: (i, 0))
    return pl.pallas_call(
        _ssm_in_kernel,
        grid=(T // tm,),
        in_specs=[tok(D_MODEL), _resident((1, D_MODEL)), _resident(wz.shape), _resident(wx.shape),
                  _resident(wd.shape), _resident(wdh.shape), _resident(bias_row.shape)],
        out_specs=[pl.BlockSpec((n_gate, tm, SLAB), lambda i: (0, i, 0)), tok(SSM_CONV_DIM), tok(LANES)],
        out_shape=[jax.ShapeDtypeStruct((n_gate, T, SLAB), BF16), jax.ShapeDtypeStruct((T, SSM_CONV_DIM), BF16),
                   jax.ShapeDtypeStruct((T, LANES), F32)],
        compiler_params=_cparams(1, 40),
        name="ssm_in",
    )(x, ln, wz, wx, wd, wdh, bias_row)


def _conv_kernel(first_ref, last_ref, prev_ref, main_ref, next_ref, shift_ref, w_ref, b_ref, out_ref, ext_ref):
    i = pl.program_id(0)
    rb = SCAN_BLOCK
    ext_rows = ext_ref.shape[0]

    @pl.when(i == 0)
    def _():
        ext_ref[rb + 2 * HALO:ext_rows, :] = jnp.zeros((ext_rows - rb - 2 * HALO, SSM_CONV_DIM), BF16)

    zero = jnp.zeros((HALO, SSM_CONV_DIM), BF16)
    ext_ref[0:HALO, :] = jnp.where(first_ref[i] == 1, zero, prev_ref[...])
    ext_ref[HALO:HALO + rb, :] = main_ref[...]
    ext_ref[HALO + rb:2 * HALO + rb, :] = jnp.where(last_ref[i] == 1, zero, next_ref[...])
    for s in range(SSM_CONV_DIM // SLAB):
        cs = slice(s * SLAB, (s + 1) * SLAB)
        for r0 in range(0, rb, CONV_ROWS):
            taps = _dot(shift_ref[...], ext_ref[r0:r0 + CONV_WINDOW, cs])
            acc = jnp.broadcast_to(b_ref[:, cs], (CONV_ROWS, SLAB))
            for j in range(SSM_CONV):
                acc = acc + taps[j * CONV_ROWS:(j + 1) * CONV_ROWS] * w_ref[j:j + 1, cs]
            out_ref[s, r0:r0 + CONV_ROWS, :] = _silu(acc).astype(BF16)


def _conv(xbc, conv_w, conv_b, first, last):
    T = xbc.shape[0]
    rb = SCAN_BLOCK
    nb = T // rb
    per = rb // HALO
    nh = T // HALO
    r = np.arange(CONV_ROWS)
    shift = np.zeros((SSM_CONV * CONV_ROWS, CONV_WINDOW), np.float32)
    for j in range(SSM_CONV):
        shift[j * CONV_ROWS + r, HALO + r + j - SSM_CONV // 2] = 1.0
    shift = jnp.asarray(shift, dtype=BF16)
    ext_rows = rb - CONV_ROWS + CONV_WINDOW
    return pl.pallas_call(
        _conv_kernel,
        grid_spec=pltpu.PrefetchScalarGridSpec(
            num_scalar_prefetch=2, grid=(nb,),
            in_specs=[
                pl.BlockSpec((HALO, SSM_CONV_DIM), lambda i, f, l: (jnp.maximum(i * per - 1, 0), 0)),
                pl.BlockSpec((rb, SSM_CONV_DIM), lambda i, f, l: (i, 0)),
                pl.BlockSpec((HALO, SSM_CONV_DIM), lambda i, f, l: (jnp.minimum((i + 1) * per, nh - 1), 0)),
                _resident(shift.shape), _resident(conv_w.shape), _resident(conv_b.shape)],
            out_specs=pl.BlockSpec((SSM_CONV_DIM // SLAB, rb, SLAB), lambda i, f, l: (0, i, 0)),
            scratch_shapes=[pltpu.VMEM((ext_rows, SSM_CONV_DIM), BF16)]),
        out_shape=jax.ShapeDtypeStruct((SSM_CONV_DIM // SLAB, T, SLAB), BF16),
        compiler_params=_cparams(1, 40),
        name="ssm_conv",
    )(first, last, xbc, xbc, xbc, shift, conv_w, conv_b)


def _ssd_tables():
    r = np.arange(CHUNK)
    tri_l = (r[None, :] <= r[:, None]).astype(np.float32)
    tri_u = (r[None, :] >= r[:, None]).astype(np.float32)
    c = np.arange(SSM_DINNER) // SSM_HEADDIM
    j = np.arange(LANES)
    e_f = (j[:, None] == c[None, :]).astype(np.float32)
    e_b = (j[:, None] == (c[None, :] + SSM_HEADS)).astype(np.float32)
    as_bf16 = lambda x: jnp.asarray(x, dtype=BF16)
    return dict(tri_l3=as_bf16(np.concatenate([tri_l] * 3, axis=1)),
                tri_u3=as_bf16(np.concatenate([tri_u] * 3, axis=1)),
                e2_f=as_bf16(np.concatenate([e_f, e_f], axis=0)),
                e2_b=as_bf16(np.concatenate([e_b, e_b], axis=0)))


def _ssd_chunk_scalars(dt, a_ref, tri_l3_ref, tri_u3_ref):
    a = dt * a_ref[...]
    a3 = jnp.concatenate(_split3(a), axis=0)
    cum_f = _dot(tri_l3_ref[...], a3)
    cum_b = _dot(tri_u3_ref[...], a3)
    lane = lax.broadcasted_iota(jnp.int32, (CHUNK, LANES), 1)
    cum = jnp.where(lane < SSM_HEADS, cum_f, cum_b)
    tot = cum_f[CHUNK - 1:CHUNK, :]
    return cum, tot


def _expand(e2_ref, *rows):
    x = jnp.concatenate(rows, axis=0)
    hi, lo = _split2(x)
    return _dot(jnp.concatenate([hi, lo], axis=1), e2_ref[...])


def _ssd_bwd_kernel(reset_ref, xa_ref, dt_ref, a_ref, tri_l3_ref, tri_u3_ref, e2_ref,
                    yb_ref, state_ref):
    nch = SCAN_BLOCK // CHUNK

    @pl.when(reset_ref[pl.program_id(0)] == 1)
    def _():
        state_ref[...] = jnp.zeros_like(state_ref)

    def chunk_body(ci, carry):
        rows = pl.ds(pl.multiple_of((nch - 1 - ci) * CHUNK, CHUNK), CHUNK)
        dt = dt_ref[rows, :]
        cum, tot = _ssd_chunk_scalars(dt, a_ref, tri_l3_ref, tri_u3_ref)
        ex = _expand(e2_ref, dt * jnp.exp(tot - cum), jnp.exp(cum),
                     jnp.broadcast_to(jnp.exp(tot), (16, LANES)))
        for g in range(SSM_GROUPS):
            gc = slice(g * SSM_GROUP_W, (g + 1) * SSM_GROUP_W)
            bm = xa_ref[SSM_GROUPS, rows, g * SSM_DSTATE:(g + 1) * SSM_DSTATE]
            cm = xa_ref[SSM_GROUPS + 1, rows, g * SSM_DSTATE:(g + 1) * SSM_DSTATE]
            state = state_ref[g]
            yb_ref[rows, gc] = (_dot(cm, state.astype(BF16)) * ex[CHUNK:2 * CHUNK, gc]).astype(BF16)
            xw = (xa_ref[g, rows, :].astype(F32) * ex[0:CHUNK, gc]).astype(BF16)
            state_ref[g] = state * ex[2 * CHUNK:2 * CHUNK + 1, gc] + _dot_tn(bm, xw)
        return carry

    lax.fori_loop(0, nch, chunk_body, 0)


def _ssd_fwd_kernel(reset_ref, xa_ref, dt_ref, gate_ref, yb_ref, a_ref, tri_l3_ref, tri_u3_ref,
                    e2_ref, dskip_ref, nw_ref, out_ref, state_ref, y_ref):
    nch = SCAN_BLOCK // CHUNK

    @pl.when(reset_ref[pl.program_id(0)] == 1)
    def _():
        state_ref[...] = jnp.zeros_like(state_ref)

    def chunk_body(ci, carry):
        rows = pl.ds(pl.multiple_of(ci * CHUNK, CHUNK), CHUNK)
        dt = dt_ref[rows, :]
        cum, tot = _ssd_chunk_scalars(dt, a_ref, tri_l3_ref, tri_u3_ref)
        ex = _expand(e2_ref, dt * jnp.exp(tot - cum), jnp.exp(cum),
                     jnp.broadcast_to(jnp.exp(tot), (16, LANES)))
        row_t = (cum - jnp.log(dt)).T
        li = lax.broadcasted_iota(jnp.int32, (CHUNK, CHUNK), 0)
        si = lax.broadcasted_iota(jnp.int32, (CHUNK, CHUNK), 1)
        lower = li >= si
        first_half = si < SSM_HEADDIM
        for g in range(SSM_GROUPS):
            gc = slice(g * SSM_GROUP_W, (g + 1) * SSM_GROUP_W)
            bm = xa_ref[SSM_GROUPS, rows, g * SSM_DSTATE:(g + 1) * SSM_DSTATE]
            cm = xa_ref[SSM_GROUPS + 1, rows, g * SSM_DSTATE:(g + 1) * SSM_DSTATE]
            cb = _dot_nt(cm, bm)
            heads_per_group = SSM_HEADS // SSM_GROUPS
            for pair in range(heads_per_group // 2):
                mats = []
                for hh in range(2):
                    h = g * heads_per_group + 2 * pair + hh
                    hb = SSM_HEADS + h
                    seg = jnp.where(lower, cum[:, h:h + 1] - row_t[h:h + 1, :],
                                    cum[:, hb:hb + 1] - row_t[hb:hb + 1, :])
                    mats.append((cb * jnp.exp(seg)).astype(BF16))
                pc = slice(2 * pair * SSM_HEADDIM, (2 * pair + 2) * SSM_HEADDIM)
                yy = _dot(jnp.concatenate(mats, axis=0), xa_ref[g, rows, pc])
                y_ref[:, pc] = jnp.where(first_half, yy[0:CHUNK], yy[CHUNK:2 * CHUNK])
            state = state_ref[g]
            xs = xa_ref[g, rows, :].astype(F32)
            y = (y_ref[...] + _dot(cm, state.astype(BF16)) * ex[CHUNK:2 * CHUNK, gc]
                 + yb_ref[rows, gc].astype(F32) + xs * dskip_ref[:, gc])
            y = y * gate_ref[g, rows, :].astype(F32)
            out_ref[rows, gc] = (_rms(y) * nw_ref[:, gc]).astype(BF16)
            xw = (xs * ex[0:CHUNK, gc]).astype(BF16)
            state_ref[g] = state * ex[2 * CHUNK:2 * CHUNK + 1, gc] + _dot_tn(bm, xw)
        return carry

    lax.fori_loop(0, nch, chunk_body, 0)


def _ssd(xa, dt, gate, a_row, dskip, norm_w, reset_f, reset_b):
    T = dt.shape[0]
    nb = T // SCAN_BLOCK
    tb = _ssd_tables()
    state = pltpu.VMEM((SSM_GROUPS, SSM_DSTATE, SSM_GROUP_W), F32)
    consts = [a_row, tb["tri_l3"], tb["tri_u3"]]
    const_specs = [_resident(c.shape) for c in consts]
    out_shape = jax.ShapeDtypeStruct((T, SSM_DINNER), BF16)

    rev = lambda i, r: (nb - 1 - i, 0)
    blk = lambda n, imap: pl.BlockSpec((SCAN_BLOCK, n), imap)
    slabs = lambda a, imap: pl.BlockSpec((a.shape[0], SCAN_BLOCK, SLAB), lambda i, r: (0, imap(i, r)[0], 0))
    y_b = pl.pallas_call(
        _ssd_bwd_kernel,
        grid_spec=pltpu.PrefetchScalarGridSpec(
            num_scalar_prefetch=1, grid=(nb,),
            in_specs=[slabs(xa, rev), blk(LANES, rev)] + const_specs + [_resident(tb["e2_b"].shape)],
            out_specs=blk(SSM_DINNER, rev), scratch_shapes=[state]),
        out_shape=out_shape, compiler_params=_cparams(1, 40), name="ssd_bwd",
    )(reset_b, xa, dt, *consts, tb["e2_b"])

    fwd = lambda i, r: (i, 0)
    return pl.pallas_call(
        _ssd_fwd_kernel,
        grid_spec=pltpu.PrefetchScalarGridSpec(
            num_scalar_prefetch=1, grid=(nb,),
            in_specs=[slabs(xa, fwd), blk(LANES, fwd), slabs(gate, fwd), blk(SSM_DINNER, fwd)]
            + const_specs + [_resident(tb["e2_f"].shape), _resident(dskip.shape), _resident(norm_w.shape)],
            out_specs=blk(SSM_DINNER, fwd),
            scratch_shapes=[state, pltpu.VMEM((CHUNK, SSM_GROUP_W), F32)]),
        out_shape=out_shape, compiler_params=_cparams(1, 40), name="ssd_fwd",
    )(reset_f, xa, dt, gate, y_b, *consts, tb["e2_f"], dskip, norm_w)


def _pack_halves(x):
    n = x.shape[1] // 2
    hi = lax.bitcast_convert_type(x[:, :n].astype(F32), jnp.uint32)
    lo = lax.bitcast_convert_type(x[:, n:].astype(F32), jnp.uint32)
    return hi | (lo >> 16)


def _unpack_halves(w):
    hi = lax.bitcast_convert_type(w & jnp.uint32(0xFFFF0000), F32)
    lo = lax.bitcast_convert_type(w << 16, F32)
    return hi, lo


def _router_kernel(y_ref, h_ref, wo_ref, ln_ref, wh_ref, wl_ref, b_ref, tri_ref,
                   h3_ref, xn_ref, info_ref, count_ref, base_ref):
    @pl.when(pl.program_id(0) == 0)
    def _():
        base_ref[...] = jnp.zeros_like(base_ref)

    h3 = h_ref[...] + _dot(y_ref[...], wo_ref[...])
    h3_ref[...] = h3
    xh, xl = _split2(_rms(h3) * ln_ref[...])
    xn_ref[...] = _pack_halves(xh)
    logits = _dot(xh, wh_ref[...]) + _dot(xl, wh_ref[...]) + _dot(xh, wl_ref[...]) + b_ref[...]
    col = lax.broadcasted_iota(jnp.int32, logits.shape, 1)
    m1 = jnp.max(logits, axis=-1, keepdims=True)
    i1 = jnp.min(jnp.where(logits == m1, col, LANES), axis=-1, keepdims=True)
    rest = jnp.where(col == i1, -jnp.inf, logits)
    m2 = jnp.max(rest, axis=-1, keepdims=True)
    i2 = jnp.min(jnp.where(rest == m2, col, LANES), axis=-1, keepdims=True)
    e = jnp.exp(m2 - m1)
    g1 = 1.0 / (1.0 + e)
    g2 = e / (1.0 + e)
    oh1 = col == i1
    oh2 = col == i2
    oh = jnp.where(oh1 | oh2, 1.0, 0.0)
    base = base_ref[...]
    prefix = _dot(tri_ref[...], oh.astype(BF16)) + base
    r1 = jnp.sum(jnp.where(oh1, prefix, 0.0), axis=-1, keepdims=True)
    r2 = jnp.sum(jnp.where(oh2, prefix, 0.0), axis=-1, keepdims=True)
    base = base + jnp.sum(oh, axis=0, keepdims=True)
    count_ref[...] = base
    base_ref[...] = base
    fields = [i1.astype(F32), i2.astype(F32), g1, g2, r1, r2]
    info = jnp.zeros(logits.shape, F32)
    for c, f in enumerate(fields):
        info = jnp.where(col == c, f, info)
    info_ref[...] = info[:, :ROUTE_COLS]


def _router(y, h, w_out, ln, wh, wl, b):
    T = h.shape[0]
    tb = MOE_TOK
    nb = T // tb
    r = np.arange(tb)
    tri = jnp.asarray((r[None, :] < r[:, None]).astype(np.float32), dtype=BF16)
    tok = lambda n: pl.BlockSpec((tb, n), lambda i: (i, 0))
    return pl.pallas_call(
        _router_kernel,
        grid=(nb,),
        in_specs=[tok(SSM_DINNER), tok(D_MODEL), _resident(w_out.shape), _resident((1, D_MODEL)),
                  _resident(wh.shape), _resident(wl.shape), _resident(b.shape), _resident(tri.shape)],
        out_specs=[tok(D_MODEL), tok(D_MODEL // 2), tok(ROUTE_COLS), pl.BlockSpec((1, LANES), lambda i: (0, 0))],
        out_shape=[jax.ShapeDtypeStruct((T, D_MODEL), F32), jax.ShapeDtypeStruct((T, D_MODEL // 2), jnp.uint32),
                   jax.ShapeDtypeStruct((T, ROUTE_COLS), F32), jax.ShapeDtypeStruct((1, LANES), F32)],
        scratch_shapes=[pltpu.VMEM((1, LANES), F32)],
        compiler_params=_cparams(1, 32),
        name="ssm_out_router",
    )(y, h, w_out, ln, wh, wl, b, tri)


def _sc_gather(table, idx):
    n_rows, width = idx.shape[0], table.shape[1]
    workers = SC_CORES * SC_SUBCORES
    nbuf = 2
    assert n_rows % (workers * SC_CHUNK * nbuf) == 0
    per_worker = n_rows // workers
    mesh = plsc.VectorSubcoreMesh(core_axis_name="c", subcore_axis_name="s")

    @functools.partial(
        pl.kernel, mesh=mesh,
        out_type=jax.ShapeDtypeStruct((n_rows, width), table.dtype),
        scratch_types=[pltpu.VMEM((nbuf, SC_CHUNK), jnp.int32),
                       pltpu.VMEM((nbuf, SC_CHUNK, width), table.dtype),
                       pltpu.SemaphoreType.DMA((nbuf,)),
                       pltpu.SemaphoreType.DMA((nbuf,))],
    )
    def gather_rows(table_hbm, idx_hbm, out_hbm, idx_v, rows_v, gsem, wsem):
        base = (lax.axis_index("s") * SC_CORES + lax.axis_index("c")) * per_worker

        def gather(c, b):
            off = pl.multiple_of(base + c * SC_CHUNK, 8)
            pltpu.sync_copy(idx_hbm.at[pl.ds(off, SC_CHUNK)], idx_v.at[b])
            return pltpu.make_async_copy(table_hbm.at[idx_v.at[b]], rows_v.at[b], gsem.at[b])

        def write(c, b):
            off = pl.multiple_of(base + c * SC_CHUNK, 8)
            return pltpu.make_async_copy(rows_v.at[b], out_hbm.at[pl.ds(off, SC_CHUNK)], wsem.at[b])

        @pl.loop(0, per_worker // SC_CHUNK, step=nbuf)
        def _(c0):
            copies = [gather(c0 + b, b) for b in range(nbuf)]
            for cp in copies:
                cp.start()
            writes = []
            for b, cp in enumerate(copies):
                cp.wait()
                writes.append(write(c0 + b, b))
                writes[-1].start()
            for wr in writes:
                wr.wait()

    return gather_rows(table, idx)


def _sc_scatter_pairs(table, pos, n_out):
    n_rows, width = table.shape
    workers = SC_CORES * SC_SUBCORES
    nbuf = 2
    assert n_rows % (workers * SC_CHUNK * nbuf) == 0
    per_worker = n_rows // workers
    mesh = plsc.VectorSubcoreMesh(core_axis_name="c", subcore_axis_name="s")

    @functools.partial(
        pl.kernel, mesh=mesh,
        out_type=jax.ShapeDtypeStruct((n_out, width), table.dtype),
        scratch_types=[pltpu.VMEM((nbuf, 2, SC_CHUNK), jnp.int32),
                       pltpu.VMEM((nbuf, SC_CHUNK, width), table.dtype),
                       pltpu.SemaphoreType.DMA((nbuf,)),
                       pltpu.SemaphoreType.DMA((nbuf,))],
    )
    def scatter_rows(table_hbm, pos_hbm, out_hbm, idx_v, rows_v, rsem, wsem):
        base = (lax.axis_index("s") * SC_CORES + lax.axis_index("c")) * per_worker

        def read(c, b):
            off = pl.multiple_of(base + c * SC_CHUNK, 8)
            return pltpu.make_async_copy(table_hbm.at[pl.ds(off, SC_CHUNK)], rows_v.at[b], rsem.at[b])

        def write(b, j):
            return pltpu.make_async_copy(rows_v.at[b], out_hbm.at[idx_v.at[b, j]], wsem.at[b])

        @pl.loop(0, per_worker // SC_CHUNK, step=nbuf)
        def _(c0):
            reads = [read(c0 + b, b) for b in range(nbuf)]
            for rd in reads:
                rd.start()
            for b in range(nbuf):
                pltpu.sync_copy(pos_hbm.at[base // SC_CHUNK + c0 + b], idx_v.at[b])
            writes = []
            for b, rd in enumerate(reads):
                rd.wait()
                for j in range(2):
                    writes.append(write(b, j))
                    writes[-1].start()
            for wr in writes:
                wr.wait()

    return scatter_rows(table, pos)


def _expert_kernel(texp_ref, trows_ref, x_ref, wg_ref, wu_ref, wd_ref, out_ref, acc_ref, xb_ref):
    i = pl.program_id(0)
    f = pl.program_id(1)
    nf = pl.num_programs(1)
    n = D_MODEL // 2

    @pl.when(trows_ref[i] > 0)
    def _():
        @pl.when(f == 0)
        def _():
            acc_ref[...] = jnp.zeros_like(acc_ref)
            row = lax.broadcasted_iota(jnp.int32, x_ref.shape, 0)
            hi, lo = _unpack_halves(jnp.where(row < trows_ref[i], x_ref[...], jnp.uint32(0)))
            xb_ref[:, :n] = hi.astype(BF16)
            xb_ref[:, n:] = lo.astype(BF16)

        x = xb_ref[...]
        act = (_silu(_dot(x, wg_ref[0].astype(BF16))) * _dot(x, wu_ref[0].astype(BF16))).astype(BF16)
        acc_ref[...] += _dot(act, wd_ref[0].astype(BF16))

        @pl.when(f == nf - 1)
        def _():
            out_ref[...] = _pack_halves(acc_ref[...].astype(BF16))

    @pl.when((trows_ref[i] == 0) & (f == nf - 1))
    def _():
        out_ref[...] = jnp.zeros_like(out_ref)


def _experts(xs, w_gu, w_down, tile_expert, tile_rows):
    n_tiles = xs.shape[0] // MOE_SLOT
    nf = FFN_EXPERT // MOE_FC
    return pl.pallas_call(
        _expert_kernel,
        grid_spec=pltpu.PrefetchScalarGridSpec(
            num_scalar_prefetch=2, grid=(n_tiles, nf),
            in_specs=[pl.BlockSpec((MOE_SLOT, D_MODEL // 2), lambda i, f, te, tv: (i, 0)),
                      pl.BlockSpec((1, D_MODEL, MOE_FC), lambda i, f, te, tv: (te[i], 0, f)),
                      pl.BlockSpec((1, D_MODEL, MOE_FC), lambda i, f, te, tv: (te[i], 0, nf + f)),
                      pl.BlockSpec((1, MOE_FC, D_MODEL), lambda i, f, te, tv: (te[i], f, 0))],
            out_specs=pl.BlockSpec((MOE_SLOT, D_MODEL // 2), lambda i, f, te, tv: (i, 0)),
            scratch_shapes=[pltpu.VMEM((MOE_SLOT, D_MODEL), F32), pltpu.VMEM((MOE_SLOT, D_MODEL), BF16)]),
        out_shape=jax.ShapeDtypeStruct(xs.shape, jnp.uint32),
        compiler_params=_cparams(2, 56),
        name="moe_experts",
    )(tile_expert, tile_rows, xs, w_gu, w_gu, w_down)


def _finalize_kernel(y0_ref, y1_ref, info_ref, h_ref, fn_ref, out_ref):
    n = D_MODEL // 2
    info = info_ref[...]
    g0 = info[:, 2:3]
    g1 = info[:, 3:4]
    a_hi, a_lo = _unpack_halves(y0_ref[...])
    b_hi, b_lo = _unpack_halves(y1_ref[...])
    left = h_ref[:, :n] + g0 * a_hi + g1 * b_hi
    right = h_ref[:, n:] + g0 * a_lo + g1 * b_lo
    ms = (jnp.sum(left * left, axis=-1, keepdims=True)
          + jnp.sum(right * right, axis=-1, keepdims=True)) * (1.0 / D_MODEL)
    scale = lax.rsqrt(ms + EPS)
    out_ref[:, :n] = left * scale * fn_ref[:, :n]
    out_ref[:, n:] = right * scale * fn_ref[:, n:]


def _finalize(yg, info, h3, final_norm, tok0):
    n = yg.shape[0] // 2
    tb = MOE_TOK
    nb = n // tb
    b0 = tok0 // tb
    tok = lambda w: pl.BlockSpec((tb, w), lambda i: (b0 + i, 0))
    loc = lambda off: pl.BlockSpec((tb, D_MODEL // 2), lambda i: (i + off, 0))
    return pl.pallas_call(
        _finalize_kernel,
        grid=(nb,),
        in_specs=[loc(0), loc(nb), tok(info.shape[1]), tok(D_MODEL), _resident((1, D_MODEL))],
        out_specs=pl.BlockSpec((tb, D_MODEL), lambda i: (i, 0)),
        out_shape=jax.ShapeDtypeStruct((n, D_MODEL), F32),
        compiler_params=_cparams(1, 32),
        name="moe_finalize",
    )(yg, yg, info, h3, final_norm)


def _moe(y_ssm, h2, w_out, ln, wr_hi, wr_lo, rb, w_gu, w_down, final_norm, n_a):
    T = h2.shape[0]
    n_tiles = 2 * T // MOE_SLOT + N_EXPERTS
    h3, xn, info, counts = _router(y_ssm, h2, w_out, ln, wr_hi, wr_lo, rb)

    counts = counts[0, :N_EXPERTS].astype(jnp.int32)
    tiles_e = (counts + MOE_SLOT - 1) // MOE_SLOT
    tiles_cum = jnp.cumsum(tiles_e)
    gstart = (tiles_cum - tiles_e) * MOE_SLOT
    routed = info.astype(jnp.int32)
    start_of = lambda e: jnp.sum(jnp.where(e[:, None] == jnp.arange(N_EXPERTS), gstart[None, :], 0), axis=1)
    pos0 = start_of(routed[:, 0]) + routed[:, 4]
    pos1 = start_of(routed[:, 1]) + routed[:, 5]
    tile_ids = jnp.arange(n_tiles, dtype=jnp.int32)
    tile_expert = jnp.minimum(jnp.sum(tile_ids[:, None] >= tiles_cum[None, :], axis=1), N_EXPERTS - 1)
    tile_expert = tile_expert.astype(jnp.int32)
    tile_rows = jnp.clip(counts[tile_expert] - (tile_ids * MOE_SLOT - gstart[tile_expert]), 0, MOE_SLOT)
    tile_rows = jnp.where(tile_ids < tiles_cum[-1], tile_rows, 0).astype(jnp.int32)
    pos = jnp.stack([pos0.reshape(-1, SC_CHUNK), pos1.reshape(-1, SC_CHUNK)], axis=1)

    xs = _sc_scatter_pairs(xn, pos, n_tiles * MOE_SLOT)
    ys = _experts(xs, w_gu, w_down, tile_expert, tile_rows)
    outs = []
    for lo, hi in ((0, n_a), (n_a, T)):
        yg = _sc_gather(ys, jnp.concatenate([pos0[lo:hi], pos1[lo:hi]]))
        outs.append(_finalize(yg, info, h3, final_norm, lo))
    return outs


def _seq_flags(seq_lens, block):
    first, last = [], []
    for n in seq_lens:
        nb = n // block
        first += [1] + [0] * (nb - 1)
        last += [0] * (nb - 1) + [1]
    return np.asarray(first, np.int32), np.asarray(last, np.int32)


def _trunk(xa, xb, seq_lens, ln_mix0, ret_w_in, ret_w_out, ln_ffn0, ffn_w_gu, ffn_w_down, ln_mix1, ssm_w_in,
           ssm_conv_w, ssm_conv_b, ssm_dt_bias, ssm_A_log, ssm_D, ssm_norm_w, ssm_w_out, ln_ffn1,
           moe_router_w, moe_router_b, moe_w_gu, moe_w_down, final_norm):
    assert all(n % SCAN_BLOCK == 0 and n % MOE_TOK == 0 and n % TOK_TILE == 0 for n in seq_lens)
    assert TOK_TILE == SCAN_BLOCK
    row = lambda v: v.astype(F32).reshape(1, -1)
    first, last = _seq_flags(seq_lens, SCAN_BLOCK)
    reset_f = jnp.asarray(first)
    reset_b = jnp.asarray(last[::-1].copy())

    half = RET_DK // 2
    inv = ROPE_BASE ** (-jnp.arange(half, dtype=F32) / half)
    ang = jnp.arange(max(seq_lens)).astype(F32)[:, None] * inv[None, :]
    pos_block = jnp.asarray(np.concatenate([np.arange(n // TOK_TILE) for n in seq_lens]).astype(np.int32))
    q, k, v, g = _ret_in(xa, xb, row(ln_mix0), ret_w_in.astype(BF16), jnp.cos(ang), jnp.sin(ang), pos_block)
    y = _ret_scan(q, k, v, g, reset_f, reset_b)
    h2 = _post0(y, xa, xb, ret_w_out.astype(BF16), row(ln_ffn0), ffn_w_gu.astype(BF16),
                ffn_w_down.astype(BF16))

    wz = ssm_w_in[:, :SSM_DINNER].astype(BF16)
    wx = ssm_w_in[:, SSM_DINNER:SSM_DINNER + SSM_CONV_DIM].astype(BF16)
    wdt = jnp.pad(ssm_w_in[:, SSM_DINNER + SSM_CONV_DIM:], ((0, 0), (0, LANES - 2 * SSM_HEADS)))
    wdh, wdl = _split2(wdt)
    pad_row = lambda v: jnp.pad(v.astype(F32).reshape(1, -1), ((0, 0), (0, LANES - 2 * SSM_HEADS)))
    gate, xbc, dt = _ssm_in(h2, row(ln_mix1), wz, wx, wdh, wdl, pad_row(ssm_dt_bias))
    xact = _conv(xbc, ssm_conv_w.reshape(SSM_CONV, SSM_CONV_DIM).astype(F32), row(ssm_conv_b),
                 jnp.asarray(first), jnp.asarray(last))
    a_row = pad_row(-jnp.exp(ssm_A_log.astype(F32)))
    dskip = jnp.repeat(ssm_D.astype(F32), SSM_HEADDIM).reshape(1, SSM_DINNER)
    y = _ssd(xact, dt, gate, a_row, dskip, row(ssm_norm_w), reset_f, reset_b)

    wr = jnp.pad(moe_router_w.astype(F32), ((0, 0), (0, LANES - N_EXPERTS)))
    wr_hi, wr_lo = _split2(wr)
    rb = jnp.pad(moe_router_b.astype(F32).reshape(1, -1), ((0, 0), (0, LANES - N_EXPERTS)),
                 constant_values=-1e30)
    return _moe(y, h2, ssm_w_out.astype(BF16), row(ln_ffn1), wr_hi, wr_lo, rb,
                moe_w_gu, moe_w_down, row(final_norm), xa.shape[0])


def kernel(x_prompt, x_sample, ln_mix0, ret_w_in, ret_w_out, ln_ffn0, ffn_w_gu, ffn_w_down, ln_mix1, ssm_w_in,
           ssm_conv_w, ssm_conv_b, ssm_dt_bias, ssm_A_log, ssm_D, ssm_norm_w, ssm_w_out, ln_ffn1, moe_router_w,
           moe_router_b, moe_w_gu, moe_w_down, final_norm):
    seq_lens = [x_prompt.shape[1]] * x_prompt.shape[0] + [x_sample.shape[1]] * x_sample.shape[0]
    out_a, out_b = _trunk(x_prompt.reshape(-1, D_MODEL), x_sample.reshape(-1, D_MODEL), seq_lens, ln_mix0, ret_w_in, ret_w_out, ln_ffn0, ffn_w_gu, ffn_w_down, ln_mix1, ssm_w_in,
                 ssm_conv_w, ssm_conv_b, ssm_dt_bias, ssm_A_log, ssm_D, ssm_norm_w, ssm_w_out, ln_ffn1,
                 moe_router_w, moe_router_b, moe_w_gu, moe_w_down, final_norm)
    return (out_a.reshape(x_prompt.shape), out_b.reshape(x_sample.shape))
```

```python
import functools

import numpy as np
import jax
import jax.numpy as jnp
from jax import lax
from jax.experimental import pallas as pl
from jax.experimental.pallas import tpu as pltpu
from jax.experimental.pallas import tpu_sc as plsc

F32 = jnp.float32
BF16 = jnp.bfloat16

D_MODEL = 1024
EPS = 1e-6
RET_HEADS = 4
RET_DK = 256
RET_DV = 512
ROPE_BASE = 10000.0
SSM_DINNER = 2048
SSM_HEADDIM = 64
SSM_HEADS = 32
SSM_GROUPS = 4
SSM_DSTATE = 128
SSM_CONV = 5
SSM_CONV_DIM = 3072
SSM_GROUP_W = SSM_DINNER // SSM_GROUPS
SLAB = SSM_GROUP_W
CONV_ROWS = 128
CONV_WINDOW = 256
HALO = 16
FFN_DENSE = 2816
FFN_DENSE_CHUNK = 1408
N_EXPERTS = 8
FFN_EXPERT = 3584

CHUNK = 128
RET_CHUNK = 256
LANES = 128
TOK_TILE = 512
PROJ_TILE = 1024
SCAN_BLOCK = 512
SSD_BLOCK = 1024
MOE_TOK = 512
MOE_SLOT = 1024
MOE_FC = 512
ROUTE_COLS = 8
SC_CORES = 2
SC_SUBCORES = 16
SC_CHUNK = 64


def _cparams(n_axes, vmem_mb):
    return pltpu.CompilerParams(dimension_semantics=("arbitrary",) * n_axes,
                                vmem_limit_bytes=vmem_mb << 20)


def _resident(shape):
    nd = len(shape)
    return pl.BlockSpec(shape, lambda *_: (0,) * nd, pipeline_mode=pl.Buffered(1))


def _rms(x):
    return x * lax.rsqrt(jnp.mean(x * x, axis=-1, keepdims=True) + EPS)


def _silu(x):
    return x * jax.nn.sigmoid(x)


def _dot(a, b):
    return jnp.dot(a, b, preferred_element_type=F32)


def _dot_nt(a, b):
    return lax.dot_general(a, b, (((1,), (1,)), ((), ())), preferred_element_type=F32)


def _dot_tn(a, b):
    return lax.dot_general(a, b, (((0,), (0,)), ((), ())), preferred_element_type=F32)


def _split2(x):
    hi = x.astype(BF16)
    lo = (x - hi.astype(F32)).astype(BF16)
    return hi, lo


def _split3(x):
    hi = x.astype(BF16)
    r = x - hi.astype(F32)
    mid = r.astype(BF16)
    lo = (r - mid.astype(F32)).astype(BF16)
    return hi, mid, lo


def _two_stream_specs(tm, n_a):
    return (pl.BlockSpec((tm, D_MODEL), lambda i, *_: (jnp.minimum(i, n_a - 1), 0)),
            pl.BlockSpec((tm, D_MODEL), lambda i, *_: (jnp.maximum(i - n_a, 0), 0)))


def _ret_in_kernel(pos_ref, xa_ref, xb_ref, ln_ref, w_ref, cos_ref, sin_ref, q_ref, k_ref, v_ref, g_ref, *, n_a):
    x = jnp.where(pl.program_id(0) < n_a, xa_ref[...], xb_ref[...])
    xn = (_rms(x) * ln_ref[...]).astype(BF16)
    cos = cos_ref[...]
    sin = sin_ref[...]
    half = RET_DK // 2

    def rotary(dst, col0, scale):
        for h in range(RET_HEADS):
            c = col0 + h * RET_DK
            p = _dot(xn, w_ref[:, c:c + RET_DK])
            p1, p2 = p[:, :half], p[:, half:]
            dst[:, h * RET_DK:h * RET_DK + half] = ((p1 * cos - p2 * sin) * scale).astype(BF16)
            dst[:, h * RET_DK + half:(h + 1) * RET_DK] = ((p1 * sin + p2 * cos) * scale).astype(BF16)

    rotary(q_ref, 0, 1.0)
    rotary(k_ref, D_MODEL, RET_DK ** -0.5)
    for j in range(2 * D_MODEL // 512):
        v_ref[:, j * 512:(j + 1) * 512] = _dot(
            xn, w_ref[:, 2 * D_MODEL + j * 512:2 * D_MODEL + (j + 1) * 512]).astype(BF16)
        g_ref[:, j * 512:(j + 1) * 512] = _dot(
            xn, w_ref[:, 4 * D_MODEL + j * 512:4 * D_MODEL + (j + 1) * 512]).astype(BF16)


def _ret_in(xa, xb, ln, w_in, cos, sin, pos_block):
    T = xa.shape[0] + xb.shape[0]
    tm = TOK_TILE
    n_a = xa.shape[0] // tm
    tok = lambda n: pl.BlockSpec((tm, n), lambda i, pb: (i, 0))
    rot = pl.BlockSpec((tm, LANES), lambda i, pb: (pb[i], 0))
    return pl.pallas_call(
        functools.partial(_ret_in_kernel, n_a=n_a),
        grid_spec=pltpu.PrefetchScalarGridSpec(
            num_scalar_prefetch=1, grid=(T // tm,),
            in_specs=[*_two_stream_specs(tm, n_a), _resident((1, D_MODEL)), _resident(w_in.shape), rot, rot],
            out_specs=[tok(D_MODEL), tok(D_MODEL), tok(2 * D_MODEL), tok(2 * D_MODEL)]),
        out_shape=[jax.ShapeDtypeStruct((T, D_MODEL), BF16), jax.ShapeDtypeStruct((T, D_MODEL), BF16),
                   jax.ShapeDtypeStruct((T, 2 * D_MODEL), BF16), jax.ShapeDtypeStruct((T, 2 * D_MODEL), BF16)],
        compiler_params=_cparams(1, 48),
        name="ret_in",
    )(pos_block, xa, xb, ln, w_in, cos, sin)


def _ret_scan_kernel(reset_ref, *refs, reverse, chunk_decay):
    if reverse:
        q_ref, k_ref, v_ref, wq_ref, wk_ref, out_ref, state_ref = refs
    else:
        q_ref, k_ref, v_ref, ob_ref, g_ref, wq_ref, wk_ref, din_ref, out_ref, state_ref = refs
    nch = SCAN_BLOCK // RET_CHUNK

    @pl.when(reset_ref[pl.program_id(0)] == 1)
    def _():
        state_ref[...] = jnp.zeros_like(state_ref)

    def chunk_body(ci, carry):
        cj = nch - 1 - ci if reverse else ci
        rows = pl.ds(pl.multiple_of(cj * RET_CHUNK, RET_CHUNK), RET_CHUNK)
        for h in range(RET_HEADS):
            kcols = slice(h * RET_DK, (h + 1) * RET_DK)
            vcols = slice(h * RET_DV, (h + 1) * RET_DV)
            qh = q_ref[rows, kcols]
            kh = k_ref[rows, kcols]
            vh = v_ref[rows, vcols]
            state = state_ref[h]
            o = _dot(qh, state.astype(BF16)) * wq_ref[h]
            if reverse:
                out_ref[rows, vcols] = o.astype(BF16)
            else:
                s = _dot_nt(qh, kh) * din_ref[h]
                o = o + _dot(s.astype(BF16), vh) + ob_ref[rows, vcols].astype(F32)
                gate = g_ref[rows, vcols].astype(F32)
                out_ref[rows, vcols] = (_silu(gate) * _rms(o)).astype(BF16)
            ks = (kh.astype(F32) * wk_ref[h]).astype(BF16)
            state_ref[h] = state * chunk_decay[h] + _dot_tn(ks, vh)
        return carry

    lax.fori_loop(0, nch, chunk_body, 0)


def _ret_tables():
    lg = np.log1p(-np.power(2.0, -5.0 - np.arange(RET_HEADS, dtype=np.float32))).astype(np.float32)
    c = RET_CHUNK
    a = np.arange(c, dtype=np.float32)
    col = lambda e, w: np.broadcast_to(np.exp(lg[:, None, None] * e[None, :, None]),
                                       (RET_HEADS, c, w)).astype(np.float32)
    dist = np.abs(a[:, None] - a[None, :])
    return dict(
        din=np.exp(lg[:, None, None] * dist[None]).astype(np.float32),
        wq_f=col(a + 1.0, RET_DV), wk_f=col(c - 1.0 - a, RET_DK),
        wq_b=col(c - a, RET_DV), wk_b=col(a, RET_DK),
        chunk_decay=tuple(float(v) for v in np.exp(lg * c)),
    )


def _ret_scan(q, k, v, g, reset_f, reset_b):
    T = q.shape[0]
    nb = T // SCAN_BLOCK
    tb = _ret_tables()
    state = pltpu.VMEM((RET_HEADS, RET_DK, RET_DV), F32)
    out_shape = jax.ShapeDtypeStruct((T, 2 * D_MODEL), BF16)

    def specs(imap):
        blk = lambda n: pl.BlockSpec((SCAN_BLOCK, n), imap)
        return blk(D_MODEL), blk(D_MODEL), blk(2 * D_MODEL)

    rev = lambda i, r: (nb - 1 - i, 0)
    bq, bk, bv = specs(rev)
    o_b = pl.pallas_call(
        functools.partial(_ret_scan_kernel, reverse=True, chunk_decay=tb["chunk_decay"]),
        grid_spec=pltpu.PrefetchScalarGridSpec(
            num_scalar_prefetch=1, grid=(nb,),
            in_specs=[bq, bk, bv, _resident(tb["wq_b"].shape), _resident(tb["wk_b"].shape)],
            out_specs=bv, scratch_shapes=[state]),
        out_shape=out_shape, compiler_params=_cparams(1, 40), name="ret_scan_bwd",
    )(reset_b, q, k, v, tb["wq_b"], tb["wk_b"])

    fwd = lambda i, r: (i, 0)
    bq, bk, bv = specs(fwd)
    return pl.pallas_call(
        functools.partial(_ret_scan_kernel, reverse=False, chunk_decay=tb["chunk_decay"]),
        grid_spec=pltpu.PrefetchScalarGridSpec(
            num_scalar_prefetch=1, grid=(nb,),
            in_specs=[bq, bk, bv, bv, bv, _resident(tb["wq_f"].shape), _resident(tb["wk_f"].shape),
                      _resident(tb["din"].shape)],
            out_specs=bv, scratch_shapes=[state]),
        out_shape=out_shape, compiler_params=_cparams(1, 40), name="ret_scan_fwd",
    )(reset_f, q, k, v, o_b, g, tb["wq_f"], tb["wk_f"], tb["din"])


def _post0_kernel(y_ref, xa_ref, xb_ref, wo_ref, ln_ref, wgu_ref, wd_ref, out_ref, *, n_a):
    x = jnp.where(pl.program_id(0) < n_a, xa_ref[...], xb_ref[...])
    h1 = x + _dot(y_ref[...], wo_ref[...])
    xn = (_rms(h1) * ln_ref[...]).astype(BF16)
    acc = h1
    fc = FFN_DENSE_CHUNK
    for c in range(FFN_DENSE // fc):
        gate = _dot(xn, wgu_ref[:, c * fc:(c + 1) * fc])
        up = _dot(xn, wgu_ref[:, FFN_DENSE + c * fc:FFN_DENSE + (c + 1) * fc])
        act = (_silu(gate) * up).astype(BF16)
        acc = acc + _dot(act, wd_ref[c * fc:(c + 1) * fc, :])
    out_ref[...] = acc


def _post0(y, xa, xb, w_out, ln, w_gu, w_down):
    T = y.shape[0]
    tm = TOK_TILE
    n_a = xa.shape[0] // tm
    tok = lambda n: pl.BlockSpec((tm, n), lambda i: (i, 0))
    return pl.pallas_call(
        functools.partial(_post0_kernel, n_a=n_a),
        grid=(T // tm,),
        in_specs=[tok(2 * D_MODEL), *_two_stream_specs(tm, n_a), _resident(w_out.shape),
                  _resident((1, D_MODEL)), _resident(w_gu.shape), _resident(w_down.shape)],
        out_specs=tok(D_MODEL),
        out_shape=jax.ShapeDtypeStruct((T, D_MODEL), F32),
        compiler_params=_cparams(1, 52),
        name="ret_out_ffn",
    )(y, xa, xb, w_out, ln, w_gu, w_down)


def _ssm_in_kernel(x_ref, ln_ref, wz_ref, wx_ref, wd_ref, wdh_ref, bias_ref, gate_ref, xbc_ref, dt_ref):
    xh, xl = _split2(_rms(x_ref[...]) * ln_ref[...])
    for j in range(SSM_DINNER // SLAB):
        gate_ref[j] = _silu(_dot(xh, wz_ref[:, j * SLAB:(j + 1) * SLAB])).astype(BF16)
    for j in range(SSM_CONV_DIM // SLAB):
        xbc_ref[:, j * SLAB:(j + 1) * SLAB] = _dot(xh, wx_ref[:, j * SLAB:(j + 1) * SLAB]).astype(BF16)
    both = _dot(xh, wd_ref[...])
    x = both[:, :LANES] + both[:, LANES:] + _dot(xl, wdh_ref[...]) + bias_ref[...]
    dt_ref[...] = jnp.maximum(x, 0.0) + jnp.log1p(jnp.exp(-jnp.abs(x)))


def _ssm_in(x, ln, wz, wx, wdh, wdl, bias_row):
    T = x.shape[0]
    tm = PROJ_TILE
    n_gate = SSM_DINNER // SLAB
    wd = jnp.concatenate([wdh, wdl], axis=1)
    tok = lambda n: pl.BlockSpec((tm, n), lambda i: (i, 0))
    return pl.pallas_call(
        _ssm_in_kernel,
        grid=(T // tm,),
        in_specs=[tok(D_MODEL), _resident((1, D_MODEL)), _resident(wz.shape), _resident(wx.shape),
                  _resident(wd.shape), _resident(wdh.shape), _resident(bias_row.shape)],
        out_specs=[pl.BlockSpec((n_gate, tm, SLAB), lambda i: (0, i, 0)), tok(SSM_CONV_DIM), tok(LANES)],
        out_shape=[jax.ShapeDtypeStruct((n_gate, T, SLAB), BF16), jax.ShapeDtypeStruct((T, SSM_CONV_DIM), BF16),
                   jax.ShapeDtypeStruct((T, LANES), F32)],
        compiler_params=_cparams(1, 52),
        name="ssm_in",
    )(x, ln, wz, wx, wd, wdh, bias_row)


def _conv_kernel(first_ref, last_ref, prev_ref, main_ref, next_ref, shift_ref, w_ref, b_ref, out_ref, ext_ref):
    i = pl.program_id(0)
    rb = SSD_BLOCK
    ext_rows = ext_ref.shape[0]

    @pl.when(i == 0)
    def _():
        ext_ref[rb + 2 * HALO:ext_rows, :] = jnp.zeros((ext_rows - rb - 2 * HALO, SSM_CONV_DIM), BF16)

    zero = jnp.zeros((HALO, SSM_CONV_DIM), BF16)
    ext_ref[0:HALO, :] = jnp.where(first_ref[i] == 1, zero, prev_ref[...])
    ext_ref[HALO:HALO + rb, :] = main_ref[...]
    ext_ref[HALO + rb:2 * HALO + rb, :] = jnp.where(last_ref[i] == 1, zero, next_ref[...])
    for s in range(SSM_CONV_DIM // SLAB):
        cs = slice(s * SLAB, (s + 1) * SLAB)
        for r0 in range(0, rb, CONV_ROWS):
            taps = _dot(shift_ref[...], ext_ref[r0:r0 + CONV_WINDOW, cs])
            acc = jnp.broadcast_to(b_ref[:, cs], (CONV_ROWS, SLAB))
            for j in range(SSM_CONV):
                acc = acc + taps[j * CONV_ROWS:(j + 1) * CONV_ROWS] * w_ref[j:j + 1, cs]
            out_ref[s, r0:r0 + CONV_ROWS, :] = _silu(acc).astype(BF16)


def _conv(xbc, conv_w, conv_b, first, last):
    T = xbc.shape[0]
    rb = SSD_BLOCK
    nb = T // rb
    per = rb // HALO
    nh = T // HALO
    r = np.arange(CONV_ROWS)
    shift = np.zeros((SSM_CONV * CONV_ROWS, CONV_WINDOW), np.float32)
    for j in range(SSM_CONV):
        shift[j * CONV_ROWS + r, HALO + r + j - SSM_CONV // 2] = 1.0
    shift = jnp.asarray(shift, dtype=BF16)
    ext_rows = rb - CONV_ROWS + CONV_WINDOW
    return pl.pallas_call(
        _conv_kernel,
        grid_spec=pltpu.PrefetchScalarGridSpec(
            num_scalar_prefetch=2, grid=(nb,),
            in_specs=[
                pl.BlockSpec((HALO, SSM_CONV_DIM), lambda i, f, l: (jnp.maximum(i * per - 1, 0), 0)),
                pl.BlockSpec((rb, SSM_CONV_DIM), lambda i, f, l: (i, 0)),
                pl.BlockSpec((HALO, SSM_CONV_DIM), lambda i, f, l: (jnp.minimum((i + 1) * per, nh - 1), 0)),
                _resident(shift.shape), _resident(conv_w.shape), _resident(conv_b.shape)],
            out_specs=pl.BlockSpec((SSM_CONV_DIM // SLAB, rb, SLAB), lambda i, f, l: (0, i, 0)),
            scratch_shapes=[pltpu.VMEM((ext_rows, SSM_CONV_DIM), BF16)]),
        out_shape=jax.ShapeDtypeStruct((SSM_CONV_DIM // SLAB, T, SLAB), BF16),
        compiler_params=_cparams(1, 48),
        name="ssm_conv",
    )(first, last, xbc, xbc, xbc, shift, conv_w, conv_b)


def _ssd_tables():
    r = np.arange(CHUNK)
    tri_l = (r[None, :] <= r[:, None]).astype(np.float32)
    tri_u = (r[None, :] >= r[:, None]).astype(np.float32)
    c = np.arange(SSM_DINNER) // SSM_HEADDIM
    j = np.arange(LANES)
    e_f = (j[:, None] == c[None, :]).astype(np.float32)
    e_b = (j[:, None] == (c[None, :] + SSM_HEADS)).astype(np.float32)
    as_bf16 = lambda x: jnp.asarray(x, dtype=BF16)
    return dict(tri_l3=as_bf16(np.concatenate([tri_l] * 3, axis=1)),
                tri_u3=as_bf16(np.concatenate([tri_u] * 3, axis=1)),
                e2_f=as_bf16(np.concatenate([e_f, e_f], axis=0)),
                e2_b=as_bf16(np.concatenate([e_b, e_b], axis=0)))


def _ssd_chunk_scalars(dt, a_ref, tri_l3_ref, tri_u3_ref):
    a = dt * a_ref[...]
    a3 = jnp.concatenate(_split3(a), axis=0)
    cum_f = _dot(tri_l3_ref[...], a3)
    cum_b = _dot(tri_u3_ref[...], a3)
    lane = lax.broadcasted_iota(jnp.int32, (CHUNK, LANES), 1)
    cum = jnp.where(lane < SSM_HEADS, cum_f, cum_b)
    tot = cum_f[CHUNK - 1:CHUNK, :]
    return cum, tot


def _expand(e2_ref, *rows):
    x = jnp.concatenate(rows, axis=0)
    hi, lo = _split2(x)
    return _dot(jnp.concatenate([hi, lo], axis=1), e2_ref[...])


def _ssd_bwd_kernel(reset_ref, xa_ref, dt_ref, a_ref, tri_l3_ref, tri_u3_ref, e2_ref,
                    yb_ref, state_ref):
    nch = SSD_BLOCK // CHUNK

    @pl.when(reset_ref[pl.program_id(0)] == 1)
    def _():
        state_ref[...] = jnp.zeros_like(state_ref)

    def chunk_body(ci, carry):
        rows = pl.ds(pl.multiple_of((nch - 1 - ci) * CHUNK, CHUNK), CHUNK)
        dt = dt_ref[rows, :]
        cum, tot = _ssd_chunk_scalars(dt, a_ref, tri_l3_ref, tri_u3_ref)
        ex = _expand(e2_ref, dt * jnp.exp(tot - cum), jnp.exp(cum),
                     jnp.broadcast_to(jnp.exp(tot), (16, LANES)))
        for g in range(SSM_GROUPS):
            gc = slice(g * SSM_GROUP_W, (g + 1) * SSM_GROUP_W)
            bm = xa_ref[SSM_GROUPS, rows, g * SSM_DSTATE:(g + 1) * SSM_DSTATE]
            cm = xa_ref[SSM_GROUPS + 1, rows, g * SSM_DSTATE:(g + 1) * SSM_DSTATE]
            state = state_ref[g]
            yb_ref[rows, gc] = (_dot(cm, state.astype(BF16)) * ex[CHUNK:2 * CHUNK, gc]).astype(BF16)
            xw = (xa_ref[g, rows, :].astype(F32) * ex[0:CHUNK, gc]).astype(BF16)
            state_ref[g] = state * ex[2 * CHUNK:2 * CHUNK + 1, gc] + _dot_tn(bm, xw)
        return carry

    lax.fori_loop(0, nch, chunk_body, 0)


def _ssd_fwd_kernel(reset_ref, xa_ref, dt_ref, gate_ref, yb_ref, a_ref, tri_l3_ref, tri_u3_ref,
                    e2_ref, dskip_ref, nw_ref, out_ref, state_ref, y_ref):
    nch = SSD_BLOCK // CHUNK

    @pl.when(reset_ref[pl.program_id(0)] == 1)
    def _():
        state_ref[...] = jnp.zeros_like(state_ref)

    def chunk_body(ci, carry):
        rows = pl.ds(pl.multiple_of(ci * CHUNK, CHUNK), CHUNK)
        dt = dt_ref[rows, :]
        cum, tot = _ssd_chunk_scalars(dt, a_ref, tri_l3_ref, tri_u3_ref)
        ex = _expand(e2_ref, dt * jnp.exp(tot - cum), jnp.exp(cum),
                     jnp.broadcast_to(jnp.exp(tot), (16, LANES)))
        row_t = (cum - jnp.log(dt)).T
        li = lax.broadcasted_iota(jnp.int32, (CHUNK, CHUNK), 0)
        si = lax.broadcasted_iota(jnp.int32, (CHUNK, CHUNK), 1)
        lower = li >= si
        first_half = si < SSM_HEADDIM
        for g in range(SSM_GROUPS):
            gc = slice(g * SSM_GROUP_W, (g + 1) * SSM_GROUP_W)
            bm = xa_ref[SSM_GROUPS, rows, g * SSM_DSTATE:(g + 1) * SSM_DSTATE]
            cm = xa_ref[SSM_GROUPS + 1, rows, g * SSM_DSTATE:(g + 1) * SSM_DSTATE]
            cb = _dot_nt(cm, bm)
            heads_per_group = SSM_HEADS // SSM_GROUPS
            for pair in range(heads_per_group // 2):
                mats = []
                for hh in range(2):
                    h = g * heads_per_group + 2 * pair + hh
                    hb = SSM_HEADS + h
                    seg = jnp.where(lower, cum[:, h:h + 1] - row_t[h:h + 1, :],
                                    cum[:, hb:hb + 1] - row_t[hb:hb + 1, :])
                    mats.append((cb * jnp.exp(seg)).astype(BF16))
                pc = slice(2 * pair * SSM_HEADDIM, (2 * pair + 2) * SSM_HEADDIM)
                yy = _dot(jnp.concatenate(mats, axis=0), xa_ref[g, rows, pc])
                y_ref[:, pc] = jnp.where(first_half, yy[0:CHUNK], yy[CHUNK:2 * CHUNK])
            state = state_ref[g]
            xs = xa_ref[g, rows, :].astype(F32)
            y = (y_ref[...] + _dot(cm, state.astype(BF16)) * ex[CHUNK:2 * CHUNK, gc]
                 + yb_ref[rows, gc].astype(F32) + xs * dskip_ref[:, gc])
            y = y * gate_ref[g, rows, :].astype(F32)
            out_ref[rows, gc] = (_rms(y) * nw_ref[:, gc]).astype(BF16)
            xw = (xs * ex[0:CHUNK, gc]).astype(BF16)
            state_ref[g] = state * ex[2 * CHUNK:2 * CHUNK + 1, gc] + _dot_tn(bm, xw)
        return carry

    lax.fori_loop(0, nch, chunk_body, 0)


def _ssd(xa, dt, gate, a_row, dskip, norm_w, reset_f, reset_b):
    T = dt.shape[0]
    nb = T // SSD_BLOCK
    tb = _ssd_tables()
    state = pltpu.VMEM((SSM_GROUPS, SSM_DSTATE, SSM_GROUP_W), F32)
    consts = [a_row, tb["tri_l3"], tb["tri_u3"]]
    const_specs = [_resident(c.shape) for c in consts]
    out_shape = jax.ShapeDtypeStruct((T, SSM_DINNER), BF16)

    rev = lambda i, r: (nb - 1 - i, 0)
    blk = lambda n, imap: pl.BlockSpec((SSD_BLOCK, n), imap)
    slabs = lambda a, imap: pl.BlockSpec((a.shape[0], SSD_BLOCK, SLAB), lambda i, r: (0, imap(i, r)[0], 0))
    y_b = pl.pallas_call(
        _ssd_bwd_kernel,
        grid_spec=pltpu.PrefetchScalarGridSpec(
            num_scalar_prefetch=1, grid=(nb,),
            in_specs=[slabs(xa, rev), blk(LANES, rev)] + const_specs + [_resident(tb["e2_b"].shape)],
            out_specs=blk(SSM_DINNER, rev), scratch_shapes=[state]),
        out_shape=out_shape, compiler_params=_cparams(1, 40), name="ssd_bwd",
    )(reset_b, xa, dt, *consts, tb["e2_b"])

    fwd = lambda i, r: (i, 0)
    return pl.pallas_call(
        _ssd_fwd_kernel,
        grid_spec=pltpu.PrefetchScalarGridSpec(
            num_scalar_prefetch=1, grid=(nb,),
            in_specs=[slabs(xa, fwd), blk(LANES, fwd), slabs(gate, fwd), blk(SSM_DINNER, fwd)]
            + const_specs + [_resident(tb["e2_f"].shape), _resident(dskip.shape), _resident(norm_w.shape)],
            out_specs=blk(SSM_DINNER, fwd),
            scratch_shapes=[state, pltpu.VMEM((CHUNK, SSM_GROUP_W), F32)]),
        out_shape=out_shape, compiler_params=_cparams(1, 52), name="ssd_fwd",
    )(reset_f, xa, dt, gate, y_b, *consts, tb["e2_f"], dskip, norm_w)


def _pack_halves(x):
    n = x.shape[1] // 2
    hi = lax.bitcast_convert_type(x[:, :n].astype(F32), jnp.uint32)
    lo = lax.bitcast_convert_type(x[:, n:].astype(F32), jnp.uint32)
    return hi | (lo >> 16)


def _unpack_halves(w):
    hi = lax.bitcast_convert_type(w & jnp.uint32(0xFFFF0000), F32)
    lo = lax.bitcast_convert_type(w << 16, F32)
    return hi, lo


def _router_kernel(y_ref, h_ref, wo_ref, ln_ref, wh_ref, wl_ref, b_ref, tri_ref,
                   h3_ref, xn_ref, info_ref, count_ref, base_ref):
    @pl.when(pl.program_id(0) == 0)
    def _():
        base_ref[...] = jnp.zeros_like(base_ref)

    h3 = h_ref[...] + _dot(y_ref[...], wo_ref[...])
    h3_ref[...] = h3
    xh, xl = _split2(_rms(h3) * ln_ref[...])
    xn_ref[...] = _pack_halves(xh)
    logits = _dot(xh, wh_ref[...]) + _dot(xl, wh_ref[...]) + _dot(xh, wl_ref[...]) + b_ref[...]
    col = lax.broadcasted_iota(jnp.int32, logits.shape, 1)
    m1 = jnp.max(logits, axis=-1, keepdims=True)
    i1 = jnp.min(jnp.where(logits == m1, col, LANES), axis=-1, keepdims=True)
    rest = jnp.where(col == i1, -jnp.inf, logits)
    m2 = jnp.max(rest, axis=-1, keepdims=True)
    i2 = jnp.min(jnp.where(rest == m2, col, LANES), axis=-1, keepdims=True)
    e = jnp.exp(m2 - m1)
    g1 = 1.0 / (1.0 + e)
    g2 = e / (1.0 + e)
    oh1 = col == i1
    oh2 = col == i2
    oh = jnp.where(oh1 | oh2, 1.0, 0.0)
    base = base_ref[...]
    prefix = _dot(tri_ref[...], oh.astype(BF16)) + base
    r1 = jnp.sum(jnp.where(oh1, prefix, 0.0), axis=-1, keepdims=True)
    r2 = jnp.sum(jnp.where(oh2, prefix, 0.0), axis=-1, keepdims=True)
    base = base + jnp.sum(oh, axis=0, keepdims=True)
    count_ref[...] = base
    base_ref[...] = base
    fields = [i1.astype(F32), i2.astype(F32), g1, g2, r1, r2]
    info = jnp.zeros(logits.shape, F32)
    for c, f in enumerate(fields):
        info = jnp.where(col == c, f, info)
    info_ref[...] = info[:, :ROUTE_COLS]


def _router(y, h, w_out, ln, wh, wl, b):
    T = h.shape[0]
    tb = MOE_TOK
    nb = T // tb
    r = np.arange(tb)
    tri = jnp.asarray((r[None, :] < r[:, None]).astype(np.float32), dtype=BF16)
    tok = lambda n: pl.BlockSpec((tb, n), lambda i: (i, 0))
    return pl.pallas_call(
        _router_kernel,
        grid=(nb,),
        in_specs=[tok(SSM_DINNER), tok(D_MODEL), _resident(w_out.shape), _resident((1, D_MODEL)),
                  _resident(wh.shape), _resident(wl.shape), _resident(b.shape), _resident(tri.shape)],
        out_specs=[tok(D_MODEL), tok(D_MODEL // 2), tok(ROUTE_COLS), pl.BlockSpec((1, LANES), lambda i: (0, 0))],
        out_shape=[jax.ShapeDtypeStruct((T, D_MODEL), F32), jax.ShapeDtypeStruct((T, D_MODEL // 2), jnp.uint32),
                   jax.ShapeDtypeStruct((T, ROUTE_COLS), F32), jax.ShapeDtypeStruct((1, LANES), F32)],
        scratch_shapes=[pltpu.VMEM((1, LANES), F32)],
        compiler_params=_cparams(1, 32),
        name="ssm_out_router",
    )(y, h, w_out, ln, wh, wl, b, tri)


def _sc_gather(table, idx):
    n_rows, width = idx.shape[0], table.shape[1]
    workers = SC_CORES * SC_SUBCORES
    nbuf = 2
    assert n_rows % (workers * SC_CHUNK * nbuf) == 0
    per_worker = n_rows // workers
    mesh = plsc.VectorSubcoreMesh(core_axis_name="c", subcore_axis_name="s")

    @functools.partial(
        pl.kernel, mesh=mesh,
        out_type=jax.ShapeDtypeStruct((n_rows, width), table.dtype),
        scratch_types=[pltpu.VMEM((nbuf, SC_CHUNK), jnp.int32),
                       pltpu.VMEM((nbuf, SC_CHUNK, width), table.dtype),
                       pltpu.SemaphoreType.DMA((nbuf,)),
                       pltpu.SemaphoreType.DMA((nbuf,))],
    )
    def gather_rows(table_hbm, idx_hbm, out_hbm, idx_v, rows_v, gsem, wsem):
        base = (lax.axis_index("s") * SC_CORES + lax.axis_index("c")) * per_worker

        def gather(c, b):
            off = pl.multiple_of(base + c * SC_CHUNK, 8)
            pltpu.sync_copy(idx_hbm.at[pl.ds(off, SC_CHUNK)], idx_v.at[b])
            return pltpu.make_async_copy(table_hbm.at[idx_v.at[b]], rows_v.at[b], gsem.at[b])

        def write(c, b):
            off = pl.multiple_of(base + c * SC_CHUNK, 8)
            return pltpu.make_async_copy(rows_v.at[b], out_hbm.at[pl.ds(off, SC_CHUNK)], wsem.at[b])

        @pl.loop(0, per_worker // SC_CHUNK, step=nbuf)
        def _(c0):
            copies = [gather(c0 + b, b) for b in range(nbuf)]
            for cp in copies:
                cp.start()
            writes = []
            for b, cp in enumerate(copies):
                cp.wait()
                writes.append(write(c0 + b, b))
                writes[-1].start()
            for wr in writes:
                wr.wait()

    return gather_rows(table, idx)


def _sc_scatter_pairs(table, pos, n_out):
    n_rows, width = table.shape
    workers = SC_CORES * SC_SUBCORES
    nbuf = 2
    assert n_rows % (workers * SC_CHUNK * nbuf) == 0
    per_worker = n_rows // workers
    mesh = plsc.VectorSubcoreMesh(core_axis_name="c", subcore_axis_name="s")

    @functools.partial(
        pl.kernel, mesh=mesh,
        out_type=jax.ShapeDtypeStruct((n_out, width), table.dtype),
        scratch_types=[pltpu.VMEM((nbuf, 2, SC_CHUNK), jnp.int32),
                       pltpu.VMEM((nbuf, SC_CHUNK, width), table.dtype),
                       pltpu.SemaphoreType.DMA((nbuf,)),
                       pltpu.SemaphoreType.DMA((nbuf,))],
    )
    def scatter_rows(table_hbm, pos_hbm, out_hbm, idx_v, rows_v, rsem, wsem):
        base = (lax.axis_index("s") * SC_CORES + lax.axis_index("c")) * per_worker

        def read(c, b):
            off = pl.multiple_of(base + c * SC_CHUNK, 8)
            return pltpu.make_async_copy(table_hbm.at[pl.ds(off, SC_CHUNK)], rows_v.at[b], rsem.at[b])

        def write(b, j):
            return pltpu.make_async_copy(rows_v.at[b], out_hbm.at[idx_v.at[b, j]], wsem.at[b])

        @pl.loop(0, per_worker // SC_CHUNK, step=nbuf)
        def _(c0):
            reads = [read(c0 + b, b) for b in range(nbuf)]
            for rd in reads:
                rd.start()
            for b in range(nbuf):
                pltpu.sync_copy(pos_hbm.at[base // SC_CHUNK + c0 + b], idx_v.at[b])
            writes = []
            for b, rd in enumerate(reads):
                rd.wait()
                for j in range(2):
                    writes.append(write(b, j))
                    writes[-1].start()
            for wr in writes:
                wr.wait()

    return scatter_rows(table, pos)


def _expert_kernel(texp_ref, trows_ref, x_ref, wg_ref, wu_ref, wd_ref, out_ref, acc_ref, xb_ref):
    i = pl.program_id(0)
    f = pl.program_id(1)
    nf = pl.num_programs(1)
    n = D_MODEL // 2

    @pl.when(trows_ref[i] > 0)
    def _():
        @pl.when(f == 0)
        def _():
            acc_ref[...] = jnp.zeros_like(acc_ref)
            row = lax.broadcasted_iota(jnp.int32, x_ref.shape, 0)
            hi, lo = _unpack_halves(jnp.where(row < trows_ref[i], x_ref[...], jnp.uint32(0)))
            xb_ref[:, :n] = hi.astype(BF16)
            xb_ref[:, n:] = lo.astype(BF16)

        x = xb_ref[...]
        act = (_silu(_dot(x, wg_ref[0].astype(BF16))) * _dot(x, wu_ref[0].astype(BF16))).astype(BF16)
        acc_ref[...] += _dot(act, wd_ref[0].astype(BF16))

        @pl.when(f == nf - 1)
        def _():
            out_ref[...] = _pack_halves(acc_ref[...].astype(BF16))

    @pl.when((trows_ref[i] == 0) & (f == nf - 1))
    def _():
        out_ref[...] = jnp.zeros_like(out_ref)


def _experts(xs, w_gu, w_down, tile_expert, tile_rows):
    n_tiles = xs.shape[0] // MOE_SLOT
    nf = FFN_EXPERT // MOE_FC
    return pl.pallas_call(
        _expert_kernel,
        grid_spec=pltpu.PrefetchScalarGridSpec(
            num_scalar_prefetch=2, grid=(n_tiles, nf),
            in_specs=[pl.BlockSpec((MOE_SLOT, D_MODEL // 2), lambda i, f, te, tv: (i, 0)),
                      pl.BlockSpec((1, D_MODEL, MOE_FC), lambda i, f, te, tv: (te[i], 0, f)),
                      pl.BlockSpec((1, D_MODEL, MOE_FC), lambda i, f, te, tv: (te[i], 0, nf + f)),
                      pl.BlockSpec((1, MOE_FC, D_MODEL), lambda i, f, te, tv: (te[i], f, 0))],
            out_specs=pl.BlockSpec((MOE_SLOT, D_MODEL // 2), lambda i, f, te, tv: (i, 0)),
            scratch_shapes=[pltpu.VMEM((MOE_SLOT, D_MODEL), F32), pltpu.VMEM((MOE_SLOT, D_MODEL), BF16)]),
        out_shape=jax.ShapeDtypeStruct(xs.shape, jnp.uint32),
        compiler_params=_cparams(2, 48),
        name="moe_experts",
    )(tile_expert, tile_rows, xs, w_gu, w_gu, w_down)


def _finalize_kernel(y0_ref, y1_ref, info_ref, h_ref, fn_ref, out_ref):
    n = D_MODEL // 2
    info = info_ref[...]
    g0 = info[:, 2:3]
    g1 = info[:, 3:4]
    a_hi, a_lo = _unpack_halves(y0_ref[...])
    b_hi, b_lo = _unpack_halves(y1_ref[...])
    left = h_ref[:, :n] + g0 * a_hi + g1 * b_hi
    right = h_ref[:, n:] + g0 * a_lo + g1 * b_lo
    ms = (jnp.sum(left * left, axis=-1, keepdims=True)
          + jnp.sum(right * right, axis=-1, keepdims=True)) * (1.0 / D_MODEL)
    scale = lax.rsqrt(ms + EPS)
    out_ref[:, :n] = left * scale * fn_ref[:, :n]
    out_ref[:, n:] = right * scale * fn_ref[:, n:]


def _finalize(yg, info, h3, final_norm, tok0):
    n = yg.shape[0] // 2
    tb = MOE_TOK
    nb = n // tb
    b0 = tok0 // tb
    tok = lambda w: pl.BlockSpec((tb, w), lambda i: (b0 + i, 0))
    loc = lambda off: pl.BlockSpec((tb, D_MODEL // 2), lambda i: (i + off, 0))
    return pl.pallas_call(
        _finalize_kernel,
        grid=(nb,),
        in_specs=[loc(0), loc(nb), tok(info.shape[1]), tok(D_MODEL), _resident((1, D_MODEL))],
        out_specs=pl.BlockSpec((tb, D_MODEL), lambda i: (i, 0)),
        out_shape=jax.ShapeDtypeStruct((n, D_MODEL), F32),
        compiler_params=_cparams(1, 32),
        name="moe_finalize",
    )(yg, yg, info, h3, final_norm)


def _moe(y_ssm, h2, w_out, ln, wr_hi, wr_lo, rb, w_gu, w_down, final_norm, n_a):
    T = h2.shape[0]
    n_tiles = 2 * T // MOE_SLOT + N_EXPERTS
    h3, xn, info, counts = _router(y_ssm, h2, w_out, ln, wr_hi, wr_lo, rb)

    counts = counts[0, :N_EXPERTS].astype(jnp.int32)
    tiles_e = (counts + MOE_SLOT - 1) // MOE_SLOT
    tiles_cum = jnp.cumsum(tiles_e)
    gstart = (tiles_cum - tiles_e) * MOE_SLOT
    routed = info.astype(jnp.int32)
    start_of = lambda e: jnp.sum(jnp.where(e[:, None] == jnp.arange(N_EXPERTS), gstart[None, :], 0), axis=1)
    pos0 = start_of(routed[:, 0]) + routed[:, 4]
    pos1 = start_of(routed[:, 1]) + routed[:, 5]
    tile_ids = jnp.arange(n_tiles, dtype=jnp.int32)
    tile_expert = jnp.minimum(jnp.sum(tile_ids[:, None] >= tiles_cum[None, :], axis=1), N_EXPERTS - 1)
    tile_expert = tile_expert.astype(jnp.int32)
    tile_rows = jnp.clip(counts[tile_expert] - (tile_ids * MOE_SLOT - gstart[tile_expert]), 0, MOE_SLOT)
    tile_rows = jnp.where(tile_ids < tiles_cum[-1], tile_rows, 0).astype(jnp.int32)
    pos = jnp.stack([pos0.reshape(-1, SC_CHUNK), pos1.reshape(-1, SC_CHUNK)], axis=1)

    xs = _sc_scatter_pairs(xn, pos, n_tiles * MOE_SLOT)
    ys = _experts(xs, w_gu, w_down, tile_expert, tile_rows)
    outs = []
    for lo, hi in ((0, n_a), (n_a, T)):
        yg = _sc_gather(ys, jnp.concatenate([pos0[lo:hi], pos1[lo:hi]]))
        outs.append(_finalize(yg, info, h3, final_norm, lo))
    return outs


def _seq_flags(seq_lens, block):
    first, last = [], []
    for n in seq_lens:
        nb = n // block
        first += [1] + [0] * (nb - 1)
        last += [0] * (nb - 1) + [1]
    return np.asarray(first, np.int32), np.asarray(last, np.int32)


def _trunk(xa, xb, seq_lens, ln_mix0, ret_w_in, ret_w_out, ln_ffn0, ffn_w_gu, ffn_w_down, ln_mix1, ssm_w_in,
           ssm_conv_w, ssm_conv_b, ssm_dt_bias, ssm_A_log, ssm_D, ssm_norm_w, ssm_w_out, ln_ffn1,
           moe_router_w, moe_router_b, moe_w_gu, moe_w_down, final_norm):
    assert all(n % b == 0 for n in seq_lens for b in (SCAN_BLOCK, SSD_BLOCK, MOE_TOK, TOK_TILE, PROJ_TILE))
    row = lambda v: v.astype(F32).reshape(1, -1)
    first, last = _seq_flags(seq_lens, SCAN_BLOCK)
    reset_f = jnp.asarray(first)
    reset_b = jnp.asarray(last[::-1].copy())
    ssd_first, ssd_last = _seq_flags(seq_lens, SSD_BLOCK)

    half = RET_DK // 2
    inv = ROPE_BASE ** (-jnp.arange(half, dtype=F32) / half)
    ang = jnp.arange(max(seq_lens)).astype(F32)[:, None] * inv[None, :]
    pos_block = jnp.asarray(np.concatenate([np.arange(n // TOK_TILE) for n in seq_lens]).astype(np.int32))
    q, k, v, g = _ret_in(xa, xb, row(ln_mix0), ret_w_in.astype(BF16), jnp.cos(ang), jnp.sin(ang), pos_block)
    y = _ret_scan(q, k, v, g, reset_f, reset_b)
    h2 = _post0(y, xa, xb, ret_w_out.astype(BF16), row(ln_ffn0), ffn_w_gu.astype(BF16),
                ffn_w_down.astype(BF16))

    wz = ssm_w_in[:, :SSM_DINNER].astype(BF16)
    wx = ssm_w_in[:, SSM_DINNER:SSM_DINNER + SSM_CONV_DIM].astype(BF16)
    wdt = jnp.pad(ssm_w_in[:, SSM_DINNER + SSM_CONV_DIM:], ((0, 0), (0, LANES - 2 * SSM_HEADS)))
    wdh, wdl = _split2(wdt)
    pad_row = lambda v: jnp.pad(v.astype(F32).reshape(1, -1), ((0, 0), (0, LANES - 2 * SSM_HEADS)))
    gate, xbc, dt = _ssm_in(h2, row(ln_mix1), wz, wx, wdh, wdl, pad_row(ssm_dt_bias))
    xact = _conv(xbc, ssm_conv_w.reshape(SSM_CONV, SSM_CONV_DIM).astype(F32), row(ssm_conv_b),
                 jnp.asarray(ssd_first), jnp.asarray(ssd_last))
    a_row = pad_row(-jnp.exp(ssm_A_log.astype(F32)))
    dskip = jnp.repeat(ssm_D.astype(F32), SSM_HEADDIM).reshape(1, SSM_DINNER)
    y = _ssd(xact, dt, gate, a_row, dskip, row(ssm_norm_w), jnp.asarray(ssd_first),
             jnp.asarray(ssd_last[::-1].copy()))

    wr = jnp.pad(moe_router_w.astype(F32), ((0, 0), (0, LANES - N_EXPERTS)))
    wr_hi, wr_lo = _split2(wr)
    rb = jnp.pad(moe_router_b.astype(F32).reshape(1, -1), ((0, 0), (0, LANES - N_EXPERTS)),
                 constant_values=-1e30)
    return _moe(y, h2, ssm_w_out.astype(BF16), row(ln_ffn1), wr_hi, wr_lo, rb,
                moe_w_gu, moe_w_down, row(final_norm), xa.shape[0])


def kernel(x_prompt, x_sample, ln_mix0, ret_w_in, ret_w_out, ln_ffn0, ffn_w_gu, ffn_w_down, ln_mix1, ssm_w_in,
           ssm_conv_w, ssm_conv_b, ssm_dt_bias, ssm_A_log, ssm_D, ssm_norm_w, ssm_w_out, ln_ffn1, moe_router_w,
           moe_router_b, moe_w_gu, moe_w_down, final_norm):
    seq_lens = [x_prompt.shape[1]] * x_prompt.shape[0] + [x_sample.shape[1]] * x_sample.shape[0]
    out_a, out_b = _trunk(x_prompt.reshape(-1, D_MODEL), x_sample.reshape(-1, D_MODEL), seq_lens, ln_mix0, ret_w_in, ret_w_out, ln_ffn0, ffn_w_gu, ffn_w_down, ln_mix1, ssm_w_in,
                 ssm_conv_w, ssm_conv_b, ssm_dt_bias, ssm_A_log, ssm_D, ssm_norm_w, ssm_w_out, ln_ffn1,
                 moe_router_w, moe_router_b, moe_w_gu, moe_w_down, final_norm)
    return (out_a.reshape(x_prompt.shape), out_b.reshape(x_sample.shape))
```

```python
import functools

import numpy as np
import jax
import jax.numpy as jnp
from jax import lax
from jax.experimental import pallas as pl
from jax.experimental.pallas import tpu as pltpu
from jax.experimental.pallas import tpu_sc as plsc

F32 = jnp.float32
BF16 = jnp.bfloat16

D_MODEL = 1024
EPS = 1e-6
RET_HEADS = 4
RET_DK = 256
RET_DV = 512
ROPE_BASE = 10000.0
SSM_DINNER = 2048
SSM_HEADDIM = 64
SSM_HEADS = 32
SSM_GROUPS = 4
SSM_DSTATE = 128
SSM_CONV = 5
SSM_CONV_DIM = 3072
SSM_GROUP_W = SSM_DINNER // SSM_GROUPS
SLAB = SSM_GROUP_W
CONV_ROWS = 128
CONV_WINDOW = 256
HALO = 16
FFN_DENSE = 2816
FFN_DENSE_CHUNK = 1408
N_EXPERTS = 8
FFN_EXPERT = 3584

CHUNK = 128
RET_CHUNK = 256
LANES = 128
TOK_TILE = 512
PROJ_TILE = 1024
SCAN_BLOCK = 512
SSD_BLOCK = 1024
MOE_TOK = 512
MOE_SLOT = 1024
MOE_FC = 512
ROUTE_COLS = 8
SC_CORES = 2
SC_SUBCORES = 16
SC_CHUNK = 64


def _cparams(n_axes, vmem_mb):
    return pltpu.CompilerParams(dimension_semantics=("arbitrary",) * n_axes,
                                vmem_limit_bytes=vmem_mb << 20)


def _resident(shape):
    nd = len(shape)
    return pl.BlockSpec(shape, lambda *_: (0,) * nd, pipeline_mode=pl.Buffered(1))


def _rms(x):
    return x * lax.rsqrt(jnp.mean(x * x, axis=-1, keepdims=True) + EPS)


def _silu(x):
    return x * jax.nn.sigmoid(x)


def _dot(a, b):
    return jnp.dot(a, b, preferred_element_type=F32)


def _dot_nt(a, b):
    return lax.dot_general(a, b, (((1,), (1,)), ((), ())), preferred_element_type=F32)


def _dot_tn(a, b):
    return lax.dot_general(a, b, (((0,), (0,)), ((), ())), preferred_element_type=F32)


def _split2(x):
    hi = x.astype(BF16)
    lo = (x - hi.astype(F32)).astype(BF16)
    return hi, lo


def _split3(x):
    hi = x.astype(BF16)
    r = x - hi.astype(F32)
    mid = r.astype(BF16)
    lo = (r - mid.astype(F32)).astype(BF16)
    return hi, mid, lo


def _two_stream_specs(tm, n_a):
    return (pl.BlockSpec((tm, D_MODEL), lambda i, *_: (jnp.minimum(i, n_a - 1), 0)),
            pl.BlockSpec((tm, D_MODEL), lambda i, *_: (jnp.maximum(i - n_a, 0), 0)))


def _ret_in_kernel(pos_ref, xa_ref, xb_ref, ln_ref, w_ref, cos_ref, sin_ref, q_ref, k_ref, v_ref, g_ref, *, n_a):
    x = jnp.where(pl.program_id(0) < n_a, xa_ref[...], xb_ref[...])
    xn = (_rms(x) * ln_ref[...]).astype(BF16)
    cos = cos_ref[...]
    sin = sin_ref[...]
    half = RET_DK // 2

    def rotary(dst, col0, scale):
        for h in range(RET_HEADS):
            c = col0 + h * RET_DK
            p = _dot(xn, w_ref[:, c:c + RET_DK])
            p1, p2 = p[:, :half], p[:, half:]
            dst[:, h * RET_DK:h * RET_DK + half] = ((p1 * cos - p2 * sin) * scale).astype(BF16)
            dst[:, h * RET_DK + half:(h + 1) * RET_DK] = ((p1 * sin + p2 * cos) * scale).astype(BF16)

    rotary(q_ref, 0, 1.0)
    rotary(k_ref, D_MODEL, RET_DK ** -0.5)
    for j in range(2 * D_MODEL // 512):
        v_ref[:, j * 512:(j + 1) * 512] = _dot(
            xn, w_ref[:, 2 * D_MODEL + j * 512:2 * D_MODEL + (j + 1) * 512]).astype(BF16)
        g_ref[:, j * 512:(j + 1) * 512] = _silu(_dot(
            xn, w_ref[:, 4 * D_MODEL + j * 512:4 * D_MODEL + (j + 1) * 512])).astype(BF16)


def _ret_in(xa, xb, ln, w_in, cos, sin, pos_block):
    T = xa.shape[0] + xb.shape[0]
    tm = TOK_TILE
    n_a = xa.shape[0] // tm
    tok = lambda n: pl.BlockSpec((tm, n), lambda i, pb: (i, 0))
    rot = pl.BlockSpec((tm, LANES), lambda i, pb: (pb[i], 0))
    return pl.pallas_call(
        functools.partial(_ret_in_kernel, n_a=n_a),
        grid_spec=pltpu.PrefetchScalarGridSpec(
            num_scalar_prefetch=1, grid=(T // tm,),
            in_specs=[*_two_stream_specs(tm, n_a), _resident((1, D_MODEL)), _resident(w_in.shape), rot, rot],
            out_specs=[tok(D_MODEL), tok(D_MODEL), tok(2 * D_MODEL), tok(2 * D_MODEL)]),
        out_shape=[jax.ShapeDtypeStruct((T, D_MODEL), BF16), jax.ShapeDtypeStruct((T, D_MODEL), BF16),
                   jax.ShapeDtypeStruct((T, 2 * D_MODEL), BF16), jax.ShapeDtypeStruct((T, 2 * D_MODEL), BF16)],
        compiler_params=_cparams(1, 48),
        name="ret_in",
    )(pos_block, xa, xb, ln, w_in, cos, sin)


def _ret_scan_kernel(reset_ref, *refs, reverse, chunk_decay):
    if reverse:
        q_ref, k_ref, v_ref, wq_ref, wk_ref, out_ref, state_ref = refs
    else:
        q_ref, k_ref, v_ref, ob_ref, g_ref, wq_ref, wk_ref, din_ref, out_ref, state_ref = refs
    nch = SCAN_BLOCK // RET_CHUNK

    @pl.when(reset_ref[pl.program_id(0)] == 1)
    def _():
        state_ref[...] = jnp.zeros_like(state_ref)

    def chunk_body(ci, carry):
        cj = nch - 1 - ci if reverse else ci
        rows = pl.ds(pl.multiple_of(cj * RET_CHUNK, RET_CHUNK), RET_CHUNK)
        for h in range(RET_HEADS):
            kcols = slice(h * RET_DK, (h + 1) * RET_DK)
            vcols = slice(h * RET_DV, (h + 1) * RET_DV)
            qh = q_ref[rows, kcols]
            kh = k_ref[rows, kcols]
            vh = v_ref[rows, vcols]
            state = state_ref[h]
            o = _dot(qh, state.astype(BF16)) * wq_ref[h]
            if reverse:
                out_ref[rows, vcols] = o.astype(BF16)
            else:
                s = _dot_nt(qh, kh) * din_ref[h]
                o = o + _dot(s.astype(BF16), vh) + ob_ref[rows, vcols].astype(F32)
                out_ref[rows, vcols] = (g_ref[rows, vcols].astype(F32) * _rms(o)).astype(BF16)
            ks = (kh.astype(F32) * wk_ref[h]).astype(BF16)
            state_ref[h] = state * chunk_decay[h] + _dot_tn(ks, vh)
        return carry

    lax.fori_loop(0, nch, chunk_body, 0)


def _ret_tables():
    lg = np.log1p(-np.power(2.0, -5.0 - np.arange(RET_HEADS, dtype=np.float32))).astype(np.float32)
    c = RET_CHUNK
    a = np.arange(c, dtype=np.float32)
    col = lambda e, w: np.broadcast_to(np.exp(lg[:, None, None] * e[None, :, None]),
                                       (RET_HEADS, c, w)).astype(np.float32)
    dist = np.abs(a[:, None] - a[None, :])
    return dict(
        din=np.exp(lg[:, None, None] * dist[None]).astype(np.float32),
        wq_f=col(a + 1.0, RET_DV), wk_f=col(c - 1.0 - a, RET_DK),
        wq_b=col(c - a, RET_DV), wk_b=col(a, RET_DK),
        chunk_decay=tuple(float(v) for v in np.exp(lg * c)),
    )


def _ret_scan(q, k, v, g, reset_f, reset_b):
    T = q.shape[0]
    nb = T // SCAN_BLOCK
    tb = _ret_tables()
    state = pltpu.VMEM((RET_HEADS, RET_DK, RET_DV), F32)
    out_shape = jax.ShapeDtypeStruct((T, 2 * D_MODEL), BF16)

    def specs(imap):
        blk = lambda n: pl.BlockSpec((SCAN_BLOCK, n), imap)
        return blk(D_MODEL), blk(D_MODEL), blk(2 * D_MODEL)

    rev = lambda i, r: (nb - 1 - i, 0)
    bq, bk, bv = specs(rev)
    o_b = pl.pallas_call(
        functools.partial(_ret_scan_kernel, reverse=True, chunk_decay=tb["chunk_decay"]),
        grid_spec=pltpu.PrefetchScalarGridSpec(
            num_scalar_prefetch=1, grid=(nb,),
            in_specs=[bq, bk, bv, _resident(tb["wq_b"].shape), _resident(tb["wk_b"].shape)],
            out_specs=bv, scratch_shapes=[state]),
        out_shape=out_shape, compiler_params=_cparams(1, 40), name="ret_scan_bwd",
    )(reset_b, q, k, v, tb["wq_b"], tb["wk_b"])

    fwd = lambda i, r: (i, 0)
    bq, bk, bv = specs(fwd)
    return pl.pallas_call(
        functools.partial(_ret_scan_kernel, reverse=False, chunk_decay=tb["chunk_decay"]),
        grid_spec=pltpu.PrefetchScalarGridSpec(
            num_scalar_prefetch=1, grid=(nb,),
            in_specs=[bq, bk, bv, bv, bv, _resident(tb["wq_f"].shape), _resident(tb["wk_f"].shape),
                      _resident(tb["din"].shape)],
            out_specs=bv, scratch_shapes=[state]),
        out_shape=out_shape, compiler_params=_cparams(1, 40), name="ret_scan_fwd",
    )(reset_f, q, k, v, o_b, g, tb["wq_f"], tb["wk_f"], tb["din"])


def _post0_kernel(y_ref, xa_ref, xb_ref, wo_ref, ln_ref, wgu_ref, wd_ref, out_ref, *, n_a):
    x = jnp.where(pl.program_id(0) < n_a, xa_ref[...], xb_ref[...])
    h1 = x + _dot(y_ref[...], wo_ref[...])
    xn = (_rms(h1) * ln_ref[...]).astype(BF16)
    acc = h1
    fc = FFN_DENSE_CHUNK
    for c in range(FFN_DENSE // fc):
        gate = _dot(xn, wgu_ref[:, c * fc:(c + 1) * fc])
        up = _dot(xn, wgu_ref[:, FFN_DENSE + c * fc:FFN_DENSE + (c + 1) * fc])
        act = (_silu(gate) * up).astype(BF16)
        acc = acc + _dot(act, wd_ref[c * fc:(c + 1) * fc, :])
    out_ref[...] = acc


def _post0(y, xa, xb, w_out, ln, w_gu, w_down):
    T = y.shape[0]
    tm = TOK_TILE
    n_a = xa.shape[0] // tm
    tok = lambda n: pl.BlockSpec((tm, n), lambda i: (i, 0))
    return pl.pallas_call(
        functools.partial(_post0_kernel, n_a=n_a),
        grid=(T // tm,),
        in_specs=[tok(2 * D_MODEL), *_two_stream_specs(tm, n_a), _resident(w_out.shape),
                  _resident((1, D_MODEL)), _resident(w_gu.shape), _resident(w_down.shape)],
        out_specs=tok(D_MODEL),
        out_shape=jax.ShapeDtypeStruct((T, D_MODEL), F32),
        compiler_params=_cparams(1, 52),
        name="ret_out_ffn",
    )(y, xa, xb, w_out, ln, w_gu, w_down)


def _ssm_in_kernel(x_ref, ln_ref, wz_ref, wx_ref, wd_ref, wdh_ref, bias_ref, gate_ref, xbc_ref, dt_ref):
    xh, xl = _split2(_rms(x_ref[...]) * ln_ref[...])
    for j in range(SSM_DINNER // SLAB):
        gate_ref[j] = _silu(_dot(xh, wz_ref[:, j * SLAB:(j + 1) * SLAB])).astype(BF16)
    for j in range(SSM_CONV_DIM // SLAB):
        xbc_ref[:, j * SLAB:(j + 1) * SLAB] = _dot(xh, wx_ref[:, j * SLAB:(j + 1) * SLAB]).astype(BF16)
    both = _dot(xh, wd_ref[...])
    x = both[:, :LANES] + both[:, LANES:] + _dot(xl, wdh_ref[...]) + bias_ref[...]
    dt_ref[...] = jnp.maximum(x, 0.0) + jnp.log1p(jnp.exp(-jnp.abs(x)))


def _ssm_in(x, ln, wz, wx, wdh, wdl, bias_row):
    T = x.shape[0]
    tm = PROJ_TILE
    n_gate = SSM_DINNER // SLAB
    wd = jnp.concatenate([wdh, wdl], axis=1)
    tok = lambda n: pl.BlockSpec((tm, n), lambda i: (i, 0))
    return pl.pallas_call(
        _ssm_in_kernel,
        grid=(T // tm,),
        in_specs=[tok(D_MODEL), _resident((1, D_MODEL)), _resident(wz.shape), _resident(wx.shape),
                  _resident(wd.shape), _resident(wdh.shape), _resident(bias_row.shape)],
        out_specs=[pl.BlockSpec((n_gate, tm, SLAB), lambda i: (0, i, 0)), tok(SSM_CONV_DIM), tok(LANES)],
        out_shape=[jax.ShapeDtypeStruct((n_gate, T, SLAB), BF16), jax.ShapeDtypeStruct((T, SSM_CONV_DIM), BF16),
                   jax.ShapeDtypeStruct((T, LANES), F32)],
        compiler_params=_cparams(1, 52),
        name="ssm_in",
    )(x, ln, wz, wx, wd, wdh, bias_row)


def _conv_kernel(first_ref, last_ref, prev_ref, main_ref, next_ref, shift_ref, w_ref, b_ref, out_ref, ext_ref):
    i = pl.program_id(0)
    rb = SSD_BLOCK
    ext_rows = ext_ref.shape[0]

    @pl.when(i == 0)
    def _():
        ext_ref[rb + 2 * HALO:ext_rows, :] = jnp.zeros((ext_rows - rb - 2 * HALO, SSM_CONV_DIM), BF16)

    zero = jnp.zeros((HALO, SSM_CONV_DIM), BF16)
    ext_ref[0:HALO, :] = jnp.where(first_ref[i] == 1, zero, prev_ref[...])
    ext_ref[HALO:HALO + rb, :] = main_ref[...]
    ext_ref[HALO + rb:2 * HALO + rb, :] = jnp.where(last_ref[i] == 1, zero, next_ref[...])
    for s in range(SSM_CONV_DIM // SLAB):
        cs = slice(s * SLAB, (s + 1) * SLAB)
        for r0 in range(0, rb, CONV_ROWS):
            taps = _dot(shift_ref[...], ext_ref[r0:r0 + CONV_WINDOW, cs])
            acc = jnp.broadcast_to(b_ref[:, cs], (CONV_ROWS, SLAB))
            for j in range(SSM_CONV):
                acc = acc + taps[j * CONV_ROWS:(j + 1) * CONV_ROWS] * w_ref[j:j + 1, cs]
            out_ref[s, r0:r0 + CONV_ROWS, :] = _silu(acc).astype(BF16)


def _conv(xbc, conv_w, conv_b, first, last):
    T = xbc.shape[0]
    rb = SSD_BLOCK
    nb = T // rb
    per = rb // HALO
    nh = T // HALO
    r = np.arange(CONV_ROWS)
    shift = np.zeros((SSM_CONV * CONV_ROWS, CONV_WINDOW), np.float32)
    for j in range(SSM_CONV):
        shift[j * CONV_ROWS + r, HALO + r + j - SSM_CONV // 2] = 1.0
    shift = jnp.asarray(shift, dtype=BF16)
    ext_rows = rb - CONV_ROWS + CONV_WINDOW
    return pl.pallas_call(
        _conv_kernel,
        grid_spec=pltpu.PrefetchScalarGridSpec(
            num_scalar_prefetch=2, grid=(nb,),
            in_specs=[
                pl.BlockSpec((HALO, SSM_CONV_DIM), lambda i, f, l: (jnp.maximum(i * per - 1, 0), 0)),
                pl.BlockSpec((rb, SSM_CONV_DIM), lambda i, f, l: (i, 0)),
                pl.BlockSpec((HALO, SSM_CONV_DIM), lambda i, f, l: (jnp.minimum((i + 1) * per, nh - 1), 0)),
                _resident(shift.shape), _resident(conv_w.shape), _resident(conv_b.shape)],
            out_specs=pl.BlockSpec((SSM_CONV_DIM // SLAB, rb, SLAB), lambda i, f, l: (0, i, 0)),
            scratch_shapes=[pltpu.VMEM((ext_rows, SSM_CONV_DIM), BF16)]),
        out_shape=jax.ShapeDtypeStruct((SSM_CONV_DIM // SLAB, T, SLAB), BF16),
        compiler_params=_cparams(1, 48),
        name="ssm_conv",
    )(first, last, xbc, xbc, xbc, shift, conv_w, conv_b)


def _ssd_tables():
    r = np.arange(CHUNK)
    tri_l = (r[None, :] <= r[:, None]).astype(np.float32)
    tri_u = (r[None, :] >= r[:, None]).astype(np.float32)
    c = np.arange(SSM_DINNER) // SSM_HEADDIM
    j = np.arange(LANES)
    e_f = (j[:, None] == c[None, :]).astype(np.float32)
    e_b = (j[:, None] == (c[None, :] + SSM_HEADS)).astype(np.float32)
    as_bf16 = lambda x: jnp.asarray(x, dtype=BF16)
    return dict(tri_l3=as_bf16(np.concatenate([tri_l] * 3, axis=1)),
                tri_u3=as_bf16(np.concatenate([tri_u] * 3, axis=1)),
                e2_f=as_bf16(np.concatenate([e_f, e_f], axis=0)),
                e2_b=as_bf16(np.concatenate([e_b, e_b], axis=0)))


def _ssd_chunk_scalars(dt, a_ref, tri_l3_ref, tri_u3_ref):
    a = dt * a_ref[...]
    a3 = jnp.concatenate(_split3(a), axis=0)
    cum_f = _dot(tri_l3_ref[...], a3)
    cum_b = _dot(tri_u3_ref[...], a3)
    lane = lax.broadcasted_iota(jnp.int32, (CHUNK, LANES), 1)
    cum = jnp.where(lane < SSM_HEADS, cum_f, cum_b)
    tot = cum_f[CHUNK - 1:CHUNK, :]
    return cum, tot


def _expand(e2_ref, *rows):
    x = jnp.concatenate(rows, axis=0)
    hi, lo = _split2(x)
    return _dot(jnp.concatenate([hi, lo], axis=1), e2_ref[...])


def _ssd_bwd_kernel(reset_ref, xa_ref, dt_ref, a_ref, tri_l3_ref, tri_u3_ref, e2_ref,
                    yb_ref, state_ref):
    nch = SSD_BLOCK // CHUNK

    @pl.when(reset_ref[pl.program_id(0)] == 1)
    def _():
        state_ref[...] = jnp.zeros_like(state_ref)

    def chunk_body(ci, carry):
        rows = pl.ds(pl.multiple_of((nch - 1 - ci) * CHUNK, CHUNK), CHUNK)
        dt = dt_ref[rows, :]
        cum, tot = _ssd_chunk_scalars(dt, a_ref, tri_l3_ref, tri_u3_ref)
        ex = _expand(e2_ref, dt * jnp.exp(tot - cum), jnp.exp(cum),
                     jnp.broadcast_to(jnp.exp(tot), (16, LANES)))
        for g in range(SSM_GROUPS):
            gc = slice(g * SSM_GROUP_W, (g + 1) * SSM_GROUP_W)
            bm = xa_ref[SSM_GROUPS, rows, g * SSM_DSTATE:(g + 1) * SSM_DSTATE]
            cm = xa_ref[SSM_GROUPS + 1, rows, g * SSM_DSTATE:(g + 1) * SSM_DSTATE]
            state = state_ref[g]
            yb_ref[rows, gc] = (_dot(cm, state.astype(BF16)) * ex[CHUNK:2 * CHUNK, gc]).astype(BF16)
            xw = (xa_ref[g, rows, :].astype(F32) * ex[0:CHUNK, gc]).astype(BF16)
            state_ref[g] = state * ex[2 * CHUNK:2 * CHUNK + 1, gc] + _dot_tn(bm, xw)
        return carry

    lax.fori_loop(0, nch, chunk_body, 0)


def _ssd_fwd_kernel(reset_ref, xa_ref, dt_ref, gate_ref, yb_ref, a_ref, tri_l3_ref, tri_u3_ref,
                    e2_ref, dskip_ref, nw_ref, out_ref, state_ref, y_ref):
    nch = SSD_BLOCK // CHUNK

    @pl.when(reset_ref[pl.program_id(0)] == 1)
    def _():
        state_ref[...] = jnp.zeros_like(state_ref)

    def chunk_body(ci, carry):
        rows = pl.ds(pl.multiple_of(ci * CHUNK, CHUNK), CHUNK)
        dt = dt_ref[rows, :]
        cum, tot = _ssd_chunk_scalars(dt, a_ref, tri_l3_ref, tri_u3_ref)
        ex = _expand(e2_ref, dt * jnp.exp(tot - cum), jnp.exp(cum),
                     jnp.broadcast_to(jnp.exp(tot), (16, LANES)))
        row_t = (cum - jnp.log(dt)).T
        li = lax.broadcasted_iota(jnp.int32, (CHUNK, CHUNK), 0)
        si = lax.broadcasted_iota(jnp.int32, (CHUNK, CHUNK), 1)
        lower = li >= si
        first_half = si < SSM_HEADDIM
        for g in range(SSM_GROUPS):
            gc = slice(g * SSM_GROUP_W, (g + 1) * SSM_GROUP_W)
            bm = xa_ref[SSM_GROUPS, rows, g * SSM_DSTATE:(g + 1) * SSM_DSTATE]
            cm = xa_ref[SSM_GROUPS + 1, rows, g * SSM_DSTATE:(g + 1) * SSM_DSTATE]
            cb = _dot_nt(cm, bm)
            heads_per_group = SSM_HEADS // SSM_GROUPS
            for pair in range(heads_per_group // 2):
                mats = []
                for hh in range(2):
                    h = g * heads_per_group + 2 * pair + hh
                    hb = SSM_HEADS + h
                    seg = jnp.where(lower, cum[:, h:h + 1] - row_t[h:h + 1, :],
                                    cum[:, hb:hb + 1] - row_t[hb:hb + 1, :])
                    mats.append((cb * jnp.exp(seg)).astype(BF16))
                pc = slice(2 * pair * SSM_HEADDIM, (2 * pair + 2) * SSM_HEADDIM)
                yy = _dot(jnp.concatenate(mats, axis=0), xa_ref[g, rows, pc])
                y_ref[:, pc] = jnp.where(first_half, yy[0:CHUNK], yy[CHUNK:2 * CHUNK])
            state = state_ref[g]
            xs = xa_ref[g, rows, :].astype(F32)
            y = (y_ref[...] + _dot(cm, state.astype(BF16)) * ex[CHUNK:2 * CHUNK, gc]
                 + yb_ref[rows, gc].astype(F32) + xs * dskip_ref[:, gc])
            y = y * gate_ref[g, rows, :].astype(F32)
            out_ref[rows, gc] = (_rms(y) * nw_ref[:, gc]).astype(BF16)
            xw = (xs * ex[0:CHUNK, gc]).astype(BF16)
            state_ref[g] = state * ex[2 * CHUNK:2 * CHUNK + 1, gc] + _dot_tn(bm, xw)
        return carry

    lax.fori_loop(0, nch, chunk_body, 0)


def _ssd(xa, dt, gate, a_row, dskip, norm_w, reset_f, reset_b):
    T = dt.shape[0]
    nb = T // SSD_BLOCK
    tb = _ssd_tables()
    state = pltpu.VMEM((SSM_GROUPS, SSM_DSTATE, SSM_GROUP_W), F32)
    consts = [a_row, tb["tri_l3"], tb["tri_u3"]]
    const_specs = [_resident(c.shape) for c in consts]
    out_shape = jax.ShapeDtypeStruct((T, SSM_DINNER), BF16)

    rev = lambda i, r: (nb - 1 - i, 0)
    blk = lambda n, imap: pl.BlockSpec((SSD_BLOCK, n), imap)
    slabs = lambda a, imap: pl.BlockSpec((a.shape[0], SSD_BLOCK, SLAB), lambda i, r: (0, imap(i, r)[0], 0))
    y_b = pl.pallas_call(
        _ssd_bwd_kernel,
        grid_spec=pltpu.PrefetchScalarGridSpec(
            num_scalar_prefetch=1, grid=(nb,),
            in_specs=[slabs(xa, rev), blk(LANES, rev)] + const_specs + [_resident(tb["e2_b"].shape)],
            out_specs=blk(SSM_DINNER, rev), scratch_shapes=[state]),
        out_shape=out_shape, compiler_params=_cparams(1, 40), name="ssd_bwd",
    )(reset_b, xa, dt, *consts, tb["e2_b"])

    fwd = lambda i, r: (i, 0)
    return pl.pallas_call(
        _ssd_fwd_kernel,
        grid_spec=pltpu.PrefetchScalarGridSpec(
            num_scalar_prefetch=1, grid=(nb,),
            in_specs=[slabs(xa, fwd), blk(LANES, fwd), slabs(gate, fwd), blk(SSM_DINNER, fwd)]
            + const_specs + [_resident(tb["e2_f"].shape), _resident(dskip.shape), _resident(norm_w.shape)],
            out_specs=blk(SSM_DINNER, fwd),
            scratch_shapes=[state, pltpu.VMEM((CHUNK, SSM_GROUP_W), F32)]),
        out_shape=out_shape, compiler_params=_cparams(1, 52), name="ssd_fwd",
    )(reset_f, xa, dt, gate, y_b, *consts, tb["e2_f"], dskip, norm_w)


def _pack_halves(x):
    n = x.shape[1] // 2
    hi = lax.bitcast_convert_type(x[:, :n].astype(F32), jnp.uint32)
    lo = lax.bitcast_convert_type(x[:, n:].astype(F32), jnp.uint32)
    return hi | (lo >> 16)


def _unpack_halves(w):
    hi = lax.bitcast_convert_type(w & jnp.uint32(0xFFFF0000), F32)
    lo = lax.bitcast_convert_type(w << 16, F32)
    return hi, lo


def _router_kernel(y_ref, h_ref, wo_ref, ln_ref, wh_ref, wl_ref, b_ref, tri_ref,
                   h3_ref, xn_ref, info_ref, info_t_ref, count_ref, base_ref):
    @pl.when(pl.program_id(0) == 0)
    def _():
        base_ref[...] = jnp.zeros_like(base_ref)

    h3 = h_ref[...] + _dot(y_ref[...], wo_ref[...])
    h3_ref[...] = h3
    xh, xl = _split2(_rms(h3) * ln_ref[...])
    xn_ref[...] = _pack_halves(xh)
    logits = _dot(xh, wh_ref[...]) + _dot(xl, wh_ref[...]) + _dot(xh, wl_ref[...]) + b_ref[...]
    col = lax.broadcasted_iota(jnp.int32, logits.shape, 1)
    m1 = jnp.max(logits, axis=-1, keepdims=True)
    i1 = jnp.min(jnp.where(logits == m1, col, LANES), axis=-1, keepdims=True)
    rest = jnp.where(col == i1, -jnp.inf, logits)
    m2 = jnp.max(rest, axis=-1, keepdims=True)
    i2 = jnp.min(jnp.where(rest == m2, col, LANES), axis=-1, keepdims=True)
    e = jnp.exp(m2 - m1)
    g1 = 1.0 / (1.0 + e)
    g2 = e / (1.0 + e)
    oh1 = col == i1
    oh2 = col == i2
    oh = jnp.where(oh1 | oh2, 1.0, 0.0)
    base = base_ref[...]
    prefix = _dot(tri_ref[...], oh.astype(BF16)) + base
    r1 = jnp.sum(jnp.where(oh1, prefix, 0.0), axis=-1, keepdims=True)
    r2 = jnp.sum(jnp.where(oh2, prefix, 0.0), axis=-1, keepdims=True)
    base = base + jnp.sum(oh, axis=0, keepdims=True)
    count_ref[...] = base
    base_ref[...] = base
    fields = [i1.astype(F32), i2.astype(F32), g1, g2, r1, r2]
    info = jnp.zeros(logits.shape, F32)
    for c, f in enumerate(fields):
        info = jnp.where(col == c, f, info)
    info_ref[...] = info[:, :ROUTE_COLS]
    info_t_ref[...] = info.T[:ROUTE_COLS, :]


def _router(y, h, w_out, ln, wh, wl, b):
    T = h.shape[0]
    tb = MOE_TOK
    nb = T // tb
    r = np.arange(tb)
    tri = jnp.asarray((r[None, :] < r[:, None]).astype(np.float32), dtype=BF16)
    tok = lambda n: pl.BlockSpec((tb, n), lambda i: (i, 0))
    return pl.pallas_call(
        _router_kernel,
        grid=(nb,),
        in_specs=[tok(SSM_DINNER), tok(D_MODEL), _resident(w_out.shape), _resident((1, D_MODEL)),
                  _resident(wh.shape), _resident(wl.shape), _resident(b.shape), _resident(tri.shape)],
        out_specs=[tok(D_MODEL), tok(D_MODEL // 2), tok(ROUTE_COLS),
                   pl.BlockSpec((ROUTE_COLS, tb), lambda i: (0, i)), pl.BlockSpec((1, LANES), lambda i: (0, 0))],
        out_shape=[jax.ShapeDtypeStruct((T, D_MODEL), F32), jax.ShapeDtypeStruct((T, D_MODEL // 2), jnp.uint32),
                   jax.ShapeDtypeStruct((T, ROUTE_COLS), F32), jax.ShapeDtypeStruct((ROUTE_COLS, T), F32),
                   jax.ShapeDtypeStruct((1, LANES), F32)],
        scratch_shapes=[pltpu.VMEM((1, LANES), F32)],
        compiler_params=_cparams(1, 32),
        name="ssm_out_router",
    )(y, h, w_out, ln, wh, wl, b, tri)


def _sc_gather(table, idx):
    n_rows, width = idx.shape[0], table.shape[1]
    workers = SC_CORES * SC_SUBCORES
    nbuf = 2
    assert n_rows % (workers * SC_CHUNK * nbuf) == 0
    per_worker = n_rows // workers
    mesh = plsc.VectorSubcoreMesh(core_axis_name="c", subcore_axis_name="s")

    @functools.partial(
        pl.kernel, mesh=mesh,
        out_type=jax.ShapeDtypeStruct((n_rows, width), table.dtype),
        scratch_types=[pltpu.VMEM((nbuf, SC_CHUNK), jnp.int32),
                       pltpu.VMEM((nbuf, SC_CHUNK, width), table.dtype),
                       pltpu.SemaphoreType.DMA((nbuf,)),
                       pltpu.SemaphoreType.DMA((nbuf,))],
    )
    def gather_rows(table_hbm, idx_hbm, out_hbm, idx_v, rows_v, gsem, wsem):
        base = (lax.axis_index("s") * SC_CORES + lax.axis_index("c")) * per_worker

        def gather(c, b):
            off = pl.multiple_of(base + c * SC_CHUNK, 8)
            pltpu.sync_copy(idx_hbm.at[pl.ds(off, SC_CHUNK)], idx_v.at[b])
            return pltpu.make_async_copy(table_hbm.at[idx_v.at[b]], rows_v.at[b], gsem.at[b])

        def write(c, b):
            off = pl.multiple_of(base + c * SC_CHUNK, 8)
            return pltpu.make_async_copy(rows_v.at[b], out_hbm.at[pl.ds(off, SC_CHUNK)], wsem.at[b])

        @pl.loop(0, per_worker // SC_CHUNK, step=nbuf)
        def _(c0):
            copies = [gather(c0 + b, b) for b in range(nbuf)]
            for cp in copies:
                cp.start()
            writes = []
            for b, cp in enumerate(copies):
                cp.wait()
                writes.append(write(c0 + b, b))
                writes[-1].start()
            for wr in writes:
                wr.wait()

    return gather_rows(table, idx)


def _sc_scatter_pairs(table, pos, n_out):
    n_rows, width = table.shape
    workers = SC_CORES * SC_SUBCORES
    nbuf = 2
    assert n_rows % (workers * SC_CHUNK * nbuf) == 0
    per_worker = n_rows // workers
    mesh = plsc.VectorSubcoreMesh(core_axis_name="c", subcore_axis_name="s")

    @functools.partial(
        pl.kernel, mesh=mesh,
        out_type=jax.ShapeDtypeStruct((n_out, width), table.dtype),
        scratch_types=[pltpu.VMEM((nbuf, 2, SC_CHUNK), jnp.int32),
                       pltpu.VMEM((nbuf, SC_CHUNK, width), table.dtype),
                       pltpu.SemaphoreType.DMA((nbuf,)),
                       pltpu.SemaphoreType.DMA((nbuf,))],
    )
    def scatter_rows(table_hbm, pos_hbm, out_hbm, idx_v, rows_v, rsem, wsem):
        base = (lax.axis_index("s") * SC_CORES + lax.axis_index("c")) * per_worker

        def read(c, b):
            off = pl.multiple_of(base + c * SC_CHUNK, 8)
            return pltpu.make_async_copy(table_hbm.at[pl.ds(off, SC_CHUNK)], rows_v.at[b], rsem.at[b])

        def write(b, j):
            return pltpu.make_async_copy(rows_v.at[b], out_hbm.at[idx_v.at[b, j]], wsem.at[b])

        @pl.loop(0, per_worker // SC_CHUNK, step=nbuf)
        def _(c0):
            reads = [read(c0 + b, b) for b in range(nbuf)]
            for rd in reads:
                rd.start()
            for b in range(nbuf):
                pltpu.sync_copy(pos_hbm.at[base // SC_CHUNK + c0 + b], idx_v.at[b])
            writes = []
            for b, rd in enumerate(reads):
                rd.wait()
                for j in range(2):
                    writes.append(write(b, j))
                    writes[-1].start()
            for wr in writes:
                wr.wait()

    return scatter_rows(table, pos)


def _expert_kernel(texp_ref, trows_ref, x_ref, wg_ref, wu_ref, wd_ref, out_ref, acc_ref, xb_ref):
    i = pl.program_id(0)
    f = pl.program_id(1)
    nf = pl.num_programs(1)
    n = D_MODEL // 2

    @pl.when(trows_ref[i] > 0)
    def _():
        @pl.when(f == 0)
        def _():
            acc_ref[...] = jnp.zeros_like(acc_ref)
            row = lax.broadcasted_iota(jnp.int32, x_ref.shape, 0)
            hi, lo = _unpack_halves(jnp.where(row < trows_ref[i], x_ref[...], jnp.uint32(0)))
            xb_ref[:, :n] = hi.astype(BF16)
            xb_ref[:, n:] = lo.astype(BF16)

        def ffn_chunk(m):
            x = xb_ref[0:m, :]
            act = (_silu(_dot(x, wg_ref[0].astype(BF16))) * _dot(x, wu_ref[0].astype(BF16))).astype(BF16)
            acc_ref[0:m, :] += _dot(act, wd_ref[0].astype(BF16))

        @pl.when(trows_ref[i] > MOE_SLOT // 2)
        def _():
            ffn_chunk(MOE_SLOT)

        @pl.when(trows_ref[i] <= MOE_SLOT // 2)
        def _():
            ffn_chunk(MOE_SLOT // 2)

        @pl.when(f == nf - 1)
        def _():
            out_ref[...] = _pack_halves(acc_ref[...].astype(BF16))

    @pl.when((trows_ref[i] == 0) & (f == nf - 1))
    def _():
        out_ref[...] = jnp.zeros_like(out_ref)


def _experts(xs, w_gu, w_down, tile_expert, tile_rows):
    n_tiles = xs.shape[0] // MOE_SLOT
    nf = FFN_EXPERT // MOE_FC
    return pl.pallas_call(
        _expert_kernel,
        grid_spec=pltpu.PrefetchScalarGridSpec(
            num_scalar_prefetch=2, grid=(n_tiles, nf),
            in_specs=[pl.BlockSpec((MOE_SLOT, D_MODEL // 2), lambda i, f, te, tv: (i, 0)),
                      pl.BlockSpec((1, D_MODEL, MOE_FC), lambda i, f, te, tv: (te[i], 0, f)),
                      pl.BlockSpec((1, D_MODEL, MOE_FC), lambda i, f, te, tv: (te[i], 0, nf + f)),
                      pl.BlockSpec((1, MOE_FC, D_MODEL), lambda i, f, te, tv: (te[i], f, 0))],
            out_specs=pl.BlockSpec((MOE_SLOT, D_MODEL // 2), lambda i, f, te, tv: (i, 0)),
            scratch_shapes=[pltpu.VMEM((MOE_SLOT, D_MODEL), F32), pltpu.VMEM((MOE_SLOT, D_MODEL), BF16)]),
        out_shape=jax.ShapeDtypeStruct(xs.shape, jnp.uint32),
        compiler_params=_cparams(2, 48),
        name="moe_experts",
    )(tile_expert, tile_rows, xs, w_gu, w_gu, w_down)


def _finalize_kernel(y0_ref, y1_ref, info_ref, h_ref, fn_ref, out_ref):
    n = D_MODEL // 2
    info = info_ref[...]
    g0 = info[:, 2:3]
    g1 = info[:, 3:4]
    a_hi, a_lo = _unpack_halves(y0_ref[...])
    b_hi, b_lo = _unpack_halves(y1_ref[...])
    left = h_ref[:, :n] + g0 * a_hi + g1 * b_hi
    right = h_ref[:, n:] + g0 * a_lo + g1 * b_lo
    ms = (jnp.sum(left * left, axis=-1, keepdims=True)
          + jnp.sum(right * right, axis=-1, keepdims=True)) * (1.0 / D_MODEL)
    scale = lax.rsqrt(ms + EPS)
    out_ref[:, :n] = left * scale * fn_ref[:, :n]
    out_ref[:, n:] = right * scale * fn_ref[:, n:]


def _finalize(yg, info, h3, final_norm, tok0):
    n = yg.shape[0] // 2
    tb = MOE_TOK
    nb = n // tb
    b0 = tok0 // tb
    tok = lambda w: pl.BlockSpec((tb, w), lambda i: (b0 + i, 0))
    loc = lambda off: pl.BlockSpec((tb, D_MODEL // 2), lambda i: (i + off, 0))
    return pl.pallas_call(
        _finalize_kernel,
        grid=(nb,),
        in_specs=[loc(0), loc(nb), tok(info.shape[1]), tok(D_MODEL), _resident((1, D_MODEL))],
        out_specs=pl.BlockSpec((tb, D_MODEL), lambda i: (i, 0)),
        out_shape=jax.ShapeDtypeStruct((n, D_MODEL), F32),
        compiler_params=_cparams(1, 32),
        name="moe_finalize",
    )(yg, yg, info, h3, final_norm)


def _moe(y_ssm, h2, w_out, ln, wr_hi, wr_lo, rb, w_gu, w_down, final_norm, n_a):
    T = h2.shape[0]
    n_tiles = 2 * T // MOE_SLOT + N_EXPERTS
    h3, xn, info, info_t, counts = _router(y_ssm, h2, w_out, ln, wr_hi, wr_lo, rb)

    counts = counts[0, :N_EXPERTS].astype(jnp.int32)
    tiles_e = (counts + MOE_SLOT - 1) // MOE_SLOT
    tiles_cum = jnp.cumsum(tiles_e)
    gstart = (tiles_cum - tiles_e) * MOE_SLOT
    routed = info_t.astype(jnp.int32)
    start_of = lambda e: sum(jnp.where(e == x, gstart[x], 0) for x in range(N_EXPERTS))
    pos0 = start_of(routed[0]) + routed[4]
    pos1 = start_of(routed[1]) + routed[5]
    tile_ids = jnp.arange(n_tiles, dtype=jnp.int32)
    tile_expert = jnp.minimum(jnp.sum(tile_ids[:, None] >= tiles_cum[None, :], axis=1), N_EXPERTS - 1)
    tile_expert = tile_expert.astype(jnp.int32)
    tile_rows = jnp.clip(counts[tile_expert] - (tile_ids * MOE_SLOT - gstart[tile_expert]), 0, MOE_SLOT)
    tile_rows = jnp.where(tile_ids < tiles_cum[-1], tile_rows, 0).astype(jnp.int32)
    pos = jnp.stack([pos0.reshape(-1, SC_CHUNK), pos1.reshape(-1, SC_CHUNK)], axis=1)

    xs = _sc_scatter_pairs(xn, pos, n_tiles * MOE_SLOT)
    ys = _experts(xs, w_gu, w_down, tile_expert, tile_rows)
    outs = []
    for lo, hi in ((0, n_a), (n_a, T)):
        yg = _sc_gather(ys, jnp.concatenate([pos0[lo:hi], pos1[lo:hi]]))
        outs.append(_finalize(yg, info, h3, final_norm, lo))
    return outs


def _seq_flags(seq_lens, block):
    first, last = [], []
    for n in seq_lens:
        nb = n // block
        first += [1] + [0] * (nb - 1)
        last += [0] * (nb - 1) + [1]
    return np.asarray(first, np.int32), np.asarray(last, np.int32)


def _trunk(xa, xb, seq_lens, ln_mix0, ret_w_in, ret_w_out, ln_ffn0, ffn_w_gu, ffn_w_down, ln_mix1, ssm_w_in,
           ssm_conv_w, ssm_conv_b, ssm_dt_bias, ssm_A_log, ssm_D, ssm_norm_w, ssm_w_out, ln_ffn1,
           moe_router_w, moe_router_b, moe_w_gu, moe_w_down, final_norm):
    assert all(n % b == 0 for n in seq_lens for b in (SCAN_BLOCK, SSD_BLOCK, MOE_TOK, TOK_TILE, PROJ_TILE))
    row = lambda v: v.astype(F32).reshape(1, -1)
    first, last = _seq_flags(seq_lens, SCAN_BLOCK)
    reset_f = jnp.asarray(first)
    reset_b = jnp.asarray(last[::-1].copy())
    ssd_first, ssd_last = _seq_flags(seq_lens, SSD_BLOCK)

    half = RET_DK // 2
    inv = ROPE_BASE ** (-jnp.arange(half, dtype=F32) / half)
    ang = jnp.arange(max(seq_lens)).astype(F32)[:, None] * inv[None, :]
    pos_block = jnp.asarray(np.concatenate([np.arange(n // TOK_TILE) for n in seq_lens]).astype(np.int32))
    q, k, v, g = _ret_in(xa, xb, row(ln_mix0), ret_w_in.astype(BF16), jnp.cos(ang), jnp.sin(ang), pos_block)
    y = _ret_scan(q, k, v, g, reset_f, reset_b)
    h2 = _post0(y, xa, xb, ret_w_out.astype(BF16), row(ln_ffn0), ffn_w_gu.astype(BF16),
                ffn_w_down.astype(BF16))

    wz = ssm_w_in[:, :SSM_DINNER].astype(BF16)
    wx = ssm_w_in[:, SSM_DINNER:SSM_DINNER + SSM_CONV_DIM].astype(BF16)
    wdt = jnp.pad(ssm_w_in[:, SSM_DINNER + SSM_CONV_DIM:], ((0, 0), (0, LANES - 2 * SSM_HEADS)))
    wdh, wdl = _split2(wdt)
    pad_row = lambda v: jnp.pad(v.astype(F32).reshape(1, -1), ((0, 0), (0, LANES - 2 * SSM_HEADS)))
    gate, xbc, dt = _ssm_in(h2, row(ln_mix1), wz, wx, wdh, wdl, pad_row(ssm_dt_bias))
    xact = _conv(xbc, ssm_conv_w.reshape(SSM_CONV, SSM_CONV_DIM).astype(F32), row(ssm_conv_b),
                 jnp.asarray(ssd_first), jnp.asarray(ssd_last))
    a_row = pad_row(-jnp.exp(ssm_A_log.astype(F32)))
    dskip = jnp.repeat(ssm_D.astype(F32), SSM_HEADDIM).reshape(1, SSM_DINNER)
    y = _ssd(xact, dt, gate, a_row, dskip, row(ssm_norm_w), jnp.asarray(ssd_first),
             jnp.asarray(ssd_last[::-1].copy()))

    wr = jnp.pad(moe_router_w.astype(F32), ((0, 0), (0, LANES - N_EXPERTS)))
    wr_hi, wr_lo = _split2(wr)
    rb = jnp.pad(moe_router_b.astype(F32).reshape(1, -1), ((0, 0), (0, LANES - N_EXPERTS)),
                 constant_values=-1e30)
    return _moe(y, h2, ssm_w_out.astype(BF16), row(ln_ffn1), wr_hi, wr_lo, rb,
                moe_w_gu, moe_w_down, row(final_norm), xa.shape[0])


def kernel(x_prompt, x_sample, ln_mix0, ret_w_in, ret_w_out, ln_ffn0, ffn_w_gu, ffn_w_down, ln_mix1, ssm_w_in,
           ssm_conv_w, ssm_conv_b, ssm_dt_bias, ssm_A_log, ssm_D, ssm_norm_w, ssm_w_out, ln_ffn1, moe_router_w,
           moe_router_b, moe_w_gu, moe_w_down, final_norm):
    seq_lens = [x_prompt.shape[1]] * x_prompt.shape[0] + [x_sample.shape[1]] * x_sample.shape[0]
    out_a, out_b = _trunk(x_prompt.reshape(-1, D_MODEL), x_sample.reshape(-1, D_MODEL), seq_lens, ln_mix0, ret_w_in, ret_w_out, ln_ffn0, ffn_w_gu, ffn_w_down, ln_mix1, ssm_w_in,
                 ssm_conv_w, ssm_conv_b, ssm_dt_bias, ssm_A_log, ssm_D, ssm_norm_w, ssm_w_out, ln_ffn1,
                 moe_router_w, moe_router_b, moe_w_gu, moe_w_down, final_norm)
    return (out_a.reshape(x_prompt.shape), out_b.reshape(x_sample.shape))
```

```python
import functools

import numpy as np
import jax
import jax.numpy as jnp
from jax import lax
from jax.experimental import pallas as pl
from jax.experimental.pallas import tpu as pltpu
from jax.experimental.pallas import tpu_sc as plsc

F32 = jnp.float32
BF16 = jnp.bfloat16

D_MODEL = 1024
EPS = 1e-6
RET_HEADS = 4
RET_DK = 256
RET_DV = 512
ROPE_BASE = 10000.0
SSM_DINNER = 2048
SSM_HEADDIM = 64
SSM_HEADS = 32
SSM_GROUPS = 4
SSM_DSTATE = 128
SSM_CONV = 5
SSM_CONV_DIM = 3072
SSM_GROUP_W = SSM_DINNER // SSM_GROUPS
SLAB = SSM_GROUP_W
CONV_ROWS = 128
CONV_WINDOW = 256
HALO = 16
FFN_DENSE = 2816
FFN_DENSE_CHUNK = 1408
N_EXPERTS = 8
FFN_EXPERT = 3584

CHUNK = 128
RET_CHUNK = 256
LANES = 128
TOK_TILE = 512
PROJ_TILE = 1024
SCAN_BLOCK = 512
SSD_BLOCK = 1024
MOE_TOK = 512
MOE_SLOT = 1024
MOE_FC = 512
ROUTE_COLS = 8
SC_CORES = 2
SC_SUBCORES = 16
SC_CHUNK = 64


def _cparams(n_axes, vmem_mb):
    return pltpu.CompilerParams(dimension_semantics=("arbitrary",) * n_axes,
                                vmem_limit_bytes=vmem_mb << 20)


def _resident(shape):
    nd = len(shape)
    return pl.BlockSpec(shape, lambda *_: (0,) * nd, pipeline_mode=pl.Buffered(1))


def _rms(x):
    return x * lax.rsqrt(jnp.mean(x * x, axis=-1, keepdims=True) + EPS)


def _silu(x):
    return x * jax.nn.sigmoid(x)


def _dot(a, b):
    return jnp.dot(a, b, preferred_element_type=F32)


def _dot_nt(a, b):
    return lax.dot_general(a, b, (((1,), (1,)), ((), ())), preferred_element_type=F32)


def _dot_tn(a, b):
    return lax.dot_general(a, b, (((0,), (0,)), ((), ())), preferred_element_type=F32)


def _split2(x):
    hi = x.astype(BF16)
    lo = (x - hi.astype(F32)).astype(BF16)
    return hi, lo


def _split3(x):
    hi = x.astype(BF16)
    r = x - hi.astype(F32)
    mid = r.astype(BF16)
    lo = (r - mid.astype(F32)).astype(BF16)
    return hi, mid, lo


def _two_stream_specs(tm, n_a):
    return (pl.BlockSpec((tm, D_MODEL), lambda i, *_: (jnp.minimum(i, n_a - 1), 0)),
            pl.BlockSpec((tm, D_MODEL), lambda i, *_: (jnp.maximum(i - n_a, 0), 0)))


def _ret_in_kernel(pos_ref, xa_ref, xb_ref, ln_ref, w_ref, cos_ref, sin_ref, q_ref, k_ref, v_ref, g_ref, *, n_a):
    x = jnp.where(pl.program_id(0) < n_a, xa_ref[...], xb_ref[...])
    xn = (_rms(x) * ln_ref[...]).astype(BF16)
    cos = cos_ref[...]
    sin = sin_ref[...]
    half = RET_DK // 2

    def rotary(dst, col0, scale):
        for h in range(RET_HEADS):
            c = col0 + h * RET_DK
            p = _dot(xn, w_ref[:, c:c + RET_DK])
            p1, p2 = p[:, :half], p[:, half:]
            dst[:, h * RET_DK:h * RET_DK + half] = ((p1 * cos - p2 * sin) * scale).astype(BF16)
            dst[:, h * RET_DK + half:(h + 1) * RET_DK] = ((p1 * sin + p2 * cos) * scale).astype(BF16)

    rotary(q_ref, 0, 1.0)
    rotary(k_ref, D_MODEL, RET_DK ** -0.5)
    for j in range(2 * D_MODEL // 512):
        v_ref[:, j * 512:(j + 1) * 512] = _dot(
            xn, w_ref[:, 2 * D_MODEL + j * 512:2 * D_MODEL + (j + 1) * 512]).astype(BF16)
        g_ref[:, j * 512:(j + 1) * 512] = _silu(_dot(
            xn, w_ref[:, 4 * D_MODEL + j * 512:4 * D_MODEL + (j + 1) * 512])).astype(BF16)


def _ret_in(xa, xb, ln, w_in, cos, sin, pos_block):
    T = xa.shape[0] + xb.shape[0]
    tm = TOK_TILE
    n_a = xa.shape[0] // tm
    tok = lambda n: pl.BlockSpec((tm, n), lambda i, pb: (i, 0))
    rot = pl.BlockSpec((tm, LANES), lambda i, pb: (pb[i], 0))
    return pl.pallas_call(
        functools.partial(_ret_in_kernel, n_a=n_a),
        grid_spec=pltpu.PrefetchScalarGridSpec(
            num_scalar_prefetch=1, grid=(T // tm,),
            in_specs=[*_two_stream_specs(tm, n_a), _resident((1, D_MODEL)), _resident(w_in.shape), rot, rot],
            out_specs=[tok(D_MODEL), tok(D_MODEL), tok(2 * D_MODEL), tok(2 * D_MODEL)]),
        out_shape=[jax.ShapeDtypeStruct((T, D_MODEL), BF16), jax.ShapeDtypeStruct((T, D_MODEL), BF16),
                   jax.ShapeDtypeStruct((T, 2 * D_MODEL), BF16), jax.ShapeDtypeStruct((T, 2 * D_MODEL), BF16)],
        compiler_params=_cparams(1, 48),
        name="ret_in",
    )(pos_block, xa, xb, ln, w_in, cos, sin)


def _ret_scan_kernel(reset_ref, *refs, reverse, chunk_decay):
    if reverse:
        q_ref, k_ref, v_ref, wq_ref, wk_ref, out_ref, state_ref = refs
    else:
        q_ref, k_ref, v_ref, ob_ref, g_ref, wq_ref, wk_ref, din_ref, out_ref, state_ref = refs
    nch = SCAN_BLOCK // RET_CHUNK

    @pl.when(reset_ref[pl.program_id(0)] == 1)
    def _():
        state_ref[...] = jnp.zeros_like(state_ref)

    def chunk_body(ci, carry):
        cj = nch - 1 - ci if reverse else ci
        rows = pl.ds(pl.multiple_of(cj * RET_CHUNK, RET_CHUNK), RET_CHUNK)
        for h in range(RET_HEADS):
            kcols = slice(h * RET_DK, (h + 1) * RET_DK)
            vcols = slice(h * RET_DV, (h + 1) * RET_DV)
            qh = q_ref[rows, kcols]
            kh = k_ref[rows, kcols]
            vh = v_ref[rows, vcols]
            state = state_ref[h]
            o = _dot(qh, state.astype(BF16)) * wq_ref[h]
            if reverse:
                out_ref[rows, vcols] = o.astype(BF16)
            else:
                s = _dot_nt(qh, kh) * din_ref[h]
                o = o + _dot(s.astype(BF16), vh) + ob_ref[rows, vcols].astype(F32)
                out_ref[rows, vcols] = (g_ref[rows, vcols].astype(F32) * _rms(o)).astype(BF16)
            ks = (kh.astype(F32) * wk_ref[h]).astype(BF16)
            state_ref[h] = state * chunk_decay[h] + _dot_tn(ks, vh)
        return carry

    lax.fori_loop(0, nch, chunk_body, 0)


def _ret_tables():
    lg = np.log1p(-np.power(2.0, -5.0 - np.arange(RET_HEADS, dtype=np.float32))).astype(np.float32)
    c = RET_CHUNK
    a = np.arange(c, dtype=np.float32)
    col = lambda e, w: np.broadcast_to(np.exp(lg[:, None, None] * e[None, :, None]),
                                       (RET_HEADS, c, w)).astype(np.float32)
    dist = np.abs(a[:, None] - a[None, :])
    return dict(
        din=np.exp(lg[:, None, None] * dist[None]).astype(np.float32),
        wq_f=col(a + 1.0, RET_DV), wk_f=col(c - 1.0 - a, RET_DK),
        wq_b=col(c - a, RET_DV), wk_b=col(a, RET_DK),
        chunk_decay=tuple(float(v) for v in np.exp(lg * c)),
    )


def _ret_scan(q, k, v, g, reset_f, reset_b):
    T = q.shape[0]
    nb = T // SCAN_BLOCK
    tb = _ret_tables()
    state = pltpu.VMEM((RET_HEADS, RET_DK, RET_DV), F32)
    out_shape = jax.ShapeDtypeStruct((T, 2 * D_MODEL), BF16)

    def specs(imap):
        blk = lambda n: pl.BlockSpec((SCAN_BLOCK, n), imap)
        return blk(D_MODEL), blk(D_MODEL), blk(2 * D_MODEL)

    rev = lambda i, r: (nb - 1 - i, 0)
    bq, bk, bv = specs(rev)
    o_b = pl.pallas_call(
        functools.partial(_ret_scan_kernel, reverse=True, chunk_decay=tb["chunk_decay"]),
        grid_spec=pltpu.PrefetchScalarGridSpec(
            num_scalar_prefetch=1, grid=(nb,),
            in_specs=[bq, bk, bv, _resident(tb["wq_b"].shape), _resident(tb["wk_b"].shape)],
            out_specs=bv, scratch_shapes=[state]),
        out_shape=out_shape, compiler_params=_cparams(1, 40), name="ret_scan_bwd",
    )(reset_b, q, k, v, tb["wq_b"], tb["wk_b"])

    fwd = lambda i, r: (i, 0)
    bq, bk, bv = specs(fwd)
    return pl.pallas_call(
        functools.partial(_ret_scan_kernel, reverse=False, chunk_decay=tb["chunk_decay"]),
        grid_spec=pltpu.PrefetchScalarGridSpec(
            num_scalar_prefetch=1, grid=(nb,),
            in_specs=[bq, bk, bv, bv, bv, _resident(tb["wq_f"].shape), _resident(tb["wk_f"].shape),
                      _resident(tb["din"].shape)],
            out_specs=bv, scratch_shapes=[state]),
        out_shape=out_shape, compiler_params=_cparams(1, 40), name="ret_scan_fwd",
    )(reset_f, q, k, v, o_b, g, tb["wq_f"], tb["wk_f"], tb["din"])


def _post0_kernel(y_ref, xa_ref, xb_ref, wo_ref, ln_ref, wgu_ref, wd_ref, out_ref, *, n_a):
    x = jnp.where(pl.program_id(0) < n_a, xa_ref[...], xb_ref[...])
    h1 = x + _dot(y_ref[...], wo_ref[...])
    xn = (_rms(h1) * ln_ref[...]).astype(BF16)
    acc = h1
    fc = FFN_DENSE_CHUNK
    for c in range(FFN_DENSE // fc):
        gate = _dot(xn, wgu_ref[:, c * fc:(c + 1) * fc])
        up = _dot(xn, wgu_ref[:, FFN_DENSE + c * fc:FFN_DENSE + (c + 1) * fc])
        act = (_silu(gate) * up).astype(BF16)
        acc = acc + _dot(act, wd_ref[c * fc:(c + 1) * fc, :])
    out_ref[...] = acc


def _post0(y, xa, xb, w_out, ln, w_gu, w_down):
    T = y.shape[0]
    tm = TOK_TILE
    n_a = xa.shape[0] // tm
    tok = lambda n: pl.BlockSpec((tm, n), lambda i: (i, 0))
    return pl.pallas_call(
        functools.partial(_post0_kernel, n_a=n_a),
        grid=(T // tm,),
        in_specs=[tok(2 * D_MODEL), *_two_stream_specs(tm, n_a), _resident(w_out.shape),
                  _resident((1, D_MODEL)), _resident(w_gu.shape), _resident(w_down.shape)],
        out_specs=tok(D_MODEL),
        out_shape=jax.ShapeDtypeStruct((T, D_MODEL), F32),
        compiler_params=_cparams(1, 52),
        name="ret_out_ffn",
    )(y, xa, xb, w_out, ln, w_gu, w_down)


def _ssm_in_kernel(x_ref, ln_ref, wz_ref, wx_ref, wd_ref, wdh_ref, bias_ref, gate_ref, xbc_ref, dt_ref):
    xh, xl = _split2(_rms(x_ref[...]) * ln_ref[...])
    for j in range(SSM_DINNER // SLAB):
        gate_ref[j] = _silu(_dot(xh, wz_ref[:, j * SLAB:(j + 1) * SLAB])).astype(BF16)
    for j in range(SSM_CONV_DIM // SLAB):
        xbc_ref[:, j * SLAB:(j + 1) * SLAB] = _dot(xh, wx_ref[:, j * SLAB:(j + 1) * SLAB]).astype(BF16)
    both = _dot(xh, wd_ref[...])
    x = both[:, :LANES] + both[:, LANES:] + _dot(xl, wdh_ref[...]) + bias_ref[...]
    dt_ref[...] = jnp.maximum(x, 0.0) + jnp.log1p(jnp.exp(-jnp.abs(x)))


def _ssm_in(x, ln, wz, wx, wdh, wdl, bias_row):
    T = x.shape[0]
    tm = PROJ_TILE
    n_gate = SSM_DINNER // SLAB
    wd = jnp.concatenate([wdh, wdl], axis=1)
    tok = lambda n: pl.BlockSpec((tm, n), lambda i: (i, 0))
    return pl.pallas_call(
        _ssm_in_kernel,
        grid=(T // tm,),
        in_specs=[tok(D_MODEL), _resident((1, D_MODEL)), _resident(wz.shape), _resident(wx.shape),
                  _resident(wd.shape), _resident(wdh.shape), _resident(bias_row.shape)],
        out_specs=[pl.BlockSpec((n_gate, tm, SLAB), lambda i: (0, i, 0)), tok(SSM_CONV_DIM), tok(LANES)],
        out_shape=[jax.ShapeDtypeStruct((n_gate, T, SLAB), BF16), jax.ShapeDtypeStruct((T, SSM_CONV_DIM), BF16),
                   jax.ShapeDtypeStruct((T, LANES), F32)],
        compiler_params=_cparams(1, 52),
        name="ssm_in",
    )(x, ln, wz, wx, wd, wdh, bias_row)


def _conv_kernel(first_ref, last_ref, prev_ref, main_ref, next_ref, shift_ref, w_ref, b_ref, out_ref, ext_ref):
    i = pl.program_id(0)
    rb = SSD_BLOCK
    ext_rows = ext_ref.shape[0]

    @pl.when(i == 0)
    def _():
        ext_ref[rb + 2 * HALO:ext_rows, :] = jnp.zeros((ext_rows - rb - 2 * HALO, SSM_CONV_DIM), BF16)

    zero = jnp.zeros((HALO, SSM_CONV_DIM), BF16)
    ext_ref[0:HALO, :] = jnp.where(first_ref[i] == 1, zero, prev_ref[...])
    ext_ref[HALO:HALO + rb, :] = main_ref[...]
    ext_ref[HALO + rb:2 * HALO + rb, :] = jnp.where(last_ref[i] == 1, zero, next_ref[...])
    for s in range(SSM_CONV_DIM // SLAB):
        cs = slice(s * SLAB, (s + 1) * SLAB)
        for r0 in range(0, rb, CONV_ROWS):
            taps = _dot(shift_ref[...], ext_ref[r0:r0 + CONV_WINDOW, cs])
            acc = jnp.broadcast_to(b_ref[:, cs], (CONV_ROWS, SLAB))
            for j in range(SSM_CONV):
                acc = acc + taps[j * CONV_ROWS:(j + 1) * CONV_ROWS] * w_ref[j:j + 1, cs]
            out_ref[s, r0:r0 + CONV_ROWS, :] = _silu(acc).astype(BF16)


def _conv(xbc, conv_w, conv_b, first, last):
    T = xbc.shape[0]
    rb = SSD_BLOCK
    nb = T // rb
    per = rb // HALO
    nh = T // HALO
    r = np.arange(CONV_ROWS)
    shift = np.zeros((SSM_CONV * CONV_ROWS, CONV_WINDOW), np.float32)
    for j in range(SSM_CONV):
        shift[j * CONV_ROWS + r, HALO + r + j - SSM_CONV // 2] = 1.0
    shift = jnp.asarray(shift, dtype=BF16)
    ext_rows = rb - CONV_ROWS + CONV_WINDOW
    return pl.pallas_call(
        _conv_kernel,
        grid_spec=pltpu.PrefetchScalarGridSpec(
            num_scalar_prefetch=2, grid=(nb,),
            in_specs=[
                pl.BlockSpec((HALO, SSM_CONV_DIM), lambda i, f, l: (jnp.maximum(i * per - 1, 0), 0)),
                pl.BlockSpec((rb, SSM_CONV_DIM), lambda i, f, l: (i, 0)),
                pl.BlockSpec((HALO, SSM_CONV_DIM), lambda i, f, l: (jnp.minimum((i + 1) * per, nh - 1), 0)),
                _resident(shift.shape), _resident(conv_w.shape), _resident(conv_b.shape)],
            out_specs=pl.BlockSpec((SSM_CONV_DIM // SLAB, rb, SLAB), lambda i, f, l: (0, i, 0)),
            scratch_shapes=[pltpu.VMEM((ext_rows, SSM_CONV_DIM), BF16)]),
        out_shape=jax.ShapeDtypeStruct((SSM_CONV_DIM // SLAB, T, SLAB), BF16),
        compiler_params=_cparams(1, 48),
        name="ssm_conv",
    )(first, last, xbc, xbc, xbc, shift, conv_w, conv_b)


def _ssd_tables():
    r = np.arange(CHUNK)
    tri_l = (r[None, :] <= r[:, None]).astype(np.float32)
    tri_u = (r[None, :] >= r[:, None]).astype(np.float32)
    c = np.arange(SSM_DINNER) // SSM_HEADDIM
    j = np.arange(LANES)
    e_f = (j[:, None] == c[None, :]).astype(np.float32)
    e_b = (j[:, None] == (c[None, :] + SSM_HEADS)).astype(np.float32)
    as_bf16 = lambda x: jnp.asarray(x, dtype=BF16)
    return dict(tri_l3=as_bf16(np.concatenate([tri_l] * 3, axis=1)),
                tri_u3=as_bf16(np.concatenate([tri_u] * 3, axis=1)),
                e2_f=as_bf16(np.concatenate([e_f, e_f], axis=0)),
                e2_b=as_bf16(np.concatenate([e_b, e_b], axis=0)))


def _ssd_chunk_scalars(dt, a_ref, tri_l3_ref, tri_u3_ref):
    a = dt * a_ref[...]
    a3 = jnp.concatenate(_split3(a), axis=0)
    cum_f = _dot(tri_l3_ref[...], a3)
    cum_b = _dot(tri_u3_ref[...], a3)
    lane = lax.broadcasted_iota(jnp.int32, (CHUNK, LANES), 1)
    cum = jnp.where(lane < SSM_HEADS, cum_f, cum_b)
    tot = cum_f[CHUNK - 1:CHUNK, :]
    return cum, tot


def _expand(e2_ref, *rows):
    x = jnp.concatenate(rows, axis=0)
    hi, lo = _split2(x)
    return _dot(jnp.concatenate([hi, lo], axis=1), e2_ref[...])


def _ssd_bwd_kernel(reset_ref, xa_ref, dt_ref, a_ref, tri_l3_ref, tri_u3_ref, e2_ref,
                    yb_ref, state_ref):
    nch = SSD_BLOCK // CHUNK

    @pl.when(reset_ref[pl.program_id(0)] == 1)
    def _():
        state_ref[...] = jnp.zeros_like(state_ref)

    def chunk_body(ci, carry):
        rows = pl.ds(pl.multiple_of((nch - 1 - ci) * CHUNK, CHUNK), CHUNK)
        dt = dt_ref[rows, :]
        cum, tot = _ssd_chunk_scalars(dt, a_ref, tri_l3_ref, tri_u3_ref)
        ex = _expand(e2_ref, dt * jnp.exp(tot - cum), jnp.exp(cum),
                     jnp.broadcast_to(jnp.exp(tot), (16, LANES)))
        for g in range(SSM_GROUPS):
            gc = slice(g * SSM_GROUP_W, (g + 1) * SSM_GROUP_W)
            bm = xa_ref[SSM_GROUPS, rows, g * SSM_DSTATE:(g + 1) * SSM_DSTATE]
            cm = xa_ref[SSM_GROUPS + 1, rows, g * SSM_DSTATE:(g + 1) * SSM_DSTATE]
            state = state_ref[g]
            yb_ref[rows, gc] = (_dot(cm, state.astype(BF16)) * ex[CHUNK:2 * CHUNK, gc]).astype(BF16)
            xw = (xa_ref[g, rows, :].astype(F32) * ex[0:CHUNK, gc]).astype(BF16)
            state_ref[g] = state * ex[2 * CHUNK:2 * CHUNK + 1, gc] + _dot_tn(bm, xw)
        return carry

    lax.fori_loop(0, nch, chunk_body, 0)


def _ssd_fwd_kernel(reset_ref, xa_ref, dt_ref, gate_ref, yb_ref, a_ref, tri_l3_ref, tri_u3_ref,
                    e2_ref, dskip_ref, nw_ref, out_ref, state_ref, y_ref):
    nch = SSD_BLOCK // CHUNK

    @pl.when(reset_ref[pl.program_id(0)] == 1)
    def _():
        state_ref[...] = jnp.zeros_like(state_ref)

    def chunk_body(ci, carry):
        rows = pl.ds(pl.multiple_of(ci * CHUNK, CHUNK), CHUNK)
        dt = dt_ref[rows, :]
        cum, tot = _ssd_chunk_scalars(dt, a_ref, tri_l3_ref, tri_u3_ref)
        ex = _expand(e2_ref, dt * jnp.exp(tot - cum), jnp.exp(cum),
                     jnp.broadcast_to(jnp.exp(tot), (16, LANES)))
        row_t = (cum - jnp.log(dt)).T
        li = lax.broadcasted_iota(jnp.int32, (CHUNK, CHUNK), 0)
        si = lax.broadcasted_iota(jnp.int32, (CHUNK, CHUNK), 1)
        lower = li >= si
        first_half = si < SSM_HEADDIM
        for g in range(SSM_GROUPS):
            gc = slice(g * SSM_GROUP_W, (g + 1) * SSM_GROUP_W)
            bm = xa_ref[SSM_GROUPS, rows, g * SSM_DSTATE:(g + 1) * SSM_DSTATE]
            cm = xa_ref[SSM_GROUPS + 1, rows, g * SSM_DSTATE:(g + 1) * SSM_DSTATE]
            cb = _dot_nt(cm, bm)
            heads_per_group = SSM_HEADS // SSM_GROUPS
            for pair in range(heads_per_group // 2):
                mats = []
                for hh in range(2):
                    h = g * heads_per_group + 2 * pair + hh
                    hb = SSM_HEADS + h
                    seg = jnp.where(lower, cum[:, h:h + 1] - row_t[h:h + 1, :],
                                    cum[:, hb:hb + 1] - row_t[hb:hb + 1, :])
                    mats.append((cb * jnp.exp(seg)).astype(BF16))
                pc = slice(2 * pair * SSM_HEADDIM, (2 * pair + 2) * SSM_HEADDIM)
                yy = _dot(jnp.concatenate(mats, axis=0), xa_ref[g, rows, pc])
                y_ref[:, pc] = jnp.where(first_half, yy[0:CHUNK], yy[CHUNK:2 * CHUNK])
            state = state_ref[g]
            xs = xa_ref[g, rows, :].astype(F32)
            y = (y_ref[...] + _dot(cm, state.astype(BF16)) * ex[CHUNK:2 * CHUNK, gc]
                 + yb_ref[rows, gc].astype(F32) + xs * dskip_ref[:, gc])
            y = y * gate_ref[g, rows, :].astype(F32)
            out_ref[rows, gc] = (_rms(y) * nw_ref[:, gc]).astype(BF16)
            xw = (xs * ex[0:CHUNK, gc]).astype(BF16)
            state_ref[g] = state * ex[2 * CHUNK:2 * CHUNK + 1, gc] + _dot_tn(bm, xw)
        return carry

    lax.fori_loop(0, nch, chunk_body, 0)


def _ssd(xa, dt, gate, a_row, dskip, norm_w, reset_f, reset_b):
    T = dt.shape[0]
    nb = T // SSD_BLOCK
    tb = _ssd_tables()
    state = pltpu.VMEM((SSM_GROUPS, SSM_DSTATE, SSM_GROUP_W), F32)
    consts = [a_row, tb["tri_l3"], tb["tri_u3"]]
    const_specs = [_resident(c.shape) for c in consts]
    out_shape = jax.ShapeDtypeStruct((T, SSM_DINNER), BF16)

    rev = lambda i, r: (nb - 1 - i, 0)
    blk = lambda n, imap: pl.BlockSpec((SSD_BLOCK, n), imap)
    slabs = lambda a, imap: pl.BlockSpec((a.shape[0], SSD_BLOCK, SLAB), lambda i, r: (0, imap(i, r)[0], 0))
    y_b = pl.pallas_call(
        _ssd_bwd_kernel,
        grid_spec=pltpu.PrefetchScalarGridSpec(
            num_scalar_prefetch=1, grid=(nb,),
            in_specs=[slabs(xa, rev), blk(LANES, rev)] + const_specs + [_resident(tb["e2_b"].shape)],
            out_specs=blk(SSM_DINNER, rev), scratch_shapes=[state]),
        out_shape=out_shape, compiler_params=_cparams(1, 40), name="ssd_bwd",
    )(reset_b, xa, dt, *consts, tb["e2_b"])

    fwd = lambda i, r: (i, 0)
    return pl.pallas_call(
        _ssd_fwd_kernel,
        grid_spec=pltpu.PrefetchScalarGridSpec(
            num_scalar_prefetch=1, grid=(nb,),
            in_specs=[slabs(xa, fwd), blk(LANES, fwd), slabs(gate, fwd), blk(SSM_DINNER, fwd)]
            + const_specs + [_resident(tb["e2_f"].shape), _resident(dskip.shape), _resident(norm_w.shape)],
            out_specs=blk(SSM_DINNER, fwd),
            scratch_shapes=[state, pltpu.VMEM((CHUNK, SSM_GROUP_W), F32)]),
        out_shape=out_shape, compiler_params=_cparams(1, 52), name="ssd_fwd",
    )(reset_f, xa, dt, gate, y_b, *consts, tb["e2_f"], dskip, norm_w)


def _pack_halves(x):
    n = x.shape[1] // 2
    hi = lax.bitcast_convert_type(x[:, :n].astype(F32), jnp.uint32)
    lo = lax.bitcast_convert_type(x[:, n:].astype(F32), jnp.uint32)
    return hi | (lo >> 16)


def _unpack_halves(w):
    hi = lax.bitcast_convert_type(w & jnp.uint32(0xFFFF0000), F32)
    lo = lax.bitcast_convert_type(w << 16, F32)
    return hi, lo


def _router_kernel(y_ref, h_ref, wo_ref, ln_ref, wh_ref, wl_ref, b_ref, tri_ref,
                   h3_ref, xn_ref, info_ref, info_t_ref, count_ref, base_ref):
    @pl.when(pl.program_id(0) == 0)
    def _():
        base_ref[...] = jnp.zeros_like(base_ref)

    h3 = h_ref[...] + _dot(y_ref[...], wo_ref[...])
    h3_ref[...] = h3
    xh, xl = _split2(_rms(h3) * ln_ref[...])
    xn_ref[...] = _pack_halves(xh)
    logits = _dot_nt(wh_ref[...], xh) + _dot_nt(wh_ref[...], xl) + _dot_nt(wl_ref[...], xh) + b_ref[...]
    row = lax.broadcasted_iota(jnp.int32, logits.shape, 0)
    m1 = jnp.max(logits, axis=0, keepdims=True)
    i1 = jnp.min(jnp.where(logits == m1, row, LANES), axis=0, keepdims=True)
    rest = jnp.where(row == i1, -jnp.inf, logits)
    m2 = jnp.max(rest, axis=0, keepdims=True)
    i2 = jnp.min(jnp.where(rest == m2, row, LANES), axis=0, keepdims=True)
    e = jnp.exp(m2 - m1)
    g1 = 1.0 / (1.0 + e)
    g2 = e / (1.0 + e)
    oh1 = row == i1
    oh2 = row == i2
    oh = jnp.where(oh1 | oh2, 1.0, 0.0)
    base = base_ref[...]
    prefix = _dot(oh.astype(BF16), tri_ref[...]) + base
    r1 = jnp.sum(jnp.where(oh1, prefix, 0.0), axis=0, keepdims=True)
    r2 = jnp.sum(jnp.where(oh2, prefix, 0.0), axis=0, keepdims=True)
    base = base + jnp.sum(oh, axis=1, keepdims=True)
    count_ref[...] = base[:, :LANES]
    base_ref[...] = base
    fields = [i1.astype(F32), i2.astype(F32), g1, g2, r1, r2]
    info_t = jnp.zeros(logits.shape, F32)
    for c, f in enumerate(fields):
        info_t = jnp.where(row == c, f, info_t)
    info_t_ref[...] = info_t[:ROUTE_COLS, :]
    info_ref[...] = info_t.T[:, :ROUTE_COLS]


def _router(y, h, w_out, ln, wh, wl, b):
    T = h.shape[0]
    tb = MOE_TOK
    nb = T // tb
    r = np.arange(tb)
    tri = jnp.asarray((r[:, None] < r[None, :]).astype(np.float32), dtype=BF16)
    tok = lambda n: pl.BlockSpec((tb, n), lambda i: (i, 0))
    return pl.pallas_call(
        _router_kernel,
        grid=(nb,),
        in_specs=[tok(SSM_DINNER), tok(D_MODEL), _resident(w_out.shape), _resident((1, D_MODEL)),
                  _resident(wh.shape), _resident(wl.shape), _resident(b.shape), _resident(tri.shape)],
        out_specs=[tok(D_MODEL), tok(D_MODEL // 2), tok(ROUTE_COLS),
                   pl.BlockSpec((ROUTE_COLS, tb), lambda i: (0, i)), pl.BlockSpec((LANES, LANES), lambda i: (0, 0))],
        out_shape=[jax.ShapeDtypeStruct((T, D_MODEL), F32), jax.ShapeDtypeStruct((T, D_MODEL // 2), jnp.uint32),
                   jax.ShapeDtypeStruct((T, ROUTE_COLS), F32), jax.ShapeDtypeStruct((ROUTE_COLS, T), F32),
                   jax.ShapeDtypeStruct((LANES, LANES), F32)],
        scratch_shapes=[pltpu.VMEM((LANES, tb), F32)],
        compiler_params=_cparams(1, 32),
        name="ssm_out_router",
    )(y, h, w_out, ln, wh, wl, b, tri)


def _sc_gather(table, idx):
    n_rows, width = idx.shape[0], table.shape[1]
    workers = SC_CORES * SC_SUBCORES
    nbuf = 2
    assert n_rows % (workers * SC_CHUNK * nbuf) == 0
    per_worker = n_rows // workers
    mesh = plsc.VectorSubcoreMesh(core_axis_name="c", subcore_axis_name="s")

    @functools.partial(
        pl.kernel, mesh=mesh,
        out_type=jax.ShapeDtypeStruct((n_rows, width), table.dtype),
        scratch_types=[pltpu.VMEM((nbuf, SC_CHUNK), jnp.int32),
                       pltpu.VMEM((nbuf, SC_CHUNK, width), table.dtype),
                       pltpu.SemaphoreType.DMA((nbuf,)),
                       pltpu.SemaphoreType.DMA((nbuf,))],
    )
    def gather_rows(table_hbm, idx_hbm, out_hbm, idx_v, rows_v, gsem, wsem):
        base = (lax.axis_index("s") * SC_CORES + lax.axis_index("c")) * per_worker

        def gather(c, b):
            off = pl.multiple_of(base + c * SC_CHUNK, 8)
            pltpu.sync_copy(idx_hbm.at[pl.ds(off, SC_CHUNK)], idx_v.at[b])
            return pltpu.make_async_copy(table_hbm.at[idx_v.at[b]], rows_v.at[b], gsem.at[b])

        def write(c, b):
            off = pl.multiple_of(base + c * SC_CHUNK, 8)
            return pltpu.make_async_copy(rows_v.at[b], out_hbm.at[pl.ds(off, SC_CHUNK)], wsem.at[b])

        @pl.loop(0, per_worker // SC_CHUNK, step=nbuf)
        def _(c0):
            copies = [gather(c0 + b, b) for b in range(nbuf)]
            for cp in copies:
                cp.start()
            writes = []
            for b, cp in enumerate(copies):
                cp.wait()
                writes.append(write(c0 + b, b))
                writes[-1].start()
            for wr in writes:
                wr.wait()

    return gather_rows(table, idx)


def _sc_scatter_pairs(table, pos, n_out):
    n_rows, width = table.shape
    workers = SC_CORES * SC_SUBCORES
    nbuf = 2
    assert n_rows % (workers * SC_CHUNK * nbuf) == 0
    per_worker = n_rows // workers
    mesh = plsc.VectorSubcoreMesh(core_axis_name="c", subcore_axis_name="s")

    @functools.partial(
        pl.kernel, mesh=mesh,
        out_type=jax.ShapeDtypeStruct((n_out, width), table.dtype),
        scratch_types=[pltpu.VMEM((nbuf, 2, SC_CHUNK), jnp.int32),
                       pltpu.VMEM((nbuf, SC_CHUNK, width), table.dtype),
                       pltpu.SemaphoreType.DMA((nbuf,)),
                       pltpu.SemaphoreType.DMA((nbuf,))],
    )
    def scatter_rows(table_hbm, pos_hbm, out_hbm, idx_v, rows_v, rsem, wsem):
        base = (lax.axis_index("s") * SC_CORES + lax.axis_index("c")) * per_worker

        def read(c, b):
            off = pl.multiple_of(base + c * SC_CHUNK, 8)
            return pltpu.make_async_copy(table_hbm.at[pl.ds(off, SC_CHUNK)], rows_v.at[b], rsem.at[b])

        def write(b, j):
            return pltpu.make_async_copy(rows_v.at[b], out_hbm.at[idx_v.at[b, j]], wsem.at[b])

        @pl.loop(0, per_worker // SC_CHUNK, step=nbuf)
        def _(c0):
            reads = [read(c0 + b, b) for b in range(nbuf)]
            for rd in reads:
                rd.start()
            for b in range(nbuf):
                pltpu.sync_copy(pos_hbm.at[base // SC_CHUNK + c0 + b], idx_v.at[b])
            writes = []
            for b, rd in enumerate(reads):
                rd.wait()
                for j in range(2):
                    writes.append(write(b, j))
                    writes[-1].start()
            for wr in writes:
                wr.wait()

    return scatter_rows(table, pos)


def _expert_kernel(texp_ref, trows_ref, x_ref, wg_ref, wu_ref, wd_ref, out_ref, acc_ref, xb_ref):
    i = pl.program_id(0)
    f = pl.program_id(1)
    nf = pl.num_programs(1)
    n = D_MODEL // 2

    @pl.when(trows_ref[i] > 0)
    def _():
        @pl.when(f == 0)
        def _():
            acc_ref[...] = jnp.zeros_like(acc_ref)
            row = lax.broadcasted_iota(jnp.int32, x_ref.shape, 0)
            hi, lo = _unpack_halves(jnp.where(row < trows_ref[i], x_ref[...], jnp.uint32(0)))
            xb_ref[:, :n] = hi.astype(BF16)
            xb_ref[:, n:] = lo.astype(BF16)

        def ffn_chunk(m):
            x = xb_ref[0:m, :]
            act = (_silu(_dot(x, wg_ref[0].astype(BF16))) * _dot(x, wu_ref[0].astype(BF16))).astype(BF16)
            acc_ref[0:m, :] += _dot(act, wd_ref[0].astype(BF16))

        @pl.when(trows_ref[i] > MOE_SLOT // 2)
        def _():
            ffn_chunk(MOE_SLOT)

        @pl.when(trows_ref[i] <= MOE_SLOT // 2)
        def _():
            ffn_chunk(MOE_SLOT // 2)

        @pl.when(f == nf - 1)
        def _():
            out_ref[...] = _pack_halves(acc_ref[...].astype(BF16))

    @pl.when((trows_ref[i] == 0) & (f == nf - 1))
    def _():
        out_ref[...] = jnp.zeros_like(out_ref)


def _experts(xs, w_gu, w_down, tile_expert, tile_rows):
    n_tiles = xs.shape[0] // MOE_SLOT
    nf = FFN_EXPERT // MOE_FC
    return pl.pallas_call(
        _expert_kernel,
        grid_spec=pltpu.PrefetchScalarGridSpec(
            num_scalar_prefetch=2, grid=(n_tiles, nf),
            in_specs=[pl.BlockSpec((MOE_SLOT, D_MODEL // 2), lambda i, f, te, tv: (i, 0)),
                      pl.BlockSpec((1, D_MODEL, MOE_FC), lambda i, f, te, tv: (te[i], 0, f)),
                      pl.BlockSpec((1, D_MODEL, MOE_FC), lambda i, f, te, tv: (te[i], 0, nf + f)),
                      pl.BlockSpec((1, MOE_FC, D_MODEL), lambda i, f, te, tv: (te[i], f, 0))],
            out_specs=pl.BlockSpec((MOE_SLOT, D_MODEL // 2), lambda i, f, te, tv: (i, 0)),
            scratch_shapes=[pltpu.VMEM((MOE_SLOT, D_MODEL), F32), pltpu.VMEM((MOE_SLOT, D_MODEL), BF16)]),
        out_shape=jax.ShapeDtypeStruct(xs.shape, jnp.uint32),
        compiler_params=_cparams(2, 48),
        name="moe_experts",
    )(tile_expert, tile_rows, xs, w_gu, w_gu, w_down)


def _finalize_kernel(y0_ref, y1_ref, info_ref, h_ref, fn_ref, out_ref):
    n = D_MODEL // 2
    info = info_ref[...]
    g0 = info[:, 2:3]
    g1 = info[:, 3:4]
    a_hi, a_lo = _unpack_halves(y0_ref[...])
    b_hi, b_lo = _unpack_halves(y1_ref[...])
    left = h_ref[:, :n] + g0 * a_hi + g1 * b_hi
    right = h_ref[:, n:] + g0 * a_lo + g1 * b_lo
    ms = (jnp.sum(left * left, axis=-1, keepdims=True)
          + jnp.sum(right * right, axis=-1, keepdims=True)) * (1.0 / D_MODEL)
    scale = lax.rsqrt(ms + EPS)
    out_ref[:, :n] = left * scale * fn_ref[:, :n]
    out_ref[:, n:] = right * scale * fn_ref[:, n:]


def _finalize(yg, info, h3, final_norm, tok0):
    n = yg.shape[0] // 2
    tb = MOE_TOK
    nb = n // tb
    b0 = tok0 // tb
    tok = lambda w: pl.BlockSpec((tb, w), lambda i: (b0 + i, 0))
    loc = lambda off: pl.BlockSpec((tb, D_MODEL // 2), lambda i: (i + off, 0))
    return pl.pallas_call(
        _finalize_kernel,
        grid=(nb,),
        in_specs=[loc(0), loc(nb), tok(info.shape[1]), tok(D_MODEL), _resident((1, D_MODEL))],
        out_specs=pl.BlockSpec((tb, D_MODEL), lambda i: (i, 0)),
        out_shape=jax.ShapeDtypeStruct((n, D_MODEL), F32),
        compiler_params=_cparams(1, 32),
        name="moe_finalize",
    )(yg, yg, info, h3, final_norm)


def _moe(y_ssm, h2, w_out, ln, wr_hi, wr_lo, rb, w_gu, w_down, final_norm, n_a):
    T = h2.shape[0]
    n_tiles = 2 * T // MOE_SLOT + N_EXPERTS
    h3, xn, info, info_t, counts = _router(y_ssm, h2, w_out, ln, wr_hi, wr_lo, rb)

    counts = counts[:N_EXPERTS, 0].astype(jnp.int32)
    tiles_e = (counts + MOE_SLOT - 1) // MOE_SLOT
    tiles_cum = jnp.cumsum(tiles_e)
    gstart = (tiles_cum - tiles_e) * MOE_SLOT
    routed = info_t.astype(jnp.int32)
    start_of = lambda e: sum(jnp.where(e == x, gstart[x], 0) for x in range(N_EXPERTS))
    pos0 = start_of(routed[0]) + routed[4]
    pos1 = start_of(routed[1]) + routed[5]
    tile_ids = jnp.arange(n_tiles, dtype=jnp.int32)
    tile_expert = jnp.minimum(jnp.sum(tile_ids[:, None] >= tiles_cum[None, :], axis=1), N_EXPERTS - 1)
    tile_expert = tile_expert.astype(jnp.int32)
    tile_rows = jnp.clip(counts[tile_expert] - (tile_ids * MOE_SLOT - gstart[tile_expert]), 0, MOE_SLOT)
    tile_rows = jnp.where(tile_ids < tiles_cum[-1], tile_rows, 0).astype(jnp.int32)
    pos = jnp.stack([pos0.reshape(-1, SC_CHUNK), pos1.reshape(-1, SC_CHUNK)], axis=1)

    xs = _sc_scatter_pairs(xn, pos, n_tiles * MOE_SLOT)
    ys = _experts(xs, w_gu, w_down, tile_expert, tile_rows)
    outs = []
    for lo, hi in ((0, n_a), (n_a, T)):
        yg = _sc_gather(ys, jnp.concatenate([pos0[lo:hi], pos1[lo:hi]]))
        outs.append(_finalize(yg, info, h3, final_norm, lo))
    return outs


def _seq_flags(seq_lens, block):
    first, last = [], []
    for n in seq_lens:
        nb = n // block
        first += [1] + [0] * (nb - 1)
        last += [0] * (nb - 1) + [1]
    return np.asarray(first, np.int32), np.asarray(last, np.int32)


def _trunk(xa, xb, seq_lens, ln_mix0, ret_w_in, ret_w_out, ln_ffn0, ffn_w_gu, ffn_w_down, ln_mix1, ssm_w_in,
           ssm_conv_w, ssm_conv_b, ssm_dt_bias, ssm_A_log, ssm_D, ssm_norm_w, ssm_w_out, ln_ffn1,
           moe_router_w, moe_router_b, moe_w_gu, moe_w_down, final_norm):
    assert all(n % b == 0 for n in seq_lens for b in (SCAN_BLOCK, SSD_BLOCK, MOE_TOK, TOK_TILE, PROJ_TILE))
    row = lambda v: v.astype(F32).reshape(1, -1)
    first, last = _seq_flags(seq_lens, SCAN_BLOCK)
    reset_f = jnp.asarray(first)
    reset_b = jnp.asarray(last[::-1].copy())
    ssd_first, ssd_last = _seq_flags(seq_lens, SSD_BLOCK)

    half = RET_DK // 2
    inv = ROPE_BASE ** (-jnp.arange(half, dtype=F32) / half)
    ang = jnp.arange(max(seq_lens)).astype(F32)[:, None] * inv[None, :]
    pos_block = jnp.asarray(np.concatenate([np.arange(n // TOK_TILE) for n in seq_lens]).astype(np.int32))
    q, k, v, g = _ret_in(xa, xb, row(ln_mix0), ret_w_in.astype(BF16), jnp.cos(ang), jnp.sin(ang), pos_block)
    y = _ret_scan(q, k, v, g, reset_f, reset_b)
    h2 = _post0(y, xa, xb, ret_w_out.astype(BF16), row(ln_ffn0), ffn_w_gu.astype(BF16),
                ffn_w_down.astype(BF16))

    wz = ssm_w_in[:, :SSM_DINNER].astype(BF16)
    wx = ssm_w_in[:, SSM_DINNER:SSM_DINNER + SSM_CONV_DIM].astype(BF16)
    wdt = jnp.pad(ssm_w_in[:, SSM_DINNER + SSM_CONV_DIM:], ((0, 0), (0, LANES - 2 * SSM_HEADS)))
    wdh, wdl = _split2(wdt)
    pad_row = lambda v: jnp.pad(v.astype(F32).reshape(1, -1), ((0, 0), (0, LANES - 2 * SSM_HEADS)))
    gate, xbc, dt = _ssm_in(h2, row(ln_mix1), wz, wx, wdh, wdl, pad_row(ssm_dt_bias))
    xact = _conv(xbc, ssm_conv_w.reshape(SSM_CONV, SSM_CONV_DIM).astype(F32), row(ssm_conv_b),
                 jnp.asarray(ssd_first), jnp.asarray(ssd_last))
    a_row = pad_row(-jnp.exp(ssm_A_log.astype(F32)))
    dskip = jnp.repeat(ssm_D.astype(F32), SSM_HEADDIM).reshape(1, SSM_DINNER)
    y = _ssd(xact, dt, gate, a_row, dskip, row(ssm_norm_w), jnp.asarray(ssd_first),
             jnp.asarray(ssd_last[::-1].copy()))

    wr = jnp.pad(moe_router_w.astype(F32).T, ((0, LANES - N_EXPERTS), (0, 0)))
    wr_hi, wr_lo = _split2(wr)
    rb = jnp.pad(moe_router_b.astype(F32).reshape(-1, 1), ((0, LANES - N_EXPERTS), (0, 0)), constant_values=-1e30)
    rb = jnp.broadcast_to(rb, (LANES, MOE_TOK))
    return _moe(y, h2, ssm_w_out.astype(BF16), row(ln_ffn1), wr_hi, wr_lo, rb,
                moe_w_gu, moe_w_down, row(final_norm), xa.shape[0])


def kernel(x_prompt, x_sample, ln_mix0, ret_w_in, ret_w_out, ln_ffn0, ffn_w_gu, ffn_w_down, ln_mix1, ssm_w_in,
           ssm_conv_w, ssm_conv_b, ssm_dt_bias, ssm_A_log, ssm_D, ssm_norm_w, ssm_w_out, ln_ffn1, moe_router_w,
           moe_router_b, moe_w_gu, moe_w_down, final_norm):
    seq_lens = [x_prompt.shape[1]] * x_prompt.shape[0] + [x_sample.shape[1]] * x_sample.shape[0]
    out_a, out_b = _trunk(x_prompt.reshape(-1, D_MODEL), x_sample.reshape(-1, D_MODEL), seq_lens, ln_mix0, ret_w_in, ret_w_out, ln_ffn0, ffn_w_gu, ffn_w_down, ln_mix1, ssm_w_in,
                 ssm_conv_w, ssm_conv_b, ssm_dt_bias, ssm_A_log, ssm_D, ssm_norm_w, ssm_w_out, ln_ffn1,
                 moe_router_w, moe_router_b, moe_w_gu, moe_w_down, final_norm)
    return (out_a.reshape(x_prompt.shape), out_b.reshape(x_sample.shape))
```

```python
import functools

import numpy as np
import jax
import jax.numpy as jnp
from jax import lax
from jax.experimental import pallas as pl
from jax.experimental.pallas import tpu as pltpu
from jax.experimental.pallas import tpu_sc as plsc

F32 = jnp.float32
BF16 = jnp.bfloat16

D_MODEL = 1024
EPS = 1e-6
RET_HEADS = 4
RET_DK = 256
RET_DV = 512
ROPE_BASE = 10000.0
SSM_DINNER = 2048
SSM_HEADDIM = 64
SSM_HEADS = 32
SSM_GROUPS = 4
SSM_DSTATE = 128
SSM_CONV = 5
SSM_CONV_DIM = 3072
SSM_GROUP_W = SSM_DINNER // SSM_GROUPS
SLAB = SSM_GROUP_W
CONV_ROWS = 128
CONV_WINDOW = 256
HALO = 16
FFN_DENSE = 2816
FFN_DENSE_CHUNK = 1408
N_EXPERTS = 8
FFN_EXPERT = 3584

CHUNK = 128
RET_CHUNK = 256
LANES = 128
TOK_TILE = 512
PROJ_TILE = 1024
SCAN_BLOCK = 512
SSD_BLOCK = 1024
MOE_TOK = 512
MOE_SLOT = 1024
MOE_FC = 512
MOE_ROWS_STEP = 256
ROUTE_COLS = 8
SC_CORES = 2
SC_SUBCORES = 16
SC_CHUNK = 64


def _cparams(n_axes, vmem_mb):
    return pltpu.CompilerParams(dimension_semantics=("arbitrary",) * n_axes,
                                vmem_limit_bytes=vmem_mb << 20)


def _resident(shape):
    nd = len(shape)
    return pl.BlockSpec(shape, lambda *_: (0,) * nd, pipeline_mode=pl.Buffered(1))


def _rms(x):
    return x * lax.rsqrt(jnp.mean(x * x, axis=-1, keepdims=True) + EPS)


def _silu(x):
    return x * jax.nn.sigmoid(x)


def _dot(a, b):
    return jnp.dot(a, b, preferred_element_type=F32)


def _dot_nt(a, b):
    return lax.dot_general(a, b, (((1,), (1,)), ((), ())), preferred_element_type=F32)


def _dot_tn(a, b):
    return lax.dot_general(a, b, (((0,), (0,)), ((), ())), preferred_element_type=F32)


def _split2(x):
    hi = x.astype(BF16)
    lo = (x - hi.astype(F32)).astype(BF16)
    return hi, lo


def _split3(x):
    hi = x.astype(BF16)
    r = x - hi.astype(F32)
    mid = r.astype(BF16)
    lo = (r - mid.astype(F32)).astype(BF16)
    return hi, mid, lo


def _two_stream_specs(tm, n_a):
    return (pl.BlockSpec((tm, D_MODEL), lambda i, *_: (jnp.minimum(i, n_a - 1), 0)),
            pl.BlockSpec((tm, D_MODEL), lambda i, *_: (jnp.maximum(i - n_a, 0), 0)))


def _ret_in_kernel(pos_ref, xa_ref, xb_ref, ln_ref, w_ref, cb_ref, sb_ref, co_ref, so_ref,
                   q_ref, k_ref, v_ref, g_ref, *, n_a):
    x = jnp.where(pl.program_id(0) < n_a, xa_ref[...], xb_ref[...])
    xn = (_rms(x) * ln_ref[...]).astype(BF16)
    cb, sb, co, so = cb_ref[0], sb_ref[0], co_ref[...], so_ref[...]
    cos = cb * co - sb * so
    sin = sb * co + cb * so
    half = RET_DK // 2

    def rotary(dst, col0, scale):
        for h in range(RET_HEADS):
            c = col0 + h * RET_DK
            p = _dot(xn, w_ref[:, c:c + RET_DK])
            p1, p2 = p[:, :half], p[:, half:]
            dst[:, h * RET_DK:h * RET_DK + half] = ((p1 * cos - p2 * sin) * scale).astype(BF16)
            dst[:, h * RET_DK + half:(h + 1) * RET_DK] = ((p1 * sin + p2 * cos) * scale).astype(BF16)

    rotary(q_ref, 0, 1.0)
    rotary(k_ref, D_MODEL, RET_DK ** -0.5)
    for j in range(2 * D_MODEL // 512):
        v_ref[:, j * 512:(j + 1) * 512] = _dot(
            xn, w_ref[:, 2 * D_MODEL + j * 512:2 * D_MODEL + (j + 1) * 512]).astype(BF16)
        g_ref[:, j * 512:(j + 1) * 512] = _silu(_dot(
            xn, w_ref[:, 4 * D_MODEL + j * 512:4 * D_MODEL + (j + 1) * 512])).astype(BF16)


def _ret_in(xa, xb, ln, w_in, cos_base, sin_base, cos_off, sin_off, pos_block):
    T = xa.shape[0] + xb.shape[0]
    tm = TOK_TILE
    n_a = xa.shape[0] // tm
    tok = lambda n: pl.BlockSpec((tm, n), lambda i, pb: (i, 0))
    rot = pl.BlockSpec((1, 1, LANES), lambda i, pb: (pb[i], 0, 0))
    return pl.pallas_call(
        functools.partial(_ret_in_kernel, n_a=n_a),
        grid_spec=pltpu.PrefetchScalarGridSpec(
            num_scalar_prefetch=1, grid=(T // tm,),
            in_specs=[*_two_stream_specs(tm, n_a), _resident((1, D_MODEL)), _resident(w_in.shape), rot, rot,
                      _resident(cos_off.shape), _resident(sin_off.shape)],
            out_specs=[tok(D_MODEL), tok(D_MODEL), tok(2 * D_MODEL), tok(2 * D_MODEL)]),
        out_shape=[jax.ShapeDtypeStruct((T, D_MODEL), BF16), jax.ShapeDtypeStruct((T, D_MODEL), BF16),
                   jax.ShapeDtypeStruct((T, 2 * D_MODEL), BF16), jax.ShapeDtypeStruct((T, 2 * D_MODEL), BF16)],
        compiler_params=_cparams(1, 48),
        name="ret_in",
    )(pos_block, xa, xb, ln, w_in, cos_base, sin_base, cos_off, sin_off)


def _ret_scan_kernel(reset_ref, *refs, reverse, chunk_decay):
    if reverse:
        q_ref, k_ref, v_ref, wq_ref, wk_ref, out_ref, state_ref = refs
    else:
        q_ref, k_ref, v_ref, ob_ref, g_ref, wq_ref, wk_ref, din_ref, out_ref, state_ref = refs
    nch = SCAN_BLOCK // RET_CHUNK

    @pl.when(reset_ref[pl.program_id(0)] == 1)
    def _():
        state_ref[...] = jnp.zeros_like(state_ref)

    def chunk_body(ci, carry):
        cj = nch - 1 - ci if reverse else ci
        rows = pl.ds(pl.multiple_of(cj * RET_CHUNK, RET_CHUNK), RET_CHUNK)
        for h in range(RET_HEADS):
            kcols = slice(h * RET_DK, (h + 1) * RET_DK)
            vcols = slice(h * RET_DV, (h + 1) * RET_DV)
            qh = q_ref[rows, kcols]
            kh = k_ref[rows, kcols]
            vh = v_ref[rows, vcols]
            state = state_ref[h]
            o = _dot(qh, state.astype(BF16)) * wq_ref[h]
            if reverse:
                out_ref[rows, vcols] = o.astype(BF16)
            else:
                s = _dot_nt(qh, kh) * din_ref[h]
                o = o + _dot(s.astype(BF16), vh) + ob_ref[rows, vcols].astype(F32)
                out_ref[rows, vcols] = (g_ref[rows, vcols].astype(F32) * _rms(o)).astype(BF16)
            ks = (kh.astype(F32) * wk_ref[h]).astype(BF16)
            state_ref[h] = state * chunk_decay[h] + _dot_tn(ks, vh)
        return carry

    lax.fori_loop(0, nch, chunk_body, 0)


def _ret_tables():
    lg = np.log1p(-np.power(2.0, -5.0 - np.arange(RET_HEADS, dtype=np.float32))).astype(np.float32)
    c = RET_CHUNK
    a = np.arange(c, dtype=np.float32)
    col = lambda e, w: np.broadcast_to(np.exp(lg[:, None, None] * e[None, :, None]),
                                       (RET_HEADS, c, w)).astype(np.float32)
    dist = np.abs(a[:, None] - a[None, :])
    return dict(
        din=np.exp(lg[:, None, None] * dist[None]).astype(np.float32),
        wq_f=col(a + 1.0, RET_DV), wk_f=col(c - 1.0 - a, RET_DK),
        wq_b=col(c - a, RET_DV), wk_b=col(a, RET_DK),
        chunk_decay=tuple(float(v) for v in np.exp(lg * c)),
    )


def _ret_scan(q, k, v, g, reset_f, reset_b):
    T = q.shape[0]
    nb = T // SCAN_BLOCK
    tb = _ret_tables()
    state = pltpu.VMEM((RET_HEADS, RET_DK, RET_DV), F32)
    out_shape = jax.ShapeDtypeStruct((T, 2 * D_MODEL), BF16)

    def specs(imap):
        blk = lambda n: pl.BlockSpec((SCAN_BLOCK, n), imap)
        return blk(D_MODEL), blk(D_MODEL), blk(2 * D_MODEL)

    rev = lambda i, r: (nb - 1 - i, 0)
    bq, bk, bv = specs(rev)
    o_b = pl.pallas_call(
        functools.partial(_ret_scan_kernel, reverse=True, chunk_decay=tb["chunk_decay"]),
        grid_spec=pltpu.PrefetchScalarGridSpec(
            num_scalar_prefetch=1, grid=(nb,),
            in_specs=[bq, bk, bv, _resident(tb["wq_b"].shape), _resident(tb["wk_b"].shape)],
            out_specs=bv, scratch_shapes=[state]),
        out_shape=out_shape, compiler_params=_cparams(1, 40), name="ret_scan_bwd",
    )(reset_b, q, k, v, tb["wq_b"], tb["wk_b"])

    fwd = lambda i, r: (i, 0)
    bq, bk, bv = specs(fwd)
    return pl.pallas_call(
        functools.partial(_ret_scan_kernel, reverse=False, chunk_decay=tb["chunk_decay"]),
        grid_spec=pltpu.PrefetchScalarGridSpec(
            num_scalar_prefetch=1, grid=(nb,),
            in_specs=[bq, bk, bv, bv, bv, _resident(tb["wq_f"].shape), _resident(tb["wk_f"].shape),
                      _resident(tb["din"].shape)],
            out_specs=bv, scratch_shapes=[state]),
        out_shape=out_shape, compiler_params=_cparams(1, 40), name="ret_scan_fwd",
    )(reset_f, q, k, v, o_b, g, tb["wq_f"], tb["wk_f"], tb["din"])


def _post0_kernel(y_ref, xa_ref, xb_ref, wo_ref, ln_ref, wgu_ref, wd_ref, out_ref, *, n_a):
    x = jnp.where(pl.program_id(0) < n_a, xa_ref[...], xb_ref[...])
    h1 = x + _dot(y_ref[...], wo_ref[...])
    xn = (_rms(h1) * ln_ref[...]).astype(BF16)
    acc = h1
    fc = FFN_DENSE_CHUNK
    for c in range(FFN_DENSE // fc):
        gate = _dot(xn, wgu_ref[:, c * fc:(c + 1) * fc])
        up = _dot(xn, wgu_ref[:, FFN_DENSE + c * fc:FFN_DENSE + (c + 1) * fc])
        act = (_silu(gate) * up).astype(BF16)
        acc = acc + _dot(act, wd_ref[c * fc:(c + 1) * fc, :])
    out_ref[...] = acc


def _post0(y, xa, xb, w_out, ln, w_gu, w_down):
    T = y.shape[0]
    tm = TOK_TILE
    n_a = xa.shape[0] // tm
    tok = lambda n: pl.BlockSpec((tm, n), lambda i: (i, 0))
    return pl.pallas_call(
        functools.partial(_post0_kernel, n_a=n_a),
        grid=(T // tm,),
        in_specs=[tok(2 * D_MODEL), *_two_stream_specs(tm, n_a), _resident(w_out.shape),
                  _resident((1, D_MODEL)), _resident(w_gu.shape), _resident(w_down.shape)],
        out_specs=tok(D_MODEL),
        out_shape=jax.ShapeDtypeStruct((T, D_MODEL), F32),
        compiler_params=_cparams(1, 52),
        name="ret_out_ffn",
    )(y, xa, xb, w_out, ln, w_gu, w_down)


def _ssm_in_kernel(x_ref, ln_ref, wz_ref, wx_ref, wd_ref, wdh_ref, bias_ref, gate_ref, xbc_ref, dt_ref):
    xh, xl = _split2(_rms(x_ref[...]) * ln_ref[...])
    for j in range(SSM_DINNER // SLAB):
        gate_ref[j] = _silu(_dot(xh, wz_ref[:, j * SLAB:(j + 1) * SLAB])).astype(BF16)
    for j in range(SSM_CONV_DIM // SLAB):
        xbc_ref[:, j * SLAB:(j + 1) * SLAB] = _dot(xh, wx_ref[:, j * SLAB:(j + 1) * SLAB]).astype(BF16)
    both = _dot(xh, wd_ref[...])
    x = both[:, :LANES] + both[:, LANES:] + _dot(xl, wdh_ref[...]) + bias_ref[...]
    dt_ref[...] = jnp.maximum(x, 0.0) + jnp.log1p(jnp.exp(-jnp.abs(x)))


def _ssm_in(x, ln, wz, wx, wdh, wdl, bias_row):
    T = x.shape[0]
    tm = PROJ_TILE
    n_gate = SSM_DINNER // SLAB
    wd = jnp.concatenate([wdh, wdl], axis=1)
    tok = lambda n: pl.BlockSpec((tm, n), lambda i: (i, 0))
    return pl.pallas_call(
        _ssm_in_kernel,
        grid=(T // tm,),
        in_specs=[tok(D_MODEL), _resident((1, D_MODEL)), _resident(wz.shape), _resident(wx.shape),
                  _resident(wd.shape), _resident(wdh.shape), _resident(bias_row.shape)],
        out_specs=[pl.BlockSpec((n_gate, tm, SLAB), lambda i: (0, i, 0)), tok(SSM_CONV_DIM), tok(LANES)],
        out_shape=[jax.ShapeDtypeStruct((n_gate, T, SLAB), BF16), jax.ShapeDtypeStruct((T, SSM_CONV_DIM), BF16),
                   jax.ShapeDtypeStruct((T, LANES), F32)],
        compiler_params=_cparams(1, 52),
        name="ssm_in",
    )(x, ln, wz, wx, wd, wdh, bias_row)


def _conv_kernel(first_ref, last_ref, prev_ref, main_ref, next_ref, shift_ref, w_ref, b_ref, out_ref, ext_ref):
    i = pl.program_id(0)
    rb = SSD_BLOCK
    ext_rows = ext_ref.shape[0]

    @pl.when(i == 0)
    def _():
        ext_ref[rb + 2 * HALO:ext_rows, :] = jnp.zeros((ext_rows - rb - 2 * HALO, SSM_CONV_DIM), BF16)

    zero = jnp.zeros((HALO, SSM_CONV_DIM), BF16)
    ext_ref[0:HALO, :] = jnp.where(first_ref[i] == 1, zero, prev_ref[...])
    ext_ref[HALO:HALO + rb, :] = main_ref[...]
    ext_ref[HALO + rb:2 * HALO + rb, :] = jnp.where(last_ref[i] == 1, zero, next_ref[...])
    for s in range(SSM_CONV_DIM // SLAB):
        cs = slice(s * SLAB, (s + 1) * SLAB)
        for r0 in range(0, rb, CONV_ROWS):
            taps = _dot(shift_ref[...], ext_ref[r0:r0 + CONV_WINDOW, cs])
            acc = jnp.broadcast_to(b_ref[:, cs], (CONV_ROWS, SLAB))
            for j in range(SSM_CONV):
                acc = acc + taps[j * CONV_ROWS:(j + 1) * CONV_ROWS] * w_ref[j:j + 1, cs]
            out_ref[s, r0:r0 + CONV_ROWS, :] = _silu(acc).astype(BF16)


def _conv(xbc, conv_w, conv_b, first, last):
    T = xbc.shape[0]
    rb = SSD_BLOCK
    nb = T // rb
    per = rb // HALO
    nh = T // HALO
    r = np.arange(CONV_ROWS)
    shift = np.zeros((SSM_CONV * CONV_ROWS, CONV_WINDOW), np.float32)
    for j in range(SSM_CONV):
        shift[j * CONV_ROWS + r, HALO + r + j - SSM_CONV // 2] = 1.0
    shift = jnp.asarray(shift, dtype=BF16)
    ext_rows = rb - CONV_ROWS + CONV_WINDOW
    return pl.pallas_call(
        _conv_kernel,
        grid_spec=pltpu.PrefetchScalarGridSpec(
            num_scalar_prefetch=2, grid=(nb,),
            in_specs=[
                pl.BlockSpec((HALO, SSM_CONV_DIM), lambda i, f, l: (jnp.maximum(i * per - 1, 0), 0)),
                pl.BlockSpec((rb, SSM_CONV_DIM), lambda i, f, l: (i, 0)),
                pl.BlockSpec((HALO, SSM_CONV_DIM), lambda i, f, l: (jnp.minimum((i + 1) * per, nh - 1), 0)),
                _resident(shift.shape), _resident(conv_w.shape), _resident(conv_b.shape)],
            out_specs=pl.BlockSpec((SSM_CONV_DIM // SLAB, rb, SLAB), lambda i, f, l: (0, i, 0)),
            scratch_shapes=[pltpu.VMEM((ext_rows, SSM_CONV_DIM), BF16)]),
        out_shape=jax.ShapeDtypeStruct((SSM_CONV_DIM // SLAB, T, SLAB), BF16),
        compiler_params=_cparams(1, 48),
        name="ssm_conv",
    )(first, last, xbc, xbc, xbc, shift, conv_w, conv_b)


def _ssd_tables():
    r = np.arange(CHUNK)
    tri_l = (r[None, :] <= r[:, None]).astype(np.float32)
    tri_u = (r[None, :] >= r[:, None]).astype(np.float32)
    c = np.arange(SSM_DINNER) // SSM_HEADDIM
    j = np.arange(LANES)
    e_f = (j[:, None] == c[None, :]).astype(np.float32)
    e_b = (j[:, None] == (c[None, :] + SSM_HEADS)).astype(np.float32)
    as_bf16 = lambda x: jnp.asarray(x, dtype=BF16)
    return dict(tri_l3=as_bf16(np.concatenate([tri_l] * 3, axis=1)),
                tri_u3=as_bf16(np.concatenate([tri_u] * 3, axis=1)),
                e2_f=as_bf16(np.concatenate([e_f, e_f], axis=0)),
                e2_b=as_bf16(np.concatenate([e_b, e_b], axis=0)))


def _ssd_chunk_scalars(dt, a_ref, tri_l3_ref, tri_u3_ref):
    a = dt * a_ref[...]
    a3 = jnp.concatenate(_split3(a), axis=0)
    cum_f = _dot(tri_l3_ref[...], a3)
    cum_b = _dot(tri_u3_ref[...], a3)
    lane = lax.broadcasted_iota(jnp.int32, (CHUNK, LANES), 1)
    cum = jnp.where(lane < SSM_HEADS, cum_f, cum_b)
    tot = cum_f[CHUNK - 1:CHUNK, :]
    return cum, tot


def _expand(e2_ref, *rows):
    x = jnp.concatenate(rows, axis=0)
    hi, lo = _split2(x)
    return _dot(jnp.concatenate([hi, lo], axis=1), e2_ref[...])


def _ssd_bwd_kernel(reset_ref, xa_ref, dt_ref, a_ref, tri_l3_ref, tri_u3_ref, e2_ref,
                    yb_ref, state_ref):
    nch = SSD_BLOCK // CHUNK

    @pl.when(reset_ref[pl.program_id(0)] == 1)
    def _():
        state_ref[...] = jnp.zeros_like(state_ref)

    def chunk_body(ci, carry):
        rows = pl.ds(pl.multiple_of((nch - 1 - ci) * CHUNK, CHUNK), CHUNK)
        dt = dt_ref[rows, :]
        cum, tot = _ssd_chunk_scalars(dt, a_ref, tri_l3_ref, tri_u3_ref)
        ex = _expand(e2_ref, dt * jnp.exp(tot - cum), jnp.exp(cum),
                     jnp.broadcast_to(jnp.exp(tot), (16, LANES)))
        for g in range(SSM_GROUPS):
            gc = slice(g * SSM_GROUP_W, (g + 1) * SSM_GROUP_W)
            bm = xa_ref[SSM_GROUPS, rows, g * SSM_DSTATE:(g + 1) * SSM_DSTATE]
            cm = xa_ref[SSM_GROUPS + 1, rows, g * SSM_DSTATE:(g + 1) * SSM_DSTATE]
            state = state_ref[g]
            yb_ref[rows, gc] = (_dot(cm, state.astype(BF16)) * ex[CHUNK:2 * CHUNK, gc]).astype(BF16)
            xw = (xa_ref[g, rows, :].astype(F32) * ex[0:CHUNK, gc]).astype(BF16)
            state_ref[g] = state * ex[2 * CHUNK:2 * CHUNK + 1, gc] + _dot_tn(bm, xw)
        return carry

    lax.fori_loop(0, nch, chunk_body, 0)


def _ssd_fwd_kernel(reset_ref, xa_ref, dt_ref, gate_ref, yb_ref, a_ref, tri_l3_ref, tri_u3_ref,
                    e2_ref, dskip_ref, nw_ref, out_ref, state_ref, y_ref):
    nch = SSD_BLOCK // CHUNK

    @pl.when(reset_ref[pl.program_id(0)] == 1)
    def _():
        state_ref[...] = jnp.zeros_like(state_ref)

    def chunk_body(ci, carry):
        rows = pl.ds(pl.multiple_of(ci * CHUNK, CHUNK), CHUNK)
        dt = dt_ref[rows, :]
        cum, tot = _ssd_chunk_scalars(dt, a_ref, tri_l3_ref, tri_u3_ref)
        ex = _expand(e2_ref, dt * jnp.exp(tot - cum), jnp.exp(cum),
                     jnp.broadcast_to(jnp.exp(tot), (16, LANES)))
        row_t = (cum - jnp.log(dt)).T
        li = lax.broadcasted_iota(jnp.int32, (CHUNK, CHUNK), 0)
        si = lax.broadcasted_iota(jnp.int32, (CHUNK, CHUNK), 1)
        lower = li >= si
        first_half = si < SSM_HEADDIM
        for g in range(SSM_GROUPS):
            gc = slice(g * SSM_GROUP_W, (g + 1) * SSM_GROUP_W)
            bm = xa_ref[SSM_GROUPS, rows, g * SSM_DSTATE:(g + 1) * SSM_DSTATE]
            cm = xa_ref[SSM_GROUPS + 1, rows, g * SSM_DSTATE:(g + 1) * SSM_DSTATE]
            cb = _dot_nt(cm, bm)
            heads_per_group = SSM_HEADS // SSM_GROUPS
            for pair in range(heads_per_group // 2):
                mats = []
                for hh in range(2):
                    h = g * heads_per_group + 2 * pair + hh
                    hb = SSM_HEADS + h
                    seg = jnp.where(lower, cum[:, h:h + 1] - row_t[h:h + 1, :],
                                    cum[:, hb:hb + 1] - row_t[hb:hb + 1, :])
                    mats.append((cb * jnp.exp(seg)).astype(BF16))
                pc = slice(2 * pair * SSM_HEADDIM, (2 * pair + 2) * SSM_HEADDIM)
                xp = xa_ref[g, rows, pc]
                zero = jnp.zeros_like(xp)
                rhs = jnp.concatenate([jnp.where(first_half, xp, zero), jnp.where(first_half, zero, xp)], axis=0)
                y_ref[:, pc] = _dot(jnp.concatenate(mats, axis=1), rhs)
            state = state_ref[g]
            xs = xa_ref[g, rows, :].astype(F32)
            y = (y_ref[...] + _dot(cm, state.astype(BF16)) * ex[CHUNK:2 * CHUNK, gc]
                 + yb_ref[rows, gc].astype(F32) + xs * dskip_ref[:, gc])
            y = y * gate_ref[g, rows, :].astype(F32)
            out_ref[rows, gc] = (_rms(y) * nw_ref[:, gc]).astype(BF16)
            xw = (xs * ex[0:CHUNK, gc]).astype(BF16)
            state_ref[g] = state * ex[2 * CHUNK:2 * CHUNK + 1, gc] + _dot_tn(bm, xw)
        return carry

    lax.fori_loop(0, nch, chunk_body, 0)


def _ssd(xa, dt, gate, a_row, dskip, norm_w, reset_f, reset_b):
    T = dt.shape[0]
    nb = T // SSD_BLOCK
    tb = _ssd_tables()
    state = pltpu.VMEM((SSM_GROUPS, SSM_DSTATE, SSM_GROUP_W), F32)
    consts = [a_row, tb["tri_l3"], tb["tri_u3"]]
    const_specs = [_resident(c.shape) for c in consts]
    out_shape = jax.ShapeDtypeStruct((T, SSM_DINNER), BF16)

    rev = lambda i, r: (nb - 1 - i, 0)
    blk = lambda n, imap: pl.BlockSpec((SSD_BLOCK, n), imap)
    slabs = lambda a, imap: pl.BlockSpec((a.shape[0], SSD_BLOCK, SLAB), lambda i, r: (0, imap(i, r)[0], 0))
    y_b = pl.pallas_call(
        _ssd_bwd_kernel,
        grid_spec=pltpu.PrefetchScalarGridSpec(
            num_scalar_prefetch=1, grid=(nb,),
            in_specs=[slabs(xa, rev), blk(LANES, rev)] + const_specs + [_resident(tb["e2_b"].shape)],
            out_specs=blk(SSM_DINNER, rev), scratch_shapes=[state]),
        out_shape=out_shape, compiler_params=_cparams(1, 40), name="ssd_bwd",
    )(reset_b, xa, dt, *consts, tb["e2_b"])

    fwd = lambda i, r: (i, 0)
    return pl.pallas_call(
        _ssd_fwd_kernel,
        grid_spec=pltpu.PrefetchScalarGridSpec(
            num_scalar_prefetch=1, grid=(nb,),
            in_specs=[slabs(xa, fwd), blk(LANES, fwd), slabs(gate, fwd), blk(SSM_DINNER, fwd)]
            + const_specs + [_resident(tb["e2_f"].shape), _resident(dskip.shape), _resident(norm_w.shape)],
            out_specs=blk(SSM_DINNER, fwd),
            scratch_shapes=[state, pltpu.VMEM((CHUNK, SSM_GROUP_W), F32)]),
        out_shape=out_shape, compiler_params=_cparams(1, 52), name="ssd_fwd",
    )(reset_f, xa, dt, gate, y_b, *consts, tb["e2_f"], dskip, norm_w)


def _pack_halves(x):
    n = x.shape[1] // 2
    hi = lax.bitcast_convert_type(x[:, :n].astype(F32), jnp.uint32)
    lo = lax.bitcast_convert_type(x[:, n:].astype(F32), jnp.uint32)
    return hi | (lo >> 16)


def _unpack_halves(w):
    hi = lax.bitcast_convert_type(w & jnp.uint32(0xFFFF0000), F32)
    lo = lax.bitcast_convert_type(w << 16, F32)
    return hi, lo


def _router_kernel(y_ref, h_ref, wo_ref, ln_ref, wh_ref, wl_ref, b_ref, tri_ref,
                   h3_ref, xn_ref, info_ref, info_t_ref, count_ref, base_ref):
    @pl.when(pl.program_id(0) == 0)
    def _():
        base_ref[...] = jnp.zeros_like(base_ref)

    h3 = h_ref[...] + _dot(y_ref[...], wo_ref[...])
    h3_ref[...] = h3
    xh, xl = _split2(_rms(h3) * ln_ref[...])
    xn_ref[...] = _pack_halves(xh)
    logits = _dot_nt(wh_ref[...], xh) + _dot_nt(wh_ref[...], xl) + _dot_nt(wl_ref[...], xh) + b_ref[...]
    row = lax.broadcasted_iota(jnp.int32, logits.shape, 0)
    m1 = jnp.max(logits, axis=0, keepdims=True)
    i1 = jnp.min(jnp.where(logits == m1, row, LANES), axis=0, keepdims=True)
    rest = jnp.where(row == i1, -jnp.inf, logits)
    m2 = jnp.max(rest, axis=0, keepdims=True)
    i2 = jnp.min(jnp.where(rest == m2, row, LANES), axis=0, keepdims=True)
    e = jnp.exp(m2 - m1)
    g1 = 1.0 / (1.0 + e)
    g2 = e / (1.0 + e)
    oh1 = row == i1
    oh2 = row == i2
    oh = jnp.where(oh1 | oh2, 1.0, 0.0)
    base = base_ref[...]
    prefix = _dot(oh.astype(BF16), tri_ref[...]) + base
    r1 = jnp.sum(jnp.where(oh1, prefix, 0.0), axis=0, keepdims=True)
    r2 = jnp.sum(jnp.where(oh2, prefix, 0.0), axis=0, keepdims=True)
    base = base + jnp.sum(oh, axis=1, keepdims=True)
    count_ref[...] = base[:, :LANES]
    base_ref[...] = base
    fields = [i1.astype(F32), i2.astype(F32), g1, g2, r1, r2]
    info_t = jnp.zeros(logits.shape, F32)
    for c, f in enumerate(fields):
        info_t = jnp.where(row == c, f, info_t)
    info_t_ref[...] = info_t[:ROUTE_COLS, :]
    info_ref[...] = info_t.T[:, :ROUTE_COLS]


def _router(y, h, w_out, ln, wh, wl, b):
    T = h.shape[0]
    tb = MOE_TOK
    nb = T // tb
    r = np.arange(tb)
    tri = jnp.asarray((r[:, None] < r[None, :]).astype(np.float32), dtype=BF16)
    tok = lambda n: pl.BlockSpec((tb, n), lambda i: (i, 0))
    return pl.pallas_call(
        _router_kernel,
        grid=(nb,),
        in_specs=[tok(SSM_DINNER), tok(D_MODEL), _resident(w_out.shape), _resident((1, D_MODEL)),
                  _resident(wh.shape), _resident(wl.shape), _resident(b.shape), _resident(tri.shape)],
        out_specs=[tok(D_MODEL), tok(D_MODEL // 2), tok(ROUTE_COLS),
                   pl.BlockSpec((ROUTE_COLS, tb), lambda i: (0, i)), pl.BlockSpec((LANES, LANES), lambda i: (0, 0))],
        out_shape=[jax.ShapeDtypeStruct((T, D_MODEL), F32), jax.ShapeDtypeStruct((T, D_MODEL // 2), jnp.uint32),
                   jax.ShapeDtypeStruct((T, ROUTE_COLS), F32), jax.ShapeDtypeStruct((ROUTE_COLS, T), F32),
                   jax.ShapeDtypeStruct((LANES, LANES), F32)],
        scratch_shapes=[pltpu.VMEM((LANES, tb), F32)],
        compiler_params=_cparams(1, 32),
        name="ssm_out_router",
    )(y, h, w_out, ln, wh, wl, b, tri)


def _sc_gather(table, idx):
    n_rows, width = idx.shape[0], table.shape[1]
    workers = SC_CORES * SC_SUBCORES
    nbuf = 2
    assert n_rows % (workers * SC_CHUNK * nbuf) == 0
    per_worker = n_rows // workers
    mesh = plsc.VectorSubcoreMesh(core_axis_name="c", subcore_axis_name="s")

    @functools.partial(
        pl.kernel, mesh=mesh,
        out_type=jax.ShapeDtypeStruct((n_rows, width), table.dtype),
        scratch_types=[pltpu.VMEM((nbuf, SC_CHUNK), jnp.int32),
                       pltpu.VMEM((nbuf, SC_CHUNK, width), table.dtype),
                       pltpu.SemaphoreType.DMA((nbuf,)),
                       pltpu.SemaphoreType.DMA((nbuf,))],
    )
    def gather_rows(table_hbm, idx_hbm, out_hbm, idx_v, rows_v, gsem, wsem):
        base = (lax.axis_index("s") * SC_CORES + lax.axis_index("c")) * per_worker

        def gather(c, b):
            off = pl.multiple_of(base + c * SC_CHUNK, 8)
            pltpu.sync_copy(idx_hbm.at[pl.ds(off, SC_CHUNK)], idx_v.at[b])
            return pltpu.make_async_copy(table_hbm.at[idx_v.at[b]], rows_v.at[b], gsem.at[b])

        def write(c, b):
            off = pl.multiple_of(base + c * SC_CHUNK, 8)
            return pltpu.make_async_copy(rows_v.at[b], out_hbm.at[pl.ds(off, SC_CHUNK)], wsem.at[b])

        @pl.loop(0, per_worker // SC_CHUNK, step=nbuf)
        def _(c0):
            copies = [gather(c0 + b, b) for b in range(nbuf)]
            for cp in copies:
                cp.start()
            writes = []
            for b, cp in enumerate(copies):
                cp.wait()
                writes.append(write(c0 + b, b))
                writes[-1].start()
            for wr in writes:
                wr.wait()

    return gather_rows(table, idx)


def _sc_scatter_pairs(table, pos, n_out):
    n_rows, width = table.shape
    workers = SC_CORES * SC_SUBCORES
    nbuf = 2
    assert n_rows % (workers * SC_CHUNK * nbuf) == 0
    per_worker = n_rows // workers
    mesh = plsc.VectorSubcoreMesh(core_axis_name="c", subcore_axis_name="s")

    @functools.partial(
        pl.kernel, mesh=mesh,
        out_type=jax.ShapeDtypeStruct((n_out, width), table.dtype),
        scratch_types=[pltpu.VMEM((nbuf, 2, SC_CHUNK), jnp.int32),
                       pltpu.VMEM((nbuf, SC_CHUNK, width), table.dtype),
                       pltpu.SemaphoreType.DMA((nbuf,)),
                       pltpu.SemaphoreType.DMA((nbuf,))],
    )
    def scatter_rows(table_hbm, pos_hbm, out_hbm, idx_v, rows_v, rsem, wsem):
        base = (lax.axis_index("s") * SC_CORES + lax.axis_index("c")) * per_worker

        def read(c, b):
            off = pl.multiple_of(base + c * SC_CHUNK, 8)
            return pltpu.make_async_copy(table_hbm.at[pl.ds(off, SC_CHUNK)], rows_v.at[b], rsem.at[b])

        def write(b, j):
            return pltpu.make_async_copy(rows_v.at[b], out_hbm.at[idx_v.at[b, j]], wsem.at[b])

        @pl.loop(0, per_worker // SC_CHUNK, step=nbuf)
        def _(c0):
            reads = [read(c0 + b, b) for b in range(nbuf)]
            for rd in reads:
                rd.start()
            for b in range(nbuf):
                pltpu.sync_copy(pos_hbm.at[base // SC_CHUNK + c0 + b], idx_v.at[b])
            writes = []
            for b, rd in enumerate(reads):
                rd.wait()
                for j in range(2):
                    writes.append(write(b, j))
                    writes[-1].start()
            for wr in writes:
                wr.wait()

    return scatter_rows(table, pos)


def _expert_kernel(texp_ref, trows_ref, x_ref, wg_ref, wu_ref, wd_ref, out_ref, acc_ref, xb_ref):
    i = pl.program_id(0)
    f = pl.program_id(1)
    nf = pl.num_programs(1)
    n = D_MODEL // 2

    @pl.when(trows_ref[i] > 0)
    def _():
        @pl.when(f == 0)
        def _():
            acc_ref[...] = jnp.zeros_like(acc_ref)
            row = lax.broadcasted_iota(jnp.int32, x_ref.shape, 0)
            hi, lo = _unpack_halves(jnp.where(row < trows_ref[i], x_ref[...], jnp.uint32(0)))
            xb_ref[:, :n] = hi.astype(BF16)
            xb_ref[:, n:] = lo.astype(BF16)

        def ffn_chunk(m):
            x = xb_ref[0:m, :]
            act = (_silu(_dot(x, wg_ref[0].astype(BF16))) * _dot(x, wu_ref[0].astype(BF16))).astype(BF16)
            acc_ref[0:m, :] += _dot(act, wd_ref[0].astype(BF16))

        for m in range(MOE_ROWS_STEP, MOE_SLOT + 1, MOE_ROWS_STEP):
            @pl.when((trows_ref[i] > m - MOE_ROWS_STEP) & (trows_ref[i] <= m))
            def _(m=m):
                ffn_chunk(m)

        @pl.when(f == nf - 1)
        def _():
            out_ref[...] = _pack_halves(acc_ref[...].astype(BF16))

    @pl.when((trows_ref[i] == 0) & (f == nf - 1))
    def _():
        out_ref[...] = jnp.zeros_like(out_ref)


def _experts(xs, w_gu, w_down, tile_expert, tile_rows):
    n_tiles = xs.shape[0] // MOE_SLOT
    nf = FFN_EXPERT // MOE_FC
    return pl.pallas_call(
        _expert_kernel,
        grid_spec=pltpu.PrefetchScalarGridSpec(
            num_scalar_prefetch=2, grid=(n_tiles, nf),
            in_specs=[pl.BlockSpec((MOE_SLOT, D_MODEL // 2), lambda i, f, te, tv: (i, 0)),
                      pl.BlockSpec((1, D_MODEL, MOE_FC), lambda i, f, te, tv: (te[i], 0, f)),
                      pl.BlockSpec((1, D_MODEL, MOE_FC), lambda i, f, te, tv: (te[i], 0, nf + f)),
                      pl.BlockSpec((1, MOE_FC, D_MODEL), lambda i, f, te, tv: (te[i], f, 0))],
            out_specs=pl.BlockSpec((MOE_SLOT, D_MODEL // 2), lambda i, f, te, tv: (i, 0)),
            scratch_shapes=[pltpu.VMEM((MOE_SLOT, D_MODEL), F32), pltpu.VMEM((MOE_SLOT, D_MODEL), BF16)]),
        out_shape=jax.ShapeDtypeStruct(xs.shape, jnp.uint32),
        compiler_params=_cparams(2, 48),
        name="moe_experts",
    )(tile_expert, tile_rows, xs, w_gu, w_gu, w_down)


def _finalize_kernel(y0_ref, y1_ref, info_ref, h_ref, fn_ref, out_ref):
    n = D_MODEL // 2
    info = info_ref[...]
    g0 = info[:, 2:3]
    g1 = info[:, 3:4]
    a_hi, a_lo = _unpack_halves(y0_ref[...])
    b_hi, b_lo = _unpack_halves(y1_ref[...])
    left = h_ref[:, :n] + g0 * a_hi + g1 * b_hi
    right = h_ref[:, n:] + g0 * a_lo + g1 * b_lo
    ms = (jnp.sum(left * left, axis=-1, keepdims=True)
          + jnp.sum(right * right, axis=-1, keepdims=True)) * (1.0 / D_MODEL)
    scale = lax.rsqrt(ms + EPS)
    out_ref[:, :n] = left * scale * fn_ref[:, :n]
    out_ref[:, n:] = right * scale * fn_ref[:, n:]


def _finalize(yg, info, h3, final_norm, tok0):
    n = yg.shape[0] // 2
    tb = MOE_TOK
    nb = n // tb
    b0 = tok0 // tb
    tok = lambda w: pl.BlockSpec((tb, w), lambda i: (b0 + i, 0))
    loc = lambda off: pl.BlockSpec((tb, D_MODEL // 2), lambda i: (i + off, 0))
    return pl.pallas_call(
        _finalize_kernel,
        grid=(nb,),
        in_specs=[loc(0), loc(nb), tok(info.shape[1]), tok(D_MODEL), _resident((1, D_MODEL))],
        out_specs=pl.BlockSpec((tb, D_MODEL), lambda i: (i, 0)),
        out_shape=jax.ShapeDtypeStruct((n, D_MODEL), F32),
        compiler_params=_cparams(1, 32),
        name="moe_finalize",
    )(yg, yg, info, h3, final_norm)


def _moe(y_ssm, h2, w_out, ln, wr_hi, wr_lo, rb, w_gu, w_down, final_norm, n_a):
    T = h2.shape[0]
    n_tiles = 2 * T // MOE_SLOT + N_EXPERTS
    h3, xn, info, info_t, counts = _router(y_ssm, h2, w_out, ln, wr_hi, wr_lo, rb)

    counts = counts[:N_EXPERTS, 0].astype(jnp.int32)
    tiles_e = (counts + MOE_SLOT - 1) // MOE_SLOT
    tiles_cum = jnp.cumsum(tiles_e)
    gstart = (tiles_cum - tiles_e) * MOE_SLOT
    routed = info_t.astype(jnp.int32)
    start_of = lambda e: sum(jnp.where(e == x, gstart[x], 0) for x in range(N_EXPERTS))
    pos0 = start_of(routed[0]) + routed[4]
    pos1 = start_of(routed[1]) + routed[5]
    tile_ids = jnp.arange(n_tiles, dtype=jnp.int32)
    tile_expert = jnp.minimum(jnp.sum(tile_ids[:, None] >= tiles_cum[None, :], axis=1), N_EXPERTS - 1)
    tile_expert = tile_expert.astype(jnp.int32)
    tile_rows = jnp.clip(counts[tile_expert] - (tile_ids * MOE_SLOT - gstart[tile_expert]), 0, MOE_SLOT)
    tile_rows = jnp.where(tile_ids < tiles_cum[-1], tile_rows, 0).astype(jnp.int32)
    pos = jnp.stack([pos0.reshape(-1, SC_CHUNK), pos1.reshape(-1, SC_CHUNK)], axis=1)

    xs = _sc_scatter_pairs(xn, pos, n_tiles * MOE_SLOT)
    ys = _experts(xs, w_gu, w_down, tile_expert, tile_rows)
    outs = []
    for lo, hi in ((0, n_a), (n_a, T)):
        yg = _sc_gather(ys, jnp.concatenate([pos0[lo:hi], pos1[lo:hi]]))
        outs.append(_finalize(yg, info, h3, final_norm, lo))
    return outs


def _seq_flags(seq_lens, block):
    first, last = [], []
    for n in seq_lens:
        nb = n // block
        first += [1] + [0] * (nb - 1)
        last += [0] * (nb - 1) + [1]
    return np.asarray(first, np.int32), np.asarray(last, np.int32)


def _trunk(xa, xb, seq_lens, ln_mix0, ret_w_in, ret_w_out, ln_ffn0, ffn_w_gu, ffn_w_down, ln_mix1, ssm_w_in,
           ssm_conv_w, ssm_conv_b, ssm_dt_bias, ssm_A_log, ssm_D, ssm_norm_w, ssm_w_out, ln_ffn1,
           moe_router_w, moe_router_b, moe_w_gu, moe_w_down, final_norm):
    assert all(n % b == 0 for n in seq_lens for b in (SCAN_BLOCK, SSD_BLOCK, MOE_TOK, TOK_TILE, PROJ_TILE))
    row = lambda v: v.astype(F32).reshape(1, -1)
    first, last = _seq_flags(seq_lens, SCAN_BLOCK)
    reset_f = jnp.asarray(first)
    reset_b = jnp.asarray(last[::-1].copy())
    ssd_first, ssd_last = _seq_flags(seq_lens, SSD_BLOCK)

    half = RET_DK // 2
    inv = ROPE_BASE ** (-jnp.arange(half, dtype=F32) / half)
    ang_base = jnp.arange(0, max(seq_lens), TOK_TILE).astype(F32)[:, None, None] * inv[None, None, :]
    ang_off = jnp.arange(TOK_TILE).astype(F32)[:, None] * inv[None, :]
    pos_block = jnp.asarray(np.concatenate([np.arange(n // TOK_TILE) for n in seq_lens]).astype(np.int32))
    q, k, v, g = _ret_in(xa, xb, row(ln_mix0), ret_w_in.astype(BF16), jnp.cos(ang_base), jnp.sin(ang_base),
                         jnp.cos(ang_off), jnp.sin(ang_off), pos_block)
    y = _ret_scan(q, k, v, g, reset_f, reset_b)
    h2 = _post0(y, xa, xb, ret_w_out.astype(BF16), row(ln_ffn0), ffn_w_gu.astype(BF16),
                ffn_w_down.astype(BF16))

    wz = ssm_w_in[:, :SSM_DINNER].astype(BF16)
    wx = ssm_w_in[:, SSM_DINNER:SSM_DINNER + SSM_CONV_DIM].astype(BF16)
    wdt = jnp.pad(ssm_w_in[:, SSM_DINNER + SSM_CONV_DIM:], ((0, 0), (0, LANES - 2 * SSM_HEADS)))
    wdh, wdl = _split2(wdt)
    pad_row = lambda v: jnp.pad(v.astype(F32).reshape(1, -1), ((0, 0), (0, LANES - 2 * SSM_HEADS)))
    gate, xbc, dt = _ssm_in(h2, row(ln_mix1), wz, wx, wdh, wdl, pad_row(ssm_dt_bias))
    xact = _conv(xbc, ssm_conv_w.reshape(SSM_CONV, SSM_CONV_DIM).astype(F32), row(ssm_conv_b),
                 jnp.asarray(ssd_first), jnp.asarray(ssd_last))
    a_row = pad_row(-jnp.exp(ssm_A_log.astype(F32)))
    dskip = jnp.repeat(ssm_D.astype(F32), SSM_HEADDIM).reshape(1, SSM_DINNER)
    y = _ssd(xact, dt, gate, a_row, dskip, row(ssm_norm_w), jnp.asarray(ssd_first),
             jnp.asarray(ssd_last[::-1].copy()))

    wr = jnp.pad(moe_router_w.astype(F32).T, ((0, LANES - N_EXPERTS), (0, 0)))
    wr_hi, wr_lo = _split2(wr)
    rb = jnp.pad(moe_router_b.astype(F32).reshape(-1, 1), ((0, LANES - N_EXPERTS), (0, 0)), constant_values=-1e30)
    rb = jnp.broadcast_to(rb, (LANES, MOE_TOK))
    return _moe(y, h2, ssm_w_out.astype(BF16), row(ln_ffn1), wr_hi, wr_lo, rb,
                moe_w_gu, moe_w_down, row(final_norm), xa.shape[0])


def kernel(x_prompt, x_sample, ln_mix0, ret_w_in, ret_w_out, ln_ffn0, ffn_w_gu, ffn_w_down, ln_mix1, ssm_w_in,
           ssm_conv_w, ssm_conv_b, ssm_dt_bias, ssm_A_log, ssm_D, ssm_norm_w, ssm_w_out, ln_ffn1, moe_router_w,
           moe_router_b, moe_w_gu, moe_w_down, final_norm):
    seq_lens = [x_prompt.shape[1]] * x_prompt.shape[0] + [x_sample.shape[1]] * x_sample.shape[0]
    out_a, out_b = _trunk(x_prompt.reshape(-1, D_MODEL), x_sample.reshape(-1, D_MODEL), seq_lens, ln_mix0, ret_w_in, ret_w_out, ln_ffn0, ffn_w_gu, ffn_w_down, ln_mix1, ssm_w_in,
                 ssm_conv_w, ssm_conv_b, ssm_dt_bias, ssm_A_log, ssm_D, ssm_norm_w, ssm_w_out, ln_ffn1,
                 moe_router_w, moe_router_b, moe_w_gu, moe_w_down, final_norm)
    return (out_a.reshape(x_prompt.shape), out_b.reshape(x_sample.shape))
```

```python
import functools

import numpy as np
import jax
import jax.numpy as jnp
from jax import lax
from jax.experimental import pallas as pl
from jax.experimental.pallas import tpu as pltpu
from jax.experimental.pallas import tpu_sc as plsc

F32 = jnp.float32
BF16 = jnp.bfloat16

D_MODEL = 1024
EPS = 1e-6
RET_HEADS = 4
RET_DK = 256
RET_DV = 512
ROPE_BASE = 10000.0
SSM_DINNER = 2048
SSM_HEADDIM = 64
SSM_HEADS = 32
SSM_GROUPS = 4
SSM_DSTATE = 128
SSM_CONV = 5
SSM_CONV_DIM = 3072
SSM_GROUP_W = SSM_DINNER // SSM_GROUPS
SLAB = SSM_GROUP_W
CONV_ROWS = 128
CONV_WINDOW = 256
HALO = 16
FFN_DENSE = 2816
FFN_DENSE_CHUNK = 1408
N_EXPERTS = 8
FFN_EXPERT = 3584

CHUNK = 128
RET_CHUNK = 256
LANES = 128
TOK_TILE = 512
PROJ_TILE = 1024
SCAN_BLOCK = 512
SSD_BLOCK = 1024
MOE_TOK = 512
MOE_SLOT = 1024
MOE_FC = 512
MOE_ROWS_STEP = 256
ROUTE_COLS = 8
SC_CORES = 2
SC_SUBCORES = 16
SC_CHUNK = 64


def _cparams(n_axes, vmem_mb):
    return pltpu.CompilerParams(dimension_semantics=("arbitrary",) * n_axes,
                                vmem_limit_bytes=vmem_mb << 20)


def _resident(shape):
    nd = len(shape)
    return pl.BlockSpec(shape, lambda *_: (0,) * nd, pipeline_mode=pl.Buffered(1))


def _rms(x):
    return x * lax.rsqrt(jnp.mean(x * x, axis=-1, keepdims=True) + EPS)


def _silu(x):
    return x * jax.nn.sigmoid(x)


def _dot(a, b):
    return jnp.dot(a, b, preferred_element_type=F32)


def _dot_nt(a, b):
    return lax.dot_general(a, b, (((1,), (1,)), ((), ())), preferred_element_type=F32)


def _dot_tn(a, b):
    return lax.dot_general(a, b, (((0,), (0,)), ((), ())), preferred_element_type=F32)


def _split2(x):
    hi = x.astype(BF16)
    lo = (x - hi.astype(F32)).astype(BF16)
    return hi, lo


def _split3(x):
    hi = x.astype(BF16)
    r = x - hi.astype(F32)
    mid = r.astype(BF16)
    lo = (r - mid.astype(F32)).astype(BF16)
    return hi, mid, lo


def _two_stream_specs(tm, n_a):
    return (pl.BlockSpec((tm, D_MODEL), lambda i, *_: (jnp.minimum(i, n_a - 1), 0)),
            pl.BlockSpec((tm, D_MODEL), lambda i, *_: (jnp.maximum(i - n_a, 0), 0)))


def _ret_in_kernel(pos_ref, xa_ref, xb_ref, ln_ref, w_ref, cb_ref, sb_ref, co_ref, so_ref,
                   q_ref, k_ref, v_ref, g_ref, *, n_a):
    x = jnp.where(pl.program_id(0) < n_a, xa_ref[...], xb_ref[...])
    xn = (_rms(x) * ln_ref[...]).astype(BF16)
    cb, sb, co, so = cb_ref[0], sb_ref[0], co_ref[...], so_ref[...]
    cos = cb * co - sb * so
    sin = sb * co + cb * so
    half = RET_DK // 2

    def rotary(dst, col0, scale):
        for h in range(RET_HEADS):
            c = col0 + h * RET_DK
            p = _dot(xn, w_ref[:, c:c + RET_DK])
            p1, p2 = p[:, :half], p[:, half:]
            dst[:, h * RET_DK:h * RET_DK + half] = ((p1 * cos - p2 * sin) * scale).astype(BF16)
            dst[:, h * RET_DK + half:(h + 1) * RET_DK] = ((p1 * sin + p2 * cos) * scale).astype(BF16)

    rotary(q_ref, 0, 1.0)
    rotary(k_ref, D_MODEL, RET_DK ** -0.5)
    for j in range(2 * D_MODEL // 512):
        v_ref[:, j * 512:(j + 1) * 512] = _dot(
            xn, w_ref[:, 2 * D_MODEL + j * 512:2 * D_MODEL + (j + 1) * 512]).astype(BF16)
        g_ref[:, j * 512:(j + 1) * 512] = _silu(_dot(
            xn, w_ref[:, 4 * D_MODEL + j * 512:4 * D_MODEL + (j + 1) * 512])).astype(BF16)


def _ret_in(xa, xb, ln, w_in, cos_base, sin_base, cos_off, sin_off, pos_block):
    T = xa.shape[0] + xb.shape[0]
    tm = TOK_TILE
    n_a = xa.shape[0] // tm
    tok = lambda n: pl.BlockSpec((tm, n), lambda i, pb: (i, 0))
    rot = pl.BlockSpec((1, 1, LANES), lambda i, pb: (pb[i], 0, 0))
    return pl.pallas_call(
        functools.partial(_ret_in_kernel, n_a=n_a),
        grid_spec=pltpu.PrefetchScalarGridSpec(
            num_scalar_prefetch=1, grid=(T // tm,),
            in_specs=[*_two_stream_specs(tm, n_a), _resident((1, D_MODEL)), _resident(w_in.shape), rot, rot,
                      _resident(cos_off.shape), _resident(sin_off.shape)],
            out_specs=[tok(D_MODEL), tok(D_MODEL), tok(2 * D_MODEL), tok(2 * D_MODEL)]),
        out_shape=[jax.ShapeDtypeStruct((T, D_MODEL), BF16), jax.ShapeDtypeStruct((T, D_MODEL), BF16),
                   jax.ShapeDtypeStruct((T, 2 * D_MODEL), BF16), jax.ShapeDtypeStruct((T, 2 * D_MODEL), BF16)],
        compiler_params=_cparams(1, 48),
        name="ret_in",
    )(pos_block, xa, xb, ln, w_in, cos_base, sin_base, cos_off, sin_off)


def _ret_scan_kernel(reset_ref, *refs, reverse, chunk_decay):
    if reverse:
        q_ref, k_ref, v_ref, wq_ref, wk_ref, out_ref, state_ref = refs
    else:
        q_ref, k_ref, v_ref, ob_ref, g_ref, wq_ref, wk_ref, din_ref, out_ref, state_ref = refs
    nch = SCAN_BLOCK // RET_CHUNK

    @pl.when(reset_ref[pl.program_id(0)] == 1)
    def _():
        state_ref[...] = jnp.zeros_like(state_ref)

    def chunk_body(ci, carry):
        cj = nch - 1 - ci if reverse else ci
        rows = pl.ds(pl.multiple_of(cj * RET_CHUNK, RET_CHUNK), RET_CHUNK)
        for h in range(RET_HEADS):
            kcols = slice(h * RET_DK, (h + 1) * RET_DK)
            vcols = slice(h * RET_DV, (h + 1) * RET_DV)
            qh = q_ref[rows, kcols]
            kh = k_ref[rows, kcols]
            vh = v_ref[rows, vcols]
            state = state_ref[h]
            o = _dot(qh, state.astype(BF16)) * wq_ref[h]
            if reverse:
                out_ref[rows, vcols] = o.astype(BF16)
            else:
                s = _dot_nt(qh, kh) * din_ref[h]
                o = o + _dot(s.astype(BF16), vh) + ob_ref[rows, vcols].astype(F32)
                out_ref[rows, vcols] = (g_ref[rows, vcols].astype(F32) * _rms(o)).astype(BF16)
            ks = (kh.astype(F32) * wk_ref[h]).astype(BF16)
            state_ref[h] = state * chunk_decay[h] + _dot_tn(ks, vh)
        return carry

    lax.fori_loop(0, nch, chunk_body, 0)


def _ret_tables():
    lg = np.log1p(-np.power(2.0, -5.0 - np.arange(RET_HEADS, dtype=np.float32))).astype(np.float32)
    c = RET_CHUNK
    a = np.arange(c, dtype=np.float32)
    col = lambda e, w: np.broadcast_to(np.exp(lg[:, None, None] * e[None, :, None]),
                                       (RET_HEADS, c, w)).astype(np.float32)
    dist = np.abs(a[:, None] - a[None, :])
    return dict(
        din=np.exp(lg[:, None, None] * dist[None]).astype(np.float32),
        wq_f=col(a + 1.0, RET_DV), wk_f=col(c - 1.0 - a, RET_DK),
        wq_b=col(c - a, RET_DV), wk_b=col(a, RET_DK),
        chunk_decay=tuple(float(v) for v in np.exp(lg * c)),
    )


def _ret_scan(q, k, v, g, reset_f, reset_b):
    T = q.shape[0]
    nb = T // SCAN_BLOCK
    tb = _ret_tables()
    state = pltpu.VMEM((RET_HEADS, RET_DK, RET_DV), F32)
    out_shape = jax.ShapeDtypeStruct((T, 2 * D_MODEL), BF16)

    def specs(imap):
        blk = lambda n: pl.BlockSpec((SCAN_BLOCK, n), imap)
        return blk(D_MODEL), blk(D_MODEL), blk(2 * D_MODEL)

    rev = lambda i, r: (nb - 1 - i, 0)
    bq, bk, bv = specs(rev)
    o_b = pl.pallas_call(
        functools.partial(_ret_scan_kernel, reverse=True, chunk_decay=tb["chunk_decay"]),
        grid_spec=pltpu.PrefetchScalarGridSpec(
            num_scalar_prefetch=1, grid=(nb,),
            in_specs=[bq, bk, bv, _resident(tb["wq_b"].shape), _resident(tb["wk_b"].shape)],
            out_specs=bv, scratch_shapes=[state]),
        out_shape=out_shape, compiler_params=_cparams(1, 40), name="ret_scan_bwd",
    )(reset_b, q, k, v, tb["wq_b"], tb["wk_b"])

    fwd = lambda i, r: (i, 0)
    bq, bk, bv = specs(fwd)
    return pl.pallas_call(
        functools.partial(_ret_scan_kernel, reverse=False, chunk_decay=tb["chunk_decay"]),
        grid_spec=pltpu.PrefetchScalarGridSpec(
            num_scalar_prefetch=1, grid=(nb,),
            in_specs=[bq, bk, bv, bv, bv, _resident(tb["wq_f"].shape), _resident(tb["wk_f"].shape),
                      _resident(tb["din"].shape)],
            out_specs=bv, scratch_shapes=[state]),
        out_shape=out_shape, compiler_params=_cparams(1, 40), name="ret_scan_fwd",
    )(reset_f, q, k, v, o_b, g, tb["wq_f"], tb["wk_f"], tb["din"])


def _post0_kernel(y_ref, xa_ref, xb_ref, wo_ref, ln_ref, wgu_ref, wd_ref, out_ref, *, n_a):
    x = jnp.where(pl.program_id(0) < n_a, xa_ref[...], xb_ref[...])
    h1 = x + _dot(y_ref[...], wo_ref[...])
    xn = (_rms(h1) * ln_ref[...]).astype(BF16)
    acc = h1
    fc = FFN_DENSE_CHUNK
    for c in range(FFN_DENSE // fc):
        gate = _dot(xn, wgu_ref[:, c * fc:(c + 1) * fc])
        up = _dot(xn, wgu_ref[:, FFN_DENSE + c * fc:FFN_DENSE + (c + 1) * fc])
        act = (_silu(gate) * up).astype(BF16)
        acc = acc + _dot(act, wd_ref[c * fc:(c + 1) * fc, :])
    out_ref[...] = acc


def _post0(y, xa, xb, w_out, ln, w_gu, w_down):
    T = y.shape[0]
    tm = TOK_TILE
    n_a = xa.shape[0] // tm
    tok = lambda n: pl.BlockSpec((tm, n), lambda i: (i, 0))
    return pl.pallas_call(
        functools.partial(_post0_kernel, n_a=n_a),
        grid=(T // tm,),
        in_specs=[tok(2 * D_MODEL), *_two_stream_specs(tm, n_a), _resident(w_out.shape),
                  _resident((1, D_MODEL)), _resident(w_gu.shape), _resident(w_down.shape)],
        out_specs=tok(D_MODEL),
        out_shape=jax.ShapeDtypeStruct((T, D_MODEL), F32),
        compiler_params=_cparams(1, 52),
        name="ret_out_ffn",
    )(y, xa, xb, w_out, ln, w_gu, w_down)


def _ssm_in_kernel(x_ref, ln_ref, wz_ref, wx_ref, wd_ref, wdh_ref, bias_ref, gate_ref, xbc_ref, dt_ref):
    xh, xl = _split2(_rms(x_ref[...]) * ln_ref[...])
    for j in range(SSM_DINNER // SLAB):
        gate_ref[j] = _silu(_dot(xh, wz_ref[:, j * SLAB:(j + 1) * SLAB])).astype(BF16)
    for j in range(SSM_CONV_DIM // SLAB):
        xbc_ref[:, j * SLAB:(j + 1) * SLAB] = _dot(xh, wx_ref[:, j * SLAB:(j + 1) * SLAB]).astype(BF16)
    both = _dot(xh, wd_ref[...])
    x = both[:, :LANES] + both[:, LANES:] + _dot(xl, wdh_ref[...]) + bias_ref[...]
    dt_ref[...] = jnp.maximum(x, 0.0) + jnp.log1p(jnp.exp(-jnp.abs(x)))


def _ssm_in(x, ln, wz, wx, wdh, wdl, bias_row):
    T = x.shape[0]
    tm = PROJ_TILE
    n_gate = SSM_DINNER // SLAB
    wd = jnp.concatenate([wdh, wdl], axis=1)
    tok = lambda n: pl.BlockSpec((tm, n), lambda i: (i, 0))
    return pl.pallas_call(
        _ssm_in_kernel,
        grid=(T // tm,),
        in_specs=[tok(D_MODEL), _resident((1, D_MODEL)), _resident(wz.shape), _resident(wx.shape),
                  _resident(wd.shape), _resident(wdh.shape), _resident(bias_row.shape)],
        out_specs=[pl.BlockSpec((n_gate, tm, SLAB), lambda i: (0, i, 0)), tok(SSM_CONV_DIM), tok(LANES)],
        out_shape=[jax.ShapeDtypeStruct((n_gate, T, SLAB), BF16), jax.ShapeDtypeStruct((T, SSM_CONV_DIM), BF16),
                   jax.ShapeDtypeStruct((T, LANES), F32)],
        compiler_params=_cparams(1, 52),
        name="ssm_in",
    )(x, ln, wz, wx, wd, wdh, bias_row)


def _conv_kernel(first_ref, last_ref, prev_ref, main_ref, next_ref, shift_ref, w_ref, b_ref, out_ref, ext_ref):
    i = pl.program_id(0)
    rb = SSD_BLOCK
    ext_rows = ext_ref.shape[0]

    @pl.when(i == 0)
    def _():
        ext_ref[rb + 2 * HALO:ext_rows, :] = jnp.zeros((ext_rows - rb - 2 * HALO, SSM_CONV_DIM), BF16)

    zero = jnp.zeros((HALO, SSM_CONV_DIM), BF16)
    ext_ref[0:HALO, :] = jnp.where(first_ref[i] == 1, zero, prev_ref[...])
    ext_ref[HALO:HALO + rb, :] = main_ref[...]
    ext_ref[HALO + rb:2 * HALO + rb, :] = jnp.where(last_ref[i] == 1, zero, next_ref[...])
    for s in range(SSM_CONV_DIM // SLAB):
        cs = slice(s * SLAB, (s + 1) * SLAB)
        for r0 in range(0, rb, CONV_ROWS):
            taps = _dot(shift_ref[...], ext_ref[r0:r0 + CONV_WINDOW, cs])
            acc = jnp.broadcast_to(b_ref[:, cs], (CONV_ROWS, SLAB))
            for j in range(SSM_CONV):
                acc = acc + taps[j * CONV_ROWS:(j + 1) * CONV_ROWS] * w_ref[j:j + 1, cs]
            out_ref[s, r0:r0 + CONV_ROWS, :] = _silu(acc).astype(BF16)


def _conv(xbc, conv_w, conv_b, first, last):
    T = xbc.shape[0]
    rb = SSD_BLOCK
    nb = T // rb
    per = rb // HALO
    nh = T // HALO
    r = np.arange(CONV_ROWS)
    shift = np.zeros((SSM_CONV * CONV_ROWS, CONV_WINDOW), np.float32)
    for j in range(SSM_CONV):
        shift[j * CONV_ROWS + r, HALO + r + j - SSM_CONV // 2] = 1.0
    shift = jnp.asarray(shift, dtype=BF16)
    ext_rows = rb - CONV_ROWS + CONV_WINDOW
    return pl.pallas_call(
        _conv_kernel,
        grid_spec=pltpu.PrefetchScalarGridSpec(
            num_scalar_prefetch=2, grid=(nb,),
            in_specs=[
                pl.BlockSpec((HALO, SSM_CONV_DIM), lambda i, f, l: (jnp.maximum(i * per - 1, 0), 0)),
                pl.BlockSpec((rb, SSM_CONV_DIM), lambda i, f, l: (i, 0)),
                pl.BlockSpec((HALO, SSM_CONV_DIM), lambda i, f, l: (jnp.minimum((i + 1) * per, nh - 1), 0)),
                _resident(shift.shape), _resident(conv_w.shape), _resident(conv_b.shape)],
            out_specs=pl.BlockSpec((SSM_CONV_DIM // SLAB, rb, SLAB), lambda i, f, l: (0, i, 0)),
            scratch_shapes=[pltpu.VMEM((ext_rows, SSM_CONV_DIM), BF16)]),
        out_shape=jax.ShapeDtypeStruct((SSM_CONV_DIM // SLAB, T, SLAB), BF16),
        compiler_params=_cparams(1, 48),
        name="ssm_conv",
    )(first, last, xbc, xbc, xbc, shift, conv_w, conv_b)


def _ssd_tables():
    r = np.arange(CHUNK)
    tri_l = (r[None, :] <= r[:, None]).astype(np.float32)
    tri_u = (r[None, :] >= r[:, None]).astype(np.float32)
    c = np.arange(SSM_DINNER) // SSM_HEADDIM
    j = np.arange(LANES)
    e_f = (j[:, None] == c[None, :]).astype(np.float32)
    e_b = (j[:, None] == (c[None, :] + SSM_HEADS)).astype(np.float32)
    as_bf16 = lambda x: jnp.asarray(x, dtype=BF16)
    return dict(tri_l3=as_bf16(np.concatenate([tri_l] * 3, axis=1)),
                tri_u3=as_bf16(np.concatenate([tri_u] * 3, axis=1)),
                e2_f=as_bf16(np.concatenate([e_f, e_f], axis=0)),
                e2_b=as_bf16(np.concatenate([e_b, e_b], axis=0)))


def _ssd_chunk_scalars(dt, a_ref, tri_l3_ref, tri_u3_ref):
    a = dt * a_ref[...]
    a3 = jnp.concatenate(_split3(a), axis=0)
    cum_f = _dot(tri_l3_ref[...], a3)
    cum_b = _dot(tri_u3_ref[...], a3)
    lane = lax.broadcasted_iota(jnp.int32, (CHUNK, LANES), 1)
    cum = jnp.where(lane < SSM_HEADS, cum_f, cum_b)
    tot = cum_f[CHUNK - 1:CHUNK, :]
    return cum, tot


def _expand(e2_ref, *rows):
    x = jnp.concatenate(rows, axis=0)
    hi, lo = _split2(x)
    return _dot(jnp.concatenate([hi, lo], axis=1), e2_ref[...])


def _ssd_bwd_kernel(reset_ref, xa_ref, dt_ref, a_ref, tri_l3_ref, tri_u3_ref, e2_ref,
                    yb_ref, state_ref):
    nch = SSD_BLOCK // CHUNK

    @pl.when(reset_ref[pl.program_id(0)] == 1)
    def _():
        state_ref[...] = jnp.zeros_like(state_ref)

    def chunk_body(ci, carry):
        rows = pl.ds(pl.multiple_of((nch - 1 - ci) * CHUNK, CHUNK), CHUNK)
        dt = dt_ref[rows, :]
        cum, tot = _ssd_chunk_scalars(dt, a_ref, tri_l3_ref, tri_u3_ref)
        ex = _expand(e2_ref, dt * jnp.exp(tot - cum), jnp.exp(cum),
                     jnp.broadcast_to(jnp.exp(tot), (16, LANES)))
        for g in range(SSM_GROUPS):
            gc = slice(g * SSM_GROUP_W, (g + 1) * SSM_GROUP_W)
            bm = xa_ref[SSM_GROUPS, rows, g * SSM_DSTATE:(g + 1) * SSM_DSTATE]
            cm = xa_ref[SSM_GROUPS + 1, rows, g * SSM_DSTATE:(g + 1) * SSM_DSTATE]
            state = state_ref[g]
            yb_ref[rows, gc] = (_dot(cm, state.astype(BF16)) * ex[CHUNK:2 * CHUNK, gc]).astype(BF16)
            xw = (xa_ref[g, rows, :].astype(F32) * ex[0:CHUNK, gc]).astype(BF16)
            state_ref[g] = state * ex[2 * CHUNK:2 * CHUNK + 1, gc] + _dot_tn(bm, xw)
        return carry

    lax.fori_loop(0, nch, chunk_body, 0)


def _ssd_fwd_kernel(reset_ref, xa_ref, dt_ref, gate_ref, yb_ref, a_ref, tri_l3_ref, tri_u3_ref,
                    e2_ref, dskip_ref, nw_ref, out_ref, state_ref, y_ref):
    nch = SSD_BLOCK // CHUNK

    @pl.when(reset_ref[pl.program_id(0)] == 1)
    def _():
        state_ref[...] = jnp.zeros_like(state_ref)

    def chunk_body(ci, carry):
        rows = pl.ds(pl.multiple_of(ci * CHUNK, CHUNK), CHUNK)
        dt = dt_ref[rows, :]
        cum, tot = _ssd_chunk_scalars(dt, a_ref, tri_l3_ref, tri_u3_ref)
        ex = _expand(e2_ref, dt * jnp.exp(tot - cum), jnp.exp(cum),
                     jnp.broadcast_to(jnp.exp(tot), (16, LANES)))
        row_t = (cum - jnp.log(dt)).T
        li = lax.broadcasted_iota(jnp.int32, (CHUNK, CHUNK), 0)
        si = lax.broadcasted_iota(jnp.int32, (CHUNK, CHUNK), 1)
        lower = li >= si
        first_half = si < SSM_HEADDIM
        for g in range(SSM_GROUPS):
            gc = slice(g * SSM_GROUP_W, (g + 1) * SSM_GROUP_W)
            bm = xa_ref[SSM_GROUPS, rows, g * SSM_DSTATE:(g + 1) * SSM_DSTATE]
            cm = xa_ref[SSM_GROUPS + 1, rows, g * SSM_DSTATE:(g + 1) * SSM_DSTATE]
            cb = _dot_nt(cm, bm)
            heads_per_group = SSM_HEADS // SSM_GROUPS
            for pair in range(heads_per_group // 2):
                mats = []
                for hh in range(2):
                    h = g * heads_per_group + 2 * pair + hh
                    hb = SSM_HEADS + h
                    seg = jnp.where(lower, cum[:, h:h + 1] - row_t[h:h + 1, :],
                                    cum[:, hb:hb + 1] - row_t[hb:hb + 1, :])
                    mats.append((cb * jnp.exp(seg)).astype(BF16))
                pc = slice(2 * pair * SSM_HEADDIM, (2 * pair + 2) * SSM_HEADDIM)
                xp = xa_ref[g, rows, pc]
                zero = jnp.zeros_like(xp)
                rhs = jnp.concatenate([jnp.where(first_half, xp, zero), jnp.where(first_half, zero, xp)], axis=0)
                y_ref[:, pc] = _dot(jnp.concatenate(mats, axis=1), rhs)
            state = state_ref[g]
            xs = xa_ref[g, rows, :].astype(F32)
            y = (y_ref[...] + _dot(cm, state.astype(BF16)) * ex[CHUNK:2 * CHUNK, gc]
                 + yb_ref[rows, gc].astype(F32) + xs * dskip_ref[:, gc])
            y = y * gate_ref[g, rows, :].astype(F32)
            out_ref[rows, gc] = (_rms(y) * nw_ref[:, gc]).astype(BF16)
            xw = (xs * ex[0:CHUNK, gc]).astype(BF16)
            state_ref[g] = state * ex[2 * CHUNK:2 * CHUNK + 1, gc] + _dot_tn(bm, xw)
        return carry

    lax.fori_loop(0, nch, chunk_body, 0)


def _ssd(xa, dt, gate, a_row, dskip, norm_w, reset_f, reset_b):
    T = dt.shape[0]
    nb = T // SSD_BLOCK
    tb = _ssd_tables()
    state = pltpu.VMEM((SSM_GROUPS, SSM_DSTATE, SSM_GROUP_W), F32)
    consts = [a_row, tb["tri_l3"], tb["tri_u3"]]
    const_specs = [_resident(c.shape) for c in consts]
    out_shape = jax.ShapeDtypeStruct((T, SSM_DINNER), BF16)

    rev = lambda i, r: (nb - 1 - i, 0)
    blk = lambda n, imap: pl.BlockSpec((SSD_BLOCK, n), imap)
    slabs = lambda a, imap: pl.BlockSpec((a.shape[0], SSD_BLOCK, SLAB), lambda i, r: (0, imap(i, r)[0], 0))
    y_b = pl.pallas_call(
        _ssd_bwd_kernel,
        grid_spec=pltpu.PrefetchScalarGridSpec(
            num_scalar_prefetch=1, grid=(nb,),
            in_specs=[slabs(xa, rev), blk(LANES, rev)] + const_specs + [_resident(tb["e2_b"].shape)],
            out_specs=blk(SSM_DINNER, rev), scratch_shapes=[state]),
        out_shape=out_shape, compiler_params=_cparams(1, 40), name="ssd_bwd",
    )(reset_b, xa, dt, *consts, tb["e2_b"])

    fwd = lambda i, r: (i, 0)
    return pl.pallas_call(
        _ssd_fwd_kernel,
        grid_spec=pltpu.PrefetchScalarGridSpec(
            num_scalar_prefetch=1, grid=(nb,),
            in_specs=[slabs(xa, fwd), blk(LANES, fwd), slabs(gate, fwd), blk(SSM_DINNER, fwd)]
            + const_specs + [_resident(tb["e2_f"].shape), _resident(dskip.shape), _resident(norm_w.shape)],
            out_specs=blk(SSM_DINNER, fwd),
            scratch_shapes=[state, pltpu.VMEM((CHUNK, SSM_GROUP_W), F32)]),
        out_shape=out_shape, compiler_params=_cparams(1, 52), name="ssd_fwd",
    )(reset_f, xa, dt, gate, y_b, *consts, tb["e2_f"], dskip, norm_w)


def _pack_halves(x):
    n = x.shape[1] // 2
    hi = lax.bitcast_convert_type(x[:, :n].astype(F32), jnp.uint32)
    lo = lax.bitcast_convert_type(x[:, n:].astype(F32), jnp.uint32)
    return hi | (lo >> 16)


def _unpack_halves(w):
    hi = lax.bitcast_convert_type(w & jnp.uint32(0xFFFF0000), F32)
    lo = lax.bitcast_convert_type(w << 16, F32)
    return hi, lo


def _router_kernel(y_ref, h_ref, wo_ref, ln_ref, wh_ref, wl_ref, b_ref, tri_ref,
                   h3_ref, xn_ref, info_ref, info_t_ref, count_ref, base_ref):
    @pl.when(pl.program_id(0) == 0)
    def _():
        base_ref[...] = jnp.zeros_like(base_ref)

    h3 = h_ref[...] + _dot(y_ref[...], wo_ref[...])
    h3_ref[...] = h3
    xh, xl = _split2(_rms(h3) * ln_ref[...])
    xn_ref[...] = _pack_halves(xh)
    logits = _dot_nt(wh_ref[...], xh) + _dot_nt(wh_ref[...], xl) + _dot_nt(wl_ref[...], xh) + b_ref[...]
    row = lax.broadcasted_iota(jnp.int32, logits.shape, 0)
    m1 = jnp.max(logits, axis=0, keepdims=True)
    i1 = jnp.min(jnp.where(logits == m1, row, LANES), axis=0, keepdims=True)
    rest = jnp.where(row == i1, -jnp.inf, logits)
    m2 = jnp.max(rest, axis=0, keepdims=True)
    i2 = jnp.min(jnp.where(rest == m2, row, LANES), axis=0, keepdims=True)
    e = jnp.exp(m2 - m1)
    g1 = 1.0 / (1.0 + e)
    g2 = e / (1.0 + e)
    oh1 = row == i1
    oh2 = row == i2
    oh = jnp.where(oh1 | oh2, 1.0, 0.0)
    base = base_ref[...]
    prefix = _dot(oh.astype(BF16), tri_ref[...]) + base
    r1 = jnp.sum(jnp.where(oh1, prefix, 0.0), axis=0, keepdims=True)
    r2 = jnp.sum(jnp.where(oh2, prefix, 0.0), axis=0, keepdims=True)
    base = base + jnp.sum(oh, axis=1, keepdims=True)
    count_ref[...] = base[:, :LANES]
    base_ref[...] = base
    fields = [i1.astype(F32), i2.astype(F32), g1, g2, r1, r2]
    info_t = jnp.zeros(logits.shape, F32)
    for c, f in enumerate(fields):
        info_t = jnp.where(row == c, f, info_t)
    info_t_ref[...] = info_t[:ROUTE_COLS, :]
    info_ref[...] = info_t.T[:, :ROUTE_COLS]


def _router(y, h, w_out, ln, wh, wl, b):
    T = h.shape[0]
    tb = MOE_TOK
    nb = T // tb
    r = np.arange(tb)
    tri = jnp.asarray((r[:, None] < r[None, :]).astype(np.float32), dtype=BF16)
    tok = lambda n: pl.BlockSpec((tb, n), lambda i: (i, 0))
    return pl.pallas_call(
        _router_kernel,
        grid=(nb,),
        in_specs=[tok(SSM_DINNER), tok(D_MODEL), _resident(w_out.shape), _resident((1, D_MODEL)),
                  _resident(wh.shape), _resident(wl.shape), _resident(b.shape), _resident(tri.shape)],
        out_specs=[tok(D_MODEL), tok(D_MODEL // 2), tok(ROUTE_COLS),
                   pl.BlockSpec((ROUTE_COLS, tb), lambda i: (0, i)), pl.BlockSpec((LANES, LANES), lambda i: (0, 0))],
        out_shape=[jax.ShapeDtypeStruct((T, D_MODEL), F32), jax.ShapeDtypeStruct((T, D_MODEL // 2), jnp.uint32),
                   jax.ShapeDtypeStruct((T, ROUTE_COLS), F32), jax.ShapeDtypeStruct((ROUTE_COLS, T), F32),
                   jax.ShapeDtypeStruct((LANES, LANES), F32)],
        scratch_shapes=[pltpu.VMEM((LANES, tb), F32)],
        compiler_params=_cparams(1, 32),
        name="ssm_out_router",
    )(y, h, w_out, ln, wh, wl, b, tri)


def _sc_gather(table, idx):
    n_rows, width = idx.shape[0], table.shape[1]
    workers = SC_CORES * SC_SUBCORES
    nbuf = 2
    assert n_rows % (workers * SC_CHUNK * nbuf) == 0
    per_worker = n_rows // workers
    mesh = plsc.VectorSubcoreMesh(core_axis_name="c", subcore_axis_name="s")

    @functools.partial(
        pl.kernel, mesh=mesh,
        out_type=jax.ShapeDtypeStruct((n_rows, width), table.dtype),
        scratch_types=[pltpu.VMEM((nbuf, SC_CHUNK), jnp.int32),
                       pltpu.VMEM((nbuf, SC_CHUNK, width), table.dtype),
                       pltpu.SemaphoreType.DMA((nbuf,)),
                       pltpu.SemaphoreType.DMA((nbuf,))],
    )
    def gather_rows(table_hbm, idx_hbm, out_hbm, idx_v, rows_v, gsem, wsem):
        base = (lax.axis_index("s") * SC_CORES + lax.axis_index("c")) * per_worker

        def gather(c, b):
            off = pl.multiple_of(base + c * SC_CHUNK, 8)
            pltpu.sync_copy(idx_hbm.at[pl.ds(off, SC_CHUNK)], idx_v.at[b])
            return pltpu.make_async_copy(table_hbm.at[idx_v.at[b]], rows_v.at[b], gsem.at[b])

        def write(c, b):
            off = pl.multiple_of(base + c * SC_CHUNK, 8)
            return pltpu.make_async_copy(rows_v.at[b], out_hbm.at[pl.ds(off, SC_CHUNK)], wsem.at[b])

        @pl.loop(0, per_worker // SC_CHUNK, step=nbuf)
        def _(c0):
            copies = [gather(c0 + b, b) for b in range(nbuf)]
            for cp in copies:
                cp.start()
            writes = []
            for b, cp in enumerate(copies):
                cp.wait()
                writes.append(write(c0 + b, b))
                writes[-1].start()
            for wr in writes:
                wr.wait()

    return gather_rows(table, idx)


def _sc_scatter_pairs(table, pos, n_out):
    n_rows, width = table.shape
    workers = SC_CORES * SC_SUBCORES
    nbuf = 2
    assert n_rows % (workers * SC_CHUNK * nbuf) == 0
    per_worker = n_rows // workers
    mesh = plsc.VectorSubcoreMesh(core_axis_name="c", subcore_axis_name="s")

    @functools.partial(
        pl.kernel, mesh=mesh,
        out_type=jax.ShapeDtypeStruct((n_out, width), table.dtype),
        scratch_types=[pltpu.VMEM((nbuf, 2, SC_CHUNK), jnp.int32),
                       pltpu.VMEM((nbuf, SC_CHUNK, width), table.dtype),
                       pltpu.SemaphoreType.DMA((nbuf,)),
                       pltpu.SemaphoreType.DMA((nbuf,))],
    )
    def scatter_rows(table_hbm, pos_hbm, out_hbm, idx_v, rows_v, rsem, wsem):
        base = (lax.axis_index("s") * SC_CORES + lax.axis_index("c")) * per_worker

        def read(c, b):
            off = pl.multiple_of(base + c * SC_CHUNK, 8)
            return pltpu.make_async_copy(table_hbm.at[pl.ds(off, SC_CHUNK)], rows_v.at[b], rsem.at[b])

        def write(b, j):
            return pltpu.make_async_copy(rows_v.at[b], out_hbm.at[idx_v.at[b, j]], wsem.at[b])

        @pl.loop(0, per_worker // SC_CHUNK, step=nbuf)
        def _(c0):
            reads = [read(c0 + b, b) for b in range(nbuf)]
            for rd in reads:
                rd.start()
            for b in range(nbuf):
                pltpu.sync_copy(pos_hbm.at[base // SC_CHUNK + c0 + b], idx_v.at[b])
            writes = []
            for b, rd in enumerate(reads):
                rd.wait()
                for j in range(2):
                    writes.append(write(b, j))
                    writes[-1].start()
            for wr in writes:
                wr.wait()

    return scatter_rows(table, pos)


def _expert_kernel(texp_ref, trows_ref, x_ref, wg_ref, wu_ref, wd_ref, out_ref, acc_ref, xb_ref):
    i = pl.program_id(0)
    f = pl.program_id(1)
    nf = pl.num_programs(1)
    n = D_MODEL // 2

    @pl.when(trows_ref[i] > 0)
    def _():
        @pl.when(f == 0)
        def _():
            acc_ref[...] = jnp.zeros_like(acc_ref)
            row = lax.broadcasted_iota(jnp.int32, x_ref.shape, 0)
            hi, lo = _unpack_halves(jnp.where(row < trows_ref[i], x_ref[...], jnp.uint32(0)))
            xb_ref[:, :n] = hi.astype(BF16)
            xb_ref[:, n:] = lo.astype(BF16)

        def ffn_chunk(m):
            x = xb_ref[0:m, :]
            down = None
            for c in range(0, MOE_FC, MOE_FC // 2):
                cs = slice(c, c + MOE_FC // 2)
                act = (_silu(_dot(x, wg_ref[0, :, cs].astype(BF16)))
                       * _dot(x, wu_ref[0, :, cs].astype(BF16))).astype(BF16)
                part = _dot(act, wd_ref[0, cs, :].astype(BF16))
                down = part if down is None else down + part
            acc_ref[0:m, :] += down

        for m in range(MOE_ROWS_STEP, MOE_SLOT + 1, MOE_ROWS_STEP):
            @pl.when((trows_ref[i] > m - MOE_ROWS_STEP) & (trows_ref[i] <= m))
            def _(m=m):
                ffn_chunk(m)

        @pl.when(f == nf - 1)
        def _():
            out_ref[...] = _pack_halves(acc_ref[...].astype(BF16))

    @pl.when((trows_ref[i] == 0) & (f == nf - 1))
    def _():
        out_ref[...] = jnp.zeros_like(out_ref)


def _experts(xs, w_gu, w_down, tile_expert, tile_rows):
    n_tiles = xs.shape[0] // MOE_SLOT
    nf = FFN_EXPERT // MOE_FC
    return pl.pallas_call(
        _expert_kernel,
        grid_spec=pltpu.PrefetchScalarGridSpec(
            num_scalar_prefetch=2, grid=(n_tiles, nf),
            in_specs=[pl.BlockSpec((MOE_SLOT, D_MODEL // 2), lambda i, f, te, tv: (i, 0)),
                      pl.BlockSpec((1, D_MODEL, MOE_FC), lambda i, f, te, tv: (te[i], 0, f)),
                      pl.BlockSpec((1, D_MODEL, MOE_FC), lambda i, f, te, tv: (te[i], 0, nf + f)),
                      pl.BlockSpec((1, MOE_FC, D_MODEL), lambda i, f, te, tv: (te[i], f, 0))],
            out_specs=pl.BlockSpec((MOE_SLOT, D_MODEL // 2), lambda i, f, te, tv: (i, 0)),
            scratch_shapes=[pltpu.VMEM((MOE_SLOT, D_MODEL), F32), pltpu.VMEM((MOE_SLOT, D_MODEL), BF16)]),
        out_shape=jax.ShapeDtypeStruct(xs.shape, jnp.uint32),
        compiler_params=_cparams(2, 48),
        name="moe_experts",
    )(tile_expert, tile_rows, xs, w_gu, w_gu, w_down)


def _finalize_kernel(y0_ref, y1_ref, info_ref, h_ref, fn_ref, out_ref):
    n = D_MODEL // 2
    info = info_ref[...]
    g0 = info[:, 2:3]
    g1 = info[:, 3:4]
    a_hi, a_lo = _unpack_halves(y0_ref[...])
    b_hi, b_lo = _unpack_halves(y1_ref[...])
    left = h_ref[:, :n] + g0 * a_hi + g1 * b_hi
    right = h_ref[:, n:] + g0 * a_lo + g1 * b_lo
    ms = (jnp.sum(left * left, axis=-1, keepdims=True)
          + jnp.sum(right * right, axis=-1, keepdims=True)) * (1.0 / D_MODEL)
    scale = lax.rsqrt(ms + EPS)
    out_ref[:, :n] = left * scale * fn_ref[:, :n]
    out_ref[:, n:] = right * scale * fn_ref[:, n:]


def _finalize(yg, info, h3, final_norm, tok0):
    n = yg.shape[0] // 2
    tb = MOE_TOK
    nb = n // tb
    b0 = tok0 // tb
    tok = lambda w: pl.BlockSpec((tb, w), lambda i: (b0 + i, 0))
    loc = lambda off: pl.BlockSpec((tb, D_MODEL // 2), lambda i: (i + off, 0))
    return pl.pallas_call(
        _finalize_kernel,
        grid=(nb,),
        in_specs=[loc(0), loc(nb), tok(info.shape[1]), tok(D_MODEL), _resident((1, D_MODEL))],
        out_specs=pl.BlockSpec((tb, D_MODEL), lambda i: (i, 0)),
        out_shape=jax.ShapeDtypeStruct((n, D_MODEL), F32),
        compiler_params=_cparams(1, 32),
        name="moe_finalize",
    )(yg, yg, info, h3, final_norm)


def _moe(y_ssm, h2, w_out, ln, wr_hi, wr_lo, rb, w_gu, w_down, final_norm, n_a):
    T = h2.shape[0]
    n_tiles = 2 * T // MOE_SLOT + N_EXPERTS
    h3, xn, info, info_t, counts = _router(y_ssm, h2, w_out, ln, wr_hi, wr_lo, rb)

    counts = counts[:N_EXPERTS, 0].astype(jnp.int32)
    tiles_e = (counts + MOE_SLOT - 1) // MOE_SLOT
    tiles_cum = jnp.cumsum(tiles_e)
    gstart = (tiles_cum - tiles_e) * MOE_SLOT
    routed = info_t.astype(jnp.int32)
    start_of = lambda e: sum(jnp.where(e == x, gstart[x], 0) for x in range(N_EXPERTS))
    pos0 = start_of(routed[0]) + routed[4]
    pos1 = start_of(routed[1]) + routed[5]
    tile_ids = jnp.arange(n_tiles, dtype=jnp.int32)
    tile_expert = jnp.minimum(jnp.sum(tile_ids[:, None] >= tiles_cum[None, :], axis=1), N_EXPERTS - 1)
    tile_expert = tile_expert.astype(jnp.int32)
    tile_rows = jnp.clip(counts[tile_expert] - (tile_ids * MOE_SLOT - gstart[tile_expert]), 0, MOE_SLOT)
    tile_rows = jnp.where(tile_ids < tiles_cum[-1], tile_rows, 0).astype(jnp.int32)
    pos = jnp.stack([pos0.reshape(-1, SC_CHUNK), pos1.reshape(-1, SC_CHUNK)], axis=1)

    xs = _sc_scatter_pairs(xn, pos, n_tiles * MOE_SLOT)
    ys = _experts(xs, w_gu, w_down, tile_expert, tile_rows)
    outs = []
    for lo, hi in ((0, n_a), (n_a, T)):
        yg = _sc_gather(ys, jnp.concatenate([pos0[lo:hi], pos1[lo:hi]]))
        outs.append(_finalize(yg, info, h3, final_norm, lo))
    return outs


def _seq_flags(seq_lens, block):
    first, last = [], []
    for n in seq_lens:
        nb = n // block
        first += [1] + [0] * (nb - 1)
        last += [0] * (nb - 1) + [1]
    return np.asarray(first, np.int32), np.asarray(last, np.int32)


def _trunk(xa, xb, seq_lens, ln_mix0, ret_w_in, ret_w_out, ln_ffn0, ffn_w_gu, ffn_w_down, ln_mix1, ssm_w_in,
           ssm_conv_w, ssm_conv_b, ssm_dt_bias, ssm_A_log, ssm_D, ssm_norm_w, ssm_w_out, ln_ffn1,
           moe_router_w, moe_router_b, moe_w_gu, moe_w_down, final_norm):
    assert all(n % b == 0 for n in seq_lens for b in (SCAN_BLOCK, SSD_BLOCK, MOE_TOK, TOK_TILE, PROJ_TILE))
    row = lambda v: v.astype(F32).reshape(1, -1)
    first, last = _seq_flags(seq_lens, SCAN_BLOCK)
    reset_f = jnp.asarray(first)
    reset_b = jnp.asarray(last[::-1].copy())
    ssd_first, ssd_last = _seq_flags(seq_lens, SSD_BLOCK)

    half = RET_DK // 2
    inv = ROPE_BASE ** (-jnp.arange(half, dtype=F32) / half)
    ang_base = jnp.arange(0, max(seq_lens), TOK_TILE).astype(F32)[:, None, None] * inv[None, None, :]
    ang_off = jnp.arange(TOK_TILE).astype(F32)[:, None] * inv[None, :]
    pos_block = jnp.asarray(np.concatenate([np.arange(n // TOK_TILE) for n in seq_lens]).astype(np.int32))
    q, k, v, g = _ret_in(xa, xb, row(ln_mix0), ret_w_in.astype(BF16), jnp.cos(ang_base), jnp.sin(ang_base),
                         jnp.cos(ang_off), jnp.sin(ang_off), pos_block)
    y = _ret_scan(q, k, v, g, reset_f, reset_b)
    h2 = _post0(y, xa, xb, ret_w_out.astype(BF16), row(ln_ffn0), ffn_w_gu.astype(BF16),
                ffn_w_down.astype(BF16))

    wz = ssm_w_in[:, :SSM_DINNER].astype(BF16)
    wx = ssm_w_in[:, SSM_DINNER:SSM_DINNER + SSM_CONV_DIM].astype(BF16)
    wdt = jnp.pad(ssm_w_in[:, SSM_DINNER + SSM_CONV_DIM:], ((0, 0), (0, LANES - 2 * SSM_HEADS)))
    wdh, wdl = _split2(wdt)
    pad_row = lambda v: jnp.pad(v.astype(F32).reshape(1, -1), ((0, 0), (0, LANES - 2 * SSM_HEADS)))
    gate, xbc, dt = _ssm_in(h2, row(ln_mix1), wz, wx, wdh, wdl, pad_row(ssm_dt_bias))
    xact = _conv(xbc, ssm_conv_w.reshape(SSM_CONV, SSM_CONV_DIM).astype(F32), row(ssm_conv_b),
                 jnp.asarray(ssd_first), jnp.asarray(ssd_last))
    a_row = pad_row(-jnp.exp(ssm_A_log.astype(F32)))
    dskip = jnp.repeat(ssm_D.astype(F32), SSM_HEADDIM).reshape(1, SSM_DINNER)
    y = _ssd(xact, dt, gate, a_row, dskip, row(ssm_norm_w), jnp.asarray(ssd_first),
             jnp.asarray(ssd_last[::-1].copy()))

    wr = jnp.pad(moe_router_w.astype(F32).T, ((0, LANES - N_EXPERTS), (0, 0)))
    wr_hi, wr_lo = _split2(wr)
    rb = jnp.pad(moe_router_b.astype(F32).reshape(-1, 1), ((0, LANES - N_EXPERTS), (0, 0)), constant_values=-1e30)
    rb = jnp.broadcast_to(rb, (LANES, MOE_TOK))
    return _moe(y, h2, ssm_w_out.astype(BF16), row(ln_ffn1), wr_hi, wr_lo, rb,
                moe_w_gu, moe_w_down, row(final_norm), xa.shape[0])


def kernel(x_prompt, x_sample, ln_mix0, ret_w_in, ret_w_out, ln_ffn0, ffn_w_gu, ffn_w_down, ln_mix1, ssm_w_in,
           ssm_conv_w, ssm_conv_b, ssm_dt_bias, ssm_A_log, ssm_D, ssm_norm_w, ssm_w_out, ln_ffn1, moe_router_w,
           moe_router_b, moe_w_gu, moe_w_down, final_norm):
    seq_lens = [x_prompt.shape[1]] * x_prompt.shape[0] + [x_sample.shape[1]] * x_sample.shape[0]
    out_a, out_b = _trunk(x_prompt.reshape(-1, D_MODEL), x_sample.reshape(-1, D_MODEL), seq_lens, ln_mix0, ret_w_in, ret_w_out, ln_ffn0, ffn_w_gu, ffn_w_down, ln_mix1, ssm_w_in,
                 ssm_conv_w, ssm_conv_b, ssm_dt_bias, ssm_A_log, ssm_D, ssm_norm_w, ssm_w_out, ln_ffn1,
                 moe_router_w, moe_router_b, moe_w_gu, moe_w_down, final_norm)
    return (out_a.reshape(x_prompt.shape), out_b.reshape(x_sample.shape))
```

```python
import functools

import numpy as np
import jax
import jax.numpy as jnp
from jax import lax
from jax.experimental import pallas as pl
from jax.experimental.pallas import tpu as pltpu
from jax.experimental.pallas import tpu_sc as plsc

F32 = jnp.float32
BF16 = jnp.bfloat16

D_MODEL = 1024
EPS = 1e-6
RET_HEADS = 4
RET_DK = 256
RET_DV = 512
ROPE_BASE = 10000.0
SSM_DINNER = 2048
SSM_HEADDIM = 64
SSM_HEADS = 32
SSM_GROUPS = 4
SSM_DSTATE = 128
SSM_CONV = 5
SSM_CONV_DIM = 3072
SSM_GROUP_W = SSM_DINNER // SSM_GROUPS
SLAB = SSM_GROUP_W
CONV_ROWS = 128
CONV_WINDOW = 256
HALO = 16
FFN_DENSE = 2816
FFN_DENSE_CHUNK = 1408
N_EXPERTS = 8
FFN_EXPERT = 3584

CHUNK = 128
RET_CHUNK = 256
LANES = 128
TOK_TILE = 512
PROJ_TILE = 1024
SCAN_BLOCK = 512
SSD_BLOCK = 1024
MOE_TOK = 512
MOE_SLOT = 1024
MOE_FC = 512
MOE_ROWS_STEP = 256
ROUTE_COLS = 8
SC_CORES = 2
SC_SUBCORES = 16
SC_CHUNK = 64


def _cparams(n_axes, vmem_mb):
    return pltpu.CompilerParams(dimension_semantics=("arbitrary",) * n_axes,
                                vmem_limit_bytes=vmem_mb << 20)


def _resident(shape):
    nd = len(shape)
    return pl.BlockSpec(shape, lambda *_: (0,) * nd, pipeline_mode=pl.Buffered(1))


def _rms(x):
    return x * lax.rsqrt(jnp.mean(x * x, axis=-1, keepdims=True) + EPS)


def _silu(x):
    return x * jax.nn.sigmoid(x)


def _dot(a, b):
    return jnp.dot(a, b, preferred_element_type=F32)


def _dot_nt(a, b):
    return lax.dot_general(a, b, (((1,), (1,)), ((), ())), preferred_element_type=F32)


def _dot_tn(a, b):
    return lax.dot_general(a, b, (((0,), (0,)), ((), ())), preferred_element_type=F32)


def _split2(x):
    hi = x.astype(BF16)
    lo = (x - hi.astype(F32)).astype(BF16)
    return hi, lo


def _split3(x):
    hi = x.astype(BF16)
    r = x - hi.astype(F32)
    mid = r.astype(BF16)
    lo = (r - mid.astype(F32)).astype(BF16)
    return hi, mid, lo


def _two_stream_specs(tm, n_a):
    return (pl.BlockSpec((tm, D_MODEL), lambda i, *_: (jnp.minimum(i, n_a - 1), 0)),
            pl.BlockSpec((tm, D_MODEL), lambda i, *_: (jnp.maximum(i - n_a, 0), 0)))


def _ret_in_kernel(pos_ref, xa_ref, xb_ref, ln_ref, w_ref, cb_ref, sb_ref, co_ref, so_ref,
                   q_ref, k_ref, v_ref, g_ref, *, n_a):
    x = jnp.where(pl.program_id(0) < n_a, xa_ref[...], xb_ref[...])
    xn = (_rms(x) * ln_ref[...]).astype(BF16)
    cb, sb, co, so = cb_ref[0], sb_ref[0], co_ref[...], so_ref[...]
    cos = cb * co - sb * so
    sin = sb * co + cb * so
    half = RET_DK // 2

    def rotary(dst, col0, scale):
        for h in range(RET_HEADS):
            c = col0 + h * RET_DK
            p = _dot(xn, w_ref[:, c:c + RET_DK])
            p1, p2 = p[:, :half], p[:, half:]
            dst[:, h * RET_DK:h * RET_DK + half] = ((p1 * cos - p2 * sin) * scale).astype(BF16)
            dst[:, h * RET_DK + half:(h + 1) * RET_DK] = ((p1 * sin + p2 * cos) * scale).astype(BF16)

    rotary(q_ref, 0, 1.0)
    rotary(k_ref, D_MODEL, RET_DK ** -0.5)
    for j in range(2 * D_MODEL // 512):
        v_ref[:, j * 512:(j + 1) * 512] = _dot(
            xn, w_ref[:, 2 * D_MODEL + j * 512:2 * D_MODEL + (j + 1) * 512]).astype(BF16)
        g_ref[:, j * 512:(j + 1) * 512] = _silu(_dot(
            xn, w_ref[:, 4 * D_MODEL + j * 512:4 * D_MODEL + (j + 1) * 512])).astype(BF16)


def _ret_in(xa, xb, ln, w_in, cos_base, sin_base, cos_off, sin_off, pos_block):
    T = xa.shape[0] + xb.shape[0]
    tm = TOK_TILE
    n_a = xa.shape[0] // tm
    tok = lambda n: pl.BlockSpec((tm, n), lambda i, pb: (i, 0))
    rot = pl.BlockSpec((1, 1, LANES), lambda i, pb: (pb[i], 0, 0))
    return pl.pallas_call(
        functools.partial(_ret_in_kernel, n_a=n_a),
        grid_spec=pltpu.PrefetchScalarGridSpec(
            num_scalar_prefetch=1, grid=(T // tm,),
            in_specs=[*_two_stream_specs(tm, n_a), _resident((1, D_MODEL)), _resident(w_in.shape), rot, rot,
                      _resident(cos_off.shape), _resident(sin_off.shape)],
            out_specs=[tok(D_MODEL), tok(D_MODEL), tok(2 * D_MODEL), tok(2 * D_MODEL)]),
        out_shape=[jax.ShapeDtypeStruct((T, D_MODEL), BF16), jax.ShapeDtypeStruct((T, D_MODEL), BF16),
                   jax.ShapeDtypeStruct((T, 2 * D_MODEL), BF16), jax.ShapeDtypeStruct((T, 2 * D_MODEL), BF16)],
        compiler_params=_cparams(1, 48),
        name="ret_in",
    )(pos_block, xa, xb, ln, w_in, cos_base, sin_base, cos_off, sin_off)


def _ret_scan_kernel(reset_ref, *refs, reverse, chunk_decay):
    if reverse:
        q_ref, k_ref, v_ref, wq_ref, wk_ref, out_ref, state_ref = refs
    else:
        q_ref, k_ref, v_ref, ob_ref, g_ref, wq_ref, wk_ref, din_ref, out_ref, state_ref = refs
    nch = SCAN_BLOCK // RET_CHUNK

    @pl.when(reset_ref[pl.program_id(0)] == 1)
    def _():
        state_ref[...] = jnp.zeros_like(state_ref)

    def chunk_body(ci, carry):
        cj = nch - 1 - ci if reverse else ci
        rows = pl.ds(pl.multiple_of(cj * RET_CHUNK, RET_CHUNK), RET_CHUNK)
        for h in range(RET_HEADS):
            kcols = slice(h * RET_DK, (h + 1) * RET_DK)
            vcols = slice(h * RET_DV, (h + 1) * RET_DV)
            qh = q_ref[rows, kcols]
            kh = k_ref[rows, kcols]
            vh = v_ref[rows, vcols]
            state = state_ref[h]
            o = _dot(qh, state.astype(BF16)) * wq_ref[h]
            if reverse:
                out_ref[rows, vcols] = o.astype(BF16)
            else:
                s = _dot_nt(qh, kh) * din_ref[h]
                o = o + _dot(s.astype(BF16), vh) + ob_ref[rows, vcols].astype(F32)
                out_ref[rows, vcols] = (g_ref[rows, vcols].astype(F32) * _rms(o)).astype(BF16)
            ks = (kh.astype(F32) * wk_ref[h]).astype(BF16)
            state_ref[h] = state * chunk_decay[h] + _dot_tn(ks, vh)
        return carry

    lax.fori_loop(0, nch, chunk_body, 0)


def _ret_tables():
    lg = np.log1p(-np.power(2.0, -5.0 - np.arange(RET_HEADS, dtype=np.float32))).astype(np.float32)
    c = RET_CHUNK
    a = np.arange(c, dtype=np.float32)
    col = lambda e, w: np.broadcast_to(np.exp(lg[:, None, None] * e[None, :, None]),
                                       (RET_HEADS, c, w)).astype(np.float32)
    dist = np.abs(a[:, None] - a[None, :])
    return dict(
        din=np.exp(lg[:, None, None] * dist[None]).astype(np.float32),
        wq_f=col(a + 1.0, RET_DV), wk_f=col(c - 1.0 - a, RET_DK),
        wq_b=col(c - a, RET_DV), wk_b=col(a, RET_DK),
        chunk_decay=tuple(float(v) for v in np.exp(lg * c)),
    )


def _ret_scan(q, k, v, g, reset_f, reset_b):
    T = q.shape[0]
    nb = T // SCAN_BLOCK
    tb = _ret_tables()
    state = pltpu.VMEM((RET_HEADS, RET_DK, RET_DV), F32)
    out_shape = jax.ShapeDtypeStruct((T, 2 * D_MODEL), BF16)

    def specs(imap):
        blk = lambda n: pl.BlockSpec((SCAN_BLOCK, n), imap)
        return blk(D_MODEL), blk(D_MODEL), blk(2 * D_MODEL)

    rev = lambda i, r: (nb - 1 - i, 0)
    bq, bk, bv = specs(rev)
    o_b = pl.pallas_call(
        functools.partial(_ret_scan_kernel, reverse=True, chunk_decay=tb["chunk_decay"]),
        grid_spec=pltpu.PrefetchScalarGridSpec(
            num_scalar_prefetch=1, grid=(nb,),
            in_specs=[bq, bk, bv, _resident(tb["wq_b"].shape), _resident(tb["wk_b"].shape)],
            out_specs=bv, scratch_shapes=[state]),
        out_shape=out_shape, compiler_params=_cparams(1, 40), name="ret_scan_bwd",
    )(reset_b, q, k, v, tb["wq_b"], tb["wk_b"])

    fwd = lambda i, r: (i, 0)
    bq, bk, bv = specs(fwd)
    return pl.pallas_call(
        functools.partial(_ret_scan_kernel, reverse=False, chunk_decay=tb["chunk_decay"]),
        grid_spec=pltpu.PrefetchScalarGridSpec(
            num_scalar_prefetch=1, grid=(nb,),
            in_specs=[bq, bk, bv, bv, bv, _resident(tb["wq_f"].shape), _resident(tb["wk_f"].shape),
                      _resident(tb["din"].shape)],
            out_specs=bv, scratch_shapes=[state]),
        out_shape=out_shape, compiler_params=_cparams(1, 40), name="ret_scan_fwd",
    )(reset_f, q, k, v, o_b, g, tb["wq_f"], tb["wk_f"], tb["din"])


def _post0_kernel(y_ref, xa_ref, xb_ref, wo_ref, ln_ref, wgu_ref, wd_ref, out_ref, *, n_a):
    x = jnp.where(pl.program_id(0) < n_a, xa_ref[...], xb_ref[...])
    h1 = x + _dot(y_ref[...], wo_ref[...])
    xn = (_rms(h1) * ln_ref[...]).astype(BF16)
    acc = h1
    fc = FFN_DENSE_CHUNK
    for c in range(FFN_DENSE // fc):
        gate = _dot(xn, wgu_ref[:, c * fc:(c + 1) * fc])
        up = _dot(xn, wgu_ref[:, FFN_DENSE + c * fc:FFN_DENSE + (c + 1) * fc])
        act = (_silu(gate) * up).astype(BF16)
        acc = acc + _dot(act, wd_ref[c * fc:(c + 1) * fc, :])
    out_ref[...] = acc


def _post0(y, xa, xb, w_out, ln, w_gu, w_down):
    T = y.shape[0]
    tm = TOK_TILE
    n_a = xa.shape[0] // tm
    tok = lambda n: pl.BlockSpec((tm, n), lambda i: (i, 0))
    return pl.pallas_call(
        functools.partial(_post0_kernel, n_a=n_a),
        grid=(T // tm,),
        in_specs=[tok(2 * D_MODEL), *_two_stream_specs(tm, n_a), _resident(w_out.shape),
                  _resident((1, D_MODEL)), _resident(w_gu.shape), _resident(w_down.shape)],
        out_specs=tok(D_MODEL),
        out_shape=jax.ShapeDtypeStruct((T, D_MODEL), F32),
        compiler_params=_cparams(1, 52),
        name="ret_out_ffn",
    )(y, xa, xb, w_out, ln, w_gu, w_down)


def _ssm_in_kernel(x_ref, ln_ref, wz_ref, wx_ref, wd_ref, wdh_ref, bias_ref, gate_ref, xbc_ref, dt_ref):
    xh, xl = _split2(_rms(x_ref[...]) * ln_ref[...])
    for j in range(SSM_DINNER // SLAB):
        gate_ref[j] = _silu(_dot(xh, wz_ref[:, j * SLAB:(j + 1) * SLAB])).astype(BF16)
    for j in range(SSM_CONV_DIM // SLAB):
        xbc_ref[:, j * SLAB:(j + 1) * SLAB] = _dot(xh, wx_ref[:, j * SLAB:(j + 1) * SLAB]).astype(BF16)
    both = _dot(xh, wd_ref[...])
    x = both[:, :LANES] + both[:, LANES:] + _dot(xl, wdh_ref[...]) + bias_ref[...]
    dt_ref[...] = jnp.maximum(x, 0.0) + jnp.log1p(jnp.exp(-jnp.abs(x)))


def _ssm_in(x, ln, wz, wx, wdh, wdl, bias_row):
    T = x.shape[0]
    tm = PROJ_TILE
    n_gate = SSM_DINNER // SLAB
    wd = jnp.concatenate([wdh, wdl], axis=1)
    tok = lambda n: pl.BlockSpec((tm, n), lambda i: (i, 0))
    return pl.pallas_call(
        _ssm_in_kernel,
        grid=(T // tm,),
        in_specs=[tok(D_MODEL), _resident((1, D_MODEL)), _resident(wz.shape), _resident(wx.shape),
                  _resident(wd.shape), _resident(wdh.shape), _resident(bias_row.shape)],
        out_specs=[pl.BlockSpec((n_gate, tm, SLAB), lambda i: (0, i, 0)), tok(SSM_CONV_DIM), tok(LANES)],
        out_shape=[jax.ShapeDtypeStruct((n_gate, T, SLAB), BF16), jax.ShapeDtypeStruct((T, SSM_CONV_DIM), BF16),
                   jax.ShapeDtypeStruct((T, LANES), F32)],
        compiler_params=_cparams(1, 52),
        name="ssm_in",
    )(x, ln, wz, wx, wd, wdh, bias_row)


def _conv_kernel(first_ref, last_ref, prev_ref, main_ref, next_ref, shift_ref, w_ref, b_ref, out_ref, ext_ref):
    i = pl.program_id(0)
    rb = SSD_BLOCK
    ext_rows = ext_ref.shape[0]

    @pl.when(i == 0)
    def _():
        ext_ref[rb + 2 * HALO:ext_rows, :] = jnp.zeros((ext_rows - rb - 2 * HALO, SSM_CONV_DIM), BF16)

    zero = jnp.zeros((HALO, SSM_CONV_DIM), BF16)
    ext_ref[0:HALO, :] = jnp.where(first_ref[i] == 1, zero, prev_ref[...])
    ext_ref[HALO:HALO + rb, :] = main_ref[...]
    ext_ref[HALO + rb:2 * HALO + rb, :] = jnp.where(last_ref[i] == 1, zero, next_ref[...])
    for s in range(SSM_CONV_DIM // SLAB):
        cs = slice(s * SLAB, (s + 1) * SLAB)
        for r0 in range(0, rb, CONV_ROWS):
            taps = _dot(shift_ref[...], ext_ref[r0:r0 + CONV_WINDOW, cs])
            acc = jnp.broadcast_to(b_ref[:, cs], (CONV_ROWS, SLAB))
            for j in range(SSM_CONV):
                acc = acc + taps[j * CONV_ROWS:(j + 1) * CONV_ROWS] * w_ref[j:j + 1, cs]
            out_ref[s, r0:r0 + CONV_ROWS, :] = _silu(acc).astype(BF16)


def _conv(xbc, conv_w, conv_b, first, last):
    T = xbc.shape[0]
    rb = SSD_BLOCK
    nb = T // rb
    per = rb // HALO
    nh = T // HALO
    r = np.arange(CONV_ROWS)
    shift = np.zeros((SSM_CONV * CONV_ROWS, CONV_WINDOW), np.float32)
    for j in range(SSM_CONV):
        shift[j * CONV_ROWS + r, HALO + r + j - SSM_CONV // 2] = 1.0
    shift = jnp.asarray(shift, dtype=BF16)
    ext_rows = rb - CONV_ROWS + CONV_WINDOW
    return pl.pallas_call(
        _conv_kernel,
        grid_spec=pltpu.PrefetchScalarGridSpec(
            num_scalar_prefetch=2, grid=(nb,),
            in_specs=[
                pl.BlockSpec((HALO, SSM_CONV_DIM), lambda i, f, l: (jnp.maximum(i * per - 1, 0), 0)),
                pl.BlockSpec((rb, SSM_CONV_DIM), lambda i, f, l: (i, 0)),
                pl.BlockSpec((HALO, SSM_CONV_DIM), lambda i, f, l: (jnp.minimum((i + 1) * per, nh - 1), 0)),
                _resident(shift.shape), _resident(conv_w.shape), _resident(conv_b.shape)],
            out_specs=pl.BlockSpec((SSM_CONV_DIM // SLAB, rb, SLAB), lambda i, f, l: (0, i, 0)),
            scratch_shapes=[pltpu.VMEM((ext_rows, SSM_CONV_DIM), BF16)]),
        out_shape=jax.ShapeDtypeStruct((SSM_CONV_DIM // SLAB, T, SLAB), BF16),
        compiler_params=_cparams(1, 48),
        name="ssm_conv",
    )(first, last, xbc, xbc, xbc, shift, conv_w, conv_b)


def _ssd_tables():
    r = np.arange(CHUNK)
    tri_l = (r[None, :] <= r[:, None]).astype(np.float32)
    tri_u = (r[None, :] >= r[:, None]).astype(np.float32)
    c = np.arange(SSM_DINNER) // SSM_HEADDIM
    j = np.arange(LANES)
    e_f = (j[:, None] == c[None, :]).astype(np.float32)
    e_b = (j[:, None] == (c[None, :] + SSM_HEADS)).astype(np.float32)
    as_bf16 = lambda x: jnp.asarray(x, dtype=BF16)
    return dict(tri_l3=as_bf16(np.concatenate([tri_l] * 3, axis=1)),
                tri_u3=as_bf16(np.concatenate([tri_u] * 3, axis=1)),
                e2_f=as_bf16(np.concatenate([e_f, e_f], axis=0)),
                e2_b=as_bf16(np.concatenate([e_b, e_b], axis=0)))


def _ssd_chunk_scalars(dt, a_ref, tri_l3_ref, tri_u3_ref):
    a = dt * a_ref[...]
    a3 = jnp.concatenate(_split3(a), axis=0)
    cum_f = _dot(tri_l3_ref[...], a3)
    cum_b = _dot(tri_u3_ref[...], a3)
    lane = lax.broadcasted_iota(jnp.int32, (CHUNK, LANES), 1)
    cum = jnp.where(lane < SSM_HEADS, cum_f, cum_b)
    tot = cum_f[CHUNK - 1:CHUNK, :]
    return cum, tot


def _expand(e2_ref, *rows):
    x = jnp.concatenate(rows, axis=0)
    hi, lo = _split2(x)
    return _dot(jnp.concatenate([hi, lo], axis=1), e2_ref[...])


def _ssd_bwd_kernel(reset_ref, xa_ref, dt_ref, a_ref, tri_l3_ref, tri_u3_ref, e2_ref,
                    yb_ref, state_ref):
    nch = SSD_BLOCK // CHUNK

    @pl.when(reset_ref[pl.program_id(0)] == 1)
    def _():
        state_ref[...] = jnp.zeros_like(state_ref)

    def chunk_body(ci, carry):
        rows = pl.ds(pl.multiple_of((nch - 1 - ci) * CHUNK, CHUNK), CHUNK)
        dt = dt_ref[rows, :]
        cum, tot = _ssd_chunk_scalars(dt, a_ref, tri_l3_ref, tri_u3_ref)
        ex = _expand(e2_ref, dt * jnp.exp(tot - cum), jnp.exp(cum),
                     jnp.broadcast_to(jnp.exp(tot), (16, LANES)))
        for g in range(SSM_GROUPS):
            gc = slice(g * SSM_GROUP_W, (g + 1) * SSM_GROUP_W)
            bm = xa_ref[SSM_GROUPS, rows, g * SSM_DSTATE:(g + 1) * SSM_DSTATE]
            cm = xa_ref[SSM_GROUPS + 1, rows, g * SSM_DSTATE:(g + 1) * SSM_DSTATE]
            state = state_ref[g]
            yb_ref[rows, gc] = (_dot(cm, state.astype(BF16)) * ex[CHUNK:2 * CHUNK, gc]).astype(BF16)
            xw = (xa_ref[g, rows, :].astype(F32) * ex[0:CHUNK, gc]).astype(BF16)
            state_ref[g] = state * ex[2 * CHUNK:2 * CHUNK + 1, gc] + _dot_tn(bm, xw)
        return carry

    lax.fori_loop(0, nch, chunk_body, 0)


def _ssd_fwd_kernel(reset_ref, xa_ref, dt_ref, gate_ref, yb_ref, a_ref, tri_l3_ref, tri_u3_ref,
                    e2_ref, dskip_ref, nw_ref, out_ref, state_ref, y_ref):
    nch = SSD_BLOCK // CHUNK

    @pl.when(reset_ref[pl.program_id(0)] == 1)
    def _():
        state_ref[...] = jnp.zeros_like(state_ref)

    def chunk_body(ci, carry):
        rows = pl.ds(pl.multiple_of(ci * CHUNK, CHUNK), CHUNK)
        dt = dt_ref[rows, :]
        cum, tot = _ssd_chunk_scalars(dt, a_ref, tri_l3_ref, tri_u3_ref)
        ex = _expand(e2_ref, dt * jnp.exp(tot - cum), jnp.exp(cum),
                     jnp.broadcast_to(jnp.exp(tot), (16, LANES)))
        row_t = (cum - jnp.log(dt)).T
        li = lax.broadcasted_iota(jnp.int32, (CHUNK, CHUNK), 0)
        si = lax.broadcasted_iota(jnp.int32, (CHUNK, CHUNK), 1)
        lower = li >= si
        first_half = si < SSM_HEADDIM
        for g in range(SSM_GROUPS):
            gc = slice(g * SSM_GROUP_W, (g + 1) * SSM_GROUP_W)
            bm = xa_ref[SSM_GROUPS, rows, g * SSM_DSTATE:(g + 1) * SSM_DSTATE]
            cm = xa_ref[SSM_GROUPS + 1, rows, g * SSM_DSTATE:(g + 1) * SSM_DSTATE]
            cb = _dot_nt(cm, bm)
            heads_per_group = SSM_HEADS // SSM_GROUPS
            for pair in range(heads_per_group // 2):
                mats = []
                for hh in range(2):
                    h = g * heads_per_group + 2 * pair + hh
                    hb = SSM_HEADS + h
                    seg = jnp.where(lower, cum[:, h:h + 1] - row_t[h:h + 1, :],
                                    cum[:, hb:hb + 1] - row_t[hb:hb + 1, :])
                    mats.append((cb * jnp.exp(seg)).astype(BF16))
                pc = slice(2 * pair * SSM_HEADDIM, (2 * pair + 2) * SSM_HEADDIM)
                xp = xa_ref[g, rows, pc]
                zero = jnp.zeros_like(xp)
                rhs = jnp.concatenate([jnp.where(first_half, xp, zero), jnp.where(first_half, zero, xp)], axis=0)
                y_ref[:, pc] = _dot(jnp.concatenate(mats, axis=1), rhs)
            state = state_ref[g]
            xs = xa_ref[g, rows, :].astype(F32)
            y = (y_ref[...] + _dot(cm, state.astype(BF16)) * ex[CHUNK:2 * CHUNK, gc]
                 + yb_ref[rows, gc].astype(F32) + xs * dskip_ref[:, gc])
            y = y * gate_ref[g, rows, :].astype(F32)
            out_ref[rows, gc] = (_rms(y) * nw_ref[:, gc]).astype(BF16)
            xw = (xs * ex[0:CHUNK, gc]).astype(BF16)
            state_ref[g] = state * ex[2 * CHUNK:2 * CHUNK + 1, gc] + _dot_tn(bm, xw)
        return carry

    lax.fori_loop(0, nch, chunk_body, 0)


def _ssd(xa, dt, gate, a_row, dskip, norm_w, reset_f, reset_b):
    T = dt.shape[0]
    nb = T // SSD_BLOCK
    tb = _ssd_tables()
    state = pltpu.VMEM((SSM_GROUPS, SSM_DSTATE, SSM_GROUP_W), F32)
    consts = [a_row, tb["tri_l3"], tb["tri_u3"]]
    const_specs = [_resident(c.shape) for c in consts]
    out_shape = jax.ShapeDtypeStruct((T, SSM_DINNER), BF16)

    rev = lambda i, r: (nb - 1 - i, 0)
    blk = lambda n, imap: pl.BlockSpec((SSD_BLOCK, n), imap)
    slabs = lambda a, imap: pl.BlockSpec((a.shape[0], SSD_BLOCK, SLAB), lambda i, r: (0, imap(i, r)[0], 0))
    y_b = pl.pallas_call(
        _ssd_bwd_kernel,
        grid_spec=pltpu.PrefetchScalarGridSpec(
            num_scalar_prefetch=1, grid=(nb,),
            in_specs=[slabs(xa, rev), blk(LANES, rev)] + const_specs + [_resident(tb["e2_b"].shape)],
            out_specs=blk(SSM_DINNER, rev), scratch_shapes=[state]),
        out_shape=out_shape, compiler_params=_cparams(1, 40), name="ssd_bwd",
    )(reset_b, xa, dt, *consts, tb["e2_b"])

    fwd = lambda i, r: (i, 0)
    return pl.pallas_call(
        _ssd_fwd_kernel,
        grid_spec=pltpu.PrefetchScalarGridSpec(
            num_scalar_prefetch=1, grid=(nb,),
            in_specs=[slabs(xa, fwd), blk(LANES, fwd), slabs(gate, fwd), blk(SSM_DINNER, fwd)]
            + const_specs + [_resident(tb["e2_f"].shape), _resident(dskip.shape), _resident(norm_w.shape)],
            out_specs=blk(SSM_DINNER, fwd),
            scratch_shapes=[state, pltpu.VMEM((CHUNK, SSM_GROUP_W), F32)]),
        out_shape=out_shape, compiler_params=_cparams(1, 52), name="ssd_fwd",
    )(reset_f, xa, dt, gate, y_b, *consts, tb["e2_f"], dskip, norm_w)


def _pack_halves(x):
    n = x.shape[1] // 2
    hi = lax.bitcast_convert_type(x[:, :n].astype(F32), jnp.uint32)
    lo = lax.bitcast_convert_type(x[:, n:].astype(F32), jnp.uint32)
    return hi | (lo >> 16)


def _unpack_halves(w):
    hi = lax.bitcast_convert_type(w & jnp.uint32(0xFFFF0000), F32)
    lo = lax.bitcast_convert_type(w << 16, F32)
    return hi, lo


def _router_kernel(y_ref, h_ref, wo_ref, ln_ref, wh_ref, wl_ref, b_ref, tri_ref,
                   h3_ref, xn_ref, info_ref, info_t_ref, count_ref, base_ref):
    @pl.when(pl.program_id(0) == 0)
    def _():
        base_ref[...] = jnp.zeros_like(base_ref)

    h3 = h_ref[...] + _dot(y_ref[...], wo_ref[...])
    h3_ref[...] = h3
    xh, xl = _split2(_rms(h3) * ln_ref[...])
    xn_ref[...] = _pack_halves(xh)
    logits = _dot_nt(wh_ref[...], xh) + _dot_nt(wh_ref[...], xl) + _dot_nt(wl_ref[...], xh) + b_ref[...]
    row = lax.broadcasted_iota(jnp.int32, logits.shape, 0)
    m1 = jnp.max(logits, axis=0, keepdims=True)
    i1 = jnp.min(jnp.where(logits == m1, row, LANES), axis=0, keepdims=True)
    rest = jnp.where(row == i1, -jnp.inf, logits)
    m2 = jnp.max(rest, axis=0, keepdims=True)
    i2 = jnp.min(jnp.where(rest == m2, row, LANES), axis=0, keepdims=True)
    e = jnp.exp(m2 - m1)
    g1 = 1.0 / (1.0 + e)
    g2 = e / (1.0 + e)
    oh1 = row == i1
    oh2 = row == i2
    oh = jnp.where(oh1 | oh2, 1.0, 0.0)
    base = base_ref[...]
    prefix = _dot(oh.astype(BF16), tri_ref[...]) + base
    r1 = jnp.sum(jnp.where(oh1, prefix, 0.0), axis=0, keepdims=True)
    r2 = jnp.sum(jnp.where(oh2, prefix, 0.0), axis=0, keepdims=True)
    base = base + jnp.sum(oh, axis=1, keepdims=True)
    count_ref[...] = base[:, :LANES]
    base_ref[...] = base
    fields = [i1.astype(F32), i2.astype(F32), g1, g2, r1, r2]
    info_t = jnp.zeros(logits.shape, F32)
    for c, f in enumerate(fields):
        info_t = jnp.where(row == c, f, info_t)
    info_t_ref[...] = info_t[:ROUTE_COLS, :]
    info_ref[...] = info_t.T[:, :ROUTE_COLS]


def _router(y, h, w_out, ln, wh, wl, b):
    T = h.shape[0]
    tb = MOE_TOK
    nb = T // tb
    r = np.arange(tb)
    tri = jnp.asarray((r[:, None] < r[None, :]).astype(np.float32), dtype=BF16)
    tok = lambda n: pl.BlockSpec((tb, n), lambda i: (i, 0))
    return pl.pallas_call(
        _router_kernel,
        grid=(nb,),
        in_specs=[tok(SSM_DINNER), tok(D_MODEL), _resident(w_out.shape), _resident((1, D_MODEL)),
                  _resident(wh.shape), _resident(wl.shape), _resident(b.shape), _resident(tri.shape)],
        out_specs=[tok(D_MODEL), tok(D_MODEL // 2), tok(ROUTE_COLS),
                   pl.BlockSpec((ROUTE_COLS, tb), lambda i: (0, i)), pl.BlockSpec((LANES, LANES), lambda i: (0, 0))],
        out_shape=[jax.ShapeDtypeStruct((T, D_MODEL), F32), jax.ShapeDtypeStruct((T, D_MODEL // 2), jnp.uint32),
                   jax.ShapeDtypeStruct((T, ROUTE_COLS), F32), jax.ShapeDtypeStruct((ROUTE_COLS, T), F32),
                   jax.ShapeDtypeStruct((LANES, LANES), F32)],
        scratch_shapes=[pltpu.VMEM((LANES, tb), F32)],
        compiler_params=_cparams(1, 32),
        name="ssm_out_router",
    )(y, h, w_out, ln, wh, wl, b, tri)


def _sc_gather(table, idx):
    n_rows, width = idx.shape[0], table.shape[1]
    workers = SC_CORES * SC_SUBCORES
    nbuf = 2
    assert n_rows % (workers * SC_CHUNK * nbuf) == 0
    per_worker = n_rows // workers
    mesh = plsc.VectorSubcoreMesh(core_axis_name="c", subcore_axis_name="s")

    @functools.partial(
        pl.kernel, mesh=mesh,
        out_type=jax.ShapeDtypeStruct((n_rows, width), table.dtype),
        scratch_types=[pltpu.VMEM((nbuf, SC_CHUNK), jnp.int32),
                       pltpu.VMEM((nbuf, SC_CHUNK, width), table.dtype),
                       pltpu.SemaphoreType.DMA((nbuf,)),
                       pltpu.SemaphoreType.DMA((nbuf,))],
    )
    def gather_rows(table_hbm, idx_hbm, out_hbm, idx_v, rows_v, gsem, wsem):
        base = (lax.axis_index("s") * SC_CORES + lax.axis_index("c")) * per_worker

        def gather(c, b):
            off = pl.multiple_of(base + c * SC_CHUNK, 8)
            pltpu.sync_copy(idx_hbm.at[pl.ds(off, SC_CHUNK)], idx_v.at[b])
            return pltpu.make_async_copy(table_hbm.at[idx_v.at[b]], rows_v.at[b], gsem.at[b])

        def write(c, b):
            off = pl.multiple_of(base + c * SC_CHUNK, 8)
            return pltpu.make_async_copy(rows_v.at[b], out_hbm.at[pl.ds(off, SC_CHUNK)], wsem.at[b])

        @pl.loop(0, per_worker // SC_CHUNK, step=nbuf)
        def _(c0):
            copies = [gather(c0 + b, b) for b in range(nbuf)]
            for cp in copies:
                cp.start()
            writes = []
            for b, cp in enumerate(copies):
                cp.wait()
                writes.append(write(c0 + b, b))
                writes[-1].start()
            for wr in writes:
                wr.wait()

    return gather_rows(table, idx)


def _sc_scatter_pairs(table, pos, n_out):
    n_rows, width = table.shape
    workers = SC_CORES * SC_SUBCORES
    nbuf = 2
    assert n_rows % (workers * SC_CHUNK * nbuf) == 0
    per_worker = n_rows // workers
    mesh = plsc.VectorSubcoreMesh(core_axis_name="c", subcore_axis_name="s")

    @functools.partial(
        pl.kernel, mesh=mesh,
        out_type=jax.ShapeDtypeStruct((n_out, width), table.dtype),
        scratch_types=[pltpu.VMEM((nbuf, 2, SC_CHUNK), jnp.int32),
                       pltpu.VMEM((nbuf, SC_CHUNK, width), table.dtype),
                       pltpu.SemaphoreType.DMA((nbuf,)),
                       pltpu.SemaphoreType.DMA((nbuf,))],
    )
    def scatter_rows(table_hbm, pos_hbm, out_hbm, idx_v, rows_v, rsem, wsem):
        base = (lax.axis_index("s") * SC_CORES + lax.axis_index("c")) * per_worker

        def read(c, b):
            off = pl.multiple_of(base + c * SC_CHUNK, 8)
            return pltpu.make_async_copy(table_hbm.at[pl.ds(off, SC_CHUNK)], rows_v.at[b], rsem.at[b])

        def write(b, j):
            return pltpu.make_async_copy(rows_v.at[b], out_hbm.at[idx_v.at[b, j]], wsem.at[b])

        @pl.loop(0, per_worker // SC_CHUNK, step=nbuf)
        def _(c0):
            reads = [read(c0 + b, b) for b in range(nbuf)]
            for rd in reads:
                rd.start()
            for b in range(nbuf):
                pltpu.sync_copy(pos_hbm.at[base // SC_CHUNK + c0 + b], idx_v.at[b])
            writes = []
            for b, rd in enumerate(reads):
                rd.wait()
                for j in range(2):
                    writes.append(write(b, j))
                    writes[-1].start()
            for wr in writes:
                wr.wait()

    return scatter_rows(table, pos)


def _expert_kernel(texp_ref, trows_ref, tfirst_ref, x_ref, wg_ref, wu_ref, wd_ref, out_ref,
                   acc_ref, xb_ref, wgc_ref, wuc_ref, wdc_ref):
    i = pl.program_id(0)
    f = pl.program_id(1)
    nf = pl.num_programs(1)
    n = D_MODEL // 2

    @pl.when(trows_ref[i] > 0)
    def _():
        @pl.when(f == 0)
        def _():
            acc_ref[...] = jnp.zeros_like(acc_ref)
            row = lax.broadcasted_iota(jnp.int32, x_ref.shape, 0)
            hi, lo = _unpack_halves(jnp.where(row < trows_ref[i], x_ref[...], jnp.uint32(0)))
            xb_ref[:, :n] = hi.astype(BF16)
            xb_ref[:, n:] = lo.astype(BF16)

        @pl.when(tfirst_ref[i] == 1)
        def _():
            wgc_ref[f] = wg_ref[0].astype(BF16)
            wuc_ref[f] = wu_ref[0].astype(BF16)
            wdc_ref[f] = wd_ref[0].astype(BF16)

        def ffn_chunk(m):
            x = xb_ref[0:m, :]
            down = None
            for c in range(0, MOE_FC, MOE_FC // 2):
                cs = slice(c, c + MOE_FC // 2)
                act = (_silu(_dot(x, wgc_ref[f, :, cs])) * _dot(x, wuc_ref[f, :, cs])).astype(BF16)
                part = _dot(act, wdc_ref[f, cs, :])
                down = part if down is None else down + part
            acc_ref[0:m, :] += down

        for m in range(MOE_ROWS_STEP, MOE_SLOT + 1, MOE_ROWS_STEP):
            @pl.when((trows_ref[i] > m - MOE_ROWS_STEP) & (trows_ref[i] <= m))
            def _(m=m):
                ffn_chunk(m)

        @pl.when(f == nf - 1)
        def _():
            out_ref[...] = _pack_halves(acc_ref[...].astype(BF16))

    @pl.when((trows_ref[i] == 0) & (f == nf - 1))
    def _():
        out_ref[...] = jnp.zeros_like(out_ref)


def _experts(xs, w_gu, w_down, tile_expert, tile_rows, tile_first):
    n_tiles = xs.shape[0] // MOE_SLOT
    nf = FFN_EXPERT // MOE_FC
    chunk = lambda i, f, tf: jnp.where(tf[i] == 1, f, nf - 1)
    return pl.pallas_call(
        _expert_kernel,
        grid_spec=pltpu.PrefetchScalarGridSpec(
            num_scalar_prefetch=3, grid=(n_tiles, nf),
            in_specs=[pl.BlockSpec((MOE_SLOT, D_MODEL // 2), lambda i, f, te, tr, tf: (i, 0)),
                      pl.BlockSpec((1, D_MODEL, MOE_FC), lambda i, f, te, tr, tf: (te[i], 0, chunk(i, f, tf))),
                      pl.BlockSpec((1, D_MODEL, MOE_FC), lambda i, f, te, tr, tf: (te[i], 0, nf + chunk(i, f, tf))),
                      pl.BlockSpec((1, MOE_FC, D_MODEL), lambda i, f, te, tr, tf: (te[i], chunk(i, f, tf), 0))],
            out_specs=pl.BlockSpec((MOE_SLOT, D_MODEL // 2), lambda i, f, te, tr, tf: (i, 0)),
            scratch_shapes=[pltpu.VMEM((MOE_SLOT, D_MODEL), F32), pltpu.VMEM((MOE_SLOT, D_MODEL), BF16),
                            pltpu.VMEM((nf, D_MODEL, MOE_FC), BF16), pltpu.VMEM((nf, D_MODEL, MOE_FC), BF16),
                            pltpu.VMEM((nf, MOE_FC, D_MODEL), BF16)]),
        out_shape=jax.ShapeDtypeStruct(xs.shape, jnp.uint32),
        compiler_params=_cparams(2, 58),
        name="moe_experts",
    )(tile_expert, tile_rows, tile_first, xs, w_gu, w_gu, w_down)


def _finalize_kernel(y0_ref, y1_ref, info_ref, h_ref, fn_ref, out_ref):
    n = D_MODEL // 2
    info = info_ref[...]
    g0 = info[:, 2:3]
    g1 = info[:, 3:4]
    a_hi, a_lo = _unpack_halves(y0_ref[...])
    b_hi, b_lo = _unpack_halves(y1_ref[...])
    left = h_ref[:, :n] + g0 * a_hi + g1 * b_hi
    right = h_ref[:, n:] + g0 * a_lo + g1 * b_lo
    ms = (jnp.sum(left * left, axis=-1, keepdims=True)
          + jnp.sum(right * right, axis=-1, keepdims=True)) * (1.0 / D_MODEL)
    scale = lax.rsqrt(ms + EPS)
    out_ref[:, :n] = left * scale * fn_ref[:, :n]
    out_ref[:, n:] = right * scale * fn_ref[:, n:]


def _finalize(yg, info, h3, final_norm, tok0):
    n = yg.shape[0] // 2
    tb = MOE_TOK
    nb = n // tb
    b0 = tok0 // tb
    tok = lambda w: pl.BlockSpec((tb, w), lambda i: (b0 + i, 0))
    loc = lambda off: pl.BlockSpec((tb, D_MODEL // 2), lambda i: (i + off, 0))
    return pl.pallas_call(
        _finalize_kernel,
        grid=(nb,),
        in_specs=[loc(0), loc(nb), tok(info.shape[1]), tok(D_MODEL), _resident((1, D_MODEL))],
        out_specs=pl.BlockSpec((tb, D_MODEL), lambda i: (i, 0)),
        out_shape=jax.ShapeDtypeStruct((n, D_MODEL), F32),
        compiler_params=_cparams(1, 32),
        name="moe_finalize",
    )(yg, yg, info, h3, final_norm)


def _moe(y_ssm, h2, w_out, ln, wr_hi, wr_lo, rb, w_gu, w_down, final_norm, n_a):
    T = h2.shape[0]
    n_tiles = 2 * T // MOE_SLOT + N_EXPERTS
    h3, xn, info, info_t, counts = _router(y_ssm, h2, w_out, ln, wr_hi, wr_lo, rb)

    counts = counts[:N_EXPERTS, 0].astype(jnp.int32)
    tiles_e = (counts + MOE_SLOT - 1) // MOE_SLOT
    tiles_cum = jnp.cumsum(tiles_e)
    gstart = (tiles_cum - tiles_e) * MOE_SLOT
    routed = info_t.astype(jnp.int32)
    start_of = lambda e: sum(jnp.where(e == x, gstart[x], 0) for x in range(N_EXPERTS))
    pos0 = start_of(routed[0]) + routed[4]
    pos1 = start_of(routed[1]) + routed[5]
    tile_ids = jnp.arange(n_tiles, dtype=jnp.int32)
    tile_expert = jnp.minimum(jnp.sum(tile_ids[:, None] >= tiles_cum[None, :], axis=1), N_EXPERTS - 1)
    tile_expert = tile_expert.astype(jnp.int32)
    tile_rows = jnp.clip(counts[tile_expert] - (tile_ids * MOE_SLOT - gstart[tile_expert]), 0, MOE_SLOT)
    tile_rows = jnp.where(tile_ids < tiles_cum[-1], tile_rows, 0).astype(jnp.int32)
    pos = jnp.stack([pos0.reshape(-1, SC_CHUNK), pos1.reshape(-1, SC_CHUNK)], axis=1)

    xs = _sc_scatter_pairs(xn, pos, n_tiles * MOE_SLOT)
    tile_first = ((tile_ids * MOE_SLOT == gstart[tile_expert]) & (tile_rows > 0)).astype(jnp.int32)
    ys = _experts(xs, w_gu, w_down, tile_expert, tile_rows, tile_first)
    outs = []
    for lo, hi in ((0, n_a), (n_a, T)):
        yg = _sc_gather(ys, jnp.concatenate([pos0[lo:hi], pos1[lo:hi]]))
        outs.append(_finalize(yg, info, h3, final_norm, lo))
    return outs


def _seq_flags(seq_lens, block):
    first, last = [], []
    for n in seq_lens:
        nb = n // block
        first += [1] + [0] * (nb - 1)
        last += [0] * (nb - 1) + [1]
    return np.asarray(first, np.int32), np.asarray(last, np.int32)


def _trunk(xa, xb, seq_lens, ln_mix0, ret_w_in, ret_w_out, ln_ffn0, ffn_w_gu, ffn_w_down, ln_mix1, ssm_w_in,
           ssm_conv_w, ssm_conv_b, ssm_dt_bias, ssm_A_log, ssm_D, ssm_norm_w, ssm_w_out, ln_ffn1,
           moe_router_w, moe_router_b, moe_w_gu, moe_w_down, final_norm):
    assert all(n % b == 0 for n in seq_lens for b in (SCAN_BLOCK, SSD_BLOCK, MOE_TOK, TOK_TILE, PROJ_TILE))
    row = lambda v: v.astype(F32).reshape(1, -1)
    first, last = _seq_flags(seq_lens, SCAN_BLOCK)
    reset_f = jnp.asarray(first)
    reset_b = jnp.asarray(last[::-1].copy())
    ssd_first, ssd_last = _seq_flags(seq_lens, SSD_BLOCK)

    half = RET_DK // 2
    inv = ROPE_BASE ** (-jnp.arange(half, dtype=F32) / half)
    ang_base = jnp.arange(0, max(seq_lens), TOK_TILE).astype(F32)[:, None, None] * inv[None, None, :]
    ang_off = jnp.arange(TOK_TILE).astype(F32)[:, None] * inv[None, :]
    pos_block = jnp.asarray(np.concatenate([np.arange(n // TOK_TILE) for n in seq_lens]).astype(np.int32))
    q, k, v, g = _ret_in(xa, xb, row(ln_mix0), ret_w_in.astype(BF16), jnp.cos(ang_base), jnp.sin(ang_base),
                         jnp.cos(ang_off), jnp.sin(ang_off), pos_block)
    y = _ret_scan(q, k, v, g, reset_f, reset_b)
    h2 = _post0(y, xa, xb, ret_w_out.astype(BF16), row(ln_ffn0), ffn_w_gu.astype(BF16),
                ffn_w_down.astype(BF16))

    wz = ssm_w_in[:, :SSM_DINNER].astype(BF16)
    wx = ssm_w_in[:, SSM_DINNER:SSM_DINNER + SSM_CONV_DIM].astype(BF16)
    wdt = jnp.pad(ssm_w_in[:, SSM_DINNER + SSM_CONV_DIM:], ((0, 0), (0, LANES - 2 * SSM_HEADS)))
    wdh, wdl = _split2(wdt)
    pad_row = lambda v: jnp.pad(v.astype(F32).reshape(1, -1), ((0, 0), (0, LANES - 2 * SSM_HEADS)))
    gate, xbc, dt = _ssm_in(h2, row(ln_mix1), wz, wx, wdh, wdl, pad_row(ssm_dt_bias))
    xact = _conv(xbc, ssm_conv_w.reshape(SSM_CONV, SSM_CONV_DIM).astype(F32), row(ssm_conv_b),
                 jnp.asarray(ssd_first), jnp.asarray(ssd_last))
    a_row = pad_row(-jnp.exp(ssm_A_log.astype(F32)))
    dskip = jnp.repeat(ssm_D.astype(F32), SSM_HEADDIM).reshape(1, SSM_DINNER)
    y = _ssd(xact, dt, gate, a_row, dskip, row(ssm_norm_w), jnp.asarray(ssd_first),
             jnp.asarray(ssd_last[::-1].copy()))

    wr = jnp.pad(moe_router_w.astype(F32).T, ((0, LANES - N_EXPERTS), (0, 0)))
    wr_hi, wr_lo = _split2(wr)
    rb = jnp.pad(moe_router_b.astype(F32).reshape(-1, 1), ((0, LANES - N_EXPERTS), (0, 0)), constant_values=-1e30)
    rb = jnp.broadcast_to(rb, (LANES, MOE_TOK))
    return _moe(y, h2, ssm_w_out.astype(BF16), row(ln_ffn1), wr_hi, wr_lo, rb,
                moe_w_gu, moe_w_down, row(final_norm), xa.shape[0])


def kernel(x_prompt, x_sample, ln_mix0, ret_w_in, ret_w_out, ln_ffn0, ffn_w_gu, ffn_w_down, ln_mix1, ssm_w_in,
           ssm_conv_w, ssm_conv_b, ssm_dt_bias, ssm_A_log, ssm_D, ssm_norm_w, ssm_w_out, ln_ffn1, moe_router_w,
           moe_router_b, moe_w_gu, moe_w_down, final_norm):
    seq_lens = [x_prompt.shape[1]] * x_prompt.shape[0] + [x_sample.shape[1]] * x_sample.shape[0]
    out_a, out_b = _trunk(x_prompt.reshape(-1, D_MODEL), x_sample.reshape(-1, D_MODEL), seq_lens, ln_mix0, ret_w_in, ret_w_out, ln_ffn0, ffn_w_gu, ffn_w_down, ln_mix1, ssm_w_in,
                 ssm_conv_w, ssm_conv_b, ssm_dt_bias, ssm_A_log, ssm_D, ssm_norm_w, ssm_w_out, ln_ffn1,
                 moe_router_w, moe_router_b, moe_w_gu, moe_w_down, final_norm)
    return (out_a.reshape(x_prompt.shape), out_b.reshape(x_sample.shape))
```

```python
import functools

import numpy as np
import jax
import jax.numpy as jnp
from jax import lax
from jax.experimental import pallas as pl
from jax.experimental.pallas import tpu as pltpu
from jax.experimental.pallas import tpu_sc as plsc

F32 = jnp.float32
BF16 = jnp.bfloat16

D_MODEL = 1024
EPS = 1e-6
RET_HEADS = 4
RET_DK = 256
RET_DV = 512
ROPE_BASE = 10000.0
SSM_DINNER = 2048
SSM_HEADDIM = 64
SSM_HEADS = 32
SSM_GROUPS = 4
SSM_DSTATE = 128
SSM_CONV = 5
SSM_CONV_DIM = 3072
SSM_GROUP_W = SSM_DINNER // SSM_GROUPS
SLAB = SSM_GROUP_W
CONV_ROWS = 128
CONV_WINDOW = 256
HALO = 16
FFN_DENSE = 2816
FFN_DENSE_CHUNK = 1408
N_EXPERTS = 8
FFN_EXPERT = 3584

CHUNK = 128
RET_CHUNK = 256
LANES = 128
TOK_TILE = 512
PROJ_TILE = 1024
SCAN_BLOCK = 512
SSD_BLOCK = 1024
MOE_TOK = 512
MOE_SLOT = 1024
MOE_FC = 512
MOE_ROWS_STEP = 256
ROUTE_COLS = 8
SC_CORES = 2
SC_SUBCORES = 16
SC_CHUNK = 64


def _cparams(n_axes, vmem_mb):
    return pltpu.CompilerParams(dimension_semantics=("arbitrary",) * n_axes,
                                vmem_limit_bytes=vmem_mb << 20)


def _resident(shape):
    nd = len(shape)
    return pl.BlockSpec(shape, lambda *_: (0,) * nd, pipeline_mode=pl.Buffered(1))


def _rms(x):
    return x * lax.rsqrt(jnp.mean(x * x, axis=-1, keepdims=True) + EPS)


def _silu(x):
    return x * jax.nn.sigmoid(x)


def _dot(a, b):
    return jnp.dot(a, b, preferred_element_type=F32)


def _dot_nt(a, b):
    return lax.dot_general(a, b, (((1,), (1,)), ((), ())), preferred_element_type=F32)


def _dot_tn(a, b):
    return lax.dot_general(a, b, (((0,), (0,)), ((), ())), preferred_element_type=F32)


def _split2(x):
    hi = x.astype(BF16)
    lo = (x - hi.astype(F32)).astype(BF16)
    return hi, lo


def _split3(x):
    hi = x.astype(BF16)
    r = x - hi.astype(F32)
    mid = r.astype(BF16)
    lo = (r - mid.astype(F32)).astype(BF16)
    return hi, mid, lo


def _two_stream_specs(tm, n_a):
    return (pl.BlockSpec((tm, D_MODEL), lambda i, *_: (jnp.minimum(i, n_a - 1), 0)),
            pl.BlockSpec((tm, D_MODEL), lambda i, *_: (jnp.maximum(i - n_a, 0), 0)))


def _ret_in_kernel(pos_ref, xa_ref, xb_ref, ln_ref, w_ref, cb_ref, sb_ref, co_ref, so_ref,
                   q_ref, k_ref, v_ref, g_ref, *, n_a):
    x = jnp.where(pl.program_id(0) < n_a, xa_ref[...], xb_ref[...])
    xn = (_rms(x) * ln_ref[...]).astype(BF16)
    cb, sb, co, so = cb_ref[0], sb_ref[0], co_ref[...], so_ref[...]
    cos = cb * co - sb * so
    sin = sb * co + cb * so
    half = RET_DK // 2

    def rotary(dst, col0, scale):
        for h in range(RET_HEADS):
            c = col0 + h * RET_DK
            p = _dot(xn, w_ref[:, c:c + RET_DK])
            p1, p2 = p[:, :half], p[:, half:]
            dst[:, h * RET_DK:h * RET_DK + half] = ((p1 * cos - p2 * sin) * scale).astype(BF16)
            dst[:, h * RET_DK + half:(h + 1) * RET_DK] = ((p1 * sin + p2 * cos) * scale).astype(BF16)

    rotary(q_ref, 0, 1.0)
    rotary(k_ref, D_MODEL, RET_DK ** -0.5)
    for j in range(2 * D_MODEL // 512):
        v_ref[:, j * 512:(j + 1) * 512] = _dot(
            xn, w_ref[:, 2 * D_MODEL + j * 512:2 * D_MODEL + (j + 1) * 512]).astype(BF16)
        g_ref[:, j * 512:(j + 1) * 512] = _silu(_dot(
            xn, w_ref[:, 4 * D_MODEL + j * 512:4 * D_MODEL + (j + 1) * 512])).astype(BF16)


def _ret_in(xa, xb, ln, w_in, cos_base, sin_base, cos_off, sin_off, pos_block):
    T = xa.shape[0] + xb.shape[0]
    tm = TOK_TILE
    n_a = xa.shape[0] // tm
    tok = lambda n: pl.BlockSpec((tm, n), lambda i, pb: (i, 0))
    rot = pl.BlockSpec((1, 1, LANES), lambda i, pb: (pb[i], 0, 0))
    return pl.pallas_call(
        functools.partial(_ret_in_kernel, n_a=n_a),
        grid_spec=pltpu.PrefetchScalarGridSpec(
            num_scalar_prefetch=1, grid=(T // tm,),
            in_specs=[*_two_stream_specs(tm, n_a), _resident((1, D_MODEL)), _resident(w_in.shape), rot, rot,
                      _resident(cos_off.shape), _resident(sin_off.shape)],
            out_specs=[tok(D_MODEL), tok(D_MODEL), tok(2 * D_MODEL), tok(2 * D_MODEL)]),
        out_shape=[jax.ShapeDtypeStruct((T, D_MODEL), BF16), jax.ShapeDtypeStruct((T, D_MODEL), BF16),
                   jax.ShapeDtypeStruct((T, 2 * D_MODEL), BF16), jax.ShapeDtypeStruct((T, 2 * D_MODEL), BF16)],
        compiler_params=_cparams(1, 48),
        name="ret_in",
    )(pos_block, xa, xb, ln, w_in, cos_base, sin_base, cos_off, sin_off)


def _ret_scan_kernel(reset_ref, *refs, reverse, chunk_decay):
    if reverse:
        q_ref, k_ref, v_ref, wq_ref, wk_ref, out_ref, state_ref = refs
    else:
        q_ref, k_ref, v_ref, ob_ref, g_ref, wq_ref, wk_ref, din_ref, out_ref, state_ref = refs
    nch = SCAN_BLOCK // RET_CHUNK

    @pl.when(reset_ref[pl.program_id(0)] == 1)
    def _():
        state_ref[...] = jnp.zeros_like(state_ref)

    def chunk_body(ci, carry):
        cj = nch - 1 - ci if reverse else ci
        rows = pl.ds(pl.multiple_of(cj * RET_CHUNK, RET_CHUNK), RET_CHUNK)
        for h in range(RET_HEADS):
            kcols = slice(h * RET_DK, (h + 1) * RET_DK)
            vcols = slice(h * RET_DV, (h + 1) * RET_DV)
            qh = q_ref[rows, kcols]
            kh = k_ref[rows, kcols]
            vh = v_ref[rows, vcols]
            state = state_ref[h]
            o = _dot(qh, state.astype(BF16)) * wq_ref[h]
            if reverse:
                out_ref[rows, vcols] = o.astype(BF16)
            else:
                s = _dot_nt(qh, kh) * din_ref[h]
                o = o + _dot(s.astype(BF16), vh) + ob_ref[rows, vcols].astype(F32)
                out_ref[rows, vcols] = (g_ref[rows, vcols].astype(F32) * _rms(o)).astype(BF16)
            ks = (kh.astype(F32) * wk_ref[h]).astype(BF16)
            state_ref[h] = state * chunk_decay[h] + _dot_tn(ks, vh)
        return carry

    lax.fori_loop(0, nch, chunk_body, 0)


def _ret_tables():
    lg = np.log1p(-np.power(2.0, -5.0 - np.arange(RET_HEADS, dtype=np.float32))).astype(np.float32)
    c = RET_CHUNK
    a = np.arange(c, dtype=np.float32)
    col = lambda e, w: np.broadcast_to(np.exp(lg[:, None, None] * e[None, :, None]),
                                       (RET_HEADS, c, w)).astype(np.float32)
    dist = np.abs(a[:, None] - a[None, :])
    return dict(
        din=np.exp(lg[:, None, None] * dist[None]).astype(np.float32),
        wq_f=col(a + 1.0, RET_DV), wk_f=col(c - 1.0 - a, RET_DK),
        wq_b=col(c - a, RET_DV), wk_b=col(a, RET_DK),
        chunk_decay=tuple(float(v) for v in np.exp(lg * c)),
    )


def _ret_scan(q, k, v, g, reset_f, reset_b):
    T = q.shape[0]
    nb = T // SCAN_BLOCK
    tb = _ret_tables()
    state = pltpu.VMEM((RET_HEADS, RET_DK, RET_DV), F32)
    out_shape = jax.ShapeDtypeStruct((T, 2 * D_MODEL), BF16)

    def specs(imap):
        blk = lambda n: pl.BlockSpec((SCAN_BLOCK, n), imap)
        return blk(D_MODEL), blk(D_MODEL), blk(2 * D_MODEL)

    rev = lambda i, r: (nb - 1 - i, 0)
    bq, bk, bv = specs(rev)
    o_b = pl.pallas_call(
        functools.partial(_ret_scan_kernel, reverse=True, chunk_decay=tb["chunk_decay"]),
        grid_spec=pltpu.PrefetchScalarGridSpec(
            num_scalar_prefetch=1, grid=(nb,),
            in_specs=[bq, bk, bv, _resident(tb["wq_b"].shape), _resident(tb["wk_b"].shape)],
            out_specs=bv, scratch_shapes=[state]),
        out_shape=out_shape, compiler_params=_cparams(1, 40), name="ret_scan_bwd",
    )(reset_b, q, k, v, tb["wq_b"], tb["wk_b"])

    fwd = lambda i, r: (i, 0)
    bq, bk, bv = specs(fwd)
    return pl.pallas_call(
        functools.partial(_ret_scan_kernel, reverse=False, chunk_decay=tb["chunk_decay"]),
        grid_spec=pltpu.PrefetchScalarGridSpec(
            num_scalar_prefetch=1, grid=(nb,),
            in_specs=[bq, bk, bv, bv, bv, _resident(tb["wq_f"].shape), _resident(tb["wk_f"].shape),
                      _resident(tb["din"].shape)],
            out_specs=bv, scratch_shapes=[state]),
        out_shape=out_shape, compiler_params=_cparams(1, 40), name="ret_scan_fwd",
    )(reset_f, q, k, v, o_b, g, tb["wq_f"], tb["wk_f"], tb["din"])


def _post0_kernel(y_ref, xa_ref, xb_ref, wo_ref, ln_ref, wgu_ref, wd_ref, out_ref, *, n_a):
    x = jnp.where(pl.program_id(0) < n_a, xa_ref[...], xb_ref[...])
    h1 = x + _dot(y_ref[...], wo_ref[...])
    xn = (_rms(h1) * ln_ref[...]).astype(BF16)
    acc = h1
    fc = FFN_DENSE_CHUNK
    for c in range(FFN_DENSE // fc):
        gate = _dot(xn, wgu_ref[:, c * fc:(c + 1) * fc])
        up = _dot(xn, wgu_ref[:, FFN_DENSE + c * fc:FFN_DENSE + (c + 1) * fc])
        act = (_silu(gate) * up).astype(BF16)
        acc = acc + _dot(act, wd_ref[c * fc:(c + 1) * fc, :])
    out_ref[...] = acc


def _post0(y, xa, xb, w_out, ln, w_gu, w_down):
    T = y.shape[0]
    tm = TOK_TILE
    n_a = xa.shape[0] // tm
    tok = lambda n: pl.BlockSpec((tm, n), lambda i: (i, 0))
    return pl.pallas_call(
        functools.partial(_post0_kernel, n_a=n_a),
        grid=(T // tm,),
        in_specs=[tok(2 * D_MODEL), *_two_stream_specs(tm, n_a), _resident(w_out.shape),
                  _resident((1, D_MODEL)), _resident(w_gu.shape), _resident(w_down.shape)],
        out_specs=tok(D_MODEL),
        out_shape=jax.ShapeDtypeStruct((T, D_MODEL), F32),
        compiler_params=_cparams(1, 52),
        name="ret_out_ffn",
    )(y, xa, xb, w_out, ln, w_gu, w_down)


def _ssm_in_kernel(x_ref, ln_ref, wz_ref, wx_ref, wd_ref, wdh_ref, bias_ref, gate_ref, xbc_ref, dt_ref):
    xh, xl = _split2(_rms(x_ref[...]) * ln_ref[...])
    for j in range(SSM_DINNER // SLAB):
        gate_ref[j] = _silu(_dot(xh, wz_ref[:, j * SLAB:(j + 1) * SLAB])).astype(BF16)
    for j in range(SSM_CONV_DIM // SLAB):
        xbc_ref[:, j * SLAB:(j + 1) * SLAB] = _dot(xh, wx_ref[:, j * SLAB:(j + 1) * SLAB]).astype(BF16)
    both = _dot(xh, wd_ref[...])
    x = both[:, :LANES] + both[:, LANES:] + _dot(xl, wdh_ref[...]) + bias_ref[...]
    dt_ref[...] = jnp.maximum(x, 0.0) + jnp.log1p(jnp.exp(-jnp.abs(x)))


def _ssm_in(x, ln, wz, wx, wdh, wdl, bias_row):
    T = x.shape[0]
    tm = PROJ_TILE
    n_gate = SSM_DINNER // SLAB
    wd = jnp.concatenate([wdh, wdl], axis=1)
    tok = lambda n: pl.BlockSpec((tm, n), lambda i: (i, 0))
    return pl.pallas_call(
        _ssm_in_kernel,
        grid=(T // tm,),
        in_specs=[tok(D_MODEL), _resident((1, D_MODEL)), _resident(wz.shape), _resident(wx.shape),
                  _resident(wd.shape), _resident(wdh.shape), _resident(bias_row.shape)],
        out_specs=[pl.BlockSpec((n_gate, tm, SLAB), lambda i: (0, i, 0)), tok(SSM_CONV_DIM), tok(LANES)],
        out_shape=[jax.ShapeDtypeStruct((n_gate, T, SLAB), BF16), jax.ShapeDtypeStruct((T, SSM_CONV_DIM), BF16),
                   jax.ShapeDtypeStruct((T, LANES), F32)],
        compiler_params=_cparams(1, 52),
        name="ssm_in",
    )(x, ln, wz, wx, wd, wdh, bias_row)


def _conv_kernel(first_ref, last_ref, prev_ref, main_ref, next_ref, shift_ref, w_ref, b_ref, out_ref, ext_ref):
    i = pl.program_id(0)
    rb = SSD_BLOCK
    ext_rows = ext_ref.shape[0]

    @pl.when(i == 0)
    def _():
        ext_ref[rb + 2 * HALO:ext_rows, :] = jnp.zeros((ext_rows - rb - 2 * HALO, SSM_CONV_DIM), BF16)

    zero = jnp.zeros((HALO, SSM_CONV_DIM), BF16)
    ext_ref[0:HALO, :] = jnp.where(first_ref[i] == 1, zero, prev_ref[...])
    ext_ref[HALO:HALO + rb, :] = main_ref[...]
    ext_ref[HALO + rb:2 * HALO + rb, :] = jnp.where(last_ref[i] == 1, zero, next_ref[...])
    pad = SSM_CONV // 2
    side_taps = [j for j in range(SSM_CONV) if j != pad]
    for s in range(SSM_CONV_DIM // SLAB):
        cs = slice(s * SLAB, (s + 1) * SLAB)
        for r0 in range(0, rb, CONV_ROWS):
            taps = _dot(shift_ref[...], ext_ref[r0:r0 + CONV_WINDOW, cs])
            centre = ext_ref[HALO + r0:HALO + r0 + CONV_ROWS, cs].astype(F32)
            acc = b_ref[:, cs] + centre * w_ref[pad:pad + 1, cs]
            for n, j in enumerate(side_taps):
                acc = acc + taps[n * CONV_ROWS:(n + 1) * CONV_ROWS] * w_ref[j:j + 1, cs]
            out_ref[s, r0:r0 + CONV_ROWS, :] = _silu(acc).astype(BF16)


def _conv(xbc, conv_w, conv_b, first, last):
    T = xbc.shape[0]
    rb = SSD_BLOCK
    nb = T // rb
    per = rb // HALO
    nh = T // HALO
    r = np.arange(CONV_ROWS)
    pad = SSM_CONV // 2
    shift = np.zeros(((SSM_CONV - 1) * CONV_ROWS, CONV_WINDOW), np.float32)
    for n, j in enumerate(j for j in range(SSM_CONV) if j != pad):
        shift[n * CONV_ROWS + r, HALO + r + j - pad] = 1.0
    shift = jnp.asarray(shift, dtype=BF16)
    ext_rows = rb - CONV_ROWS + CONV_WINDOW
    return pl.pallas_call(
        _conv_kernel,
        grid_spec=pltpu.PrefetchScalarGridSpec(
            num_scalar_prefetch=2, grid=(nb,),
            in_specs=[
                pl.BlockSpec((HALO, SSM_CONV_DIM), lambda i, f, l: (jnp.maximum(i * per - 1, 0), 0)),
                pl.BlockSpec((rb, SSM_CONV_DIM), lambda i, f, l: (i, 0)),
                pl.BlockSpec((HALO, SSM_CONV_DIM), lambda i, f, l: (jnp.minimum((i + 1) * per, nh - 1), 0)),
                _resident(shift.shape), _resident(conv_w.shape), _resident(conv_b.shape)],
            out_specs=pl.BlockSpec((SSM_CONV_DIM // SLAB, rb, SLAB), lambda i, f, l: (0, i, 0)),
            scratch_shapes=[pltpu.VMEM((ext_rows, SSM_CONV_DIM), BF16)]),
        out_shape=jax.ShapeDtypeStruct((SSM_CONV_DIM // SLAB, T, SLAB), BF16),
        compiler_params=_cparams(1, 48),
        name="ssm_conv",
    )(first, last, xbc, xbc, xbc, shift, conv_w, conv_b)


def _ssd_tables():
    r = np.arange(CHUNK)
    tri_l = (r[None, :] <= r[:, None]).astype(np.float32)
    tri_u = (r[None, :] >= r[:, None]).astype(np.float32)
    c = np.arange(SSM_DINNER) // SSM_HEADDIM
    j = np.arange(LANES)
    e_f = (j[:, None] == c[None, :]).astype(np.float32)
    e_b = (j[:, None] == (c[None, :] + SSM_HEADS)).astype(np.float32)
    as_bf16 = lambda x: jnp.asarray(x, dtype=BF16)
    return dict(tri_l3=as_bf16(np.concatenate([tri_l] * 3, axis=1)),
                tri_u3=as_bf16(np.concatenate([tri_u] * 3, axis=1)),
                e2_f=as_bf16(np.concatenate([e_f, e_f], axis=0)),
                e2_b=as_bf16(np.concatenate([e_b, e_b], axis=0)))


def _ssd_chunk_scalars(dt, a_ref, tri_l3_ref, tri_u3_ref):
    a = dt * a_ref[...]
    a3 = jnp.concatenate(_split3(a), axis=0)
    cum_f = _dot(tri_l3_ref[...], a3)
    cum_b = _dot(tri_u3_ref[...], a3)
    lane = lax.broadcasted_iota(jnp.int32, (CHUNK, LANES), 1)
    cum = jnp.where(lane < SSM_HEADS, cum_f, cum_b)
    tot = cum_f[CHUNK - 1:CHUNK, :]
    return cum, tot


def _expand(e2_ref, *rows):
    x = jnp.concatenate(rows, axis=0)
    hi, lo = _split2(x)
    return _dot(jnp.concatenate([hi, lo], axis=1), e2_ref[...])


def _ssd_bwd_kernel(reset_ref, xa_ref, dt_ref, a_ref, tri_l3_ref, tri_u3_ref, e2_ref,
                    yb_ref, state_ref):
    nch = SSD_BLOCK // CHUNK

    @pl.when(reset_ref[pl.program_id(0)] == 1)
    def _():
        state_ref[...] = jnp.zeros_like(state_ref)

    def chunk_body(ci, carry):
        rows = pl.ds(pl.multiple_of((nch - 1 - ci) * CHUNK, CHUNK), CHUNK)
        dt = dt_ref[rows, :]
        cum, tot = _ssd_chunk_scalars(dt, a_ref, tri_l3_ref, tri_u3_ref)
        ex = _expand(e2_ref, dt * jnp.exp(tot - cum), jnp.exp(cum),
                     jnp.broadcast_to(jnp.exp(tot), (16, LANES)))
        for g in range(SSM_GROUPS):
            gc = slice(g * SSM_GROUP_W, (g + 1) * SSM_GROUP_W)
            bm = xa_ref[SSM_GROUPS, rows, g * SSM_DSTATE:(g + 1) * SSM_DSTATE]
            cm = xa_ref[SSM_GROUPS + 1, rows, g * SSM_DSTATE:(g + 1) * SSM_DSTATE]
            state = state_ref[g]
            yb_ref[rows, gc] = (_dot(cm, state.astype(BF16)) * ex[CHUNK:2 * CHUNK, gc]).astype(BF16)
            xw = (xa_ref[g, rows, :].astype(F32) * ex[0:CHUNK, gc]).astype(BF16)
            state_ref[g] = state * ex[2 * CHUNK:2 * CHUNK + 1, gc] + _dot_tn(bm, xw)
        return carry

    lax.fori_loop(0, nch, chunk_body, 0)


def _ssd_fwd_kernel(reset_ref, xa_ref, dt_ref, gate_ref, yb_ref, a_ref, tri_l3_ref, tri_u3_ref,
                    e2_ref, dskip_ref, nw_ref, out_ref, state_ref, y_ref):
    nch = SSD_BLOCK // CHUNK

    @pl.when(reset_ref[pl.program_id(0)] == 1)
    def _():
        state_ref[...] = jnp.zeros_like(state_ref)

    def chunk_body(ci, carry):
        rows = pl.ds(pl.multiple_of(ci * CHUNK, CHUNK), CHUNK)
        dt = dt_ref[rows, :]
        cum, tot = _ssd_chunk_scalars(dt, a_ref, tri_l3_ref, tri_u3_ref)
        ex = _expand(e2_ref, dt * jnp.exp(tot - cum), jnp.exp(cum),
                     jnp.broadcast_to(jnp.exp(tot), (16, LANES)))
        row_t = (cum - jnp.log(dt)).T
        li = lax.broadcasted_iota(jnp.int32, (CHUNK, CHUNK), 0)
        si = lax.broadcasted_iota(jnp.int32, (CHUNK, CHUNK), 1)
        lower = li >= si
        first_half = si < SSM_HEADDIM
        for g in range(SSM_GROUPS):
            gc = slice(g * SSM_GROUP_W, (g + 1) * SSM_GROUP_W)
            bm = xa_ref[SSM_GROUPS, rows, g * SSM_DSTATE:(g + 1) * SSM_DSTATE]
            cm = xa_ref[SSM_GROUPS + 1, rows, g * SSM_DSTATE:(g + 1) * SSM_DSTATE]
            cb = _dot_nt(cm, bm)
            heads_per_group = SSM_HEADS // SSM_GROUPS
            for pair in range(heads_per_group // 2):
                mats = []
                for hh in range(2):
                    h = g * heads_per_group + 2 * pair + hh
                    hb = SSM_HEADS + h
                    seg = jnp.where(lower, cum[:, h:h + 1] - row_t[h:h + 1, :],
                                    cum[:, hb:hb + 1] - row_t[hb:hb + 1, :])
                    mats.append((cb * jnp.exp(seg)).astype(BF16))
                pc = slice(2 * pair * SSM_HEADDIM, (2 * pair + 2) * SSM_HEADDIM)
                xp = xa_ref[g, rows, pc]
                zero = jnp.zeros_like(xp)
                rhs = jnp.concatenate([jnp.where(first_half, xp, zero), jnp.where(first_half, zero, xp)], axis=0)
                y_ref[:, pc] = _dot(jnp.concatenate(mats, axis=1), rhs)
            state = state_ref[g]
            xs = xa_ref[g, rows, :].astype(F32)
            y = (y_ref[...] + _dot(cm, state.astype(BF16)) * ex[CHUNK:2 * CHUNK, gc]
                 + yb_ref[rows, gc].astype(F32) + xs * dskip_ref[:, gc])
            y = y * gate_ref[g, rows, :].astype(F32)
            out_ref[rows, gc] = (_rms(y) * nw_ref[:, gc]).astype(BF16)
            xw = (xs * ex[0:CHUNK, gc]).astype(BF16)
            state_ref[g] = state * ex[2 * CHUNK:2 * CHUNK + 1, gc] + _dot_tn(bm, xw)
        return carry

    lax.fori_loop(0, nch, chunk_body, 0)


def _ssd(xa, dt, gate, a_row, dskip, norm_w, reset_f, reset_b):
    T = dt.shape[0]
    nb = T // SSD_BLOCK
    tb = _ssd_tables()
    state = pltpu.VMEM((SSM_GROUPS, SSM_DSTATE, SSM_GROUP_W), F32)
    consts = [a_row, tb["tri_l3"], tb["tri_u3"]]
    const_specs = [_resident(c.shape) for c in consts]
    out_shape = jax.ShapeDtypeStruct((T, SSM_DINNER), BF16)

    rev = lambda i, r: (nb - 1 - i, 0)
    blk = lambda n, imap: pl.BlockSpec((SSD_BLOCK, n), imap)
    slabs = lambda a, imap: pl.BlockSpec((a.shape[0], SSD_BLOCK, SLAB), lambda i, r: (0, imap(i, r)[0], 0))
    y_b = pl.pallas_call(
        _ssd_bwd_kernel,
        grid_spec=pltpu.PrefetchScalarGridSpec(
            num_scalar_prefetch=1, grid=(nb,),
            in_specs=[slabs(xa, rev), blk(LANES, rev)] + const_specs + [_resident(tb["e2_b"].shape)],
            out_specs=blk(SSM_DINNER, rev), scratch_shapes=[state]),
        out_shape=out_shape, compiler_params=_cparams(1, 40), name="ssd_bwd",
    )(reset_b, xa, dt, *consts, tb["e2_b"])

    fwd = lambda i, r: (i, 0)
    return pl.pallas_call(
        _ssd_fwd_kernel,
        grid_spec=pltpu.PrefetchScalarGridSpec(
            num_scalar_prefetch=1, grid=(nb,),
            in_specs=[slabs(xa, fwd), blk(LANES, fwd), slabs(gate, fwd), blk(SSM_DINNER, fwd)]
            + const_specs + [_resident(tb["e2_f"].shape), _resident(dskip.shape), _resident(norm_w.shape)],
            out_specs=blk(SSM_DINNER, fwd),
            scratch_shapes=[state, pltpu.VMEM((CHUNK, SSM_GROUP_W), F32)]),
        out_shape=out_shape, compiler_params=_cparams(1, 52), name="ssd_fwd",
    )(reset_f, xa, dt, gate, y_b, *consts, tb["e2_f"], dskip, norm_w)


def _pack_halves(x):
    n = x.shape[1] // 2
    hi = lax.bitcast_convert_type(x[:, :n].astype(F32), jnp.uint32)
    lo = lax.bitcast_convert_type(x[:, n:].astype(F32), jnp.uint32)
    return hi | (lo >> 16)


def _unpack_halves(w):
    hi = lax.bitcast_convert_type(w & jnp.uint32(0xFFFF0000), F32)
    lo = lax.bitcast_convert_type(w << 16, F32)
    return hi, lo


def _router_kernel(y_ref, h_ref, wo_ref, ln_ref, wh_ref, wl_ref, b_ref, tri_ref,
                   h3_ref, xn_ref, info_ref, info_t_ref, count_ref, base_ref):
    @pl.when(pl.program_id(0) == 0)
    def _():
        base_ref[...] = jnp.zeros_like(base_ref)

    h3 = h_ref[...] + _dot(y_ref[...], wo_ref[...])
    h3_ref[...] = h3
    xh, xl = _split2(_rms(h3) * ln_ref[...])
    xn_ref[...] = _pack_halves(xh)
    logits = _dot_nt(wh_ref[...], xh) + _dot_nt(wh_ref[...], xl) + _dot_nt(wl_ref[...], xh) + b_ref[...]
    row = lax.broadcasted_iota(jnp.int32, logits.shape, 0)
    m1 = jnp.max(logits, axis=0, keepdims=True)
    i1 = jnp.min(jnp.where(logits == m1, row, LANES), axis=0, keepdims=True)
    rest = jnp.where(row == i1, -jnp.inf, logits)
    m2 = jnp.max(rest, axis=0, keepdims=True)
    i2 = jnp.min(jnp.where(rest == m2, row, LANES), axis=0, keepdims=True)
    e = jnp.exp(m2 - m1)
    g1 = 1.0 / (1.0 + e)
    g2 = e / (1.0 + e)
    oh1 = row == i1
    oh2 = row == i2
    oh = jnp.where(oh1 | oh2, 1.0, 0.0)
    base = base_ref[...]
    prefix = _dot(oh.astype(BF16), tri_ref[...]) + base
    r1 = jnp.sum(jnp.where(oh1, prefix, 0.0), axis=0, keepdims=True)
    r2 = jnp.sum(jnp.where(oh2, prefix, 0.0), axis=0, keepdims=True)
    base = base + jnp.sum(oh, axis=1, keepdims=True)
    count_ref[...] = base[:, :LANES]
    base_ref[...] = base
    fields = [i1.astype(F32), i2.astype(F32), g1, g2, r1, r2]
    info_t = jnp.zeros(logits.shape, F32)
    for c, f in enumerate(fields):
        info_t = jnp.where(row == c, f, info_t)
    info_t_ref[...] = info_t[:ROUTE_COLS, :]
    info_ref[...] = info_t.T[:, :ROUTE_COLS]


def _router(y, h, w_out, ln, wh, wl, b):
    T = h.shape[0]
    tb = MOE_TOK
    nb = T // tb
    r = np.arange(tb)
    tri = jnp.asarray((r[:, None] < r[None, :]).astype(np.float32), dtype=BF16)
    tok = lambda n: pl.BlockSpec((tb, n), lambda i: (i, 0))
    return pl.pallas_call(
        _router_kernel,
        grid=(nb,),
        in_specs=[tok(SSM_DINNER), tok(D_MODEL), _resident(w_out.shape), _resident((1, D_MODEL)),
                  _resident(wh.shape), _resident(wl.shape), _resident(b.shape), _resident(tri.shape)],
        out_specs=[tok(D_MODEL), tok(D_MODEL // 2), tok(ROUTE_COLS),
                   pl.BlockSpec((ROUTE_COLS, tb), lambda i: (0, i)), pl.BlockSpec((LANES, LANES), lambda i: (0, 0))],
        out_shape=[jax.ShapeDtypeStruct((T, D_MODEL), F32), jax.ShapeDtypeStruct((T, D_MODEL // 2), jnp.uint32),
                   jax.ShapeDtypeStruct((T, ROUTE_COLS), F32), jax.ShapeDtypeStruct((ROUTE_COLS, T), F32),
                   jax.ShapeDtypeStruct((LANES, LANES), F32)],
        scratch_shapes=[pltpu.VMEM((LANES, tb), F32)],
        compiler_params=_cparams(1, 32),
        name="ssm_out_router",
    )(y, h, w_out, ln, wh, wl, b, tri)


def _sc_gather(table, idx):
    n_rows, width = idx.shape[0], table.shape[1]
    workers = SC_CORES * SC_SUBCORES
    nbuf = 2
    assert n_rows % (workers * SC_CHUNK * nbuf) == 0
    per_worker = n_rows // workers
    mesh = plsc.VectorSubcoreMesh(core_axis_name="c", subcore_axis_name="s")

    @functools.partial(
        pl.kernel, mesh=mesh,
        out_type=jax.ShapeDtypeStruct((n_rows, width), table.dtype),
        scratch_types=[pltpu.VMEM((nbuf, SC_CHUNK), jnp.int32),
                       pltpu.VMEM((nbuf, SC_CHUNK, width), table.dtype),
                       pltpu.SemaphoreType.DMA((nbuf,)),
                       pltpu.SemaphoreType.DMA((nbuf,))],
    )
    def gather_rows(table_hbm, idx_hbm, out_hbm, idx_v, rows_v, gsem, wsem):
        base = (lax.axis_index("s") * SC_CORES + lax.axis_index("c")) * per_worker

        def gather(c, b):
            off = pl.multiple_of(base + c * SC_CHUNK, 8)
            pltpu.sync_copy(idx_hbm.at[pl.ds(off, SC_CHUNK)], idx_v.at[b])
            return pltpu.make_async_copy(table_hbm.at[idx_v.at[b]], rows_v.at[b], gsem.at[b])

        def write(c, b):
            off = pl.multiple_of(base + c * SC_CHUNK, 8)
            return pltpu.make_async_copy(rows_v.at[b], out_hbm.at[pl.ds(off, SC_CHUNK)], wsem.at[b])

        @pl.loop(0, per_worker // SC_CHUNK, step=nbuf)
        def _(c0):
            copies = [gather(c0 + b, b) for b in range(nbuf)]
            for cp in copies:
                cp.start()
            writes = []
            for b, cp in enumerate(copies):
                cp.wait()
                writes.append(write(c0 + b, b))
                writes[-1].start()
            for wr in writes:
                wr.wait()

    return gather_rows(table, idx)


def _sc_scatter_pairs(table, pos, n_out):
    n_rows, width = table.shape
    workers = SC_CORES * SC_SUBCORES
    nbuf = 2
    assert n_rows % (workers * SC_CHUNK * nbuf) == 0
    per_worker = n_rows // workers
    mesh = plsc.VectorSubcoreMesh(core_axis_name="c", subcore_axis_name="s")

    @functools.partial(
        pl.kernel, mesh=mesh,
        out_type=jax.ShapeDtypeStruct((n_out, width), table.dtype),
        scratch_types=[pltpu.VMEM((nbuf, 2, SC_CHUNK), jnp.int32),
                       pltpu.VMEM((nbuf, SC_CHUNK, width), table.dtype),
                       pltpu.SemaphoreType.DMA((nbuf,)),
                       pltpu.SemaphoreType.DMA((nbuf,))],
    )
    def scatter_rows(table_hbm, pos_hbm, out_hbm, idx_v, rows_v, rsem, wsem):
        base = (lax.axis_index("s") * SC_CORES + lax.axis_index("c")) * per_worker

        def read(c, b):
            off = pl.multiple_of(base + c * SC_CHUNK, 8)
            return pltpu.make_async_copy(table_hbm.at[pl.ds(off, SC_CHUNK)], rows_v.at[b], rsem.at[b])

        def write(b, j):
            return pltpu.make_async_copy(rows_v.at[b], out_hbm.at[idx_v.at[b, j]], wsem.at[b])

        @pl.loop(0, per_worker // SC_CHUNK, step=nbuf)
        def _(c0):
            reads = [read(c0 + b, b) for b in range(nbuf)]
            for rd in reads:
                rd.start()
            for b in range(nbuf):
                pltpu.sync_copy(pos_hbm.at[base // SC_CHUNK + c0 + b], idx_v.at[b])
            writes = []
            for b, rd in enumerate(reads):
                rd.wait()
                for j in range(2):
                    writes.append(write(b, j))
                    writes[-1].start()
            for wr in writes:
                wr.wait()

    return scatter_rows(table, pos)


def _expert_kernel(texp_ref, trows_ref, tfirst_ref, x_ref, wg_ref, wu_ref, wd_ref, out_ref,
                   acc_ref, xb_ref, wgc_ref, wuc_ref, wdc_ref):
    i = pl.program_id(0)
    f = pl.program_id(1)
    nf = pl.num_programs(1)
    n = D_MODEL // 2

    @pl.when(trows_ref[i] > 0)
    def _():
        @pl.when(f == 0)
        def _():
            acc_ref[...] = jnp.zeros_like(acc_ref)
            row = lax.broadcasted_iota(jnp.int32, x_ref.shape, 0)
            hi, lo = _unpack_halves(jnp.where(row < trows_ref[i], x_ref[...], jnp.uint32(0)))
            xb_ref[:, :n] = hi.astype(BF16)
            xb_ref[:, n:] = lo.astype(BF16)

        @pl.when(tfirst_ref[i] == 1)
        def _():
            wgc_ref[f] = wg_ref[0].astype(BF16)
            wuc_ref[f] = wu_ref[0].astype(BF16)
            wdc_ref[f] = wd_ref[0].astype(BF16)

        def ffn_chunk(m):
            x = xb_ref[0:m, :]
            down = None
            for c in range(0, MOE_FC, MOE_FC // 2):
                cs = slice(c, c + MOE_FC // 2)
                act = (_silu(_dot(x, wgc_ref[f, :, cs])) * _dot(x, wuc_ref[f, :, cs])).astype(BF16)
                part = _dot(act, wdc_ref[f, cs, :])
                down = part if down is None else down + part
            acc_ref[0:m, :] += down

        for m in range(MOE_ROWS_STEP, MOE_SLOT + 1, MOE_ROWS_STEP):
            @pl.when((trows_ref[i] > m - MOE_ROWS_STEP) & (trows_ref[i] <= m))
            def _(m=m):
                ffn_chunk(m)

        @pl.when(f == nf - 1)
        def _():
            out_ref[...] = _pack_halves(acc_ref[...].astype(BF16))

    @pl.when((trows_ref[i] == 0) & (f == nf - 1))
    def _():
        out_ref[...] = jnp.zeros_like(out_ref)


def _experts(xs, w_gu, w_down, tile_expert, tile_rows, tile_first):
    n_tiles = xs.shape[0] // MOE_SLOT
    nf = FFN_EXPERT // MOE_FC
    chunk = lambda i, f, tf: jnp.where(tf[i] == 1, f, nf - 1)
    return pl.pallas_call(
        _expert_kernel,
        grid_spec=pltpu.PrefetchScalarGridSpec(
            num_scalar_prefetch=3, grid=(n_tiles, nf),
            in_specs=[pl.BlockSpec((MOE_SLOT, D_MODEL // 2), lambda i, f, te, tr, tf: (i, 0)),
                      pl.BlockSpec((1, D_MODEL, MOE_FC), lambda i, f, te, tr, tf: (te[i], 0, chunk(i, f, tf))),
                      pl.BlockSpec((1, D_MODEL, MOE_FC), lambda i, f, te, tr, tf: (te[i], 0, nf + chunk(i, f, tf))),
                      pl.BlockSpec((1, MOE_FC, D_MODEL), lambda i, f, te, tr, tf: (te[i], chunk(i, f, tf), 0))],
            out_specs=pl.BlockSpec((MOE_SLOT, D_MODEL // 2), lambda i, f, te, tr, tf: (i, 0)),
            scratch_shapes=[pltpu.VMEM((MOE_SLOT, D_MODEL), F32), pltpu.VMEM((MOE_SLOT, D_MODEL), BF16),
                            pltpu.VMEM((nf, D_MODEL, MOE_FC), BF16), pltpu.VMEM((nf, D_MODEL, MOE_FC), BF16),
                            pltpu.VMEM((nf, MOE_FC, D_MODEL), BF16)]),
        out_shape=jax.ShapeDtypeStruct(xs.shape, jnp.uint32),
        compiler_params=_cparams(2, 58),
        name="moe_experts",
    )(tile_expert, tile_rows, tile_first, xs, w_gu, w_gu, w_down)


def _finalize_kernel(y0_ref, y1_ref, info_ref, h_ref, fn_ref, out_ref):
    n = D_MODEL // 2
    info = info_ref[...]
    g0 = info[:, 2:3]
    g1 = info[:, 3:4]
    a_hi, a_lo = _unpack_halves(y0_ref[...])
    b_hi, b_lo = _unpack_halves(y1_ref[...])
    left = h_ref[:, :n] + g0 * a_hi + g1 * b_hi
    right = h_ref[:, n:] + g0 * a_lo + g1 * b_lo
    ms = (jnp.sum(left * left, axis=-1, keepdims=True)
          + jnp.sum(right * right, axis=-1, keepdims=True)) * (1.0 / D_MODEL)
    scale = lax.rsqrt(ms + EPS)
    out_ref[:, :n] = left * scale * fn_ref[:, :n]
    out_ref[:, n:] = right * scale * fn_ref[:, n:]


def _finalize(yg, info, h3, final_norm, tok0):
    n = yg.shape[0] // 2
    tb = MOE_TOK
    nb = n // tb
    b0 = tok0 // tb
    tok = lambda w: pl.BlockSpec((tb, w), lambda i: (b0 + i, 0))
    loc = lambda off: pl.BlockSpec((tb, D_MODEL // 2), lambda i: (i + off, 0))
    return pl.pallas_call(
        _finalize_kernel,
        grid=(nb,),
        in_specs=[loc(0), loc(nb), tok(info.shape[1]), tok(D_MODEL), _resident((1, D_MODEL))],
        out_specs=pl.BlockSpec((tb, D_MODEL), lambda i: (i, 0)),
        out_shape=jax.ShapeDtypeStruct((n, D_MODEL), F32),
        compiler_params=_cparams(1, 32),
        name="moe_finalize",
    )(yg, yg, info, h3, final_norm)


def _moe(y_ssm, h2, w_out, ln, wr_hi, wr_lo, rb, w_gu, w_down, final_norm, n_a):
    T = h2.shape[0]
    n_tiles = 2 * T // MOE_SLOT + N_EXPERTS
    h3, xn, info, info_t, counts = _router(y_ssm, h2, w_out, ln, wr_hi, wr_lo, rb)

    counts = counts[:N_EXPERTS, 0].astype(jnp.int32)
    tiles_e = (counts + MOE_SLOT - 1) // MOE_SLOT
    tiles_cum = jnp.cumsum(tiles_e)
    gstart = (tiles_cum - tiles_e) * MOE_SLOT
    routed = info_t.astype(jnp.int32)
    start_of = lambda e: sum(jnp.where(e == x, gstart[x], 0) for x in range(N_EXPERTS))
    pos0 = start_of(routed[0]) + routed[4]
    pos1 = start_of(routed[1]) + routed[5]
    tile_ids = jnp.arange(n_tiles, dtype=jnp.int32)
    tile_expert = jnp.minimum(jnp.sum(tile_ids[:, None] >= tiles_cum[None, :], axis=1), N_EXPERTS - 1)
    tile_expert = tile_expert.astype(jnp.int32)
    tile_rows = jnp.clip(counts[tile_expert] - (tile_ids * MOE_SLOT - gstart[tile_expert]), 0, MOE_SLOT)
    tile_rows = jnp.where(tile_ids < tiles_cum[-1], tile_rows, 0).astype(jnp.int32)
    pos = jnp.stack([pos0.reshape(-1, SC_CHUNK), pos1.reshape(-1, SC_CHUNK)], axis=1)

    xs = _sc_scatter_pairs(xn, pos, n_tiles * MOE_SLOT)
    tile_first = ((tile_ids * MOE_SLOT == gstart[tile_expert]) & (tile_rows > 0)).astype(jnp.int32)
    ys = _experts(xs, w_gu, w_down, tile_expert, tile_rows, tile_first)
    outs = []
    for lo, hi in ((0, n_a), (n_a, T)):
        yg = _sc_gather(ys, jnp.concatenate([pos0[lo:hi], pos1[lo:hi]]))
        outs.append(_finalize(yg, info, h3, final_norm, lo))
    return outs


def _seq_flags(seq_lens, block):
    first, last = [], []
    for n in seq_lens:
        nb = n // block
        first += [1] + [0] * (nb - 1)
        last += [0] * (nb - 1) + [1]
    return np.asarray(first, np.int32), np.asarray(last, np.int32)


def _trunk(xa, xb, seq_lens, ln_mix0, ret_w_in, ret_w_out, ln_ffn0, ffn_w_gu, ffn_w_down, ln_mix1, ssm_w_in,
           ssm_conv_w, ssm_conv_b, ssm_dt_bias, ssm_A_log, ssm_D, ssm_norm_w, ssm_w_out, ln_ffn1,
           moe_router_w, moe_router_b, moe_w_gu, moe_w_down, final_norm):
    assert all(n % b == 0 for n in seq_lens for b in (SCAN_BLOCK, SSD_BLOCK, MOE_TOK, TOK_TILE, PROJ_TILE))
    row = lambda v: v.astype(F32).reshape(1, -1)
    first, last = _seq_flags(seq_lens, SCAN_BLOCK)
    reset_f = jnp.asarray(first)
    reset_b = jnp.asarray(last[::-1].copy())
    ssd_first, ssd_last = _seq_flags(seq_lens, SSD_BLOCK)

    half = RET_DK // 2
    inv = ROPE_BASE ** (-jnp.arange(half, dtype=F32) / half)
    ang_base = jnp.arange(0, max(seq_lens), TOK_TILE).astype(F32)[:, None, None] * inv[None, None, :]
    ang_off = jnp.arange(TOK_TILE).astype(F32)[:, None] * inv[None, :]
    pos_block = jnp.asarray(np.concatenate([np.arange(n // TOK_TILE) for n in seq_lens]).astype(np.int32))
    q, k, v, g = _ret_in(xa, xb, row(ln_mix0), ret_w_in.astype(BF16), jnp.cos(ang_base), jnp.sin(ang_base),
                         jnp.cos(ang_off), jnp.sin(ang_off), pos_block)
    y = _ret_scan(q, k, v, g, reset_f, reset_b)
    h2 = _post0(y, xa, xb, ret_w_out.astype(BF16), row(ln_ffn0), ffn_w_gu.astype(BF16),
                ffn_w_down.astype(BF16))

    wz = ssm_w_in[:, :SSM_DINNER].astype(BF16)
    wx = ssm_w_in[:, SSM_DINNER:SSM_DINNER + SSM_CONV_DIM].astype(BF16)
    wdt = jnp.pad(ssm_w_in[:, SSM_DINNER + SSM_CONV_DIM:], ((0, 0), (0, LANES - 2 * SSM_HEADS)))
    wdh, wdl = _split2(wdt)
    pad_row = lambda v: jnp.pad(v.astype(F32).reshape(1, -1), ((0, 0), (0, LANES - 2 * SSM_HEADS)))
    gate, xbc, dt = _ssm_in(h2, row(ln_mix1), wz, wx, wdh, wdl, pad_row(ssm_dt_bias))
    xact = _conv(xbc, ssm_conv_w.reshape(SSM_CONV, SSM_CONV_DIM).astype(F32), row(ssm_conv_b),
                 jnp.asarray(ssd_first), jnp.asarray(ssd_last))
    a_row = pad_row(-jnp.exp(ssm_A_log.astype(F32)))
    dskip = jnp.repeat(ssm_D.astype(F32), SSM_HEADDIM).reshape(1, SSM_DINNER)
    y = _ssd(xact, dt, gate, a_row, dskip, row(ssm_norm_w), jnp.asarray(ssd_first),
             jnp.asarray(ssd_last[::-1].copy()))

    wr = jnp.pad(moe_router_w.astype(F32).T, ((0, LANES - N_EXPERTS), (0, 0)))
    wr_hi, wr_lo = _split2(wr)
    rb = jnp.pad(moe_router_b.astype(F32).reshape(-1, 1), ((0, LANES - N_EXPERTS), (0, 0)), constant_values=-1e30)
    rb = jnp.broadcast_to(rb, (LANES, MOE_TOK))
    return _moe(y, h2, ssm_w_out.astype(BF16), row(ln_ffn1), wr_hi, wr_lo, rb,
                moe_w_gu, moe_w_down, row(final_norm), xa.shape[0])


def kernel(x_prompt, x_sample, ln_mix0, ret_w_in, ret_w_out, ln_ffn0, ffn_w_gu, ffn_w_down, ln_mix1, ssm_w_in,
           ssm_conv_w, ssm_conv_b, ssm_dt_bias, ssm_A_log, ssm_D, ssm_norm_w, ssm_w_out, ln_ffn1, moe_router_w,
           moe_router_b, moe_w_gu, moe_w_down, final_norm):
    seq_lens = [x_prompt.shape[1]] * x_prompt.shape[0] + [x_sample.shape[1]] * x_sample.shape[0]
    out_a, out_b = _trunk(x_prompt.reshape(-1, D_MODEL), x_sample.reshape(-1, D_MODEL), seq_lens, ln_mix0, ret_w_in, ret_w_out, ln_ffn0, ffn_w_gu, ffn_w_down, ln_mix1, ssm_w_in,
                 ssm_conv_w, ssm_conv_b, ssm_dt_bias, ssm_A_log, ssm_D, ssm_norm_w, ssm_w_out, ln_ffn1,
                 moe_router_w, moe_router_b, moe_w_gu, moe_w_down, final_norm)
    return (out_a.reshape(x_prompt.shape), out_b.reshape(x_sample.shape))
```

```python
import functools

import numpy as np
import jax
import jax.numpy as jnp
from jax import lax
from jax.experimental import pallas as pl
from jax.experimental.pallas import tpu as pltpu
from jax.experimental.pallas import tpu_sc as plsc

F32 = jnp.float32
BF16 = jnp.bfloat16

D_MODEL = 1024
EPS = 1e-6
RET_HEADS = 4
RET_DK = 256
RET_DV = 512
ROPE_BASE = 10000.0
SSM_DINNER = 2048
SSM_HEADDIM = 64
SSM_HEADS = 32
SSM_GROUPS = 4
SSM_DSTATE = 128
SSM_CONV = 5
SSM_CONV_DIM = 3072
SSM_GROUP_W = SSM_DINNER // SSM_GROUPS
SLAB = SSM_GROUP_W
CONV_ROWS = 128
CONV_WINDOW = 256
HALO = 16
FFN_DENSE = 2816
FFN_DENSE_CHUNK = 1408
N_EXPERTS = 8
FFN_EXPERT = 3584

CHUNK = 128
RET_CHUNK = 256
LANES = 128
TOK_TILE = 512
PROJ_TILE = 1024
SCAN_BLOCK = 1024
SSD_BLOCK = 1024
MOE_TOK = 512
FINAL_TILE = 1024
MOE_SLOT = 1024
MOE_FC = 512
MOE_ROWS_STEP = 256
ROUTE_COLS = 8
SC_CORES = 2
SC_SUBCORES = 16
SC_CHUNK = 64


def _cparams(n_axes, vmem_mb):
    return pltpu.CompilerParams(dimension_semantics=("arbitrary",) * n_axes,
                                vmem_limit_bytes=vmem_mb << 20)


def _resident(shape):
    nd = len(shape)
    return pl.BlockSpec(shape, lambda *_: (0,) * nd, pipeline_mode=pl.Buffered(1))


def _rms(x):
    return x * lax.rsqrt(jnp.mean(x * x, axis=-1, keepdims=True) + EPS)


def _silu(x):
    return x * jax.nn.sigmoid(x)


def _dot(a, b):
    return jnp.dot(a, b, preferred_element_type=F32)


def _dot_nt(a, b):
    return lax.dot_general(a, b, (((1,), (1,)), ((), ())), preferred_element_type=F32)


def _dot_tn(a, b):
    return lax.dot_general(a, b, (((0,), (0,)), ((), ())), preferred_element_type=F32)


def _split2(x):
    hi = x.astype(BF16)
    lo = (x - hi.astype(F32)).astype(BF16)
    return hi, lo


def _split3(x):
    hi = x.astype(BF16)
    r = x - hi.astype(F32)
    mid = r.astype(BF16)
    lo = (r - mid.astype(F32)).astype(BF16)
    return hi, mid, lo


def _two_stream_specs(tm, n_a):
    return (pl.BlockSpec((tm, D_MODEL), lambda i, *_: (jnp.minimum(i, n_a - 1), 0)),
            pl.BlockSpec((tm, D_MODEL), lambda i, *_: (jnp.maximum(i - n_a, 0), 0)))


def _ret_in_kernel(pos_ref, xa_ref, xb_ref, ln_ref, w_ref, cb_ref, sb_ref, co_ref, so_ref,
                   q_ref, k_ref, v_ref, g_ref, *, n_a):
    x = jnp.where(pl.program_id(0) < n_a, xa_ref[...], xb_ref[...])
    xn = (_rms(x) * ln_ref[...]).astype(BF16)
    cb, sb, co, so = cb_ref[0], sb_ref[0], co_ref[...], so_ref[...]
    cos = cb * co - sb * so
    sin = sb * co + cb * so
    half = RET_DK // 2

    def rotary(dst, col0, scale):
        for h in range(RET_HEADS):
            c = col0 + h * RET_DK
            p = _dot(xn, w_ref[:, c:c + RET_DK])
            p1, p2 = p[:, :half], p[:, half:]
            dst[:, h * RET_DK:h * RET_DK + half] = ((p1 * cos - p2 * sin) * scale).astype(BF16)
            dst[:, h * RET_DK + half:(h + 1) * RET_DK] = ((p1 * sin + p2 * cos) * scale).astype(BF16)

    rotary(q_ref, 0, 1.0)
    rotary(k_ref, D_MODEL, RET_DK ** -0.5)
    for j in range(2 * D_MODEL // 512):
        v_ref[:, j * 512:(j + 1) * 512] = _dot(
            xn, w_ref[:, 2 * D_MODEL + j * 512:2 * D_MODEL + (j + 1) * 512]).astype(BF16)
        g_ref[:, j * 512:(j + 1) * 512] = _silu(_dot(
            xn, w_ref[:, 4 * D_MODEL + j * 512:4 * D_MODEL + (j + 1) * 512])).astype(BF16)


def _ret_in(xa, xb, ln, w_in, cos_base, sin_base, cos_off, sin_off, pos_block):
    T = xa.shape[0] + xb.shape[0]
    tm = TOK_TILE
    n_a = xa.shape[0] // tm
    tok = lambda n: pl.BlockSpec((tm, n), lambda i, pb: (i, 0))
    rot = pl.BlockSpec((1, 1, LANES), lambda i, pb: (pb[i], 0, 0))
    return pl.pallas_call(
        functools.partial(_ret_in_kernel, n_a=n_a),
        grid_spec=pltpu.PrefetchScalarGridSpec(
            num_scalar_prefetch=1, grid=(T // tm,),
            in_specs=[*_two_stream_specs(tm, n_a), _resident((1, D_MODEL)), _resident(w_in.shape), rot, rot,
                      _resident(cos_off.shape), _resident(sin_off.shape)],
            out_specs=[tok(D_MODEL), tok(D_MODEL), tok(2 * D_MODEL), tok(2 * D_MODEL)]),
        out_shape=[jax.ShapeDtypeStruct((T, D_MODEL), BF16), jax.ShapeDtypeStruct((T, D_MODEL), BF16),
                   jax.ShapeDtypeStruct((T, 2 * D_MODEL), BF16), jax.ShapeDtypeStruct((T, 2 * D_MODEL), BF16)],
        compiler_params=_cparams(1, 48),
        name="ret_in",
    )(pos_block, xa, xb, ln, w_in, cos_base, sin_base, cos_off, sin_off)


def _ret_scan_kernel(reset_ref, *refs, reverse, chunk_decay):
    if reverse:
        q_ref, k_ref, v_ref, wq_ref, wk_ref, out_ref, state_ref = refs
    else:
        q_ref, k_ref, v_ref, ob_ref, g_ref, wq_ref, wk_ref, din_ref, out_ref, state_ref = refs
    nch = SCAN_BLOCK // RET_CHUNK

    @pl.when(reset_ref[pl.program_id(0)] == 1)
    def _():
        state_ref[...] = jnp.zeros_like(state_ref)

    def chunk_body(ci, carry):
        cj = nch - 1 - ci if reverse else ci
        rows = pl.ds(pl.multiple_of(cj * RET_CHUNK, RET_CHUNK), RET_CHUNK)
        for h in range(RET_HEADS):
            kcols = slice(h * RET_DK, (h + 1) * RET_DK)
            vcols = slice(h * RET_DV, (h + 1) * RET_DV)
            qh = q_ref[rows, kcols]
            kh = k_ref[rows, kcols]
            vh = v_ref[rows, vcols]
            state = state_ref[h]
            o = _dot(qh, state.astype(BF16)) * wq_ref[h]
            if reverse:
                out_ref[rows, vcols] = o.astype(BF16)
            else:
                s = _dot_nt(qh, kh) * din_ref[h]
                o = o + _dot(s.astype(BF16), vh) + ob_ref[rows, vcols].astype(F32)
                out_ref[rows, vcols] = (g_ref[rows, vcols].astype(F32) * _rms(o)).astype(BF16)
            ks = (kh.astype(F32) * wk_ref[h]).astype(BF16)
            state_ref[h] = state * chunk_decay[h] + _dot_tn(ks, vh)
        return carry

    lax.fori_loop(0, nch, chunk_body, 0)


def _ret_tables():
    lg = np.log1p(-np.power(2.0, -5.0 - np.arange(RET_HEADS, dtype=np.float32))).astype(np.float32)
    c = RET_CHUNK
    a = np.arange(c, dtype=np.float32)
    col = lambda e, w: np.broadcast_to(np.exp(lg[:, None, None] * e[None, :, None]),
                                       (RET_HEADS, c, w)).astype(np.float32)
    dist = np.abs(a[:, None] - a[None, :])
    return dict(
        din=np.exp(lg[:, None, None] * dist[None]).astype(np.float32),
        wq_f=col(a + 1.0, RET_DV), wk_f=col(c - 1.0 - a, RET_DK),
        wq_b=col(c - a, RET_DV), wk_b=col(a, RET_DK),
        chunk_decay=tuple(float(v) for v in np.exp(lg * c)),
    )


def _ret_scan(q, k, v, g, reset_f, reset_b):
    T = q.shape[0]
    nb = T // SCAN_BLOCK
    tb = _ret_tables()
    state = pltpu.VMEM((RET_HEADS, RET_DK, RET_DV), F32)
    out_shape = jax.ShapeDtypeStruct((T, 2 * D_MODEL), BF16)

    def specs(imap):
        blk = lambda n: pl.BlockSpec((SCAN_BLOCK, n), imap)
        return blk(D_MODEL), blk(D_MODEL), blk(2 * D_MODEL)

    rev = lambda i, r: (nb - 1 - i, 0)
    bq, bk, bv = specs(rev)
    o_b = pl.pallas_call(
        functools.partial(_ret_scan_kernel, reverse=True, chunk_decay=tb["chunk_decay"]),
        grid_spec=pltpu.PrefetchScalarGridSpec(
            num_scalar_prefetch=1, grid=(nb,),
            in_specs=[bq, bk, bv, _resident(tb["wq_b"].shape), _resident(tb["wk_b"].shape)],
            out_specs=bv, scratch_shapes=[state]),
        out_shape=out_shape, compiler_params=_cparams(1, 48), name="ret_scan_bwd",
    )(reset_b, q, k, v, tb["wq_b"], tb["wk_b"])

    fwd = lambda i, r: (i, 0)
    bq, bk, bv = specs(fwd)
    return pl.pallas_call(
        functools.partial(_ret_scan_kernel, reverse=False, chunk_decay=tb["chunk_decay"]),
        grid_spec=pltpu.PrefetchScalarGridSpec(
            num_scalar_prefetch=1, grid=(nb,),
            in_specs=[bq, bk, bv, bv, bv, _resident(tb["wq_f"].shape), _resident(tb["wk_f"].shape),
                      _resident(tb["din"].shape)],
            out_specs=bv, scratch_shapes=[state]),
        out_shape=out_shape, compiler_params=_cparams(1, 56), name="ret_scan_fwd",
    )(reset_f, q, k, v, o_b, g, tb["wq_f"], tb["wk_f"], tb["din"])


def _post0_kernel(y_ref, xa_ref, xb_ref, wo_ref, ln_ref, wgu_ref, wd_ref, out_ref, *, n_a):
    x = jnp.where(pl.program_id(0) < n_a, xa_ref[...], xb_ref[...])
    h1 = x + _dot(y_ref[...], wo_ref[...])
    xn = (_rms(h1) * ln_ref[...]).astype(BF16)
    acc = h1
    fc = FFN_DENSE_CHUNK
    for c in range(FFN_DENSE // fc):
        gate = _dot(xn, wgu_ref[:, c * fc:(c + 1) * fc])
        up = _dot(xn, wgu_ref[:, FFN_DENSE + c * fc:FFN_DENSE + (c + 1) * fc])
        act = (_silu(gate) * up).astype(BF16)
        acc = acc + _dot(act, wd_ref[c * fc:(c + 1) * fc, :])
    out_ref[...] = acc


def _post0(y, xa, xb, w_out, ln, w_gu, w_down):
    T = y.shape[0]
    tm = TOK_TILE
    n_a = xa.shape[0] // tm
    tok = lambda n: pl.BlockSpec((tm, n), lambda i: (i, 0))
    return pl.pallas_call(
        functools.partial(_post0_kernel, n_a=n_a),
        grid=(T // tm,),
        in_specs=[tok(2 * D_MODEL), *_two_stream_specs(tm, n_a), _resident(w_out.shape),
                  _resident((1, D_MODEL)), _resident(w_gu.shape), _resident(w_down.shape)],
        out_specs=tok(D_MODEL),
        out_shape=jax.ShapeDtypeStruct((T, D_MODEL), F32),
        compiler_params=_cparams(1, 52),
        name="ret_out_ffn",
    )(y, xa, xb, w_out, ln, w_gu, w_down)


def _ssm_in_kernel(x_ref, ln_ref, wz_ref, wx_ref, wd_ref, wdh_ref, bias_ref, gate_ref, xbc_ref, dt_ref):
    xh, xl = _split2(_rms(x_ref[...]) * ln_ref[...])
    for j in range(SSM_DINNER // SLAB):
        gate_ref[j] = _silu(_dot(xh, wz_ref[:, j * SLAB:(j + 1) * SLAB])).astype(BF16)
    for j in range(SSM_CONV_DIM // SLAB):
        xbc_ref[:, j * SLAB:(j + 1) * SLAB] = _dot(xh, wx_ref[:, j * SLAB:(j + 1) * SLAB]).astype(BF16)
    both = _dot(xh, wd_ref[...])
    x = both[:, :LANES] + both[:, LANES:] + _dot(xl, wdh_ref[...]) + bias_ref[...]
    dt_ref[...] = jnp.maximum(x, 0.0) + jnp.log1p(jnp.exp(-jnp.abs(x)))


def _ssm_in(x, ln, wz, wx, wdh, wdl, bias_row):
    T = x.shape[0]
    tm = PROJ_TILE
    n_gate = SSM_DINNER // SLAB
    wd = jnp.concatenate([wdh, wdl], axis=1)
    tok = lambda n: pl.BlockSpec((tm, n), lambda i: (i, 0))
    return pl.pallas_call(
        _ssm_in_kernel,
        grid=(T // tm,),
        in_specs=[tok(D_MODEL), _resident((1, D_MODEL)), _resident(wz.shape), _resident(wx.shape),
                  _resident(wd.shape), _resident(wdh.shape), _resident(bias_row.shape)],
        out_specs=[pl.BlockSpec((n_gate, tm, SLAB), lambda i: (0, i, 0)), tok(SSM_CONV_DIM), tok(LANES)],
        out_shape=[jax.ShapeDtypeStruct((n_gate, T, SLAB), BF16), jax.ShapeDtypeStruct((T, SSM_CONV_DIM), BF16),
                   jax.ShapeDtypeStruct((T, LANES), F32)],
        compiler_params=_cparams(1, 52),
        name="ssm_in",
    )(x, ln, wz, wx, wd, wdh, bias_row)


def _conv_kernel(first_ref, last_ref, prev_ref, main_ref, next_ref, shift_ref, w_ref, b_ref, out_ref, ext_ref):
    i = pl.program_id(0)
    rb = SSD_BLOCK
    ext_rows = ext_ref.shape[0]

    @pl.when(i == 0)
    def _():
        ext_ref[rb + 2 * HALO:ext_rows, :] = jnp.zeros((ext_rows - rb - 2 * HALO, SSM_CONV_DIM), BF16)

    zero = jnp.zeros((HALO, SSM_CONV_DIM), BF16)
    ext_ref[0:HALO, :] = jnp.where(first_ref[i] == 1, zero, prev_ref[...])
    ext_ref[HALO:HALO + rb, :] = main_ref[...]
    ext_ref[HALO + rb:2 * HALO + rb, :] = jnp.where(last_ref[i] == 1, zero, next_ref[...])
    for s in range(SSM_CONV_DIM // SLAB):
        cs = slice(s * SLAB, (s + 1) * SLAB)
        for r0 in range(0, rb, CONV_ROWS):
            taps = _dot(shift_ref[...], ext_ref[r0:r0 + CONV_WINDOW, cs])
            acc = jnp.broadcast_to(b_ref[:, cs], (CONV_ROWS, SLAB))
            for j in range(SSM_CONV):
                acc = acc + taps[j * CONV_ROWS:(j + 1) * CONV_ROWS] * w_ref[j:j + 1, cs]
            out_ref[s, r0:r0 + CONV_ROWS, :] = _silu(acc).astype(BF16)


def _conv(xbc, conv_w, conv_b, first, last):
    T = xbc.shape[0]
    rb = SSD_BLOCK
    nb = T // rb
    per = rb // HALO
    nh = T // HALO
    r = np.arange(CONV_ROWS)
    shift = np.zeros((SSM_CONV * CONV_ROWS, CONV_WINDOW), np.float32)
    for j in range(SSM_CONV):
        shift[j * CONV_ROWS + r, HALO + r + j - SSM_CONV // 2] = 1.0
    shift = jnp.asarray(shift, dtype=BF16)
    ext_rows = rb - CONV_ROWS + CONV_WINDOW
    return pl.pallas_call(
        _conv_kernel,
        grid_spec=pltpu.PrefetchScalarGridSpec(
            num_scalar_prefetch=2, grid=(nb,),
            in_specs=[
                pl.BlockSpec((HALO, SSM_CONV_DIM), lambda i, f, l: (jnp.maximum(i * per - 1, 0), 0)),
                pl.BlockSpec((rb, SSM_CONV_DIM), lambda i, f, l: (i, 0)),
                pl.BlockSpec((HALO, SSM_CONV_DIM), lambda i, f, l: (jnp.minimum((i + 1) * per, nh - 1), 0)),
                _resident(shift.shape), _resident(conv_w.shape), _resident(conv_b.shape)],
            out_specs=pl.BlockSpec((SSM_CONV_DIM // SLAB, rb, SLAB), lambda i, f, l: (0, i, 0)),
            scratch_shapes=[pltpu.VMEM((ext_rows, SSM_CONV_DIM), BF16)]),
        out_shape=jax.ShapeDtypeStruct((SSM_CONV_DIM // SLAB, T, SLAB), BF16),
        compiler_params=_cparams(1, 48),
        name="ssm_conv",
    )(first, last, xbc, xbc, xbc, shift, conv_w, conv_b)


def _ssd_tables():
    r = np.arange(CHUNK)
    tri_l = (r[None, :] <= r[:, None]).astype(np.float32)
    tri_u = (r[None, :] >= r[:, None]).astype(np.float32)
    c = np.arange(SSM_DINNER) // SSM_HEADDIM
    j = np.arange(LANES)
    e_f = (j[:, None] == c[None, :]).astype(np.float32)
    e_b = (j[:, None] == (c[None, :] + SSM_HEADS)).astype(np.float32)
    as_bf16 = lambda x: jnp.asarray(x, dtype=BF16)
    return dict(tri_l3=as_bf16(np.concatenate([tri_l] * 3, axis=1)),
                tri_u3=as_bf16(np.concatenate([tri_u] * 3, axis=1)),
                e2_f=as_bf16(np.concatenate([e_f, e_f], axis=0)),
                e2_b=as_bf16(np.concatenate([e_b, e_b], axis=0)))


def _ssd_chunk_scalars(dt, a_ref, tri_l3_ref, tri_u3_ref):
    a = dt * a_ref[...]
    a3 = jnp.concatenate(_split3(a), axis=0)
    cum_f = _dot(tri_l3_ref[...], a3)
    cum_b = _dot(tri_u3_ref[...], a3)
    lane = lax.broadcasted_iota(jnp.int32, (CHUNK, LANES), 1)
    cum = jnp.where(lane < SSM_HEADS, cum_f, cum_b)
    tot = cum_f[CHUNK - 1:CHUNK, :]
    return cum, tot


def _expand(e2_ref, *rows):
    x = jnp.concatenate(rows, axis=0)
    hi, lo = _split2(x)
    return _dot(jnp.concatenate([hi, lo], axis=1), e2_ref[...])


def _ssd_bwd_kernel(reset_ref, xa_ref, dt_ref, a_ref, tri_l3_ref, tri_u3_ref, e2_ref,
                    yb_ref, state_ref):
    nch = SSD_BLOCK // CHUNK

    @pl.when(reset_ref[pl.program_id(0)] == 1)
    def _():
        state_ref[...] = jnp.zeros_like(state_ref)

    def chunk_body(ci, carry):
        rows = pl.ds(pl.multiple_of((nch - 1 - ci) * CHUNK, CHUNK), CHUNK)
        dt = dt_ref[rows, :]
        cum, tot = _ssd_chunk_scalars(dt, a_ref, tri_l3_ref, tri_u3_ref)
        ex = _expand(e2_ref, dt * jnp.exp(tot - cum), jnp.exp(cum),
                     jnp.broadcast_to(jnp.exp(tot), (16, LANES)))
        for g in range(SSM_GROUPS):
            gc = slice(g * SSM_GROUP_W, (g + 1) * SSM_GROUP_W)
            bm = xa_ref[SSM_GROUPS, rows, g * SSM_DSTATE:(g + 1) * SSM_DSTATE]
            cm = xa_ref[SSM_GROUPS + 1, rows, g * SSM_DSTATE:(g + 1) * SSM_DSTATE]
            state = state_ref[g]
            yb_ref[rows, gc] = (_dot(cm, state.astype(BF16)) * ex[CHUNK:2 * CHUNK, gc]).astype(BF16)
            xw = (xa_ref[g, rows, :].astype(F32) * ex[0:CHUNK, gc]).astype(BF16)
            state_ref[g] = state * ex[2 * CHUNK:2 * CHUNK + 1, gc] + _dot_tn(bm, xw)
        return carry

    lax.fori_loop(0, nch, chunk_body, 0)


def _ssd_fwd_kernel(reset_ref, xa_ref, dt_ref, gate_ref, yb_ref, a_ref, tri_l3_ref, tri_u3_ref,
                    e2_ref, dskip_ref, nw_ref, out_ref, state_ref, y_ref):
    nch = SSD_BLOCK // CHUNK

    @pl.when(reset_ref[pl.program_id(0)] == 1)
    def _():
        state_ref[...] = jnp.zeros_like(state_ref)

    def chunk_body(ci, carry):
        rows = pl.ds(pl.multiple_of(ci * CHUNK, CHUNK), CHUNK)
        dt = dt_ref[rows, :]
        cum, tot = _ssd_chunk_scalars(dt, a_ref, tri_l3_ref, tri_u3_ref)
        ex = _expand(e2_ref, dt * jnp.exp(tot - cum), jnp.exp(cum),
                     jnp.broadcast_to(jnp.exp(tot), (16, LANES)))
        row_t = (cum - jnp.log(dt)).T
        li = lax.broadcasted_iota(jnp.int32, (CHUNK, CHUNK), 0)
        si = lax.broadcasted_iota(jnp.int32, (CHUNK, CHUNK), 1)
        lower = li >= si
        first_half = si < SSM_HEADDIM
        for g in range(SSM_GROUPS):
            gc = slice(g * SSM_GROUP_W, (g + 1) * SSM_GROUP_W)
            bm = xa_ref[SSM_GROUPS, rows, g * SSM_DSTATE:(g + 1) * SSM_DSTATE]
            cm = xa_ref[SSM_GROUPS + 1, rows, g * SSM_DSTATE:(g + 1) * SSM_DSTATE]
            cb = _dot_nt(cm, bm)
            heads_per_group = SSM_HEADS // SSM_GROUPS
            for pair in range(heads_per_group // 2):
                mats = []
                for hh in range(2):
                    h = g * heads_per_group + 2 * pair + hh
                    hb = SSM_HEADS + h
                    seg = jnp.where(lower, cum[:, h:h + 1] - row_t[h:h + 1, :],
                                    cum[:, hb:hb + 1] - row_t[hb:hb + 1, :])
                    mats.append((cb * jnp.exp(seg)).astype(BF16))
                pc = slice(2 * pair * SSM_HEADDIM, (2 * pair + 2) * SSM_HEADDIM)
                xp = xa_ref[g, rows, pc]
                zero = jnp.zeros_like(xp)
                rhs = jnp.concatenate([jnp.where(first_half, xp, zero), jnp.where(first_half, zero, xp)], axis=0)
                y_ref[:, pc] = _dot(jnp.concatenate(mats, axis=1), rhs)
            state = state_ref[g]
            xs = xa_ref[g, rows, :].astype(F32)
            y = (y_ref[...] + _dot(cm, state.astype(BF16)) * ex[CHUNK:2 * CHUNK, gc]
                 + yb_ref[rows, gc].astype(F32) + xs * dskip_ref[:, gc])
            y = y * gate_ref[g, rows, :].astype(F32)
            out_ref[rows, gc] = (_rms(y) * nw_ref[:, gc]).astype(BF16)
            xw = (xs * ex[0:CHUNK, gc]).astype(BF16)
            state_ref[g] = state * ex[2 * CHUNK:2 * CHUNK + 1, gc] + _dot_tn(bm, xw)
        return carry

    lax.fori_loop(0, nch, chunk_body, 0)


def _ssd(xa, dt, gate, a_row, dskip, norm_w, reset_f, reset_b):
    T = dt.shape[0]
    nb = T // SSD_BLOCK
    tb = _ssd_tables()
    state = pltpu.VMEM((SSM_GROUPS, SSM_DSTATE, SSM_GROUP_W), F32)
    consts = [a_row, tb["tri_l3"], tb["tri_u3"]]
    const_specs = [_resident(c.shape) for c in consts]
    out_shape = jax.ShapeDtypeStruct((T, SSM_DINNER), BF16)

    rev = lambda i, r: (nb - 1 - i, 0)
    blk = lambda n, imap: pl.BlockSpec((SSD_BLOCK, n), imap)
    slabs = lambda a, imap: pl.BlockSpec((a.shape[0], SSD_BLOCK, SLAB), lambda i, r: (0, imap(i, r)[0], 0))
    y_b = pl.pallas_call(
        _ssd_bwd_kernel,
        grid_spec=pltpu.PrefetchScalarGridSpec(
            num_scalar_prefetch=1, grid=(nb,),
            in_specs=[slabs(xa, rev), blk(LANES, rev)] + const_specs + [_resident(tb["e2_b"].shape)],
            out_specs=blk(SSM_DINNER, rev), scratch_shapes=[state]),
        out_shape=out_shape, compiler_params=_cparams(1, 40), name="ssd_bwd",
    )(reset_b, xa, dt, *consts, tb["e2_b"])

    fwd = lambda i, r: (i, 0)
    return pl.pallas_call(
        _ssd_fwd_kernel,
        grid_spec=pltpu.PrefetchScalarGridSpec(
            num_scalar_prefetch=1, grid=(nb,),
            in_specs=[slabs(xa, fwd), blk(LANES, fwd), slabs(gate, fwd), blk(SSM_DINNER, fwd)]
            + const_specs + [_resident(tb["e2_f"].shape), _resident(dskip.shape), _resident(norm_w.shape)],
            out_specs=blk(SSM_DINNER, fwd),
            scratch_shapes=[state, pltpu.VMEM((CHUNK, SSM_GROUP_W), F32)]),
        out_shape=out_shape, compiler_params=_cparams(1, 52), name="ssd_fwd",
    )(reset_f, xa, dt, gate, y_b, *consts, tb["e2_f"], dskip, norm_w)


def _pack_halves(x):
    n = x.shape[1] // 2
    hi = lax.bitcast_convert_type(x[:, :n].astype(F32), jnp.uint32)
    lo = lax.bitcast_convert_type(x[:, n:].astype(F32), jnp.uint32)
    return hi | (lo >> 16)


def _unpack_halves(w):
    hi = lax.bitcast_convert_type(w & jnp.uint32(0xFFFF0000), F32)
    lo = lax.bitcast_convert_type(w << 16, F32)
    return hi, lo


def _router_kernel(y_ref, h_ref, wo_ref, ln_ref, wh_ref, wl_ref, b_ref, tri_ref,
                   h3_ref, xn_ref, info_ref, info_t_ref, count_ref, base_ref):
    @pl.when(pl.program_id(0) == 0)
    def _():
        base_ref[...] = jnp.zeros_like(base_ref)

    h3 = h_ref[...] + _dot(y_ref[...], wo_ref[...])
    h3_ref[...] = h3
    xh, xl = _split2(_rms(h3) * ln_ref[...])
    xn_ref[...] = _pack_halves(xh)
    logits = _dot_nt(wh_ref[...], xh) + _dot_nt(wh_ref[...], xl) + _dot_nt(wl_ref[...], xh) + b_ref[...]
    row = lax.broadcasted_iota(jnp.int32, logits.shape, 0)
    m1 = jnp.max(logits, axis=0, keepdims=True)
    i1 = jnp.min(jnp.where(logits == m1, row, LANES), axis=0, keepdims=True)
    rest = jnp.where(row == i1, -jnp.inf, logits)
    m2 = jnp.max(rest, axis=0, keepdims=True)
    i2 = jnp.min(jnp.where(rest == m2, row, LANES), axis=0, keepdims=True)
    e = jnp.exp(m2 - m1)
    g1 = 1.0 / (1.0 + e)
    g2 = e / (1.0 + e)
    oh1 = row == i1
    oh2 = row == i2
    oh = jnp.where(oh1 | oh2, 1.0, 0.0)
    base = base_ref[...]
    prefix = _dot(oh.astype(BF16), tri_ref[...]) + base
    r1 = jnp.sum(jnp.where(oh1, prefix, 0.0), axis=0, keepdims=True)
    r2 = jnp.sum(jnp.where(oh2, prefix, 0.0), axis=0, keepdims=True)
    base = base + jnp.sum(oh, axis=1, keepdims=True)
    count_ref[...] = base[:, :LANES]
    base_ref[...] = base
    fields = [i1.astype(F32), i2.astype(F32), g1, g2, r1, r2]
    info_t = jnp.zeros(logits.shape, F32)
    for c, f in enumerate(fields):
        info_t = jnp.where(row == c, f, info_t)
    info_t_ref[...] = info_t[:ROUTE_COLS, :]
    info_ref[...] = info_t.T[:, :ROUTE_COLS]


def _router(y, h, w_out, ln, wh, wl, b):
    T = h.shape[0]
    tb = MOE_TOK
    nb = T // tb
    r = np.arange(tb)
    tri = jnp.asarray((r[:, None] < r[None, :]).astype(np.float32), dtype=BF16)
    tok = lambda n: pl.BlockSpec((tb, n), lambda i: (i, 0))
    return pl.pallas_call(
        _router_kernel,
        grid=(nb,),
        in_specs=[tok(SSM_DINNER), tok(D_MODEL), _resident(w_out.shape), _resident((1, D_MODEL)),
                  _resident(wh.shape), _resident(wl.shape), _resident(b.shape), _resident(tri.shape)],
        out_specs=[tok(D_MODEL), tok(D_MODEL // 2), tok(ROUTE_COLS),
                   pl.BlockSpec((ROUTE_COLS, tb), lambda i: (0, i)), pl.BlockSpec((LANES, LANES), lambda i: (0, 0))],
        out_shape=[jax.ShapeDtypeStruct((T, D_MODEL), F32), jax.ShapeDtypeStruct((T, D_MODEL // 2), jnp.uint32),
                   jax.ShapeDtypeStruct((T, ROUTE_COLS), F32), jax.ShapeDtypeStruct((ROUTE_COLS, T), F32),
                   jax.ShapeDtypeStruct((LANES, LANES), F32)],
        scratch_shapes=[pltpu.VMEM((LANES, tb), F32)],
        compiler_params=_cparams(1, 32),
        name="ssm_out_router",
    )(y, h, w_out, ln, wh, wl, b, tri)


def _sc_gather(table, idx):
    n_rows, width = idx.shape[0], table.shape[1]
    workers = SC_CORES * SC_SUBCORES
    nbuf = 2
    assert n_rows % (workers * SC_CHUNK * nbuf) == 0
    per_worker = n_rows // workers
    mesh = plsc.VectorSubcoreMesh(core_axis_name="c", subcore_axis_name="s")

    @functools.partial(
        pl.kernel, mesh=mesh,
        out_type=jax.ShapeDtypeStruct((n_rows, width), table.dtype),
        scratch_types=[pltpu.VMEM((nbuf, SC_CHUNK), jnp.int32),
                       pltpu.VMEM((nbuf, SC_CHUNK, width), table.dtype),
                       pltpu.SemaphoreType.DMA((nbuf,)),
                       pltpu.SemaphoreType.DMA((nbuf,))],
    )
    def gather_rows(table_hbm, idx_hbm, out_hbm, idx_v, rows_v, gsem, wsem):
        base = (lax.axis_index("s") * SC_CORES + lax.axis_index("c")) * per_worker

        def gather(c, b):
            off = pl.multiple_of(base + c * SC_CHUNK, 8)
            pltpu.sync_copy(idx_hbm.at[pl.ds(off, SC_CHUNK)], idx_v.at[b])
            return pltpu.make_async_copy(table_hbm.at[idx_v.at[b]], rows_v.at[b], gsem.at[b])

        def write(c, b):
            off = pl.multiple_of(base + c * SC_CHUNK, 8)
            return pltpu.make_async_copy(rows_v.at[b], out_hbm.at[pl.ds(off, SC_CHUNK)], wsem.at[b])

        @pl.loop(0, per_worker // SC_CHUNK, step=nbuf)
        def _(c0):
            copies = [gather(c0 + b, b) for b in range(nbuf)]
            for cp in copies:
                cp.start()
            writes = []
            for b, cp in enumerate(copies):
                cp.wait()
                writes.append(write(c0 + b, b))
                writes[-1].start()
            for wr in writes:
                wr.wait()

    return gather_rows(table, idx)


def _sc_scatter_pairs(table, pos, n_out):
    n_rows, width = table.shape
    workers = SC_CORES * SC_SUBCORES
    nbuf = 2
    assert n_rows % (workers * SC_CHUNK * nbuf) == 0
    per_worker = n_rows // workers
    mesh = plsc.VectorSubcoreMesh(core_axis_name="c", subcore_axis_name="s")

    @functools.partial(
        pl.kernel, mesh=mesh,
        out_type=jax.ShapeDtypeStruct((n_out, width), table.dtype),
        scratch_types=[pltpu.VMEM((nbuf, 2, SC_CHUNK), jnp.int32),
                       pltpu.VMEM((nbuf, SC_CHUNK, width), table.dtype),
                       pltpu.SemaphoreType.DMA((nbuf,)),
                       pltpu.SemaphoreType.DMA((nbuf,))],
    )
    def scatter_rows(table_hbm, pos_hbm, out_hbm, idx_v, rows_v, rsem, wsem):
        base = (lax.axis_index("s") * SC_CORES + lax.axis_index("c")) * per_worker

        def read(c, b):
            off = pl.multiple_of(base + c * SC_CHUNK, 8)
            return pltpu.make_async_copy(table_hbm.at[pl.ds(off, SC_CHUNK)], rows_v.at[b], rsem.at[b])

        def write(b, j):
            return pltpu.make_async_copy(rows_v.at[b], out_hbm.at[idx_v.at[b, j]], wsem.at[b])

        @pl.loop(0, per_worker // SC_CHUNK, step=nbuf)
        def _(c0):
            reads = [read(c0 + b, b) for b in range(nbuf)]
            for rd in reads:
                rd.start()
            for b in range(nbuf):
                pltpu.sync_copy(pos_hbm.at[base // SC_CHUNK + c0 + b], idx_v.at[b])
            writes = []
            for b, rd in enumerate(reads):
                rd.wait()
                for j in range(2):
                    writes.append(write(b, j))
                    writes[-1].start()
            for wr in writes:
                wr.wait()

    return scatter_rows(table, pos)


def _expert_kernel(texp_ref, trows_ref, tfirst_ref, x_ref, wg_ref, wu_ref, wd_ref, out_ref,
                   acc_ref, xb_ref, wgc_ref, wuc_ref, wdc_ref):
    i = pl.program_id(0)
    f = pl.program_id(1)
    nf = pl.num_programs(1)
    n = D_MODEL // 2

    @pl.when(trows_ref[i] > 0)
    def _():
        @pl.when(f == 0)
        def _():
            acc_ref[...] = jnp.zeros_like(acc_ref)
            row = lax.broadcasted_iota(jnp.int32, x_ref.shape, 0)
            hi, lo = _unpack_halves(jnp.where(row < trows_ref[i], x_ref[...], jnp.uint32(0)))
            xb_ref[:, :n] = hi.astype(BF16)
            xb_ref[:, n:] = lo.astype(BF16)

        @pl.when(tfirst_ref[i] == 1)
        def _():
            wgc_ref[f] = wg_ref[0].astype(BF16)
            wuc_ref[f] = wu_ref[0].astype(BF16)
            wdc_ref[f] = wd_ref[0].astype(BF16)

        def ffn_chunk(m):
            x = xb_ref[0:m, :]
            down = None
            for c in range(0, MOE_FC, MOE_FC // 2):
                cs = slice(c, c + MOE_FC // 2)
                act = (_silu(_dot(x, wgc_ref[f, :, cs])) * _dot(x, wuc_ref[f, :, cs])).astype(BF16)
                part = _dot(act, wdc_ref[f, cs, :])
                down = part if down is None else down + part
            acc_ref[0:m, :] += down

        for m in range(MOE_ROWS_STEP, MOE_SLOT + 1, MOE_ROWS_STEP):
            @pl.when((trows_ref[i] > m - MOE_ROWS_STEP) & (trows_ref[i] <= m))
            def _(m=m):
                ffn_chunk(m)

        @pl.when(f == nf - 1)
        def _():
            out_ref[...] = _pack_halves(acc_ref[...].astype(BF16))

    @pl.when((trows_ref[i] == 0) & (f == nf - 1))
    def _():
        out_ref[...] = jnp.zeros_like(out_ref)


def _experts(xs, w_gu, w_down, tile_expert, tile_rows, tile_first):
    n_tiles = xs.shape[0] // MOE_SLOT
    nf = FFN_EXPERT // MOE_FC
    chunk = lambda i, f, tf: jnp.where(tf[i] == 1, f, nf - 1)
    return pl.pallas_call(
        _expert_kernel,
        grid_spec=pltpu.PrefetchScalarGridSpec(
            num_scalar_prefetch=3, grid=(n_tiles, nf),
            in_specs=[pl.BlockSpec((MOE_SLOT, D_MODEL // 2), lambda i, f, te, tr, tf: (i, 0)),
                      pl.BlockSpec((1, D_MODEL, MOE_FC), lambda i, f, te, tr, tf: (te[i], 0, chunk(i, f, tf))),
                      pl.BlockSpec((1, D_MODEL, MOE_FC), lambda i, f, te, tr, tf: (te[i], 0, nf + chunk(i, f, tf))),
                      pl.BlockSpec((1, MOE_FC, D_MODEL), lambda i, f, te, tr, tf: (te[i], chunk(i, f, tf), 0))],
            out_specs=pl.BlockSpec((MOE_SLOT, D_MODEL // 2), lambda i, f, te, tr, tf: (i, 0)),
            scratch_shapes=[pltpu.VMEM((MOE_SLOT, D_MODEL), F32), pltpu.VMEM((MOE_SLOT, D_MODEL), BF16),
                            pltpu.VMEM((nf, D_MODEL, MOE_FC), BF16), pltpu.VMEM((nf, D_MODEL, MOE_FC), BF16),
                            pltpu.VMEM((nf, MOE_FC, D_MODEL), BF16)]),
        out_shape=jax.ShapeDtypeStruct(xs.shape, jnp.uint32),
        compiler_params=_cparams(2, 58),
        name="moe_experts",
    )(tile_expert, tile_rows, tile_first, xs, w_gu, w_gu, w_down)


def _finalize_kernel(y0_ref, y1_ref, info_ref, h_ref, fn_ref, out_ref):
    n = D_MODEL // 2
    info = info_ref[...]
    g0 = info[:, 2:3]
    g1 = info[:, 3:4]
    a_hi, a_lo = _unpack_halves(y0_ref[...])
    b_hi, b_lo = _unpack_halves(y1_ref[...])
    left = h_ref[:, :n] + g0 * a_hi + g1 * b_hi
    right = h_ref[:, n:] + g0 * a_lo + g1 * b_lo
    ms = (jnp.sum(left * left, axis=-1, keepdims=True)
          + jnp.sum(right * right, axis=-1, keepdims=True)) * (1.0 / D_MODEL)
    scale = lax.rsqrt(ms + EPS)
    out_ref[:, :n] = left * scale * fn_ref[:, :n]
    out_ref[:, n:] = right * scale * fn_ref[:, n:]


def _finalize(yg, info, h3, final_norm, tok0):
    n = yg.shape[0] // 2
    tb = FINAL_TILE
    nb = n // tb
    b0 = tok0 // tb
    tok = lambda w: pl.BlockSpec((tb, w), lambda i: (b0 + i, 0))
    loc = lambda off: pl.BlockSpec((tb, D_MODEL // 2), lambda i: (i + off, 0))
    return pl.pallas_call(
        _finalize_kernel,
        grid=(nb,),
        in_specs=[loc(0), loc(nb), tok(info.shape[1]), tok(D_MODEL), _resident((1, D_MODEL))],
        out_specs=pl.BlockSpec((tb, D_MODEL), lambda i: (i, 0)),
        out_shape=jax.ShapeDtypeStruct((n, D_MODEL), F32),
        compiler_params=_cparams(1, 32),
        name="moe_finalize",
    )(yg, yg, info, h3, final_norm)


def _moe(y_ssm, h2, w_out, ln, wr_hi, wr_lo, rb, w_gu, w_down, final_norm, n_a):
    T = h2.shape[0]
    n_tiles = 2 * T // MOE_SLOT + N_EXPERTS
    h3, xn, info, info_t, counts = _router(y_ssm, h2, w_out, ln, wr_hi, wr_lo, rb)

    counts = counts[:N_EXPERTS, 0].astype(jnp.int32)
    tiles_e = (counts + MOE_SLOT - 1) // MOE_SLOT
    tiles_cum = jnp.cumsum(tiles_e)
    gstart = (tiles_cum - tiles_e) * MOE_SLOT
    routed = info_t.astype(jnp.int32)
    start_of = lambda e: sum(jnp.where(e == x, gstart[x], 0) for x in range(N_EXPERTS))
    pos0 = start_of(routed[0]) + routed[4]
    pos1 = start_of(routed[1]) + routed[5]
    tile_ids = jnp.arange(n_tiles, dtype=jnp.int32)
    tile_expert = jnp.minimum(jnp.sum(tile_ids[:, None] >= tiles_cum[None, :], axis=1), N_EXPERTS - 1)
    tile_expert = tile_expert.astype(jnp.int32)
    tile_rows = jnp.clip(counts[tile_expert] - (tile_ids * MOE_SLOT - gstart[tile_expert]), 0, MOE_SLOT)
    tile_rows = jnp.where(tile_ids < tiles_cum[-1], tile_rows, 0).astype(jnp.int32)
    pos = jnp.stack([pos0.reshape(-1, SC_CHUNK), pos1.reshape(-1, SC_CHUNK)], axis=1)

    xs = _sc_scatter_pairs(xn, pos, n_tiles * MOE_SLOT)
    tile_first = ((tile_ids * MOE_SLOT == gstart[tile_expert]) & (tile_rows > 0)).astype(jnp.int32)
    ys = _experts(xs, w_gu, w_down, tile_expert, tile_rows, tile_first)
    outs = []
    for lo, hi in ((0, n_a), (n_a, T)):
        yg = _sc_gather(ys, jnp.concatenate([pos0[lo:hi], pos1[lo:hi]]))
        outs.append(_finalize(yg, info, h3, final_norm, lo))
    return outs


def _seq_flags(seq_lens, block):
    first, last = [], []
    for n in seq_lens:
        nb = n // block
        first += [1] + [0] * (nb - 1)
        last += [0] * (nb - 1) + [1]
    return np.asarray(first, np.int32), np.asarray(last, np.int32)


def _trunk(xa, xb, seq_lens, ln_mix0, ret_w_in, ret_w_out, ln_ffn0, ffn_w_gu, ffn_w_down, ln_mix1, ssm_w_in,
           ssm_conv_w, ssm_conv_b, ssm_dt_bias, ssm_A_log, ssm_D, ssm_norm_w, ssm_w_out, ln_ffn1,
           moe_router_w, moe_router_b, moe_w_gu, moe_w_down, final_norm):
    assert all(n % b == 0 for n in seq_lens for b in (SCAN_BLOCK, SSD_BLOCK, MOE_TOK, TOK_TILE, PROJ_TILE))
    row = lambda v: v.astype(F32).reshape(1, -1)
    first, last = _seq_flags(seq_lens, SCAN_BLOCK)
    reset_f = jnp.asarray(first)
    reset_b = jnp.asarray(last[::-1].copy())
    ssd_first, ssd_last = _seq_flags(seq_lens, SSD_BLOCK)

    half = RET_DK // 2
    inv = ROPE_BASE ** (-jnp.arange(half, dtype=F32) / half)
    ang_base = jnp.arange(0, max(seq_lens), TOK_TILE).astype(F32)[:, None, None] * inv[None, None, :]
    ang_off = jnp.arange(TOK_TILE).astype(F32)[:, None] * inv[None, :]
    pos_block = jnp.asarray(np.concatenate([np.arange(n // TOK_TILE) for n in seq_lens]).astype(np.int32))
    q, k, v, g = _ret_in(xa, xb, row(ln_mix0), ret_w_in.astype(BF16), jnp.cos(ang_base), jnp.sin(ang_base),
                         jnp.cos(ang_off), jnp.sin(ang_off), pos_block)
    y = _ret_scan(q, k, v, g, reset_f, reset_b)
    h2 = _post0(y, xa, xb, ret_w_out.astype(BF16), row(ln_ffn0), ffn_w_gu.astype(BF16),
                ffn_w_down.astype(BF16))

    wz = ssm_w_in[:, :SSM_DINNER].astype(BF16)
    wx = ssm_w_in[:, SSM_DINNER:SSM_DINNER + SSM_CONV_DIM].astype(BF16)
    wdt = jnp.pad(ssm_w_in[:, SSM_DINNER + SSM_CONV_DIM:], ((0, 0), (0, LANES - 2 * SSM_HEADS)))
    wdh, wdl = _split2(wdt)
    pad_row = lambda v: jnp.pad(v.astype(F32).reshape(1, -1), ((0, 0), (0, LANES - 2 * SSM_HEADS)))
    gate, xbc, dt = _ssm_in(h2, row(ln_mix1), wz, wx, wdh, wdl, pad_row(ssm_dt_bias))
    xact = _conv(xbc, ssm_conv_w.reshape(SSM_CONV, SSM_CONV_DIM).astype(F32), row(ssm_conv_b),
                 jnp.asarray(ssd_first), jnp.asarray(ssd_last))
    a_row = pad_row(-jnp.exp(ssm_A_log.astype(F32)))
    dskip = jnp.repeat(ssm_D.astype(F32), SSM_HEADDIM).reshape(1, SSM_DINNER)
    y = _ssd(xact, dt, gate, a_row, dskip, row(ssm_norm_w), jnp.asarray(ssd_first),
             jnp.asarray(ssd_last[::-1].copy()))

    wr = jnp.pad(moe_router_w.astype(F32).T, ((0, LANES - N_EXPERTS), (0, 0)))
    wr_hi, wr_lo = _split2(wr)
    rb = jnp.pad(moe_router_b.astype(F32).reshape(-1, 1), ((0, LANES - N_EXPERTS), (0, 0)), constant_values=-1e30)
    rb = jnp.broadcast_to(rb, (LANES, MOE_TOK))
    return _moe(y, h2, ssm_w_out.astype(BF16), row(ln_ffn1), wr_hi, wr_lo, rb,
                moe_w_gu, moe_w_down, row(final_norm), xa.shape[0])


def kernel(x_prompt, x_sample, ln_mix0, ret_w_in, ret_w_out, ln_ffn0, ffn_w_gu, ffn_w_down, ln_mix1, ssm_w_in,
           ssm_conv_w, ssm_conv_b, ssm_dt_bias, ssm_A_log, ssm_D, ssm_norm_w, ssm_w_out, ln_ffn1, moe_router_w,
           moe_router_b, moe_w_gu, moe_w_down, final_norm):
    seq_lens = [x_prompt.shape[1]] * x_prompt.shape[0] + [x_sample.shape[1]] * x_sample.shape[0]
    out_a, out_b = _trunk(x_prompt.reshape(-1, D_MODEL), x_sample.reshape(-1, D_MODEL), seq_lens, ln_mix0, ret_w_in, ret_w_out, ln_ffn0, ffn_w_gu, ffn_w_down, ln_mix1, ssm_w_in,
                 ssm_conv_w, ssm_conv_b, ssm_dt_bias, ssm_A_log, ssm_D, ssm_norm_w, ssm_w_out, ln_ffn1,
                 moe_router_w, moe_router_b, moe_w_gu, moe_w_down, final_norm)
    return (out_a.reshape(x_prompt.shape), out_b.reshape(x_sample.shape))
```

```python
import functools

import numpy as np
import jax
import jax.numpy as jnp
from jax import lax
from jax.experimental import pallas as pl
from jax.experimental.pallas import tpu as pltpu
from jax.experimental.pallas import tpu_sc as plsc

F32 = jnp.float32
BF16 = jnp.bfloat16

D_MODEL = 1024
EPS = 1e-6
RET_HEADS = 4
RET_DK = 256
RET_DV = 512
ROPE_BASE = 10000.0
SSM_DINNER = 2048
SSM_HEADDIM = 64
SSM_HEADS = 32
SSM_GROUPS = 4
SSM_DSTATE = 128
SSM_CONV = 5
SSM_CONV_DIM = 3072
SSM_GROUP_W = SSM_DINNER // SSM_GROUPS
SLAB = SSM_GROUP_W
CONV_ROWS = 128
CONV_WINDOW = 256
HALO = 16
FFN_DENSE = 2816
FFN_DENSE_CHUNK = 1408
N_EXPERTS = 8
FFN_EXPERT = 3584

CHUNK = 128
RET_CHUNK = 256
LANES = 128
TOK_TILE = 512
PROJ_TILE = 1024
SCAN_BLOCK = 1024
SSD_BLOCK = 1024
MOE_TOK = 1024
FINAL_TILE = 1024
MOE_SLOT = 1024
MOE_FC = 512
MOE_ROWS_STEP = 256
ROUTE_COLS = 8
SC_CORES = 2
SC_SUBCORES = 16
SC_CHUNK = 64


def _cparams(n_axes, vmem_mb):
    return pltpu.CompilerParams(dimension_semantics=("arbitrary",) * n_axes,
                                vmem_limit_bytes=vmem_mb << 20)


def _resident(shape):
    nd = len(shape)
    return pl.BlockSpec(shape, lambda *_: (0,) * nd, pipeline_mode=pl.Buffered(1))


def _rms(x):
    return x * lax.rsqrt(jnp.mean(x * x, axis=-1, keepdims=True) + EPS)


def _silu(x):
    return x * jax.nn.sigmoid(x)


def _dot(a, b):
    return jnp.dot(a, b, preferred_element_type=F32)


def _dot_nt(a, b):
    return lax.dot_general(a, b, (((1,), (1,)), ((), ())), preferred_element_type=F32)


def _dot_tn(a, b):
    return lax.dot_general(a, b, (((0,), (0,)), ((), ())), preferred_element_type=F32)


def _split2(x):
    hi = x.astype(BF16)
    lo = (x - hi.astype(F32)).astype(BF16)
    return hi, lo


def _split3(x):
    hi = x.astype(BF16)
    r = x - hi.astype(F32)
    mid = r.astype(BF16)
    lo = (r - mid.astype(F32)).astype(BF16)
    return hi, mid, lo


def _two_stream_specs(tm, n_a):
    return (pl.BlockSpec((tm, D_MODEL), lambda i, *_: (jnp.minimum(i, n_a - 1), 0)),
            pl.BlockSpec((tm, D_MODEL), lambda i, *_: (jnp.maximum(i - n_a, 0), 0)))


def _ret_in_kernel(pos_ref, xa_ref, xb_ref, ln_ref, w_ref, cb_ref, sb_ref, co_ref, so_ref,
                   q_ref, k_ref, v_ref, g_ref, *, n_a):
    x = jnp.where(pl.program_id(0) < n_a, xa_ref[...], xb_ref[...])
    xn = (_rms(x) * ln_ref[...]).astype(BF16)
    cb, sb, co, so = cb_ref[0], sb_ref[0], co_ref[...], so_ref[...]
    cos = cb * co - sb * so
    sin = sb * co + cb * so
    half = RET_DK // 2

    def rotary(dst, col0, scale):
        for h in range(RET_HEADS):
            c = col0 + h * RET_DK
            p = _dot(xn, w_ref[:, c:c + RET_DK])
            p1, p2 = p[:, :half], p[:, half:]
            dst[:, h * RET_DK:h * RET_DK + half] = ((p1 * cos - p2 * sin) * scale).astype(BF16)
            dst[:, h * RET_DK + half:(h + 1) * RET_DK] = ((p1 * sin + p2 * cos) * scale).astype(BF16)

    rotary(q_ref, 0, 1.0)
    rotary(k_ref, D_MODEL, RET_DK ** -0.5)
    for j in range(2 * D_MODEL // 512):
        v_ref[:, j * 512:(j + 1) * 512] = _dot(
            xn, w_ref[:, 2 * D_MODEL + j * 512:2 * D_MODEL + (j + 1) * 512]).astype(BF16)
        g_ref[:, j * 512:(j + 1) * 512] = _silu(_dot(
            xn, w_ref[:, 4 * D_MODEL + j * 512:4 * D_MODEL + (j + 1) * 512])).astype(BF16)


def _ret_in(xa, xb, ln, w_in, cos_base, sin_base, cos_off, sin_off, pos_block):
    T = xa.shape[0] + xb.shape[0]
    tm = TOK_TILE
    n_a = xa.shape[0] // tm
    tok = lambda n: pl.BlockSpec((tm, n), lambda i, pb: (i, 0))
    rot = pl.BlockSpec((1, 1, LANES), lambda i, pb: (pb[i], 0, 0))
    return pl.pallas_call(
        functools.partial(_ret_in_kernel, n_a=n_a),
        grid_spec=pltpu.PrefetchScalarGridSpec(
            num_scalar_prefetch=1, grid=(T // tm,),
            in_specs=[*_two_stream_specs(tm, n_a), _resident((1, D_MODEL)), _resident(w_in.shape), rot, rot,
                      _resident(cos_off.shape), _resident(sin_off.shape)],
            out_specs=[tok(D_MODEL), tok(D_MODEL), tok(2 * D_MODEL), tok(2 * D_MODEL)]),
        out_shape=[jax.ShapeDtypeStruct((T, D_MODEL), BF16), jax.ShapeDtypeStruct((T, D_MODEL), BF16),
                   jax.ShapeDtypeStruct((T, 2 * D_MODEL), BF16), jax.ShapeDtypeStruct((T, 2 * D_MODEL), BF16)],
        compiler_params=_cparams(1, 48),
        name="ret_in",
    )(pos_block, xa, xb, ln, w_in, cos_base, sin_base, cos_off, sin_off)


def _ret_scan_kernel(reset_ref, *refs, reverse, chunk_decay):
    if reverse:
        q_ref, k_ref, v_ref, wq_ref, wk_ref, out_ref, state_ref = refs
    else:
        q_ref, k_ref, v_ref, ob_ref, g_ref, wq_ref, wk_ref, din_ref, out_ref, state_ref = refs
    nch = SCAN_BLOCK // RET_CHUNK

    @pl.when(reset_ref[pl.program_id(0)] == 1)
    def _():
        state_ref[...] = jnp.zeros_like(state_ref)

    def chunk_body(ci, carry):
        cj = nch - 1 - ci if reverse else ci
        rows = pl.ds(pl.multiple_of(cj * RET_CHUNK, RET_CHUNK), RET_CHUNK)
        for h in range(RET_HEADS):
            kcols = slice(h * RET_DK, (h + 1) * RET_DK)
            vcols = slice(h * RET_DV, (h + 1) * RET_DV)
            qh = q_ref[rows, kcols]
            kh = k_ref[rows, kcols]
            vh = v_ref[rows, vcols]
            state = state_ref[h]
            o = _dot(qh, state.astype(BF16)) * wq_ref[h]
            if reverse:
                out_ref[rows, vcols] = o.astype(BF16)
            else:
                s = _dot_nt(qh, kh) * din_ref[h]
                o = o + _dot(s.astype(BF16), vh) + ob_ref[rows, vcols].astype(F32)
                out_ref[rows, vcols] = (g_ref[rows, vcols].astype(F32) * _rms(o)).astype(BF16)
            ks = (kh.astype(F32) * wk_ref[h]).astype(BF16)
            state_ref[h] = state * chunk_decay[h] + _dot_tn(ks, vh)
        return carry

    lax.fori_loop(0, nch, chunk_body, 0)


def _ret_tables():
    lg = np.log1p(-np.power(2.0, -5.0 - np.arange(RET_HEADS, dtype=np.float32))).astype(np.float32)
    c = RET_CHUNK
    a = np.arange(c, dtype=np.float32)
    col = lambda e, w: np.broadcast_to(np.exp(lg[:, None, None] * e[None, :, None]),
                                       (RET_HEADS, c, w)).astype(np.float32)
    dist = np.abs(a[:, None] - a[None, :])
    return dict(
        din=np.exp(lg[:, None, None] * dist[None]).astype(np.float32),
        wq_f=col(a + 1.0, RET_DV), wk_f=col(c - 1.0 - a, RET_DK),
        wq_b=col(c - a, RET_DV), wk_b=col(a, RET_DK),
        chunk_decay=tuple(float(v) for v in np.exp(lg * c)),
    )


def _ret_scan(q, k, v, g, reset_f, reset_b):
    T = q.shape[0]
    nb = T // SCAN_BLOCK
    tb = _ret_tables()
    state = pltpu.VMEM((RET_HEADS, RET_DK, RET_DV), F32)
    out_shape = jax.ShapeDtypeStruct((T, 2 * D_MODEL), BF16)

    def specs(imap):
        blk = lambda n: pl.BlockSpec((SCAN_BLOCK, n), imap)
        return blk(D_MODEL), blk(D_MODEL), blk(2 * D_MODEL)

    rev = lambda i, r: (nb - 1 - i, 0)
    bq, bk, bv = specs(rev)
    o_b = pl.pallas_call(
        functools.partial(_ret_scan_kernel, reverse=True, chunk_decay=tb["chunk_decay"]),
        grid_spec=pltpu.PrefetchScalarGridSpec(
            num_scalar_prefetch=1, grid=(nb,),
            in_specs=[bq, bk, bv, _resident(tb["wq_b"].shape), _resident(tb["wk_b"].shape)],
            out_specs=bv, scratch_shapes=[state]),
        out_shape=out_shape, compiler_params=_cparams(1, 48), name="ret_scan_bwd",
    )(reset_b, q, k, v, tb["wq_b"], tb["wk_b"])

    fwd = lambda i, r: (i, 0)
    bq, bk, bv = specs(fwd)
    return pl.pallas_call(
        functools.partial(_ret_scan_kernel, reverse=False, chunk_decay=tb["chunk_decay"]),
        grid_spec=pltpu.PrefetchScalarGridSpec(
            num_scalar_prefetch=1, grid=(nb,),
            in_specs=[bq, bk, bv, bv, bv, _resident(tb["wq_f"].shape), _resident(tb["wk_f"].shape),
                      _resident(tb["din"].shape)],
            out_specs=bv, scratch_shapes=[state]),
        out_shape=out_shape, compiler_params=_cparams(1, 56), name="ret_scan_fwd",
    )(reset_f, q, k, v, o_b, g, tb["wq_f"], tb["wk_f"], tb["din"])


def _post0_kernel(y_ref, xa_ref, xb_ref, wo_ref, ln_ref, wgu_ref, wd_ref, out_ref, *, n_a):
    x = jnp.where(pl.program_id(0) < n_a, xa_ref[...], xb_ref[...])
    h1 = x + _dot(y_ref[...], wo_ref[...])
    xn = (_rms(h1) * ln_ref[...]).astype(BF16)
    acc = h1
    fc = FFN_DENSE_CHUNK
    for c in range(FFN_DENSE // fc):
        gate = _dot(xn, wgu_ref[:, c * fc:(c + 1) * fc])
        up = _dot(xn, wgu_ref[:, FFN_DENSE + c * fc:FFN_DENSE + (c + 1) * fc])
        act = (_silu(gate) * up).astype(BF16)
        acc = acc + _dot(act, wd_ref[c * fc:(c + 1) * fc, :])
    out_ref[...] = acc


def _post0(y, xa, xb, w_out, ln, w_gu, w_down):
    T = y.shape[0]
    tm = TOK_TILE
    n_a = xa.shape[0] // tm
    tok = lambda n: pl.BlockSpec((tm, n), lambda i: (i, 0))
    return pl.pallas_call(
        functools.partial(_post0_kernel, n_a=n_a),
        grid=(T // tm,),
        in_specs=[tok(2 * D_MODEL), *_two_stream_specs(tm, n_a), _resident(w_out.shape),
                  _resident((1, D_MODEL)), _resident(w_gu.shape), _resident(w_down.shape)],
        out_specs=tok(D_MODEL),
        out_shape=jax.ShapeDtypeStruct((T, D_MODEL), F32),
        compiler_params=_cparams(1, 52),
        name="ret_out_ffn",
    )(y, xa, xb, w_out, ln, w_gu, w_down)


def _ssm_in_kernel(x_ref, ln_ref, wz_ref, wx_ref, wd_ref, wdh_ref, bias_ref, gate_ref, xbc_ref, dt_ref):
    xh, xl = _split2(_rms(x_ref[...]) * ln_ref[...])
    for j in range(SSM_DINNER // SLAB):
        gate_ref[j] = _silu(_dot(xh, wz_ref[:, j * SLAB:(j + 1) * SLAB])).astype(BF16)
    for j in range(SSM_CONV_DIM // SLAB):
        xbc_ref[:, j * SLAB:(j + 1) * SLAB] = _dot(xh, wx_ref[:, j * SLAB:(j + 1) * SLAB]).astype(BF16)
    both = _dot(xh, wd_ref[...])
    x = both[:, :LANES] + both[:, LANES:] + _dot(xl, wdh_ref[...]) + bias_ref[...]
    dt_ref[...] = jnp.maximum(x, 0.0) + jnp.log1p(jnp.exp(-jnp.abs(x)))


def _ssm_in(x, ln, wz, wx, wdh, wdl, bias_row):
    T = x.shape[0]
    tm = PROJ_TILE
    n_gate = SSM_DINNER // SLAB
    wd = jnp.concatenate([wdh, wdl], axis=1)
    tok = lambda n: pl.BlockSpec((tm, n), lambda i: (i, 0))
    return pl.pallas_call(
        _ssm_in_kernel,
        grid=(T // tm,),
        in_specs=[tok(D_MODEL), _resident((1, D_MODEL)), _resident(wz.shape), _resident(wx.shape),
                  _resident(wd.shape), _resident(wdh.shape), _resident(bias_row.shape)],
        out_specs=[pl.BlockSpec((n_gate, tm, SLAB), lambda i: (0, i, 0)), tok(SSM_CONV_DIM), tok(LANES)],
        out_shape=[jax.ShapeDtypeStruct((n_gate, T, SLAB), BF16), jax.ShapeDtypeStruct((T, SSM_CONV_DIM), BF16),
                   jax.ShapeDtypeStruct((T, LANES), F32)],
        compiler_params=_cparams(1, 52),
        name="ssm_in",
    )(x, ln, wz, wx, wd, wdh, bias_row)


def _conv_kernel(first_ref, last_ref, prev_ref, main_ref, next_ref, shift_ref, w_ref, b_ref, out_ref, ext_ref):
    i = pl.program_id(0)
    rb = SSD_BLOCK
    ext_rows = ext_ref.shape[0]

    @pl.when(i == 0)
    def _():
        ext_ref[rb + 2 * HALO:ext_rows, :] = jnp.zeros((ext_rows - rb - 2 * HALO, SSM_CONV_DIM), BF16)

    zero = jnp.zeros((HALO, SSM_CONV_DIM), BF16)
    ext_ref[0:HALO, :] = jnp.where(first_ref[i] == 1, zero, prev_ref[...])
    ext_ref[HALO:HALO + rb, :] = main_ref[...]
    ext_ref[HALO + rb:2 * HALO + rb, :] = jnp.where(last_ref[i] == 1, zero, next_ref[...])
    for s in range(SSM_CONV_DIM // SLAB):
        cs = slice(s * SLAB, (s + 1) * SLAB)
        for r0 in range(0, rb, CONV_ROWS):
            taps = _dot(shift_ref[...], ext_ref[r0:r0 + CONV_WINDOW, cs])
            acc = jnp.broadcast_to(b_ref[:, cs], (CONV_ROWS, SLAB))
            for j in range(SSM_CONV):
                acc = acc + taps[j * CONV_ROWS:(j + 1) * CONV_ROWS] * w_ref[j:j + 1, cs]
            out_ref[s, r0:r0 + CONV_ROWS, :] = _silu(acc).astype(BF16)


def _conv(xbc, conv_w, conv_b, first, last):
    T = xbc.shape[0]
    rb = SSD_BLOCK
    nb = T // rb
    per = rb // HALO
    nh = T // HALO
    r = np.arange(CONV_ROWS)
    shift = np.zeros((SSM_CONV * CONV_ROWS, CONV_WINDOW), np.float32)
    for j in range(SSM_CONV):
        shift[j * CONV_ROWS + r, HALO + r + j - SSM_CONV // 2] = 1.0
    shift = jnp.asarray(shift, dtype=BF16)
    ext_rows = rb - CONV_ROWS + CONV_WINDOW
    return pl.pallas_call(
        _conv_kernel,
        grid_spec=pltpu.PrefetchScalarGridSpec(
            num_scalar_prefetch=2, grid=(nb,),
            in_specs=[
                pl.BlockSpec((HALO, SSM_CONV_DIM), lambda i, f, l: (jnp.maximum(i * per - 1, 0), 0)),
                pl.BlockSpec((rb, SSM_CONV_DIM), lambda i, f, l: (i, 0)),
                pl.BlockSpec((HALO, SSM_CONV_DIM), lambda i, f, l: (jnp.minimum((i + 1) * per, nh - 1), 0)),
                _resident(shift.shape), _resident(conv_w.shape), _resident(conv_b.shape)],
            out_specs=pl.BlockSpec((SSM_CONV_DIM // SLAB, rb, SLAB), lambda i, f, l: (0, i, 0)),
            scratch_shapes=[pltpu.VMEM((ext_rows, SSM_CONV_DIM), BF16)]),
        out_shape=jax.ShapeDtypeStruct((SSM_CONV_DIM // SLAB, T, SLAB), BF16),
        compiler_params=_cparams(1, 48),
        name="ssm_conv",
    )(first, last, xbc, xbc, xbc, shift, conv_w, conv_b)


def _ssd_tables():
    r = np.arange(CHUNK)
    tri_l = (r[None, :] <= r[:, None]).astype(np.float32)
    tri_u = (r[None, :] >= r[:, None]).astype(np.float32)
    c = np.arange(SSM_DINNER) // SSM_HEADDIM
    j = np.arange(LANES)
    e_f = (j[:, None] == c[None, :]).astype(np.float32)
    e_b = (j[:, None] == (c[None, :] + SSM_HEADS)).astype(np.float32)
    as_bf16 = lambda x: jnp.asarray(x, dtype=BF16)
    return dict(tri_l3=as_bf16(np.concatenate([tri_l] * 3, axis=1)),
                tri_u3=as_bf16(np.concatenate([tri_u] * 3, axis=1)),
                e2_f=as_bf16(np.concatenate([e_f, e_f], axis=0)),
                e2_b=as_bf16(np.concatenate([e_b, e_b], axis=0)))


def _ssd_chunk_scalars(dt, a_ref, tri_l3_ref, tri_u3_ref):
    a = dt * a_ref[...]
    a3 = jnp.concatenate(_split3(a), axis=0)
    cum_f = _dot(tri_l3_ref[...], a3)
    cum_b = _dot(tri_u3_ref[...], a3)
    lane = lax.broadcasted_iota(jnp.int32, (CHUNK, LANES), 1)
    cum = jnp.where(lane < SSM_HEADS, cum_f, cum_b)
    tot = cum_f[CHUNK - 1:CHUNK, :]
    return cum, tot


def _expand(e2_ref, *rows):
    x = jnp.concatenate(rows, axis=0)
    hi, lo = _split2(x)
    return _dot(jnp.concatenate([hi, lo], axis=1), e2_ref[...])


def _ssd_bwd_kernel(reset_ref, xa_ref, dt_ref, a_ref, tri_l3_ref, tri_u3_ref, e2_ref,
                    yb_ref, state_ref):
    nch = SSD_BLOCK // CHUNK

    @pl.when(reset_ref[pl.program_id(0)] == 1)
    def _():
        state_ref[...] = jnp.zeros_like(state_ref)

    def chunk_body(ci, carry):
        rows = pl.ds(pl.multiple_of((nch - 1 - ci) * CHUNK, CHUNK), CHUNK)
        dt = dt_ref[rows, :]
        cum, tot = _ssd_chunk_scalars(dt, a_ref, tri_l3_ref, tri_u3_ref)
        ex = _expand(e2_ref, dt * jnp.exp(tot - cum), jnp.exp(cum),
                     jnp.broadcast_to(jnp.exp(tot), (16, LANES)))
        for g in range(SSM_GROUPS):
            gc = slice(g * SSM_GROUP_W, (g + 1) * SSM_GROUP_W)
            bm = xa_ref[SSM_GROUPS, rows, g * SSM_DSTATE:(g + 1) * SSM_DSTATE]
            cm = xa_ref[SSM_GROUPS + 1, rows, g * SSM_DSTATE:(g + 1) * SSM_DSTATE]
            state = state_ref[g]
            yb_ref[rows, gc] = (_dot(cm, state.astype(BF16)) * ex[CHUNK:2 * CHUNK, gc]).astype(BF16)
            xw = (xa_ref[g, rows, :].astype(F32) * ex[0:CHUNK, gc]).astype(BF16)
            state_ref[g] = state * ex[2 * CHUNK:2 * CHUNK + 1, gc] + _dot_tn(bm, xw)
        return carry

    lax.fori_loop(0, nch, chunk_body, 0)


def _ssd_fwd_kernel(reset_ref, xa_ref, dt_ref, gate_ref, yb_ref, a_ref, tri_l3_ref, tri_u3_ref,
                    e2_ref, dskip_ref, nw_ref, out_ref, state_ref, y_ref):
    nch = SSD_BLOCK // CHUNK

    @pl.when(reset_ref[pl.program_id(0)] == 1)
    def _():
        state_ref[...] = jnp.zeros_like(state_ref)

    def chunk_body(ci, carry):
        rows = pl.ds(pl.multiple_of(ci * CHUNK, CHUNK), CHUNK)
        dt = dt_ref[rows, :]
        cum, tot = _ssd_chunk_scalars(dt, a_ref, tri_l3_ref, tri_u3_ref)
        ex = _expand(e2_ref, dt * jnp.exp(tot - cum), jnp.exp(cum),
                     jnp.broadcast_to(jnp.exp(tot), (16, LANES)))
        row_t = (cum - jnp.log(dt)).T
        li = lax.broadcasted_iota(jnp.int32, (CHUNK, CHUNK), 0)
        si = lax.broadcasted_iota(jnp.int32, (CHUNK, CHUNK), 1)
        lower = li >= si
        first_half = si < SSM_HEADDIM
        for g in range(SSM_GROUPS):
            gc = slice(g * SSM_GROUP_W, (g + 1) * SSM_GROUP_W)
            bm = xa_ref[SSM_GROUPS, rows, g * SSM_DSTATE:(g + 1) * SSM_DSTATE]
            cm = xa_ref[SSM_GROUPS + 1, rows, g * SSM_DSTATE:(g + 1) * SSM_DSTATE]
            cb = _dot_nt(cm, bm)
            heads_per_group = SSM_HEADS // SSM_GROUPS
            for pair in range(heads_per_group // 2):
                mats = []
                for hh in range(2):
                    h = g * heads_per_group + 2 * pair + hh
                    hb = SSM_HEADS + h
                    seg = jnp.where(lower, cum[:, h:h + 1] - row_t[h:h + 1, :],
                                    cum[:, hb:hb + 1] - row_t[hb:hb + 1, :])
                    mats.append((cb * jnp.exp(seg)).astype(BF16))
                pc = slice(2 * pair * SSM_HEADDIM, (2 * pair + 2) * SSM_HEADDIM)
                xp = xa_ref[g, rows, pc]
                zero = jnp.zeros_like(xp)
                rhs = jnp.concatenate([jnp.where(first_half, xp, zero), jnp.where(first_half, zero, xp)], axis=0)
                y_ref[:, pc] = _dot(jnp.concatenate(mats, axis=1), rhs)
            state = state_ref[g]
            xs = xa_ref[g, rows, :].astype(F32)
            y = (y_ref[...] + _dot(cm, state.astype(BF16)) * ex[CHUNK:2 * CHUNK, gc]
                 + yb_ref[rows, gc].astype(F32) + xs * dskip_ref[:, gc])
            y = y * gate_ref[g, rows, :].astype(F32)
            out_ref[rows, gc] = (_rms(y) * nw_ref[:, gc]).astype(BF16)
            xw = (xs * ex[0:CHUNK, gc]).astype(BF16)
            state_ref[g] = state * ex[2 * CHUNK:2 * CHUNK + 1, gc] + _dot_tn(bm, xw)
        return carry

    lax.fori_loop(0, nch, chunk_body, 0)


def _ssd(xa, dt, gate, a_row, dskip, norm_w, reset_f, reset_b):
    T = dt.shape[0]
    nb = T // SSD_BLOCK
    tb = _ssd_tables()
    state = pltpu.VMEM((SSM_GROUPS, SSM_DSTATE, SSM_GROUP_W), F32)
    consts = [a_row, tb["tri_l3"], tb["tri_u3"]]
    const_specs = [_resident(c.shape) for c in consts]
    out_shape = jax.ShapeDtypeStruct((T, SSM_DINNER), BF16)

    rev = lambda i, r: (nb - 1 - i, 0)
    blk = lambda n, imap: pl.BlockSpec((SSD_BLOCK, n), imap)
    slabs = lambda a, imap: pl.BlockSpec((a.shape[0], SSD_BLOCK, SLAB), lambda i, r: (0, imap(i, r)[0], 0))
    y_b = pl.pallas_call(
        _ssd_bwd_kernel,
        grid_spec=pltpu.PrefetchScalarGridSpec(
            num_scalar_prefetch=1, grid=(nb,),
            in_specs=[slabs(xa, rev), blk(LANES, rev)] + const_specs + [_resident(tb["e2_b"].shape)],
            out_specs=blk(SSM_DINNER, rev), scratch_shapes=[state]),
        out_shape=out_shape, compiler_params=_cparams(1, 40), name="ssd_bwd",
    )(reset_b, xa, dt, *consts, tb["e2_b"])

    fwd = lambda i, r: (i, 0)
    return pl.pallas_call(
        _ssd_fwd_kernel,
        grid_spec=pltpu.PrefetchScalarGridSpec(
            num_scalar_prefetch=1, grid=(nb,),
            in_specs=[slabs(xa, fwd), blk(LANES, fwd), slabs(gate, fwd), blk(SSM_DINNER, fwd)]
            + const_specs + [_resident(tb["e2_f"].shape), _resident(dskip.shape), _resident(norm_w.shape)],
            out_specs=blk(SSM_DINNER, fwd),
            scratch_shapes=[state, pltpu.VMEM((CHUNK, SSM_GROUP_W), F32)]),
        out_shape=out_shape, compiler_params=_cparams(1, 52), name="ssd_fwd",
    )(reset_f, xa, dt, gate, y_b, *consts, tb["e2_f"], dskip, norm_w)


def _pack_halves(x):
    n = x.shape[1] // 2
    hi = lax.bitcast_convert_type(x[:, :n].astype(F32), jnp.uint32)
    lo = lax.bitcast_convert_type(x[:, n:].astype(F32), jnp.uint32)
    return hi | (lo >> 16)


def _unpack_halves(w):
    hi = lax.bitcast_convert_type(w & jnp.uint32(0xFFFF0000), F32)
    lo = lax.bitcast_convert_type(w << 16, F32)
    return hi, lo


def _router_kernel(y_ref, h_ref, wo_ref, ln_ref, wh_ref, wl_ref, b_ref, tri_ref,
                   h3_ref, xn_ref, info_ref, info_t_ref, count_ref, base_ref):
    @pl.when(pl.program_id(0) == 0)
    def _():
        base_ref[...] = jnp.zeros_like(base_ref)

    h3 = h_ref[...] + _dot(y_ref[...], wo_ref[...])
    h3_ref[...] = h3
    xh, xl = _split2(_rms(h3) * ln_ref[...])
    xn_ref[...] = _pack_halves(xh)
    logits = _dot_nt(wh_ref[...], xh) + _dot_nt(wh_ref[...], xl) + _dot_nt(wl_ref[...], xh) + b_ref[...]
    row = lax.broadcasted_iota(jnp.int32, logits.shape, 0)
    m1 = jnp.max(logits, axis=0, keepdims=True)
    i1 = jnp.min(jnp.where(logits == m1, row, LANES), axis=0, keepdims=True)
    rest = jnp.where(row == i1, -jnp.inf, logits)
    m2 = jnp.max(rest, axis=0, keepdims=True)
    i2 = jnp.min(jnp.where(rest == m2, row, LANES), axis=0, keepdims=True)
    e = jnp.exp(m2 - m1)
    g1 = 1.0 / (1.0 + e)
    g2 = e / (1.0 + e)
    oh1 = row == i1
    oh2 = row == i2
    oh = jnp.where(oh1 | oh2, 1.0, 0.0)
    base = base_ref[...]
    prefix = _dot(oh.astype(BF16), tri_ref[...]) + base
    r1 = jnp.sum(jnp.where(oh1, prefix, 0.0), axis=0, keepdims=True)
    r2 = jnp.sum(jnp.where(oh2, prefix, 0.0), axis=0, keepdims=True)
    base = base + jnp.sum(oh, axis=1, keepdims=True)
    count_ref[...] = base[:, :LANES]
    base_ref[...] = base
    fields = [i1.astype(F32), i2.astype(F32), g1, g2, r1, r2]
    info_t = jnp.zeros(logits.shape, F32)
    for c, f in enumerate(fields):
        info_t = jnp.where(row == c, f, info_t)
    info_t_ref[...] = info_t[:ROUTE_COLS, :]
    info_ref[...] = info_t.T[:, :ROUTE_COLS]


def _router(y, h, w_out, ln, wh, wl, b):
    T = h.shape[0]
    tb = MOE_TOK
    nb = T // tb
    r = np.arange(tb)
    tri = jnp.asarray((r[:, None] < r[None, :]).astype(np.float32), dtype=BF16)
    tok = lambda n: pl.BlockSpec((tb, n), lambda i: (i, 0))
    return pl.pallas_call(
        _router_kernel,
        grid=(nb,),
        in_specs=[tok(SSM_DINNER), tok(D_MODEL), _resident(w_out.shape), _resident((1, D_MODEL)),
                  _resident(wh.shape), _resident(wl.shape), _resident(b.shape), _resident(tri.shape)],
        out_specs=[tok(D_MODEL), tok(D_MODEL // 2), tok(ROUTE_COLS),
                   pl.BlockSpec((ROUTE_COLS, tb), lambda i: (0, i)), pl.BlockSpec((LANES, LANES), lambda i: (0, 0))],
        out_shape=[jax.ShapeDtypeStruct((T, D_MODEL), F32), jax.ShapeDtypeStruct((T, D_MODEL // 2), jnp.uint32),
                   jax.ShapeDtypeStruct((T, ROUTE_COLS), F32), jax.ShapeDtypeStruct((ROUTE_COLS, T), F32),
                   jax.ShapeDtypeStruct((LANES, LANES), F32)],
        scratch_shapes=[pltpu.VMEM((LANES, tb), F32)],
        compiler_params=_cparams(1, 52),
        name="ssm_out_router",
    )(y, h, w_out, ln, wh, wl, b, tri)


def _sc_gather(table, idx):
    n_rows, width = idx.shape[0], table.shape[1]
    workers = SC_CORES * SC_SUBCORES
    nbuf = 2
    assert n_rows % (workers * SC_CHUNK * nbuf) == 0
    per_worker = n_rows // workers
    mesh = plsc.VectorSubcoreMesh(core_axis_name="c", subcore_axis_name="s")

    @functools.partial(
        pl.kernel, mesh=mesh,
        out_type=jax.ShapeDtypeStruct((n_rows, width), table.dtype),
        scratch_types=[pltpu.VMEM((nbuf, SC_CHUNK), jnp.int32),
                       pltpu.VMEM((nbuf, SC_CHUNK, width), table.dtype),
                       pltpu.SemaphoreType.DMA((nbuf,)),
                       pltpu.SemaphoreType.DMA((nbuf,))],
    )
    def gather_rows(table_hbm, idx_hbm, out_hbm, idx_v, rows_v, gsem, wsem):
        base = (lax.axis_index("s") * SC_CORES + lax.axis_index("c")) * per_worker

        def gather(c, b):
            off = pl.multiple_of(base + c * SC_CHUNK, 8)
            pltpu.sync_copy(idx_hbm.at[pl.ds(off, SC_CHUNK)], idx_v.at[b])
            return pltpu.make_async_copy(table_hbm.at[idx_v.at[b]], rows_v.at[b], gsem.at[b])

        def write(c, b):
            off = pl.multiple_of(base + c * SC_CHUNK, 8)
            return pltpu.make_async_copy(rows_v.at[b], out_hbm.at[pl.ds(off, SC_CHUNK)], wsem.at[b])

        @pl.loop(0, per_worker // SC_CHUNK, step=nbuf)
        def _(c0):
            copies = [gather(c0 + b, b) for b in range(nbuf)]
            for cp in copies:
                cp.start()
            writes = []
            for b, cp in enumerate(copies):
                cp.wait()
                writes.append(write(c0 + b, b))
                writes[-1].start()
            for wr in writes:
                wr.wait()

    return gather_rows(table, idx)


def _sc_scatter_pairs(table, pos, n_out):
    n_rows, width = table.shape
    workers = SC_CORES * SC_SUBCORES
    nbuf = 2
    assert n_rows % (workers * SC_CHUNK * nbuf) == 0
    per_worker = n_rows // workers
    mesh = plsc.VectorSubcoreMesh(core_axis_name="c", subcore_axis_name="s")

    @functools.partial(
        pl.kernel, mesh=mesh,
        out_type=jax.ShapeDtypeStruct((n_out, width), table.dtype),
        scratch_types=[pltpu.VMEM((nbuf, 2, SC_CHUNK), jnp.int32),
                       pltpu.VMEM((nbuf, SC_CHUNK, width), table.dtype),
                       pltpu.SemaphoreType.DMA((nbuf,)),
                       pltpu.SemaphoreType.DMA((nbuf,))],
    )
    def scatter_rows(table_hbm, pos_hbm, out_hbm, idx_v, rows_v, rsem, wsem):
        base = (lax.axis_index("s") * SC_CORES + lax.axis_index("c")) * per_worker

        def read(c, b):
            off = pl.multiple_of(base + c * SC_CHUNK, 8)
            return pltpu.make_async_copy(table_hbm.at[pl.ds(off, SC_CHUNK)], rows_v.at[b], rsem.at[b])

        def write(b, j):
            return pltpu.make_async_copy(rows_v.at[b], out_hbm.at[idx_v.at[b, j]], wsem.at[b])

        @pl.loop(0, per_worker // SC_CHUNK, step=nbuf)
        def _(c0):
            reads = [read(c0 + b, b) for b in range(nbuf)]
            for rd in reads:
                rd.start()
            for b in range(nbuf):
                pltpu.sync_copy(pos_hbm.at[base // SC_CHUNK + c0 + b], idx_v.at[b])
            writes = []
            for b, rd in enumerate(reads):
                rd.wait()
                for j in range(2):
                    writes.append(write(b, j))
                    writes[-1].start()
            for wr in writes:
                wr.wait()

    return scatter_rows(table, pos)


def _expert_kernel(texp_ref, trows_ref, tfirst_ref, x_ref, wg_ref, wu_ref, wd_ref, out_ref,
                   acc_ref, xb_ref, wgc_ref, wuc_ref, wdc_ref):
    i = pl.program_id(0)
    f = pl.program_id(1)
    nf = pl.num_programs(1)
    n = D_MODEL // 2

    @pl.when(trows_ref[i] > 0)
    def _():
        @pl.when(f == 0)
        def _():
            acc_ref[...] = jnp.zeros_like(acc_ref)
            row = lax.broadcasted_iota(jnp.int32, x_ref.shape, 0)
            hi, lo = _unpack_halves(jnp.where(row < trows_ref[i], x_ref[...], jnp.uint32(0)))
            xb_ref[:, :n] = hi.astype(BF16)
            xb_ref[:, n:] = lo.astype(BF16)

        @pl.when(tfirst_ref[i] == 1)
        def _():
            wgc_ref[f] = wg_ref[0].astype(BF16)
            wuc_ref[f] = wu_ref[0].astype(BF16)
            wdc_ref[f] = wd_ref[0].astype(BF16)

        def ffn_chunk(m):
            x = xb_ref[0:m, :]
            down = None
            for c in range(0, MOE_FC, MOE_FC // 2):
                cs = slice(c, c + MOE_FC // 2)
                act = (_silu(_dot(x, wgc_ref[f, :, cs])) * _dot(x, wuc_ref[f, :, cs])).astype(BF16)
                part = _dot(act, wdc_ref[f, cs, :])
                down = part if down is None else down + part
            acc_ref[0:m, :] += down

        for m in range(MOE_ROWS_STEP, MOE_SLOT + 1, MOE_ROWS_STEP):
            @pl.when((trows_ref[i] > m - MOE_ROWS_STEP) & (trows_ref[i] <= m))
            def _(m=m):
                ffn_chunk(m)

        @pl.when(f == nf - 1)
        def _():
            out_ref[...] = _pack_halves(acc_ref[...].astype(BF16))

    @pl.when((trows_ref[i] == 0) & (f == nf - 1))
    def _():
        out_ref[...] = jnp.zeros_like(out_ref)


def _experts(xs, w_gu, w_down, tile_expert, tile_rows, tile_first):
    n_tiles = xs.shape[0] // MOE_SLOT
    nf = FFN_EXPERT // MOE_FC
    chunk = lambda i, f, tf: jnp.where(tf[i] == 1, f, nf - 1)
    return pl.pallas_call(
        _expert_kernel,
        grid_spec=pltpu.PrefetchScalarGridSpec(
            num_scalar_prefetch=3, grid=(n_tiles, nf),
            in_specs=[pl.BlockSpec((MOE_SLOT, D_MODEL // 2), lambda i, f, te, tr, tf: (i, 0)),
                      pl.BlockSpec((1, D_MODEL, MOE_FC), lambda i, f, te, tr, tf: (te[i], 0, chunk(i, f, tf))),
                      pl.BlockSpec((1, D_MODEL, MOE_FC), lambda i, f, te, tr, tf: (te[i], 0, nf + chunk(i, f, tf))),
                      pl.BlockSpec((1, MOE_FC, D_MODEL), lambda i, f, te, tr, tf: (te[i], chunk(i, f, tf), 0))],
            out_specs=pl.BlockSpec((MOE_SLOT, D_MODEL // 2), lambda i, f, te, tr, tf: (i, 0)),
            scratch_shapes=[pltpu.VMEM((MOE_SLOT, D_MODEL), F32), pltpu.VMEM((MOE_SLOT, D_MODEL), BF16),
                            pltpu.VMEM((nf, D_MODEL, MOE_FC), BF16), pltpu.VMEM((nf, D_MODEL, MOE_FC), BF16),
                            pltpu.VMEM((nf, MOE_FC, D_MODEL), BF16)]),
        out_shape=jax.ShapeDtypeStruct(xs.shape, jnp.uint32),
        compiler_params=_cparams(2, 58),
        name="moe_experts",
    )(tile_expert, tile_rows, tile_first, xs, w_gu, w_gu, w_down)


def _finalize_kernel(y0_ref, y1_ref, info_ref, h_ref, fn_ref, out_ref):
    n = D_MODEL // 2
    info = info_ref[...]
    g0 = info[:, 2:3]
    g1 = info[:, 3:4]
    a_hi, a_lo = _unpack_halves(y0_ref[...])
    b_hi, b_lo = _unpack_halves(y1_ref[...])
    left = h_ref[:, :n] + g0 * a_hi + g1 * b_hi
    right = h_ref[:, n:] + g0 * a_lo + g1 * b_lo
    ms = (jnp.sum(left * left, axis=-1, keepdims=True)
          + jnp.sum(right * right, axis=-1, keepdims=True)) * (1.0 / D_MODEL)
    scale = lax.rsqrt(ms + EPS)
    out_ref[:, :n] = left * scale * fn_ref[:, :n]
    out_ref[:, n:] = right * scale * fn_ref[:, n:]


def _finalize(yg, info, h3, final_norm, tok0):
    n = yg.shape[0] // 2
    tb = FINAL_TILE
    nb = n // tb
    b0 = tok0 // tb
    tok = lambda w: pl.BlockSpec((tb, w), lambda i: (b0 + i, 0))
    loc = lambda off: pl.BlockSpec((tb, D_MODEL // 2), lambda i: (i + off, 0))
    return pl.pallas_call(
        _finalize_kernel,
        grid=(nb,),
        in_specs=[loc(0), loc(nb), tok(info.shape[1]), tok(D_MODEL), _resident((1, D_MODEL))],
        out_specs=pl.BlockSpec((tb, D_MODEL), lambda i: (i, 0)),
        out_shape=jax.ShapeDtypeStruct((n, D_MODEL), F32),
        compiler_params=_cparams(1, 32),
        name="moe_finalize",
    )(yg, yg, info, h3, final_norm)


def _moe(y_ssm, h2, w_out, ln, wr_hi, wr_lo, rb, w_gu, w_down, final_norm, n_a):
    T = h2.shape[0]
    n_tiles = 2 * T // MOE_SLOT + N_EXPERTS
    h3, xn, info, info_t, counts = _router(y_ssm, h2, w_out, ln, wr_hi, wr_lo, rb)

    counts = counts[:N_EXPERTS, 0].astype(jnp.int32)
    tiles_e = (counts + MOE_SLOT - 1) // MOE_SLOT
    tiles_cum = jnp.cumsum(tiles_e)
    gstart = (tiles_cum - tiles_e) * MOE_SLOT
    routed = info_t.astype(jnp.int32)
    start_of = lambda e: sum(jnp.where(e == x, gstart[x], 0) for x in range(N_EXPERTS))
    pos0 = start_of(routed[0]) + routed[4]
    pos1 = start_of(routed[1]) + routed[5]
    tile_ids = jnp.arange(n_tiles, dtype=jnp.int32)
    tile_expert = jnp.minimum(jnp.sum(tile_ids[:, None] >= tiles_cum[None, :], axis=1), N_EXPERTS - 1)
    tile_expert = tile_expert.astype(jnp.int32)
    tile_rows = jnp.clip(counts[tile_expert] - (tile_ids * MOE_SLOT - gstart[tile_expert]), 0, MOE_SLOT)
    tile_rows = jnp.where(tile_ids < tiles_cum[-1], tile_rows, 0).astype(jnp.int32)
    pos = jnp.stack([pos0.reshape(-1, SC_CHUNK), pos1.reshape(-1, SC_CHUNK)], axis=1)

    xs = _sc_scatter_pairs(xn, pos, n_tiles * MOE_SLOT)
    tile_first = ((tile_ids * MOE_SLOT == gstart[tile_expert]) & (tile_rows > 0)).astype(jnp.int32)
    ys = _experts(xs, w_gu, w_down, tile_expert, tile_rows, tile_first)
    outs = []
    for lo, hi in ((0, n_a), (n_a, T)):
        yg = _sc_gather(ys, jnp.concatenate([pos0[lo:hi], pos1[lo:hi]]))
        outs.append(_finalize(yg, info, h3, final_norm, lo))
    return outs


def _seq_flags(seq_lens, block):
    first, last = [], []
    for n in seq_lens:
        nb = n // block
        first += [1] + [0] * (nb - 1)
        last += [0] * (nb - 1) + [1]
    return np.asarray(first, np.int32), np.asarray(last, np.int32)


def _trunk(xa, xb, seq_lens, ln_mix0, ret_w_in, ret_w_out, ln_ffn0, ffn_w_gu, ffn_w_down, ln_mix1, ssm_w_in,
           ssm_conv_w, ssm_conv_b, ssm_dt_bias, ssm_A_log, ssm_D, ssm_norm_w, ssm_w_out, ln_ffn1,
           moe_router_w, moe_router_b, moe_w_gu, moe_w_down, final_norm):
    assert all(n % b == 0 for n in seq_lens for b in (SCAN_BLOCK, SSD_BLOCK, MOE_TOK, TOK_TILE, PROJ_TILE))
    row = lambda v: v.astype(F32).reshape(1, -1)
    first, last = _seq_flags(seq_lens, SCAN_BLOCK)
    reset_f = jnp.asarray(first)
    reset_b = jnp.asarray(last[::-1].copy())
    ssd_first, ssd_last = _seq_flags(seq_lens, SSD_BLOCK)

    half = RET_DK // 2
    inv = ROPE_BASE ** (-jnp.arange(half, dtype=F32) / half)
    ang_base = jnp.arange(0, max(seq_lens), TOK_TILE).astype(F32)[:, None, None] * inv[None, None, :]
    ang_off = jnp.arange(TOK_TILE).astype(F32)[:, None] * inv[None, :]
    pos_block = jnp.asarray(np.concatenate([np.arange(n // TOK_TILE) for n in seq_lens]).astype(np.int32))
    q, k, v, g = _ret_in(xa, xb, row(ln_mix0), ret_w_in.astype(BF16), jnp.cos(ang_base), jnp.sin(ang_base),
                         jnp.cos(ang_off), jnp.sin(ang_off), pos_block)
    y = _ret_scan(q, k, v, g, reset_f, reset_b)
    h2 = _post0(y, xa, xb, ret_w_out.astype(BF16), row(ln_ffn0), ffn_w_gu.astype(BF16),
                ffn_w_down.astype(BF16))

    wz = ssm_w_in[:, :SSM_DINNER].astype(BF16)
    wx = ssm_w_in[:, SSM_DINNER:SSM_DINNER + SSM_CONV_DIM].astype(BF16)
    wdt = jnp.pad(ssm_w_in[:, SSM_DINNER + SSM_CONV_DIM:], ((0, 0), (0, LANES - 2 * SSM_HEADS)))
    wdh, wdl = _split2(wdt)
    pad_row = lambda v: jnp.pad(v.astype(F32).reshape(1, -1), ((0, 0), (0, LANES - 2 * SSM_HEADS)))
    gate, xbc, dt = _ssm_in(h2, row(ln_mix1), wz, wx, wdh, wdl, pad_row(ssm_dt_bias))
    xact = _conv(xbc, ssm_conv_w.reshape(SSM_CONV, SSM_CONV_DIM).astype(F32), row(ssm_conv_b),
                 jnp.asarray(ssd_first), jnp.asarray(ssd_last))
    a_row = pad_row(-jnp.exp(ssm_A_log.astype(F32)))
    dskip = jnp.repeat(ssm_D.astype(F32), SSM_HEADDIM).reshape(1, SSM_DINNER)
    y = _ssd(xact, dt, gate, a_row, dskip, row(ssm_norm_w), jnp.asarray(ssd_first),
             jnp.asarray(ssd_last[::-1].copy()))

    wr = jnp.pad(moe_router_w.astype(F32).T, ((0, LANES - N_EXPERTS), (0, 0)))
    wr_hi, wr_lo = _split2(wr)
    rb = jnp.pad(moe_router_b.astype(F32).reshape(-1, 1), ((0, LANES - N_EXPERTS), (0, 0)), constant_values=-1e30)
    rb = jnp.broadcast_to(rb, (LANES, MOE_TOK))
    return _moe(y, h2, ssm_w_out.astype(BF16), row(ln_ffn1), wr_hi, wr_lo, rb,
                moe_w_gu, moe_w_down, row(final_norm), xa.shape[0])


def kernel(x_prompt, x_sample, ln_mix0, ret_w_in, ret_w_out, ln_ffn0, ffn_w_gu, ffn_w_down, ln_mix1, ssm_w_in,
           ssm_conv_w, ssm_conv_b, ssm_dt_bias, ssm_A_log, ssm_D, ssm_norm_w, ssm_w_out, ln_ffn1, moe_router_w,
           moe_router_b, moe_w_gu, moe_w_down, final_norm):
    seq_lens = [x_prompt.shape[1]] * x_prompt.shape[0] + [x_sample.shape[1]] * x_sample.shape[0]
    out_a, out_b = _trunk(x_prompt.reshape(-1, D_MODEL), x_sample.reshape(-1, D_MODEL), seq_lens, ln_mix0, ret_w_in, ret_w_out, ln_ffn0, ffn_w_gu, ffn_w_down, ln_mix1, ssm_w_in,
                 ssm_conv_w, ssm_conv_b, ssm_dt_bias, ssm_A_log, ssm_D, ssm_norm_w, ssm_w_out, ln_ffn1,
                 moe_router_w, moe_router_b, moe_w_gu, moe_w_down, final_norm)
    return (out_a.reshape(x_prompt.shape), out_b.reshape(x_sample.shape))
```

```python
import functools

import numpy as np
import jax
import jax.numpy as jnp
from jax import lax
from jax.experimental import pallas as pl
from jax.experimental.pallas import tpu as pltpu
from jax.experimental.pallas import tpu_sc as plsc

F32 = jnp.float32
BF16 = jnp.bfloat16

D_MODEL = 1024
EPS = 1e-6
RET_HEADS = 4
RET_DK = 256
RET_DV = 512
ROPE_BASE = 10000.0
SSM_DINNER = 2048
SSM_HEADDIM = 64
SSM_HEADS = 32
SSM_GROUPS = 4
SSM_DSTATE = 128
SSM_CONV = 5
SSM_CONV_DIM = 3072
SSM_GROUP_W = SSM_DINNER // SSM_GROUPS
SLAB = SSM_GROUP_W
CONV_ROWS = 128
CONV_WINDOW = 256
HALO = 16
FFN_DENSE = 2816
FFN_DENSE_CHUNK = 1408
N_EXPERTS = 8
FFN_EXPERT = 3584

CHUNK = 128
RET_CHUNK = 256
LANES = 128
TOK_TILE = 512
PROJ_TILE = 1024
SCAN_BLOCK = 1024
SSD_BLOCK = 1024
MOE_TOK = 1024
FINAL_TILE = 1024
MOE_SLOT = 1024
MOE_FC = 512
MOE_ROWS_STEP = 256
ROUTE_COLS = 8
SC_CORES = 2
SC_SUBCORES = 16
SC_CHUNK = 64


def _cparams(n_axes, vmem_mb):
    return pltpu.CompilerParams(dimension_semantics=("arbitrary",) * n_axes,
                                vmem_limit_bytes=vmem_mb << 20)


def _resident(shape):
    nd = len(shape)
    return pl.BlockSpec(shape, lambda *_: (0,) * nd, pipeline_mode=pl.Buffered(1))


def _rms(x):
    return x * lax.rsqrt(jnp.mean(x * x, axis=-1, keepdims=True) + EPS)


def _silu(x):
    return x * jax.nn.sigmoid(x)


def _dot(a, b):
    return jnp.dot(a, b, preferred_element_type=F32)


def _dot_nt(a, b):
    return lax.dot_general(a, b, (((1,), (1,)), ((), ())), preferred_element_type=F32)


def _dot_tn(a, b):
    return lax.dot_general(a, b, (((0,), (0,)), ((), ())), preferred_element_type=F32)


def _split2(x):
    hi = x.astype(BF16)
    lo = (x - hi.astype(F32)).astype(BF16)
    return hi, lo


def _split3(x):
    hi = x.astype(BF16)
    r = x - hi.astype(F32)
    mid = r.astype(BF16)
    lo = (r - mid.astype(F32)).astype(BF16)
    return hi, mid, lo


def _two_stream_specs(tm, n_a):
    return (pl.BlockSpec((tm, D_MODEL), lambda i, *_: (jnp.minimum(i, n_a - 1), 0)),
            pl.BlockSpec((tm, D_MODEL), lambda i, *_: (jnp.maximum(i - n_a, 0), 0)))


def _ret_in_kernel(pos_ref, xa_ref, xb_ref, ln_ref, w_ref, cb_ref, sb_ref, co_ref, so_ref,
                   q_ref, k_ref, v_ref, g_ref, *, n_a):
    x = jnp.where(pl.program_id(0) < n_a, xa_ref[...], xb_ref[...])
    xn = (_rms(x) * ln_ref[...]).astype(BF16)
    cb, sb, co, so = cb_ref[0], sb_ref[0], co_ref[...], so_ref[...]
    cos = cb * co - sb * so
    sin = sb * co + cb * so
    half = RET_DK // 2

    def rotary(dst, col0, scale):
        for h in range(RET_HEADS):
            c = col0 + h * RET_DK
            p = _dot(xn, w_ref[:, c:c + RET_DK])
            p1, p2 = p[:, :half], p[:, half:]
            dst[:, h * RET_DK:h * RET_DK + half] = ((p1 * cos - p2 * sin) * scale).astype(BF16)
            dst[:, h * RET_DK + half:(h + 1) * RET_DK] = ((p1 * sin + p2 * cos) * scale).astype(BF16)

    rotary(q_ref, 0, 1.0)
    rotary(k_ref, D_MODEL, RET_DK ** -0.5)
    for j in range(2 * D_MODEL // 512):
        v_ref[:, j * 512:(j + 1) * 512] = _dot(
            xn, w_ref[:, 2 * D_MODEL + j * 512:2 * D_MODEL + (j + 1) * 512]).astype(BF16)
        g_ref[:, j * 512:(j + 1) * 512] = _silu(_dot(
            xn, w_ref[:, 4 * D_MODEL + j * 512:4 * D_MODEL + (j + 1) * 512])).astype(BF16)


def _ret_in(xa, xb, ln, w_in, cos_base, sin_base, cos_off, sin_off, pos_block):
    T = xa.shape[0] + xb.shape[0]
    tm = TOK_TILE
    n_a = xa.shape[0] // tm
    tok = lambda n: pl.BlockSpec((tm, n), lambda i, pb: (i, 0))
    rot = pl.BlockSpec((1, 1, LANES), lambda i, pb: (pb[i], 0, 0))
    return pl.pallas_call(
        functools.partial(_ret_in_kernel, n_a=n_a),
        grid_spec=pltpu.PrefetchScalarGridSpec(
            num_scalar_prefetch=1, grid=(T // tm,),
            in_specs=[*_two_stream_specs(tm, n_a), _resident((1, D_MODEL)), _resident(w_in.shape), rot, rot,
                      _resident(cos_off.shape), _resident(sin_off.shape)],
            out_specs=[tok(D_MODEL), tok(D_MODEL), tok(2 * D_MODEL), tok(2 * D_MODEL)]),
        out_shape=[jax.ShapeDtypeStruct((T, D_MODEL), BF16), jax.ShapeDtypeStruct((T, D_MODEL), BF16),
                   jax.ShapeDtypeStruct((T, 2 * D_MODEL), BF16), jax.ShapeDtypeStruct((T, 2 * D_MODEL), BF16)],
        compiler_params=_cparams(1, 48),
        name="ret_in",
    )(pos_block, xa, xb, ln, w_in, cos_base, sin_base, cos_off, sin_off)


def _ret_scan_kernel(reset_ref, *refs, reverse, chunk_decay):
    if reverse:
        q_ref, k_ref, v_ref, wq_ref, wk_ref, out_ref, state_ref = refs
    else:
        q_ref, k_ref, v_ref, ob_ref, g_ref, wq_ref, wk_ref, din_ref, out_ref, state_ref = refs
    nch = SCAN_BLOCK // RET_CHUNK

    @pl.when(reset_ref[pl.program_id(0)] == 1)
    def _():
        state_ref[...] = jnp.zeros_like(state_ref)

    def chunk_body(ci, carry):
        cj = nch - 1 - ci if reverse else ci
        rows = pl.ds(pl.multiple_of(cj * RET_CHUNK, RET_CHUNK), RET_CHUNK)
        for h in range(RET_HEADS):
            kcols = slice(h * RET_DK, (h + 1) * RET_DK)
            vcols = slice(h * RET_DV, (h + 1) * RET_DV)
            qh = q_ref[rows, kcols]
            kh = k_ref[rows, kcols]
            vh = v_ref[rows, vcols]
            state = state_ref[h]
            o = _dot(qh, state.astype(BF16)) * wq_ref[h]
            if reverse:
                out_ref[rows, vcols] = o.astype(BF16)
            else:
                s = _dot_nt(qh, kh) * din_ref[h]
                o = o + _dot(s.astype(BF16), vh) + ob_ref[rows, vcols].astype(F32)
                out_ref[rows, vcols] = (g_ref[rows, vcols].astype(F32) * _rms(o)).astype(BF16)
            ks = (kh.astype(F32) * wk_ref[h]).astype(BF16)
            state_ref[h] = state * chunk_decay[h] + _dot_tn(ks, vh)
        return carry

    lax.fori_loop(0, nch, chunk_body, 0)


def _ret_tables():
    lg = np.log1p(-np.power(2.0, -5.0 - np.arange(RET_HEADS, dtype=np.float32))).astype(np.float32)
    c = RET_CHUNK
    a = np.arange(c, dtype=np.float32)
    col = lambda e, w: np.broadcast_to(np.exp(lg[:, None, None] * e[None, :, None]),
                                       (RET_HEADS, c, w)).astype(np.float32)
    dist = np.abs(a[:, None] - a[None, :])
    return dict(
        din=np.exp(lg[:, None, None] * dist[None]).astype(np.float32),
        wq_f=col(a + 1.0, RET_DV), wk_f=col(c - 1.0 - a, RET_DK),
        wq_b=col(c - a, RET_DV), wk_b=col(a, RET_DK),
        chunk_decay=tuple(float(v) for v in np.exp(lg * c)),
    )


def _ret_scan(q, k, v, g, reset_f, reset_b):
    T = q.shape[0]
    nb = T // SCAN_BLOCK
    tb = _ret_tables()
    state = pltpu.VMEM((RET_HEADS, RET_DK, RET_DV), F32)
    out_shape = jax.ShapeDtypeStruct((T, 2 * D_MODEL), BF16)

    def specs(imap):
        blk = lambda n: pl.BlockSpec((SCAN_BLOCK, n), imap)
        return blk(D_MODEL), blk(D_MODEL), blk(2 * D_MODEL)

    rev = lambda i, r: (nb - 1 - i, 0)
    bq, bk, bv = specs(rev)
    o_b = pl.pallas_call(
        functools.partial(_ret_scan_kernel, reverse=True, chunk_decay=tb["chunk_decay"]),
        grid_spec=pltpu.PrefetchScalarGridSpec(
            num_scalar_prefetch=1, grid=(nb,),
            in_specs=[bq, bk, bv, _resident(tb["wq_b"].shape), _resident(tb["wk_b"].shape)],
            out_specs=bv, scratch_shapes=[state]),
        out_shape=out_shape, compiler_params=_cparams(1, 48), name="ret_scan_bwd",
    )(reset_b, q, k, v, tb["wq_b"], tb["wk_b"])

    fwd = lambda i, r: (i, 0)
    bq, bk, bv = specs(fwd)
    return pl.pallas_call(
        functools.partial(_ret_scan_kernel, reverse=False, chunk_decay=tb["chunk_decay"]),
        grid_spec=pltpu.PrefetchScalarGridSpec(
            num_scalar_prefetch=1, grid=(nb,),
            in_specs=[bq, bk, bv, bv, bv, _resident(tb["wq_f"].shape), _resident(tb["wk_f"].shape),
                      _resident(tb["din"].shape)],
            out_specs=bv, scratch_shapes=[state]),
        out_shape=out_shape, compiler_params=_cparams(1, 56), name="ret_scan_fwd",
    )(reset_f, q, k, v, o_b, g, tb["wq_f"], tb["wk_f"], tb["din"])


def _post0_kernel(y_ref, xa_ref, xb_ref, wo_ref, ln_ref, wgu_ref, wd_ref, out_ref, *, n_a):
    x = jnp.where(pl.program_id(0) < n_a, xa_ref[...], xb_ref[...])
    h1 = x + _dot(y_ref[...], wo_ref[...])
    xn = (_rms(h1) * ln_ref[...]).astype(BF16)
    acc = h1
    fc = FFN_DENSE_CHUNK
    for c in range(FFN_DENSE // fc):
        gate = _dot(xn, wgu_ref[:, c * fc:(c + 1) * fc])
        up = _dot(xn, wgu_ref[:, FFN_DENSE + c * fc:FFN_DENSE + (c + 1) * fc])
        act = (_silu(gate) * up).astype(BF16)
        acc = acc + _dot(act, wd_ref[c * fc:(c + 1) * fc, :])
    out_ref[...] = acc


def _post0(y, xa, xb, w_out, ln, w_gu, w_down):
    T = y.shape[0]
    tm = TOK_TILE
    n_a = xa.shape[0] // tm
    tok = lambda n: pl.BlockSpec((tm, n), lambda i: (i, 0))
    return pl.pallas_call(
        functools.partial(_post0_kernel, n_a=n_a),
        grid=(T // tm,),
        in_specs=[tok(2 * D_MODEL), *_two_stream_specs(tm, n_a), _resident(w_out.shape),
                  _resident((1, D_MODEL)), _resident(w_gu.shape), _resident(w_down.shape)],
        out_specs=tok(D_MODEL),
        out_shape=jax.ShapeDtypeStruct((T, D_MODEL), F32),
        compiler_params=_cparams(1, 52),
        name="ret_out_ffn",
    )(y, xa, xb, w_out, ln, w_gu, w_down)


def _ssm_in_kernel(x_ref, ln_ref, wz_ref, wx_ref, wd_ref, wdh_ref, bias_ref, gate_ref, xbc_ref, dt_ref):
    xh, xl = _split2(_rms(x_ref[...]) * ln_ref[...])
    for j in range(SSM_DINNER // SLAB):
        gate_ref[j] = _silu(_dot(xh, wz_ref[:, j * SLAB:(j + 1) * SLAB])).astype(BF16)
    for j in range(SSM_CONV_DIM // SLAB):
        xbc_ref[:, j * SLAB:(j + 1) * SLAB] = _dot(xh, wx_ref[:, j * SLAB:(j + 1) * SLAB]).astype(BF16)
    both = _dot(xh, wd_ref[...])
    x = both[:, :LANES] + both[:, LANES:] + _dot(xl, wdh_ref[...]) + bias_ref[...]
    dt_ref[...] = jnp.maximum(x, 0.0) + jnp.log1p(jnp.exp(-jnp.abs(x)))


def _ssm_in(x, ln, wz, wx, wdh, wdl, bias_row):
    T = x.shape[0]
    tm = PROJ_TILE
    n_gate = SSM_DINNER // SLAB
    wd = jnp.concatenate([wdh, wdl], axis=1)
    tok = lambda n: pl.BlockSpec((tm, n), lambda i: (i, 0))
    return pl.pallas_call(
        _ssm_in_kernel,
        grid=(T // tm,),
        in_specs=[tok(D_MODEL), _resident((1, D_MODEL)), _resident(wz.shape), _resident(wx.shape),
                  _resident(wd.shape), _resident(wdh.shape), _resident(bias_row.shape)],
        out_specs=[pl.BlockSpec((n_gate, tm, SLAB), lambda i: (0, i, 0)), tok(SSM_CONV_DIM), tok(LANES)],
        out_shape=[jax.ShapeDtypeStruct((n_gate, T, SLAB), BF16), jax.ShapeDtypeStruct((T, SSM_CONV_DIM), BF16),
                   jax.ShapeDtypeStruct((T, LANES), F32)],
        compiler_params=_cparams(1, 52),
        name="ssm_in",
    )(x, ln, wz, wx, wd, wdh, bias_row)


def _conv_kernel(first_ref, last_ref, prev_ref, main_ref, next_ref, shift_ref, w_ref, b_ref, out_ref, ext_ref):
    i = pl.program_id(0)
    rb = SSD_BLOCK
    ext_rows = ext_ref.shape[0]

    @pl.when(i == 0)
    def _():
        ext_ref[rb + 2 * HALO:ext_rows, :] = jnp.zeros((ext_rows - rb - 2 * HALO, SSM_CONV_DIM), BF16)

    zero = jnp.zeros((HALO, SSM_CONV_DIM), BF16)
    ext_ref[0:HALO, :] = jnp.where(first_ref[i] == 1, zero, prev_ref[...])
    ext_ref[HALO:HALO + rb, :] = main_ref[...]
    ext_ref[HALO + rb:2 * HALO + rb, :] = jnp.where(last_ref[i] == 1, zero, next_ref[...])
    for s in range(SSM_CONV_DIM // SLAB):
        cs = slice(s * SLAB, (s + 1) * SLAB)
        for r0 in range(0, rb, CONV_ROWS):
            taps = _dot(shift_ref[...], ext_ref[r0:r0 + CONV_WINDOW, cs])
            acc = jnp.broadcast_to(b_ref[:, cs], (CONV_ROWS, SLAB))
            for j in range(SSM_CONV):
                acc = acc + taps[j * CONV_ROWS:(j + 1) * CONV_ROWS] * w_ref[j:j + 1, cs]
            out_ref[s, r0:r0 + CONV_ROWS, :] = _silu(acc).astype(BF16)


def _conv(xbc, conv_w, conv_b, first, last):
    T = xbc.shape[0]
    rb = SSD_BLOCK
    nb = T // rb
    per = rb // HALO
    nh = T // HALO
    r = np.arange(CONV_ROWS)
    shift = np.zeros((SSM_CONV * CONV_ROWS, CONV_WINDOW), np.float32)
    for j in range(SSM_CONV):
        shift[j * CONV_ROWS + r, HALO + r + j - SSM_CONV // 2] = 1.0
    shift = jnp.asarray(shift, dtype=BF16)
    ext_rows = rb - CONV_ROWS + CONV_WINDOW
    return pl.pallas_call(
        _conv_kernel,
        grid_spec=pltpu.PrefetchScalarGridSpec(
            num_scalar_prefetch=2, grid=(nb,),
            in_specs=[
                pl.BlockSpec((HALO, SSM_CONV_DIM), lambda i, f, l: (jnp.maximum(i * per - 1, 0), 0)),
                pl.BlockSpec((rb, SSM_CONV_DIM), lambda i, f, l: (i, 0)),
                pl.BlockSpec((HALO, SSM_CONV_DIM), lambda i, f, l: (jnp.minimum((i + 1) * per, nh - 1), 0)),
                _resident(shift.shape), _resident(conv_w.shape), _resident(conv_b.shape)],
            out_specs=pl.BlockSpec((SSM_CONV_DIM // SLAB, rb, SLAB), lambda i, f, l: (0, i, 0)),
            scratch_shapes=[pltpu.VMEM((ext_rows, SSM_CONV_DIM), BF16)]),
        out_shape=jax.ShapeDtypeStruct((SSM_CONV_DIM // SLAB, T, SLAB), BF16),
        compiler_params=_cparams(1, 48),
        name="ssm_conv",
    )(first, last, xbc, xbc, xbc, shift, conv_w, conv_b)


def _ssd_tables():
    r = np.arange(CHUNK)
    tri_l = (r[None, :] <= r[:, None]).astype(np.float32)
    tri_u = (r[None, :] >= r[:, None]).astype(np.float32)
    c = np.arange(SSM_DINNER) // SSM_HEADDIM
    j = np.arange(LANES)
    e_f = (j[:, None] == c[None, :]).astype(np.float32)
    e_b = (j[:, None] == (c[None, :] + SSM_HEADS)).astype(np.float32)
    as_bf16 = lambda x: jnp.asarray(x, dtype=BF16)
    return dict(tri_l3=as_bf16(np.concatenate([tri_l] * 3, axis=1)),
                tri_u3=as_bf16(np.concatenate([tri_u] * 3, axis=1)),
                e2_f=as_bf16(np.concatenate([e_f, e_f], axis=0)),
                e2_b=as_bf16(np.concatenate([e_b, e_b], axis=0)))


def _ssd_chunk_scalars(dt, a_ref, tri_l3_ref, tri_u3_ref):
    a = dt * a_ref[...]
    a3 = jnp.concatenate(_split3(a), axis=0)
    cum_f = _dot(tri_l3_ref[...], a3)
    cum_b = _dot(tri_u3_ref[...], a3)
    lane = lax.broadcasted_iota(jnp.int32, (CHUNK, LANES), 1)
    cum = jnp.where(lane < SSM_HEADS, cum_f, cum_b)
    tot = cum_f[CHUNK - 1:CHUNK, :]
    return cum, tot


def _expand(e2_ref, *rows):
    x = jnp.concatenate(rows, axis=0)
    hi, lo = _split2(x)
    return _dot(jnp.concatenate([hi, lo], axis=1), e2_ref[...])


def _ssd_bwd_kernel(reset_ref, xa_ref, dt_ref, a_ref, tri_l3_ref, tri_u3_ref, e2_ref,
                    yb_ref, state_ref):
    nch = SSD_BLOCK // CHUNK

    @pl.when(reset_ref[pl.program_id(0)] == 1)
    def _():
        state_ref[...] = jnp.zeros_like(state_ref)

    def chunk_body(ci, carry):
        rows = pl.ds(pl.multiple_of((nch - 1 - ci) * CHUNK, CHUNK), CHUNK)
        dt = dt_ref[rows, :]
        cum, tot = _ssd_chunk_scalars(dt, a_ref, tri_l3_ref, tri_u3_ref)
        ex = _expand(e2_ref, dt * jnp.exp(tot - cum), jnp.exp(cum),
                     jnp.broadcast_to(jnp.exp(tot), (16, LANES)))
        for g in range(SSM_GROUPS):
            gc = slice(g * SSM_GROUP_W, (g + 1) * SSM_GROUP_W)
            bm = xa_ref[SSM_GROUPS, rows, g * SSM_DSTATE:(g + 1) * SSM_DSTATE]
            cm = xa_ref[SSM_GROUPS + 1, rows, g * SSM_DSTATE:(g + 1) * SSM_DSTATE]
            state = state_ref[g]
            yb_ref[rows, gc] = (_dot(cm, state.astype(BF16)) * ex[CHUNK:2 * CHUNK, gc]).astype(BF16)
            xw = (xa_ref[g, rows, :].astype(F32) * ex[0:CHUNK, gc]).astype(BF16)
            state_ref[g] = state * ex[2 * CHUNK:2 * CHUNK + 1, gc] + _dot_tn(bm, xw)
        return carry

    lax.fori_loop(0, nch, chunk_body, 0)


def _ssd_fwd_kernel(reset_ref, xa_ref, dt_ref, gate_ref, yb_ref, a_ref, tri_l3_ref, tri_u3_ref,
                    e2_ref, dskip_ref, nw_ref, out_ref, state_ref, y_ref):
    nch = SSD_BLOCK // CHUNK

    @pl.when(reset_ref[pl.program_id(0)] == 1)
    def _():
        state_ref[...] = jnp.zeros_like(state_ref)

    def chunk_body(ci, carry):
        rows = pl.ds(pl.multiple_of(ci * CHUNK, CHUNK), CHUNK)
        dt = dt_ref[rows, :]
        cum, tot = _ssd_chunk_scalars(dt, a_ref, tri_l3_ref, tri_u3_ref)
        ex = _expand(e2_ref, dt * jnp.exp(tot - cum), jnp.exp(cum),
                     jnp.broadcast_to(jnp.exp(tot), (16, LANES)))
        row_t = (cum - jnp.log(dt)).T
        li = lax.broadcasted_iota(jnp.int32, (CHUNK, CHUNK), 0)
        si = lax.broadcasted_iota(jnp.int32, (CHUNK, CHUNK), 1)
        lower = li >= si
        first_half = si < SSM_HEADDIM
        for g in range(SSM_GROUPS):
            gc = slice(g * SSM_GROUP_W, (g + 1) * SSM_GROUP_W)
            bm = xa_ref[SSM_GROUPS, rows, g * SSM_DSTATE:(g + 1) * SSM_DSTATE]
            cm = xa_ref[SSM_GROUPS + 1, rows, g * SSM_DSTATE:(g + 1) * SSM_DSTATE]
            cb = _dot_nt(cm, bm)
            heads_per_group = SSM_HEADS // SSM_GROUPS
            for pair in range(heads_per_group // 2):
                mats = []
                for hh in range(2):
                    h = g * heads_per_group + 2 * pair + hh
                    hb = SSM_HEADS + h
                    seg = jnp.where(lower, cum[:, h:h + 1] - row_t[h:h + 1, :],
                                    cum[:, hb:hb + 1] - row_t[hb:hb + 1, :])
                    mats.append((cb * jnp.exp(seg)).astype(BF16))
                pc = slice(2 * pair * SSM_HEADDIM, (2 * pair + 2) * SSM_HEADDIM)
                xp = xa_ref[g, rows, pc]
                zero = jnp.zeros_like(xp)
                rhs = jnp.concatenate([jnp.where(first_half, xp, zero), jnp.where(first_half, zero, xp)], axis=0)
                y_ref[:, pc] = _dot(jnp.concatenate(mats, axis=1), rhs)
            state = state_ref[g]
            xs = xa_ref[g, rows, :].astype(F32)
            y = (y_ref[...] + _dot(cm, state.astype(BF16)) * ex[CHUNK:2 * CHUNK, gc]
                 + yb_ref[rows, gc].astype(F32) + xs * dskip_ref[:, gc])
            y = y * gate_ref[g, rows, :].astype(F32)
            out_ref[rows, gc] = (_rms(y) * nw_ref[:, gc]).astype(BF16)
            xw = (xs * ex[0:CHUNK, gc]).astype(BF16)
            state_ref[g] = state * ex[2 * CHUNK:2 * CHUNK + 1, gc] + _dot_tn(bm, xw)
        return carry

    lax.fori_loop(0, nch, chunk_body, 0)


def _ssd(xa, dt, gate, a_row, dskip, norm_w, reset_f, reset_b):
    T = dt.shape[0]
    nb = T // SSD_BLOCK
    tb = _ssd_tables()
    state = pltpu.VMEM((SSM_GROUPS, SSM_DSTATE, SSM_GROUP_W), F32)
    consts = [a_row, tb["tri_l3"], tb["tri_u3"]]
    const_specs = [_resident(c.shape) for c in consts]
    out_shape = jax.ShapeDtypeStruct((T, SSM_DINNER), BF16)

    rev = lambda i, r: (nb - 1 - i, 0)
    blk = lambda n, imap: pl.BlockSpec((SSD_BLOCK, n), imap)
    slabs = lambda a, imap: pl.BlockSpec((a.shape[0], SSD_BLOCK, SLAB), lambda i, r: (0, imap(i, r)[0], 0))
    y_b = pl.pallas_call(
        _ssd_bwd_kernel,
        grid_spec=pltpu.PrefetchScalarGridSpec(
            num_scalar_prefetch=1, grid=(nb,),
            in_specs=[slabs(xa, rev), blk(LANES, rev)] + const_specs + [_resident(tb["e2_b"].shape)],
            out_specs=blk(SSM_DINNER, rev), scratch_shapes=[state]),
        out_shape=out_shape, compiler_params=_cparams(1, 40), name="ssd_bwd",
    )(reset_b, xa, dt, *consts, tb["e2_b"])

    fwd = lambda i, r: (i, 0)
    return pl.pallas_call(
        _ssd_fwd_kernel,
        grid_spec=pltpu.PrefetchScalarGridSpec(
            num_scalar_prefetch=1, grid=(nb,),
            in_specs=[slabs(xa, fwd), blk(LANES, fwd), slabs(gate, fwd), blk(SSM_DINNER, fwd)]
            + const_specs + [_resident(tb["e2_f"].shape), _resident(dskip.shape), _resident(norm_w.shape)],
            out_specs=blk(SSM_DINNER, fwd),
            scratch_shapes=[state, pltpu.VMEM((CHUNK, SSM_GROUP_W), F32)]),
        out_shape=out_shape, compiler_params=_cparams(1, 52), name="ssd_fwd",
    )(reset_f, xa, dt, gate, y_b, *consts, tb["e2_f"], dskip, norm_w)


def _pack_halves(x):
    n = x.shape[1] // 2
    hi = lax.bitcast_convert_type(x[:, :n].astype(F32), jnp.uint32)
    lo = lax.bitcast_convert_type(x[:, n:].astype(F32), jnp.uint32)
    return hi | (lo >> 16)


def _unpack_halves(w):
    hi = lax.bitcast_convert_type(w & jnp.uint32(0xFFFF0000), F32)
    lo = lax.bitcast_convert_type(w << 16, F32)
    return hi, lo


def _router_kernel(y_ref, h_ref, wo_ref, ln_ref, wh_ref, wl_ref, b_ref, tri_ref,
                   h3_ref, xn_ref, info_ref, info_t_ref, count_ref, base_ref):
    @pl.when(pl.program_id(0) == 0)
    def _():
        base_ref[...] = jnp.zeros_like(base_ref)

    h3 = h_ref[...] + _dot(y_ref[...], wo_ref[...])
    h3_ref[...] = h3
    xh, xl = _split2(_rms(h3) * ln_ref[...])
    xn_ref[...] = _pack_halves(xh)
    logits = _dot_nt(wh_ref[...], xh) + _dot_nt(wh_ref[...], xl) + _dot_nt(wl_ref[...], xh) + b_ref[...]
    row = lax.broadcasted_iota(jnp.int32, logits.shape, 0)
    m1 = jnp.max(logits, axis=0, keepdims=True)
    i1 = jnp.min(jnp.where(logits == m1, row, LANES), axis=0, keepdims=True)
    rest = jnp.where(row == i1, -jnp.inf, logits)
    m2 = jnp.max(rest, axis=0, keepdims=True)
    i2 = jnp.min(jnp.where(rest == m2, row, LANES), axis=0, keepdims=True)
    e = jnp.exp(m2 - m1)
    g1 = 1.0 / (1.0 + e)
    g2 = e / (1.0 + e)
    oh1 = row == i1
    oh2 = row == i2
    oh = jnp.where(oh1 | oh2, 1.0, 0.0)
    base = base_ref[...]
    prefix = _dot(oh.astype(BF16), tri_ref[...]) + base
    r1 = jnp.sum(jnp.where(oh1, prefix, 0.0), axis=0, keepdims=True)
    r2 = jnp.sum(jnp.where(oh2, prefix, 0.0), axis=0, keepdims=True)
    base = base + jnp.sum(oh, axis=1, keepdims=True)
    count_ref[...] = base[:, :LANES]
    base_ref[...] = base
    fields = [i1.astype(F32), i2.astype(F32), g1, g2, r1, r2]
    info_t = jnp.zeros(logits.shape, F32)
    for c, f in enumerate(fields):
        info_t = jnp.where(row == c, f, info_t)
    info_t_ref[...] = info_t[:ROUTE_COLS, :]
    info_ref[...] = info_t.T[:, :ROUTE_COLS]


def _router(y, h, w_out, ln, wh, wl, b):
    T = h.shape[0]
    tb = MOE_TOK
    nb = T // tb
    r = np.arange(tb)
    tri = jnp.asarray((r[:, None] < r[None, :]).astype(np.float32), dtype=BF16)
    tok = lambda n: pl.BlockSpec((tb, n), lambda i: (i, 0))
    return pl.pallas_call(
        _router_kernel,
        grid=(nb,),
        in_specs=[tok(SSM_DINNER), tok(D_MODEL), _resident(w_out.shape), _resident((1, D_MODEL)),
                  _resident(wh.shape), _resident(wl.shape), _resident(b.shape), _resident(tri.shape)],
        out_specs=[tok(D_MODEL), tok(D_MODEL // 2), tok(ROUTE_COLS),
                   pl.BlockSpec((ROUTE_COLS, tb), lambda i: (0, i)), pl.BlockSpec((LANES, LANES), lambda i: (0, 0))],
        out_shape=[jax.ShapeDtypeStruct((T, D_MODEL), F32), jax.ShapeDtypeStruct((T, D_MODEL // 2), jnp.uint32),
                   jax.ShapeDtypeStruct((T, ROUTE_COLS), F32), jax.ShapeDtypeStruct((ROUTE_COLS, T), F32),
                   jax.ShapeDtypeStruct((LANES, LANES), F32)],
        scratch_shapes=[pltpu.VMEM((LANES, tb), F32)],
        compiler_params=_cparams(1, 52),
        name="ssm_out_router",
    )(y, h, w_out, ln, wh, wl, b, tri)


def _sc_gather(table, idx):
    n_rows, width = idx.shape[0], table.shape[1]
    workers = SC_CORES * SC_SUBCORES
    nbuf = 2
    assert n_rows % (workers * SC_CHUNK * nbuf) == 0
    per_worker = n_rows // workers
    mesh = plsc.VectorSubcoreMesh(core_axis_name="c", subcore_axis_name="s")

    @functools.partial(
        pl.kernel, mesh=mesh,
        out_type=jax.ShapeDtypeStruct((n_rows, width), table.dtype),
        scratch_types=[pltpu.VMEM((nbuf, SC_CHUNK), jnp.int32),
                       pltpu.VMEM((nbuf, SC_CHUNK, width), table.dtype),
                       pltpu.SemaphoreType.DMA((nbuf,)),
                       pltpu.SemaphoreType.DMA((nbuf,))],
    )
    def gather_rows(table_hbm, idx_hbm, out_hbm, idx_v, rows_v, gsem, wsem):
        base = (lax.axis_index("s") * SC_CORES + lax.axis_index("c")) * per_worker

        def gather(c, b):
            off = pl.multiple_of(base + c * SC_CHUNK, 8)
            pltpu.sync_copy(idx_hbm.at[pl.ds(off, SC_CHUNK)], idx_v.at[b])
            return pltpu.make_async_copy(table_hbm.at[idx_v.at[b]], rows_v.at[b], gsem.at[b])

        def write(c, b):
            off = pl.multiple_of(base + c * SC_CHUNK, 8)
            return pltpu.make_async_copy(rows_v.at[b], out_hbm.at[pl.ds(off, SC_CHUNK)], wsem.at[b])

        @pl.loop(0, per_worker // SC_CHUNK, step=nbuf)
        def _(c0):
            copies = [gather(c0 + b, b) for b in range(nbuf)]
            for cp in copies:
                cp.start()
            writes = []
            for b, cp in enumerate(copies):
                cp.wait()
                writes.append(write(c0 + b, b))
                writes[-1].start()
            for wr in writes:
                wr.wait()

    return gather_rows(table, idx)


def _sc_scatter_pairs(table, pos, n_out):
    n_rows, width = table.shape
    workers = SC_CORES * SC_SUBCORES
    nbuf = 2
    assert n_rows % (workers * SC_CHUNK * nbuf) == 0
    per_worker = n_rows // workers
    mesh = plsc.VectorSubcoreMesh(core_axis_name="c", subcore_axis_name="s")

    @functools.partial(
        pl.kernel, mesh=mesh,
        out_type=jax.ShapeDtypeStruct((n_out, width), table.dtype),
        scratch_types=[pltpu.VMEM((nbuf, 2, SC_CHUNK), jnp.int32),
                       pltpu.VMEM((nbuf, SC_CHUNK, width), table.dtype),
                       pltpu.SemaphoreType.DMA((nbuf,)),
                       pltpu.SemaphoreType.DMA((nbuf,))],
    )
    def scatter_rows(table_hbm, pos_hbm, out_hbm, idx_v, rows_v, rsem, wsem):
        base = (lax.axis_index("s") * SC_CORES + lax.axis_index("c")) * per_worker

        def read(c, b):
            off = pl.multiple_of(base + c * SC_CHUNK, 8)
            return pltpu.make_async_copy(table_hbm.at[pl.ds(off, SC_CHUNK)], rows_v.at[b], rsem.at[b])

        def write(b, j):
            return pltpu.make_async_copy(rows_v.at[b], out_hbm.at[idx_v.at[b, j]], wsem.at[b])

        @pl.loop(0, per_worker // SC_CHUNK, step=nbuf)
        def _(c0):
            reads = [read(c0 + b, b) for b in range(nbuf)]
            for rd in reads:
                rd.start()
            for b in range(nbuf):
                pltpu.sync_copy(pos_hbm.at[base // SC_CHUNK + c0 + b], idx_v.at[b])
            writes = []
            for b, rd in enumerate(reads):
                rd.wait()
                for j in range(2):
                    writes.append(write(b, j))
                    writes[-1].start()
            for wr in writes:
                wr.wait()

    return scatter_rows(table, pos)


def _expert_kernel(texp_ref, trows_ref, tfirst_ref, x_ref, wg_ref, wu_ref, wd_ref, out_ref,
                   acc_ref, xb_ref, wgc_ref, wuc_ref, wdc_ref):
    i = pl.program_id(0)
    f = pl.program_id(1)
    nf = pl.num_programs(1)
    n = D_MODEL // 2

    @pl.when(trows_ref[i] > 0)
    def _():
        @pl.when(f == 0)
        def _():
            acc_ref[...] = jnp.zeros_like(acc_ref)
            row = lax.broadcasted_iota(jnp.int32, x_ref.shape, 0)
            hi, lo = _unpack_halves(jnp.where(row < trows_ref[i], x_ref[...], jnp.uint32(0)))
            xb_ref[:, :n] = hi.astype(BF16)
            xb_ref[:, n:] = lo.astype(BF16)

        @pl.when(tfirst_ref[i] == 1)
        def _():
            wgc_ref[f] = wg_ref[0].astype(BF16)
            wuc_ref[f] = wu_ref[0].astype(BF16)
            wdc_ref[f] = wd_ref[0].astype(BF16)

        def ffn_chunk(m, k):
            x = xb_ref[0:m, :]
            down = None
            for c in range(0, MOE_FC, MOE_FC // 2):
                cs = slice(c, c + MOE_FC // 2)
                act = (_silu(_dot(x, wgc_ref[k, :, cs])) * _dot(x, wuc_ref[k, :, cs])).astype(BF16)
                part = _dot(act, wdc_ref[k, cs, :])
                down = part if down is None else down + part
            acc_ref[0:m, :] += down

        for m in range(MOE_ROWS_STEP, MOE_SLOT + 1, MOE_ROWS_STEP):
            @pl.when((trows_ref[i] > m - MOE_ROWS_STEP) & (trows_ref[i] <= m))
            def _(m=m):
                @pl.when(tfirst_ref[i] == 1)
                def _():
                    ffn_chunk(m, f)

                @pl.when((tfirst_ref[i] == 0) & (f == 0))
                def _():
                    def body(k, carry):
                        ffn_chunk(m, k)
                        return carry
                    lax.fori_loop(0, FFN_EXPERT // MOE_FC, body, 0)

        @pl.when(f == nf - 1)
        def _():
            out_ref[...] = _pack_halves(acc_ref[...].astype(BF16))

    @pl.when((trows_ref[i] == 0) & (f == nf - 1))
    def _():
        out_ref[...] = jnp.zeros_like(out_ref)


def _experts(xs, w_gu, w_down, tile_expert, tile_rows, tile_first):
    n_tiles = xs.shape[0] // MOE_SLOT
    nf = FFN_EXPERT // MOE_FC
    chunk = lambda i, f, tf: jnp.where(tf[i] == 1, f, nf - 1)
    return pl.pallas_call(
        _expert_kernel,
        grid_spec=pltpu.PrefetchScalarGridSpec(
            num_scalar_prefetch=3, grid=(n_tiles, nf),
            in_specs=[pl.BlockSpec((MOE_SLOT, D_MODEL // 2), lambda i, f, te, tr, tf: (i, 0)),
                      pl.BlockSpec((1, D_MODEL, MOE_FC), lambda i, f, te, tr, tf: (te[i], 0, chunk(i, f, tf))),
                      pl.BlockSpec((1, D_MODEL, MOE_FC), lambda i, f, te, tr, tf: (te[i], 0, nf + chunk(i, f, tf))),
                      pl.BlockSpec((1, MOE_FC, D_MODEL), lambda i, f, te, tr, tf: (te[i], chunk(i, f, tf), 0))],
            out_specs=pl.BlockSpec((MOE_SLOT, D_MODEL // 2), lambda i, f, te, tr, tf: (i, 0)),
            scratch_shapes=[pltpu.VMEM((MOE_SLOT, D_MODEL), F32), pltpu.VMEM((MOE_SLOT, D_MODEL), BF16),
                            pltpu.VMEM((nf, D_MODEL, MOE_FC), BF16), pltpu.VMEM((nf, D_MODEL, MOE_FC), BF16),
                            pltpu.VMEM((nf, MOE_FC, D_MODEL), BF16)]),
        out_shape=jax.ShapeDtypeStruct(xs.shape, jnp.uint32),
        compiler_params=_cparams(2, 58),
        name="moe_experts",
    )(tile_expert, tile_rows, tile_first, xs, w_gu, w_gu, w_down)


def _finalize_kernel(y0_ref, y1_ref, info_ref, h_ref, fn_ref, out_ref):
    n = D_MODEL // 2
    info = info_ref[...]
    g0 = info[:, 2:3]
    g1 = info[:, 3:4]
    a_hi, a_lo = _unpack_halves(y0_ref[...])
    b_hi, b_lo = _unpack_halves(y1_ref[...])
    left = h_ref[:, :n] + g0 * a_hi + g1 * b_hi
    right = h_ref[:, n:] + g0 * a_lo + g1 * b_lo
    ms = (jnp.sum(left * left, axis=-1, keepdims=True)
          + jnp.sum(right * right, axis=-1, keepdims=True)) * (1.0 / D_MODEL)
    scale = lax.rsqrt(ms + EPS)
    out_ref[:, :n] = left * scale * fn_ref[:, :n]
    out_ref[:, n:] = right * scale * fn_ref[:, n:]


def _finalize(yg, info, h3, final_norm, tok0):
    n = yg.shape[0] // 2
    tb = FINAL_TILE
    nb = n // tb
    b0 = tok0 // tb
    tok = lambda w: pl.BlockSpec((tb, w), lambda i: (b0 + i, 0))
    loc = lambda off: pl.BlockSpec((tb, D_MODEL // 2), lambda i: (i + off, 0))
    return pl.pallas_call(
        _finalize_kernel,
        grid=(nb,),
        in_specs=[loc(0), loc(nb), tok(info.shape[1]), tok(D_MODEL), _resident((1, D_MODEL))],
        out_specs=pl.BlockSpec((tb, D_MODEL), lambda i: (i, 0)),
        out_shape=jax.ShapeDtypeStruct((n, D_MODEL), F32),
        compiler_params=_cparams(1, 32),
        name="moe_finalize",
    )(yg, yg, info, h3, final_norm)


def _moe(y_ssm, h2, w_out, ln, wr_hi, wr_lo, rb, w_gu, w_down, final_norm, n_a):
    T = h2.shape[0]
    n_tiles = 2 * T // MOE_SLOT + N_EXPERTS
    h3, xn, info, info_t, counts = _router(y_ssm, h2, w_out, ln, wr_hi, wr_lo, rb)

    counts = counts[:N_EXPERTS, 0].astype(jnp.int32)
    tiles_e = (counts + MOE_SLOT - 1) // MOE_SLOT
    tiles_cum = jnp.cumsum(tiles_e)
    gstart = (tiles_cum - tiles_e) * MOE_SLOT
    routed = info_t.astype(jnp.int32)
    start_of = lambda e: sum(jnp.where(e == x, gstart[x], 0) for x in range(N_EXPERTS))
    pos0 = start_of(routed[0]) + routed[4]
    pos1 = start_of(routed[1]) + routed[5]
    tile_ids = jnp.arange(n_tiles, dtype=jnp.int32)
    tile_expert = jnp.minimum(jnp.sum(tile_ids[:, None] >= tiles_cum[None, :], axis=1), N_EXPERTS - 1)
    tile_expert = tile_expert.astype(jnp.int32)
    tile_rows = jnp.clip(counts[tile_expert] - (tile_ids * MOE_SLOT - gstart[tile_expert]), 0, MOE_SLOT)
    tile_rows = jnp.where(tile_ids < tiles_cum[-1], tile_rows, 0).astype(jnp.int32)
    pos = jnp.stack([pos0.reshape(-1, SC_CHUNK), pos1.reshape(-1, SC_CHUNK)], axis=1)

    xs = _sc_scatter_pairs(xn, pos, n_tiles * MOE_SLOT)
    tile_first = ((tile_ids * MOE_SLOT == gstart[tile_expert]) & (tile_rows > 0)).astype(jnp.int32)
    ys = _experts(xs, w_gu, w_down, tile_expert, tile_rows, tile_first)
    outs = []
    for lo, hi in ((0, n_a), (n_a, T)):
        yg = _sc_gather(ys, jnp.concatenate([pos0[lo:hi], pos1[lo:hi]]))
        outs.append(_finalize(yg, info, h3, final_norm, lo))
    return outs


def _seq_flags(seq_lens, block):
    first, last = [], []
    for n in seq_lens:
        nb = n // block
        first += [1] + [0] * (nb - 1)
        last += [0] * (nb - 1) + [1]
    return np.asarray(first, np.int32), np.asarray(last, np.int32)


def _trunk(xa, xb, seq_lens, ln_mix0, ret_w_in, ret_w_out, ln_ffn0, ffn_w_gu, ffn_w_down, ln_mix1, ssm_w_in,
           ssm_conv_w, ssm_conv_b, ssm_dt_bias, ssm_A_log, ssm_D, ssm_norm_w, ssm_w_out, ln_ffn1,
           moe_router_w, moe_router_b, moe_w_gu, moe_w_down, final_norm):
    assert all(n % b == 0 for n in seq_lens for b in (SCAN_BLOCK, SSD_BLOCK, MOE_TOK, TOK_TILE, PROJ_TILE))
    row = lambda v: v.astype(F32).reshape(1, -1)
    first, last = _seq_flags(seq_lens, SCAN_BLOCK)
    reset_f = jnp.asarray(first)
    reset_b = jnp.asarray(last[::-1].copy())
    ssd_first, ssd_last = _seq_flags(seq_lens, SSD_BLOCK)

    half = RET_DK // 2
    inv = ROPE_BASE ** (-jnp.arange(half, dtype=F32) / half)
    ang_base = jnp.arange(0, max(seq_lens), TOK_TILE).astype(F32)[:, None, None] * inv[None, None, :]
    ang_off = jnp.arange(TOK_TILE).astype(F32)[:, None] * inv[None, :]
    pos_block = jnp.asarray(np.concatenate([np.arange(n // TOK_TILE) for n in seq_lens]).astype(np.int32))
    q, k, v, g = _ret_in(xa, xb, row(ln_mix0), ret_w_in.astype(BF16), jnp.cos(ang_base), jnp.sin(ang_base),
                         jnp.cos(ang_off), jnp.sin(ang_off), pos_block)
    y = _ret_scan(q, k, v, g, reset_f, reset_b)
    h2 = _post0(y, xa, xb, ret_w_out.astype(BF16), row(ln_ffn0), ffn_w_gu.astype(BF16),
                ffn_w_down.astype(BF16))

    wz = ssm_w_in[:, :SSM_DINNER].astype(BF16)
    wx = ssm_w_in[:, SSM_DINNER:SSM_DINNER + SSM_CONV_DIM].astype(BF16)
    wdt = jnp.pad(ssm_w_in[:, SSM_DINNER + SSM_CONV_DIM:], ((0, 0), (0, LANES - 2 * SSM_HEADS)))
    wdh, wdl = _split2(wdt)
    pad_row = lambda v: jnp.pad(v.astype(F32).reshape(1, -1), ((0, 0), (0, LANES - 2 * SSM_HEADS)))
    gate, xbc, dt = _ssm_in(h2, row(ln_mix1), wz, wx, wdh, wdl, pad_row(ssm_dt_bias))
    xact = _conv(xbc, ssm_conv_w.reshape(SSM_CONV, SSM_CONV_DIM).astype(F32), row(ssm_conv_b),
                 jnp.asarray(ssd_first), jnp.asarray(ssd_last))
    a_row = pad_row(-jnp.exp(ssm_A_log.astype(F32)))
    dskip = jnp.repeat(ssm_D.astype(F32), SSM_HEADDIM).reshape(1, SSM_DINNER)
    y = _ssd(xact, dt, gate, a_row, dskip, row(ssm_norm_w), jnp.asarray(ssd_first),
             jnp.asarray(ssd_last[::-1].copy()))

    wr = jnp.pad(moe_router_w.astype(F32).T, ((0, LANES - N_EXPERTS), (0, 0)))
    wr_hi, wr_lo = _split2(wr)
    rb = jnp.pad(moe_router_b.astype(F32).reshape(-1, 1), ((0, LANES - N_EXPERTS), (0, 0)), constant_values=-1e30)
    rb = jnp.broadcast_to(rb, (LANES, MOE_TOK))
    return _moe(y, h2, ssm_w_out.astype(BF16), row(ln_ffn1), wr_hi, wr_lo, rb,
                moe_w_gu, moe_w_down, row(final_norm), xa.shape[0])


def kernel(x_prompt, x_sample, ln_mix0, ret_w_in, ret_w_out, ln_ffn0, ffn_w_gu, ffn_w_down, ln_mix1, ssm_w_in,
           ssm_conv_w, ssm_conv_b, ssm_dt_bias, ssm_A_log, ssm_D, ssm_norm_w, ssm_w_out, ln_ffn1, moe_router_w,
           moe_router_b, moe_w_gu, moe_w_down, final_norm):
    seq_lens = [x_prompt.shape[1]] * x_prompt.shape[0] + [x_sample.shape[1]] * x_sample.shape[0]
    out_a, out_b = _trunk(x_prompt.reshape(-1, D_MODEL), x_sample.reshape(-1, D_MODEL), seq_lens, ln_mix0, ret_w_in, ret_w_out, ln_ffn0, ffn_w_gu, ffn_w_down, ln_mix1, ssm_w_in,
                 ssm_conv_w, ssm_conv_b, ssm_dt_bias, ssm_A_log, ssm_D, ssm_norm_w, ssm_w_out, ln_ffn1,
                 moe_router_w, moe_router_b, moe_w_gu, moe_w_down, final_norm)
    return (out_a.reshape(x_prompt.shape), out_b.reshape(x_sample.shape))
```
